```python
import math
import jax
import jax.numpy as jnp
from jax import lax
import numpy as np

D_MODEL = 1024
BATCH = 16
SEQ = 2048
DEPTH = 2

GRID_W = 64
CTX_LEN = 256
D_MIX = D_MODEL
BRANCH = D_MIX // 4
HEAD_DIM = 64
BRANCH_HEADS = BRANCH // HEAD_DIM
SSD_HEADS = BRANCH_HEADS
SSD_GROUPS = 2
SSD_STATE = 128
SSD_BC = SSD_GROUPS * SSD_STATE
SSD_CONV = 5
SSD_CONV_CH = BRANCH + 2 * SSD_BC
SSD_CHUNK = 2 * GRID_W
GM_CHUNK = 2 * GRID_W
GM_HEADS = BRANCH_HEADS
GM_HEAD_DIM = BRANCH // GM_HEADS
SC_CONV = 3
CF_CONV = 31
D_FF = 2816
N_EXPERTS = 8
TOP_K = 2
E_FF = 3584
N_DENSE = (DEPTH + 1) // 2
N_MOE = DEPTH // 2
IN_WIDTHS = (SSD_CONV_CH, 2 * SSD_HEADS, BRANCH, 2 * BRANCH, 3 * BRANCH, 2 * BRANCH)
IN_COLS = sum(IN_WIDTHS)
IN_SPLITS = tuple(int(s) for s in np.cumsum(IN_WIDTHS)[:-1])
SSD_CTX_COLS = SSD_CONV_CH + 2 * SSD_HEADS

kernel_name = 'hybrid_ssd_gmlp_conv_moe_dit'


def rmsnorm(x, g, eps=1e-6):
    xf = x.astype(jnp.float32)
    y = xf * lax.rsqrt(jnp.mean(xf * xf, axis=-1, keepdims=True) + eps)
    return y.astype(x.dtype) * g


def layernorm(x, g, b, eps=1e-5):
    xf = x.astype(jnp.float32)
    mu = jnp.mean(xf, axis=-1, keepdims=True)
    var = jnp.mean(jnp.square(xf - mu), axis=-1, keepdims=True)
    return ((xf - mu) * lax.rsqrt(var + eps)).astype(x.dtype) * g + b


def modulate(h, shift, scale):
    return h * (1 + scale) + shift


def dwconv(x, w):
    k, ch = w.shape
    return lax.conv_general_dilated(x, w[:, None, :], (1,), [(k // 2, k // 2)],
                                    dimension_numbers=('NWC', 'WIO', 'NWC'), feature_group_count=ch)


def flip(t):
    return jnp.flip(t, axis=1)


def to_chunks(t):
    bsz, length = t.shape[:2]
    return t.reshape(bsz, length // SSD_CHUNK, SSD_CHUNK, *t.shape[2:])


def ssd_prepare(xs, dt, a):
    bsz, length, nh, hd = xs.shape
    nc = length // SSD_CHUNK
    xdt = (xs * dt[..., None]).reshape(bsz, nc, SSD_CHUNK, nh, hd)
    a_cs = jnp.cumsum((dt * a).reshape(bsz, nc, SSD_CHUNK, nh), axis=2).transpose(0, 3, 1, 2)
    return xdt, a_cs


def ssd_chunk_states(xdt, a_cs, b_c, init):
    decay_to_end = jnp.exp(a_cs[..., -1:] - a_cs)
    chunk_states = jnp.einsum('bclhn,bhcl,bclhp->bchpn', b_c, decay_to_end, xdt)
    chunk_decay = jnp.exp(a_cs[..., -1])

    def step(state, inp):
        s_c, d_c = inp
        return state * d_c[..., None, None] + s_c, state

    final, entering = lax.scan(step, init, (jnp.moveaxis(chunk_states, 1, 0), jnp.moveaxis(chunk_decay, 2, 0)))
    return jnp.moveaxis(entering, 0, 1), final


def ssd_scan(xs, dt, a, bs, cs, init):
    xdt, a_cs = ssd_prepare(xs, dt, a)
    b_c, c_c = to_chunks(bs), to_chunks(cs)
    entering, final = ssd_chunk_states(xdt, a_cs, b_c, init)
    lower = jnp.tril(jnp.ones((SSD_CHUNK, SSD_CHUNK), dtype=bool))
    lmat = jnp.exp(jnp.where(lower, a_cs[..., :, None] - a_cs[..., None, :], -jnp.inf))
    y_diag = jnp.einsum('bclhn,bcshn,bhcls,bcshp->bclhp', c_c, b_c, lmat, xdt)
    y_off = jnp.einsum('bclhn,bchpn,bhcl->bclhp', c_c, entering, jnp.exp(a_cs))
    return (y_diag + y_off).reshape(xs.shape), final


def ssd_inputs(xbc_raw, dt_raw, p):
    bsz, length, _ = xbc_raw.shape
    xbc = jax.nn.silu(dwconv(xbc_raw, p['ssd_conv_w']) + p['ssd_conv_b']).astype(jnp.float32)
    xs, bs, cs = jnp.split(xbc, (BRANCH, BRANCH + SSD_BC), axis=-1)
    rep = SSD_HEADS // SSD_GROUPS
    xs = xs.reshape(bsz, length, SSD_HEADS, HEAD_DIM)
    bs = jnp.repeat(bs.reshape(bsz, length, SSD_GROUPS, SSD_STATE), rep, axis=2)
    cs = jnp.repeat(cs.reshape(bsz, length, SSD_GROUPS, SSD_STATE), rep, axis=2)
    dt = jax.nn.softplus(dt_raw.astype(jnp.float32).reshape(bsz, length, 2, SSD_HEADS)
                         + p['ssd_dt_bias'].astype(jnp.float32))
    a = -jnp.exp(p['ssd_a_log'].astype(jnp.float32))
    return xs, bs, cs, dt, a


def ssd_bidir(xs, bs, cs, dt, a, init_f, init_b):
    y_f, fin_f = ssd_scan(xs, dt[:, :, 0], a[0], bs, cs, init_f)
    y_b, fin_b = ssd_scan(flip(xs), flip(dt[:, :, 1]), a[1], flip(bs), flip(cs), init_b)
    return y_f + flip(y_b), fin_f, fin_b


def ssd_context_states(zc, p, init):
    xbc_raw, dt_raw = jnp.split(zc, (SSD_CONV_CH,), axis=-1)
    xs, bs, _, dt, a = ssd_inputs(xbc_raw, dt_raw, p)
    xdt_f, acs_f = ssd_prepare(xs, dt[:, :, 0], a[0])
    fin_f = ssd_chunk_states(xdt_f, acs_f, to_chunks(bs), init)[1]
    xdt_b, acs_b = ssd_prepare(flip(xs), flip(dt[:, :, 1]), a[1])
    fin_b = ssd_chunk_states(xdt_b, acs_b, to_chunks(flip(bs)), init)[1]
    return fin_f, fin_b


def mixer_branches(z, p, init_f, init_b):
    xbc_raw, dt_raw, ssd_z, uv, sc, cf = jnp.split(z, IN_SPLITS, axis=-1)
    bsz, length, _ = z.shape
    xs, bs, cs, dt, a = ssd_inputs(xbc_raw, dt_raw, p)
    y, fin_f, fin_b = ssd_bidir(xs, bs, cs, dt, a, init_f, init_b)
    y = (y + p['ssd_d'].astype(jnp.float32)[:, None] * xs).reshape(bsz, length, BRANCH).astype(z.dtype)
    y_ssd = rmsnorm(y * jax.nn.silu(ssd_z), p['ssd_norm_g'])
    u, v = jnp.split(jax.nn.gelu(uv), 2, axis=-1)
    v = layernorm(v, p['gm_norm_g'], p['gm_norm_b'])
    v = v.reshape(bsz, length // GM_CHUNK, GM_CHUNK, GM_HEADS, GM_HEAD_DIM)
    s = jnp.einsum('gts,bcsgd->bctgd', p['gm_ws'], v) + p['gm_bs'].T[:, :, None]
    y_gm = u * s.reshape(bsz, length, BRANCH)
    gb, gc, xin = jnp.split(sc, 3, axis=-1)
    y_sc = gb * dwconv(gc * xin, p['sc_conv_w'])
    ca, cg = jnp.split(cf, 2, axis=-1)
    cv = dwconv(ca * jax.nn.sigmoid(cg), p['cf_conv_w']) + p['cf_conv_b']
    y_cf = jax.nn.silu(layernorm(cv, p['cf_norm_g'], p['cf_norm_b']))
    return jnp.concatenate([y_ssd, y_gm, y_sc, y_cf], axis=-1), fin_f, fin_b


def swiglu(h, w1, w3, w2):
    return (jax.nn.silu(h @ w1) * (h @ w3)) @ w2


def moe_swiglu(h, router, w1, w3, w2):
    logits = (h @ router).astype(jnp.float32)
    top_v, top_i = lax.top_k(logits, TOP_K)
    probs = jax.nn.softmax(top_v, axis=-1)
    combine = jnp.sum(jax.nn.one_hot(top_i, N_EXPERTS, dtype=jnp.float32) * probs[..., None], axis=-2).astype(h.dtype)
    out = jnp.zeros_like(h)
    for e in range(N_EXPERTS):
        out = out + combine[..., e:e + 1] * swiglu(h, w1[e], w3[e], w2[e])
    return out


def channel_mixer(t, l, ffn_w1, ffn_w3, ffn_w2, moe_router, moe_w1, moe_w3, moe_w2):
    i = l // 2
    if l % 2 == 0:
        return swiglu(t, ffn_w1[i], ffn_w3[i], ffn_w2[i])
    return moe_swiglu(t, moe_router[i], moe_w1[i], moe_w3[i], moe_w2[i])


def setup_inputs(seed: int = 0) -> dict:
    key = jax.random.key(seed)
    ks = iter(jax.random.split(key, 48))

    def nrm(shape, scale):
        return jax.random.normal(next(ks), shape, jnp.float32) * scale

    def gain(shape):
        return 1.0 + nrm(shape, 0.05)

    dt0 = jnp.exp(jax.random.uniform(next(ks), (DEPTH, 2, SSD_HEADS), jnp.float32, math.log(1e-3), math.log(1e-1)))
    a0 = jax.random.uniform(next(ks), (DEPTH, 2, SSD_HEADS), jnp.float32, 1.0, 16.0)
    return {
        'x': nrm((BATCH, SEQ, D_MODEL), 1.0),
        'c': nrm((BATCH, D_MODEL), 1.0),
        'ctx': nrm((BATCH, CTX_LEN, D_MODEL), 1.0),
        'c_ctx': nrm((D_MODEL,), 1.0),
        'mod_w': nrm((DEPTH, D_MODEL, 6 * D_MODEL), 0.5 * D_MODEL ** -0.5),
        'mod_b': nrm((DEPTH, 6 * D_MODEL), 0.02),
        'norm1_g': gain((DEPTH, D_MODEL)),
        'norm2_g': gain((DEPTH, D_MODEL)),
        'w_in': nrm((DEPTH, D_MODEL, IN_COLS), D_MODEL ** -0.5),
        'w_out': nrm((DEPTH, D_MIX, D_MODEL), D_MIX ** -0.5),
        'ssd_conv_w': nrm((DEPTH, SSD_CONV, SSD_CONV_CH), SSD_CONV ** -0.5),
        'ssd_conv_b': nrm((DEPTH, SSD_CONV_CH), 0.02),
        'ssd_dt_bias': dt0 + jnp.log(-jnp.expm1(-dt0)),
        'ssd_a_log': jnp.log(a0),
        'ssd_d': gain((DEPTH, SSD_HEADS)),
        'ssd_norm_g': gain((DEPTH, BRANCH)),
        'gm_norm_g': gain((DEPTH, BRANCH)),
        'gm_norm_b': nrm((DEPTH, BRANCH), 0.02),
        'gm_ws': nrm((DEPTH, GM_HEADS, GM_CHUNK, GM_CHUNK), GM_CHUNK ** -0.5),
        'gm_bs': gain((DEPTH, GM_HEADS, GM_CHUNK)),
        'sc_conv_w': nrm((DEPTH, SC_CONV, BRANCH), SC_CONV ** -0.5),
        'cf_conv_w': nrm((DEPTH, CF_CONV, BRANCH), CF_CONV ** -0.5),
        'cf_conv_b': nrm((DEPTH, BRANCH), 0.02),
        'cf_norm_g': gain((DEPTH, BRANCH)),
        'cf_norm_b': nrm((DEPTH, BRANCH), 0.02),
        'ffn_w1': nrm((N_DENSE, D_MODEL, D_FF), D_MODEL ** -0.5),
        'ffn_w3': nrm((N_DENSE, D_MODEL, D_FF), D_MODEL ** -0.5),
        'ffn_w2': nrm((N_DENSE, D_FF, D_MODEL), D_FF ** -0.5),
        'moe_router': nrm((N_MOE, D_MODEL, N_EXPERTS), D_MODEL ** -0.5),
        'moe_w1': nrm((N_MOE, N_EXPERTS, D_MODEL, E_FF), D_MODEL ** -0.5),
        'moe_w3': nrm((N_MOE, N_EXPERTS, D_MODEL, E_FF), D_MODEL ** -0.5),
        'moe_w2': nrm((N_MOE, N_EXPERTS, E_FF, D_MODEL), E_FF ** -0.5),
        'final_norm_g': gain((D_MODEL,)),
    }


def reference(x, c, ctx, c_ctx, mod_w, mod_b, norm1_g, norm2_g, w_in, w_out,
              ssd_conv_w, ssd_conv_b, ssd_dt_bias, ssd_a_log, ssd_d, ssd_norm_g,
              gm_norm_g, gm_norm_b, gm_ws, gm_bs, sc_conv_w, cf_conv_w, cf_conv_b, cf_norm_g, cf_norm_b,
              ffn_w1, ffn_w3, ffn_w2, moe_router, moe_w1, moe_w3, moe_w2, final_norm_g):
    bsz = x.shape[0]
    xc = ctx
    silu_c = jax.nn.silu(c)
    silu_cc = jax.nn.silu(c_ctx)
    zero_state = jnp.zeros((bsz, SSD_HEADS, HEAD_DIM, SSD_STATE), jnp.float32)
    for l in range(DEPTH):
        last = l == DEPTH - 1
        p = {'ssd_conv_w': ssd_conv_w[l], 'ssd_conv_b': ssd_conv_b[l], 'ssd_dt_bias': ssd_dt_bias[l],
             'ssd_a_log': ssd_a_log[l], 'ssd_d': ssd_d[l], 'ssd_norm_g': ssd_norm_g[l],
             'gm_norm_g': gm_norm_g[l], 'gm_norm_b': gm_norm_b[l], 'gm_ws': gm_ws[l], 'gm_bs': gm_bs[l],
             'sc_conv_w': sc_conv_w[l], 'cf_conv_w': cf_conv_w[l], 'cf_conv_b': cf_conv_b[l],
             'cf_norm_g': cf_norm_g[l], 'cf_norm_b': cf_norm_b[l]}
        sh1, sc1, g1, sh2, sc2, g2 = jnp.split((silu_c @ mod_w[l] + mod_b[l])[:, None, :], 6, axis=-1)
        csh1, csc1, cg1, csh2, csc2, cg2 = jnp.split(silu_cc @ mod_w[l] + mod_b[l], 6, axis=-1)
        h = modulate(rmsnorm(x, norm1_g[l]), sh1, sc1)
        hc = modulate(rmsnorm(xc, norm1_g[l]), csh1, csc1)
        if last:
            st_f, st_b = ssd_context_states(hc @ w_in[l][:, :SSD_CTX_COLS], p, zero_state)
        else:
            mix_c, st_f, st_b = mixer_branches(hc @ w_in[l], p, zero_state, zero_state)
            xc = xc + cg1 * (mix_c @ w_out[l])
        mix, _, _ = mixer_branches(h @ w_in[l], p, st_f, st_b)
        x = x + g1 * (mix @ w_out[l])
        h2 = modulate(rmsnorm(x, norm2_g[l]), sh2, sc2)
        x = x + g2 * channel_mixer(h2, l, ffn_w1, ffn_w3, ffn_w2, moe_router, moe_w1, moe_w3, moe_w2)
        if not last:
            hc2 = modulate(rmsnorm(xc, norm2_g[l]), csh2, csc2)
            xc = xc + cg2 * channel_mixer(hc2, l, ffn_w1, ffn_w3, ffn_w2, moe_router, moe_w1, moe_w3, moe_w2)
    return rmsnorm(x, final_norm_g)
```

```python
import functools

import jax
import jax.numpy as jnp
from jax import lax
from jax.experimental import pallas as pl
from jax.experimental.pallas import tpu as pltpu

F32 = jnp.float32
BF16 = jnp.bfloat16

LANES = 128
VMEM_LIMIT_BYTES = 56 * 1024 * 1024

D_MODEL = 1024
BRANCH = 256
HEAD_DIM = 64
SSD_HEADS = 4
SSD_GROUPS = 2
SSD_STATE = 128
CHUNK = 128
SSD_CONV_CH = BRANCH + 2 * SSD_GROUPS * SSD_STATE
N_EXPERTS = 8
HALO = 16
RMS_EPS = 1e-6
LN_EPS = 1e-5

_C_XBC = 0
_C_DT = _C_XBC + SSD_CONV_CH
_C_Z = _C_DT + 2 * SSD_HEADS
_C_UV = _C_Z + BRANCH
_C_SC = _C_UV + 2 * BRANCH
_C_CF = _C_SC + 3 * BRANCH
_C_END = _C_CF + 2 * BRANCH
W_SSD = SSD_CONV_CH + BRANCH
W_UVCF = 4 * BRANCH
W_SC = 3 * BRANCH
W_MAIN = W_SSD + W_UVCF + W_SC


def _params(*sem):
    return pltpu.CompilerParams(dimension_semantics=sem, vmem_limit_bytes=VMEM_LIMIT_BYTES)


def _silu(v):
    return v * (1.0 / (1.0 + jnp.exp(-v)))


def _sigmoid(v):
    return 1.0 / (1.0 + jnp.exp(-v))


def _softplus(v):
    return jnp.maximum(v, 0.0) + jnp.log1p(jnp.exp(-jnp.abs(v)))


def _gelu_tanh(v):
    return 0.5 * v * (1.0 + jnp.tanh(0.7978845608028654 * (v + 0.044715 * (v * v * v))))


def _norm_modulate(x, g, shift, scale):
    ms = jnp.mean(x * x, axis=-1, keepdims=True)
    return (x * lax.rsqrt(ms + RMS_EPS) * g) * (1.0 + scale) + shift


def _mod_kernel(c_ref, w_ref, b_ref, o_ref):
    o_ref[0, 0] = jnp.dot(_silu(c_ref[...]), w_ref[0], preferred_element_type=F32,
                          precision=lax.Precision.HIGHEST) + b_ref[0, 0]


def _modulation(c_all, mod_w, mod_b):
    depth = mod_w.shape[0]
    rows = c_all.shape[0]
    out = pl.pallas_call(
        _mod_kernel,
        grid=(depth, 6),
        in_specs=[
            pl.BlockSpec((rows, D_MODEL), lambda l, j: (0, 0)),
            pl.BlockSpec((1, D_MODEL, D_MODEL), lambda l, j: (l, 0, j)),
            pl.BlockSpec((1, 1, 1, D_MODEL), lambda l, j: (l, j, 0, 0)),
        ],
        out_specs=pl.BlockSpec((1, 1, rows, D_MODEL), lambda l, j: (l, j, 0, 0)),
        out_shape=jax.ShapeDtypeStruct((depth, 6, rows, D_MODEL), F32),
        compiler_params=_params("arbitrary", "arbitrary"),
        name="modulation",
    )(c_all, mod_w, mod_b.reshape(depth, 6, 1, D_MODEL))
    return jnp.transpose(out, (0, 2, 1, 3))


def _mod_index(tm, rows_per_mod, mod_base):
    return lambda i: (mod_base + (i * tm) // rows_per_mod, 0, 0)


def _inproj_kernel(x_ref, mod_ref, g_ref, w_ref, wdt_ref, zssd_ref, zuvcf_ref, zsc_ref, dt_ref):
    h = _norm_modulate(x_ref[...], g_ref[...], mod_ref[0, 0:1, :], mod_ref[0, 1:2, :]).astype(BF16)
    z = jnp.dot(h, w_ref[...], preferred_element_type=F32)
    zssd_ref[...] = z[:, 0:W_SSD].astype(BF16)
    zuvcf_ref[...] = z[:, W_SSD:W_SSD + W_UVCF].astype(BF16)
    zsc_ref[...] = z[:, W_SSD + W_UVCF:W_MAIN].astype(BF16)
    dt_ref[...] = jnp.dot(h, wdt_ref[...], preferred_element_type=F32)


def _in_projection(x2d, mod, g, w_main, w_dt, *, tm, rows_per_mod, mod_base):
    n = x2d.shape[0]
    row = lambda i: (i, 0)
    const = lambda i: (0, 0)
    return pl.pallas_call(
        _inproj_kernel,
        grid=(n // tm,),
        in_specs=[
            pl.BlockSpec((tm, D_MODEL), row),
            pl.BlockSpec((1, 6, D_MODEL), _mod_index(tm, rows_per_mod, mod_base)),
            pl.BlockSpec((1, D_MODEL), const),
            pl.BlockSpec((D_MODEL, W_MAIN), const),
            pl.BlockSpec((D_MODEL, LANES), const),
        ],
        out_specs=[
            pl.BlockSpec((tm, W_SSD), row),
            pl.BlockSpec((tm, W_UVCF), row),
            pl.BlockSpec((tm, W_SC), row),
            pl.BlockSpec((tm, LANES), row),
        ],
        out_shape=[
            jax.ShapeDtypeStruct((n, W_SSD), BF16),
            jax.ShapeDtypeStruct((n, W_UVCF), BF16),
            jax.ShapeDtypeStruct((n, W_SC), BF16),
            jax.ShapeDtypeStruct((n, LANES), F32),
        ],
        compiler_params=_params("parallel"),
        name="in_projection",
    )(x2d, mod, g, w_main, w_dt)


def _stage_window(stage_ref, src, i, nblocks, rows):
    r0 = pl.multiple_of(i * rows, rows)
    stage_ref[HALO:HALO + rows, :] = src(r0, rows)

    @pl.when(i > 0)
    def _():
        stage_ref[0:HALO, :] = src(pl.multiple_of(r0 - HALO, HALO), HALO)

    @pl.when(i == 0)
    def _():
        stage_ref[0:HALO, :] = jnp.zeros((HALO, stage_ref.shape[1]), F32)

    @pl.when(i < nblocks - 1)
    def _():
        stage_ref[HALO + rows:2 * HALO + rows, :] = src(pl.multiple_of(r0 + rows, HALO), HALO)

    @pl.when(i == nblocks - 1)
    def _():
        stage_ref[HALO + rows:2 * HALO + rows, :] = jnp.zeros((HALO, stage_ref.shape[1]), F32)


def _dwconv_from_stage(stage_ref, w_ref, taps, rows, c0, width):
    acc = None
    for k in range(taps):
        lo = HALO + k - taps // 2
        term = stage_ref[lo:lo + rows, c0:c0 + width] * w_ref[k:k + 1, c0:c0 + width]
        acc = term if acc is None else acc + term
    return acc


def _ssd_kernel(z_ref, dt_ref, init_ref, convw_ref, convb_ref, dtp_ref, vec_ref,
                y_ref, fin_ref, stage_ref, xbc_ref, p_ref, dts_ref, yacc_ref, st_ref, *, seq):
    nc = seq // CHUNK
    lane = lax.broadcasted_iota(jnp.int32, (CHUNK, LANES), 1)
    row_i = lax.broadcasted_iota(jnp.int32, (CHUNK, CHUNK), 0)
    col_i = lax.broadcasted_iota(jnp.int32, (CHUNK, CHUNK), 1)
    lower = col_i <= row_i
    upper = col_i >= row_i
    low_f = lower.astype(F32)
    up_f = upper.astype(F32)
    head0 = lane < HEAD_DIM

    dt_bias = dtp_ref[0:1, :]
    a_row = -jnp.exp(dtp_ref[1:2, :])
    a_row = jnp.where(lane[0:1, :] < 2 * SSD_HEADS, a_row, 0.0)

    def prep(c, carry):
        r0 = pl.multiple_of(c * CHUNK, CHUNK)
        _stage_window(stage_ref, lambda s, n: z_ref[0, pl.ds(s, n), 0:SSD_CONV_CH].astype(F32),
                      c, nc, CHUNK)
        for c0 in range(0, SSD_CONV_CH, BRANCH):
            v = _dwconv_from_stage(stage_ref, convw_ref, 5, CHUNK, c0, BRANCH) + convb_ref[0:1, c0:c0 + BRANCH]
            xbc_ref[pl.ds(r0, CHUNK), c0:c0 + BRANCH] = _silu(v)
        dt = _softplus(dt_ref[0, pl.ds(r0, CHUNK), :] + dt_bias)
        adt = dt * a_row
        adt_f = jnp.where(lane < SSD_HEADS, adt, 0.0)
        adt_b = jnp.where(lane >= SSD_HEADS, adt, 0.0)
        p_ref[pl.ds(r0, CHUNK), :] = (
            jnp.dot(low_f, adt_f, preferred_element_type=F32, precision=lax.Precision.HIGHEST)
            + jnp.dot(up_f, adt_b, preferred_element_type=F32, precision=lax.Precision.HIGHEST))
        dts_ref[pl.ds(r0, CHUNK), :] = dt
        return carry

    lax.fori_loop(0, nc, prep, 0)

    def col(m, j):
        return m[:, j:j + 1]

    def pair(m, j0, j1):
        return jnp.where(head0, col(m, j0), col(m, j1))

    st_ref[...] = init_ref[0, 0]

    def fwd(c, carry):
        r0 = pl.multiple_of(c * CHUNK, CHUNK)
        rows = pl.ds(r0, CHUNK)
        pm = p_ref[rows, :]
        dt = dts_ref[rows, :]
        pm_t = pm.T
        dt_t = dt.T
        for g in range(SSD_GROUPS):
            xg = xbc_ref[rows, g * LANES:(g + 1) * LANES]
            bg = xbc_ref[rows, BRANCH + g * SSD_STATE:BRANCH + (g + 1) * SSD_STATE].astype(BF16)
            cg = xbc_ref[rows, BRANCH + (SSD_GROUPS + g) * SSD_STATE:
                         BRANCH + (SSD_GROUPS + g + 1) * SSD_STATE].astype(BF16)
            gram = lax.dot_general(cg, bg, (((1,), (1,)), ((), ())), preferred_element_type=F32)
            s_prev = st_ref[g]
            y = jnp.dot(cg, s_prev.astype(BF16), preferred_element_type=F32)
            hf0, hf1 = 2 * g, 2 * g + 1
            hb0, hb1 = SSD_HEADS + hf0, SSD_HEADS + hf1
            y = y * jnp.exp(pair(pm, hf0, hf1))
            for hf, hb, mask in ((hf0, hb0, head0), (hf1, hb1, jnp.logical_not(head0))):
                dec_f = jnp.exp(jnp.where(lower, col(pm, hf) - pm_t[hf:hf + 1, :], -jnp.inf))
                dec_b = jnp.exp(jnp.where(upper, col(pm, hb) - pm_t[hb:hb + 1, :], -jnp.inf))
                m = gram * (dec_f * dt_t[hf:hf + 1, :] + dec_b * dt_t[hb:hb + 1, :])
                xh = jnp.where(mask, xg, 0.0).astype(BF16)
                y = y + jnp.dot(m.astype(BF16), xh, preferred_element_type=F32)
            yacc_ref[rows, g * LANES:(g + 1) * LANES] = y
            last = pm[CHUNK - 1:CHUNK, :]
            w_end = pair(dt, hf0, hf1) * jnp.exp(pair(last - pm, hf0, hf1))
            upd = lax.dot_general(bg, (xg * w_end).astype(BF16), (((0,), (0,)), ((), ())),
                                  preferred_element_type=F32)
            decay = jnp.exp(jnp.where(head0[0:1, :], last[:, hf0:hf0 + 1], last[:, hf1:hf1 + 1]))
            st_ref[g] = s_prev * decay + upd
        return carry

    lax.fori_loop(0, nc, fwd, 0)
    fin_ref[0, 0] = st_ref[...]

    st_ref[...] = init_ref[0, 1]
    d_skip = vec_ref[0:1, :]
    norm_g = vec_ref[1:2, :]

    def bwd(k, carry):
        c = nc - 1 - k
        r0 = pl.multiple_of(c * CHUNK, CHUNK)
        rows = pl.ds(r0, CHUNK)
        pm = p_ref[rows, :]
        dt = dts_ref[rows, :]
        for g in range(SSD_GROUPS):
            xg = xbc_ref[rows, g * LANES:(g + 1) * LANES]
            bg = xbc_ref[rows, BRANCH + g * SSD_STATE:BRANCH + (g + 1) * SSD_STATE].astype(BF16)
            cg = xbc_ref[rows, BRANCH + (SSD_GROUPS + g) * SSD_STATE:
                         BRANCH + (SSD_GROUPS + g + 1) * SSD_STATE].astype(BF16)
            hb0, hb1 = SSD_HEADS + 2 * g, SSD_HEADS + 2 * g + 1
            s_prev = st_ref[g]
            y = jnp.dot(cg, s_prev.astype(BF16), preferred_element_type=F32) * jnp.exp(pair(pm, hb0, hb1))
            yacc_ref[rows, g * LANES:(g + 1) * LANES] = (
                yacc_ref[rows, g * LANES:(g + 1) * LANES] + y + d_skip[:, g * LANES:(g + 1) * LANES] * xg)
            first = pm[0:1, :]
            w_end = pair(dt, hb0, hb1) * jnp.exp(pair(first - pm, hb0, hb1))
            upd = lax.dot_general(bg, (xg * w_end).astype(BF16), (((0,), (0,)), ((), ())),
                                  preferred_element_type=F32)
            decay = jnp.exp(jnp.where(head0[0:1, :], first[:, hb0:hb0 + 1], first[:, hb1:hb1 + 1]))
            st_ref[g] = s_prev * decay + upd
        gated = yacc_ref[rows, :] * _silu(z_ref[0, rows, SSD_CONV_CH:W_SSD].astype(F32))
        ms = jnp.mean(gated * gated, axis=-1, keepdims=True)
        y_ref[0, rows, :] = (gated * lax.rsqrt(ms + RMS_EPS) * norm_g).astype(BF16)
        return carry

    lax.fori_loop(0, nc, bwd, 0)
    fin_ref[0, 1] = st_ref[...]


def _ssd_mixer(z_ssd, dt, init, lp):
    bsz, seq, _ = z_ssd.shape
    seq_map = lambda b: (b, 0, 0)
    st_map = lambda b: (b, 0, 0, 0, 0)
    const = lambda b: (0, 0)
    st_shape = (bsz, 2, SSD_GROUPS, SSD_STATE, 2 * HEAD_DIM)
    return pl.pallas_call(
        functools.partial(_ssd_kernel, seq=seq),
        grid=(bsz,),
        in_specs=[
            pl.BlockSpec((1, seq, W_SSD), seq_map),
            pl.BlockSpec((1, seq, LANES), seq_map),
            pl.BlockSpec((1,) + st_shape[1:], st_map),
            pl.BlockSpec((5, SSD_CONV_CH), const),
            pl.BlockSpec((1, SSD_CONV_CH), const),
            pl.BlockSpec((2, LANES), const),
            pl.BlockSpec((2, BRANCH), const),
        ],
        out_specs=[
            pl.BlockSpec((1, seq, BRANCH), seq_map),
            pl.BlockSpec((1,) + st_shape[1:], st_map),
        ],
        out_shape=[
            jax.ShapeDtypeStruct((bsz, seq, BRANCH), BF16),
            jax.ShapeDtypeStruct(st_shape, F32),
        ],
        scratch_shapes=[
            pltpu.VMEM((CHUNK + 2 * HALO, SSD_CONV_CH), F32),
            pltpu.VMEM((seq, SSD_CONV_CH), F32),
            pltpu.VMEM((seq, LANES), F32),
            pltpu.VMEM((seq, LANES), F32),
            pltpu.VMEM((seq, BRANCH), F32),
            pltpu.VMEM((SSD_GROUPS, SSD_STATE, 2 * HEAD_DIM), F32),
        ],
        compiler_params=_params("parallel"),
        name="ssd_mixer",
    )(z_ssd, dt, init, lp["ssd_conv_w"], lp["ssd_conv_b"], lp["ssd_dtp"], lp["ssd_vec"])


def _local_mixers_kernel(zuvcf_ref, zsc_ref, ws_ref, bs_ref, scw_ref, cfw_ref, vec_ref,
                         y_ref, stage_ref, *, seq):
    nb = seq // CHUNK
    lane = lax.broadcasted_iota(jnp.int32, (CHUNK, BRANCH), 1)
    gm_g, gm_b = vec_ref[0:1, :], vec_ref[1:2, :]
    cf_b, cf_g, cf_nb = vec_ref[2:3, :], vec_ref[3:4, :], vec_ref[4:5, :]

    def layer_norm(v, g, b):
        mu = jnp.mean(v, axis=-1, keepdims=True)
        var = jnp.mean(jnp.square(v - mu), axis=-1, keepdims=True)
        return (v - mu) * lax.rsqrt(var + LN_EPS) * g + b

    def sc_src(s, n):
        zs = zsc_ref[0, pl.ds(s, n), :].astype(F32)
        return zs[:, BRANCH:2 * BRANCH] * zs[:, 2 * BRANCH:3 * BRANCH]

    def cf_src(s, n):
        zc = zuvcf_ref[0, pl.ds(s, n), 2 * BRANCH:4 * BRANCH].astype(F32)
        return zc[:, 0:BRANCH] * _sigmoid(zc[:, BRANCH:2 * BRANCH])

    def block(i, carry):
        r0 = pl.multiple_of(i * CHUNK, CHUNK)
        rows = pl.ds(r0, CHUNK)
        ge = _gelu_tanh(zuvcf_ref[0, rows, 0:2 * BRANCH].astype(F32))
        u = ge[:, 0:BRANCH]
        v = layer_norm(ge[:, BRANCH:2 * BRANCH], gm_g, gm_b)
        s = bs_ref[...]
        for h in range(BRANCH // HEAD_DIM):
            in_head = (lane >= h * HEAD_DIM) & (lane < (h + 1) * HEAD_DIM)
            s = s + jnp.dot(ws_ref[h].astype(BF16), jnp.where(in_head, v, 0.0).astype(BF16),
                            preferred_element_type=F32)
        y_ref[0, rows, 0:BRANCH] = (u * s).astype(BF16)
        _stage_window(stage_ref, sc_src, i, nb, CHUNK)
        gate = zsc_ref[0, rows, 0:BRANCH].astype(F32)
        y_ref[0, rows, BRANCH:2 * BRANCH] = (gate * _dwconv_from_stage(stage_ref, scw_ref, 3, CHUNK, 0, BRANCH)
                                             ).astype(BF16)
        _stage_window(stage_ref, cf_src, i, nb, CHUNK)
        cv = _dwconv_from_stage(stage_ref, cfw_ref, 31, CHUNK, 0, BRANCH) + cf_b
        y_ref[0, rows, 2 * BRANCH:3 * BRANCH] = _silu(layer_norm(cv, cf_g, cf_nb)).astype(BF16)
        return carry

    lax.fori_loop(0, nb, block, 0)


def _local_mixers(z_uvcf, z_sc, lp):
    bsz, seq, _ = z_uvcf.shape
    seq_map = lambda b: (b, 0, 0)
    const2 = lambda b: (0, 0)
    return pl.pallas_call(
        functools.partial(_local_mixers_kernel, seq=seq),
        grid=(bsz,),
        in_specs=[
            pl.BlockSpec((1, seq, W_UVCF), seq_map),
            pl.BlockSpec((1, seq, W_SC), seq_map),
            pl.BlockSpec((BRANCH // HEAD_DIM, CHUNK, CHUNK), lambda b: (0, 0, 0)),
            pl.BlockSpec((CHUNK, BRANCH), const2),
            pl.BlockSpec((3, BRANCH), const2),
            pl.BlockSpec((31, BRANCH), const2),
            pl.BlockSpec((5, BRANCH), const2),
        ],
        out_specs=pl.BlockSpec((1, seq, 3 * BRANCH), seq_map),
        out_shape=jax.ShapeDtypeStruct((bsz, seq, 3 * BRANCH), BF16),
        scratch_shapes=[pltpu.VMEM((CHUNK + 2 * HALO, BRANCH), F32)],
        compiler_params=_params("parallel"),
        name="local_mixers",
    )(z_uvcf, z_sc, lp["gm_ws"], lp["gm_bs"], lp["sc_conv_w"], lp["cf_conv_w"], lp["local_vec"])


def _outproj_kernel(x_ref, mod_ref, ya_ref, yb_ref, wa_ref, wb_ref, o_ref):
    proj = (jnp.dot(ya_ref[...], wa_ref[...], preferred_element_type=F32)
            + jnp.dot(yb_ref[...], wb_ref[...], preferred_element_type=F32))
    o_ref[...] = x_ref[...] + mod_ref[0, 2:3, :] * proj


def _out_projection(x2d, mod, y_ssd, y_local, w_a, w_b, *, tm, rows_per_mod, mod_base):
    n = y_ssd.shape[0]
    row = lambda i: (i, 0)
    const = lambda i: (0, 0)
    return pl.pallas_call(
        _outproj_kernel,
        grid=(n // tm,),
        in_specs=[
            pl.BlockSpec((tm, D_MODEL), row),
            pl.BlockSpec((1, 6, D_MODEL), _mod_index(tm, rows_per_mod, mod_base)),
            pl.BlockSpec((tm, BRANCH), row),
            pl.BlockSpec((tm, 3 * BRANCH), row),
            pl.BlockSpec((BRANCH, D_MODEL), const),
            pl.BlockSpec((3 * BRANCH, D_MODEL), const),
        ],
        out_specs=pl.BlockSpec((tm, D_MODEL), row),
        out_shape=jax.ShapeDtypeStruct((n, D_MODEL), F32),
        compiler_params=_params("parallel"),
        name="out_projection",
    )(x2d, mod, y_ssd, y_local, w_a, w_b)


def _router_kernel(x_ref, mod_ref, g_ref, r_ref, comb_ref):
    h = _norm_modulate(x_ref[...], g_ref[...], mod_ref[0, 3:4, :], mod_ref[0, 4:5, :])
    logits = jnp.dot(h, r_ref[...], preferred_element_type=F32, precision=lax.Precision.HIGHEST)
    lane = lax.broadcasted_iota(jnp.int32, logits.shape, 1)
    logits = jnp.where(lane < N_EXPERTS, logits, -jnp.inf)
    m1 = jnp.max(logits, axis=-1, keepdims=True)
    i1 = jnp.min(jnp.where(logits == m1, lane, LANES), axis=-1, keepdims=True)
    rest = jnp.where(lane == i1, -jnp.inf, logits)
    m2 = jnp.max(rest, axis=-1, keepdims=True)
    i2 = jnp.min(jnp.where(rest == m2, lane, LANES), axis=-1, keepdims=True)
    e2 = jnp.exp(m2 - m1)
    denom = 1.0 + e2
    comb_ref[...] = jnp.where(lane == i1, 1.0 / denom, 0.0) + jnp.where(lane == i2, e2 / denom, 0.0)


def _router(x2d, mod, g, router_pad, *, tm, rows_per_mod, mod_base):
    n = x2d.shape[0]
    row = lambda i: (i, 0)
    const = lambda i: (0, 0)
    return pl.pallas_call(
        _router_kernel,
        grid=(n // tm,),
        in_specs=[
            pl.BlockSpec((tm, D_MODEL), row),
            pl.BlockSpec((1, 6, D_MODEL), _mod_index(tm, rows_per_mod, mod_base)),
            pl.BlockSpec((1, D_MODEL), const),
            pl.BlockSpec((D_MODEL, LANES), const),
        ],
        out_specs=pl.BlockSpec((tm, LANES), row),
        out_shape=jax.ShapeDtypeStruct((n, LANES), F32),
        compiler_params=_params("parallel"),
        name="moe_router",
    )(x2d, mod, g, router_pad)


def _swiglu_kernel(*refs, n_experts, final_norm):
    x_ref, mod_ref, g_ref = refs[0:3]
    pos = 3
    comb_ref = None
    if n_experts > 1:
        comb_ref = refs[pos]
        pos += 1
    w1_ref, w3_ref, w2_ref = refs[pos:pos + 3]
    pos += 3
    fg_ref = None
    if final_norm:
        fg_ref = refs[pos]
        pos += 1
    o_ref, h_ref, acc_ref = refs[pos:pos + 3]

    e = pl.program_id(1)
    j = pl.program_id(2)
    first = (e == 0) & (j == 0)
    last = (e == pl.num_programs(1) - 1) & (j == pl.num_programs(2) - 1)

    @pl.when(first)
    def _():
        h_ref[...] = _norm_modulate(x_ref[...], g_ref[...], mod_ref[0, 3:4, :], mod_ref[0, 4:5, :]).astype(BF16)
        acc_ref[...] = jnp.zeros_like(acc_ref)

    h = h_ref[...]
    a = jnp.dot(h, w1_ref[0], preferred_element_type=F32)
    b = jnp.dot(h, w3_ref[0], preferred_element_type=F32)
    t = _silu(a) * b
    if comb_ref is not None:
        lane = lax.broadcasted_iota(jnp.int32, comb_ref.shape, 1)
        t = t * jnp.sum(jnp.where(lane == e, comb_ref[...], 0.0), axis=-1, keepdims=True)
    acc_ref[...] += jnp.dot(t.astype(BF16), w2_ref[0], preferred_element_type=F32)

    @pl.when(last)
    def _():
        out = x_ref[...] + mod_ref[0, 5:6, :] * acc_ref[...]
        if fg_ref is not None:
            ms = jnp.mean(out * out, axis=-1, keepdims=True)
            out = out * lax.rsqrt(ms + RMS_EPS) * fg_ref[...]
        o_ref[...] = out


def _channel_mixer(x2d, mod, g, w1, w3, w2, *, tm, tf, rows_per_mod, mod_base, n_rows=None,
                   combine=None, final_g=None):
    n = x2d.shape[0] if n_rows is None else n_rows
    n_experts, _, ff = w1.shape
    row = lambda i, e, j: (i, 0)
    const = lambda i, e, j: (0, 0)
    mod_idx = _mod_index(tm, rows_per_mod, mod_base)
    in_specs = [
        pl.BlockSpec((tm, D_MODEL), row),
        pl.BlockSpec((1, 6, D_MODEL), lambda i, e, j: mod_idx(i)),
        pl.BlockSpec((1, D_MODEL), const),
    ]
    args = [x2d, mod, g]
    if combine is not None:
        in_specs.append(pl.BlockSpec((tm, LANES), row))
        args.append(combine)
    in_specs += [
        pl.BlockSpec((1, D_MODEL, tf), lambda i, e, j: (e, 0, j)),
        pl.BlockSpec((1, D_MODEL, tf), lambda i, e, j: (e, 0, j)),
        pl.BlockSpec((1, tf, D_MODEL), lambda i, e, j: (e, j, 0)),
    ]
    args += [w1, w3, w2]
    if final_g is not None:
        in_specs.append(pl.BlockSpec((1, D_MODEL), const))
        args.append(final_g)
    return pl.pallas_call(
        functools.partial(_swiglu_kernel, n_experts=n_experts, final_norm=final_g is not None),
        grid=(n // tm, n_experts, ff // tf),
        in_specs=in_specs,
        out_specs=pl.BlockSpec((tm, D_MODEL), row),
        out_shape=jax.ShapeDtypeStruct((n, D_MODEL), F32),
        scratch_shapes=[pltpu.VMEM((tm, D_MODEL), BF16), pltpu.VMEM((tm, D_MODEL), F32)],
        compiler_params=_params("parallel", "arbitrary", "arbitrary"),
        name="channel_mixer",
    )(*args)


def _layer_params(l, w_in, w_out, ssd_conv_w, ssd_conv_b, ssd_dt_bias, ssd_a_log, ssd_d, ssd_norm_g,
                  gm_norm_g, gm_norm_b, gm_ws, gm_bs, sc_conv_w, cf_conv_w, cf_conv_b, cf_norm_g, cf_norm_b):
    wi = w_in[l]
    w_main = jnp.concatenate(
        [wi[:, _C_XBC:_C_DT], wi[:, _C_Z:_C_UV], wi[:, _C_UV:_C_SC], wi[:, _C_CF:_C_END], wi[:, _C_SC:_C_CF]],
        axis=1).astype(BF16)
    w_dt = jnp.pad(wi[:, _C_DT:_C_Z], ((0, 0), (0, LANES - 2 * SSD_HEADS))).astype(BF16)
    pad8 = lambda v: jnp.pad(v.reshape(1, -1), ((0, 0), (0, LANES - 2 * SSD_HEADS)))
    return {
        "w_main": w_main,
        "w_dt": w_dt,
        "w_out_a": w_out[l, 0:BRANCH].astype(BF16),
        "w_out_b": w_out[l, BRANCH:].astype(BF16),
        "ssd_conv_w": ssd_conv_w[l],
        "ssd_conv_b": ssd_conv_b[l].reshape(1, -1),
        "ssd_dtp": jnp.concatenate([pad8(ssd_dt_bias[l]), pad8(ssd_a_log[l])], axis=0),
        "ssd_vec": jnp.stack([jnp.repeat(ssd_d[l], HEAD_DIM), ssd_norm_g[l]]),
        "gm_ws": gm_ws[l],
        "gm_bs": jnp.repeat(gm_bs[l].T, HEAD_DIM, axis=1),
        "sc_conv_w": sc_conv_w[l],
        "cf_conv_w": cf_conv_w[l],
        "local_vec": jnp.stack([gm_norm_g[l], gm_norm_b[l], cf_conv_b[l], cf_norm_g[l], cf_norm_b[l]]),
    }


def kernel(x, c, ctx, c_ctx, mod_w, mod_b, norm1_g, norm2_g, w_in, w_out, ssd_conv_w, ssd_conv_b, ssd_dt_bias, ssd_a_log, ssd_d, ssd_norm_g, gm_norm_g, gm_norm_b, gm_ws, gm_bs, sc_conv_w, cf_conv_w, cf_conv_b, cf_norm_g, cf_norm_b, ffn_w1, ffn_w3, ffn_w2, moe_router, moe_w1, moe_w3, moe_w2, final_norm_g):
    bsz, seq, _ = x.shape
    ctx_len = ctx.shape[1]
    depth = w_in.shape[0]
    n_x, n_c = bsz * seq, bsz * ctx_len
    tm = 512

    mod = _modulation(jnp.concatenate([c, c_ctx[None, :]], axis=0), mod_w, mod_b)
    x_map = dict(tm=tm, rows_per_mod=seq, mod_base=0)
    c_map = dict(tm=tm, rows_per_mod=n_c, mod_base=bsz)
    zero_state = jnp.zeros((bsz, 2, SSD_GROUPS, SSD_STATE, 2 * HEAD_DIM), F32)

    xs = x.reshape(n_x, D_MODEL)
    xc = ctx.reshape(n_c, D_MODEL)
    for l in range(depth):
        last = l == depth - 1
        lp = _layer_params(l, w_in, w_out, ssd_conv_w, ssd_conv_b, ssd_dt_bias, ssd_a_log, ssd_d, ssd_norm_g,
                           gm_norm_g, gm_norm_b, gm_ws, gm_bs, sc_conv_w, cf_conv_w, cf_conv_b, cf_norm_g,
                           cf_norm_b)
        g1 = norm1_g[l].reshape(1, D_MODEL)
        g2 = norm2_g[l].reshape(1, D_MODEL)
        zc_ssd, zc_uvcf, zc_sc, dtc = _in_projection(xc, mod[l], g1, lp["w_main"], lp["w_dt"], **c_map)
        yc_ssd, states = _ssd_mixer(zc_ssd.reshape(bsz, ctx_len, W_SSD), dtc.reshape(bsz, ctx_len, LANES),
                                    zero_state, lp)
        if not last:
            yc_loc = _local_mixers(zc_uvcf.reshape(bsz, ctx_len, W_UVCF), zc_sc.reshape(bsz, ctx_len, W_SC), lp)
            xc = _out_projection(xc, mod[l], yc_ssd.reshape(n_c, BRANCH), yc_loc.reshape(n_c, 3 * BRANCH),
                                 lp["w_out_a"], lp["w_out_b"], **c_map)
        z_ssd, z_uvcf, z_sc, dtx = _in_projection(xs, mod[l], g1, lp["w_main"], lp["w_dt"], **x_map)
        y_ssd, _ = _ssd_mixer(z_ssd.reshape(bsz, seq, W_SSD), dtx.reshape(bsz, seq, LANES), states, lp)
        y_loc = _local_mixers(z_uvcf.reshape(bsz, seq, W_UVCF), z_sc.reshape(bsz, seq, W_SC), lp)
        xs = _out_projection(xs, mod[l], y_ssd.reshape(n_x, BRANCH), y_loc.reshape(n_x, 3 * BRANCH),
                             lp["w_out_a"], lp["w_out_b"], **x_map)
        fg = final_norm_g.reshape(1, D_MODEL) if last else None
        i = l // 2
        if l % 2 == 0:
            w1, w3, w2 = (ffn_w1[i:i + 1].astype(BF16), ffn_w3[i:i + 1].astype(BF16),
                          ffn_w2[i:i + 1].astype(BF16))
            tf = w1.shape[2] // 2
            streams = [(xs, x_map)] + ([] if last else [(xc, c_map)])
            outs = [_channel_mixer(t, mod[l], g2, w1, w3, w2, tf=tf, final_g=fg if t is xs else None, **m)
                    for t, m in streams]
        else:
            w1, w3, w2 = moe_w1[i].astype(BF16), moe_w3[i].astype(BF16), moe_w2[i].astype(BF16)
            r_pad = jnp.pad(moe_router[i], ((0, 0), (0, LANES - N_EXPERTS)))
            tf = w1.shape[2] // 2
            streams = [(xs, x_map)] + ([] if last else [(xc, c_map)])
            outs = []
            for t, m in streams:
                comb = _router(t, mod[l], g2, r_pad, **m)
                outs.append(_channel_mixer(t, mod[l], g2, w1, w3, w2, tf=tf, combine=comb,
                                           final_g=fg if t is xs else None, **m))
        xs = outs[0]
        if not last:
            xc = outs[1]
    return xs.reshape(bsz, seq, D_MODEL)
```

```python
import functools

import jax
import jax.numpy as jnp
from jax import lax
from jax.experimental import pallas as pl
from jax.experimental.pallas import tpu as pltpu

F32 = jnp.float32
BF16 = jnp.bfloat16

LANES = 128
SUBLANES = 8
VMEM_LIMIT_BYTES = 56 * 1024 * 1024

D_MODEL = 1024
BRANCH = 256
HEAD_DIM = 64
SSD_HEADS = 4
SSD_GROUPS = 2
SSD_STATE = 128
CHUNK = 128
SSD_CONV_CH = BRANCH + 2 * SSD_GROUPS * SSD_STATE
N_EXPERTS = 8
TOP_K = 2
TOK_TILE = 256
ROW_BLOCK = 512
SUB_BLOCK = 256
N_SUB = ROW_BLOCK // SUB_BLOCK
GATHER_TILES = 11
ROUTER_TM = 512
COMBINE_TM = 512
COMBINE_WIN = 256
HALO = 16
RMS_EPS = 1e-6
LN_EPS = 1e-5

_C_XBC = 0
_C_DT = _C_XBC + SSD_CONV_CH
_C_Z = _C_DT + 2 * SSD_HEADS
_C_UV = _C_Z + BRANCH
_C_SC = _C_UV + 2 * BRANCH
_C_CF = _C_SC + 3 * BRANCH
_C_END = _C_CF + 2 * BRANCH
W_SSD = SSD_CONV_CH + BRANCH
W_UVCF = 4 * BRANCH
W_SC = 3 * BRANCH
W_MAIN = W_SSD + W_UVCF + W_SC


def _params(*sem):
    return pltpu.CompilerParams(dimension_semantics=sem, vmem_limit_bytes=VMEM_LIMIT_BYTES)


def _sigmoid(v):
    return 0.5 + 0.5 * jnp.tanh(0.5 * v)


def _silu(v):
    half = 0.5 * v
    return half + half * jnp.tanh(half)


def _softplus(v):
    return jnp.maximum(v, 0.0) + jnp.log1p(jnp.exp(-jnp.abs(v)))


def _gelu_tanh(v):
    return 0.5 * v * (1.0 + jnp.tanh(0.7978845608028654 * (v + 0.044715 * (v * v * v))))


def _norm_modulate(x, g, shift, scale):
    ms = jnp.mean(x * x, axis=-1, keepdims=True)
    return (x * lax.rsqrt(ms + RMS_EPS) * g) * (1.0 + scale) + shift


def _mod_kernel(c_ref, w_ref, b_ref, o_ref):
    o_ref[0, 0] = jnp.dot(_silu(c_ref[...]), w_ref[0], preferred_element_type=F32,
                          precision=lax.Precision.HIGHEST) + b_ref[0, 0]


def _modulation(c_all, mod_w, mod_b):
    depth = mod_w.shape[0]
    rows = c_all.shape[0]
    out = pl.pallas_call(
        _mod_kernel,
        grid=(depth, 6),
        in_specs=[
            pl.BlockSpec((rows, D_MODEL), lambda l, j: (0, 0)),
            pl.BlockSpec((1, D_MODEL, D_MODEL), lambda l, j: (l, 0, j)),
            pl.BlockSpec((1, 1, 1, D_MODEL), lambda l, j: (l, j, 0, 0)),
        ],
        out_specs=pl.BlockSpec((1, 1, rows, D_MODEL), lambda l, j: (l, j, 0, 0)),
        out_shape=jax.ShapeDtypeStruct((depth, 6, rows, D_MODEL), F32),
        compiler_params=_params("arbitrary", "arbitrary"),
        name="modulation",
    )(c_all, mod_w, mod_b.reshape(depth, 6, 1, D_MODEL))
    return jnp.transpose(out, (0, 2, 1, 3))


def _mod_index(tm, rows_per_mod, mod_base):
    return lambda i: (mod_base + (i * tm) // rows_per_mod, 0, 0)


def _inproj_kernel(x_ref, mod_ref, g_ref, w_ref, wdt_ref, zssd_ref, zuvcf_ref, zsc_ref, dt_ref):
    h = _norm_modulate(x_ref[...], g_ref[...], mod_ref[0, 0:1, :], mod_ref[0, 1:2, :]).astype(BF16)
    z = jnp.dot(h, w_ref[...], preferred_element_type=F32)
    zssd_ref[...] = z[:, 0:W_SSD].astype(BF16)
    zuvcf_ref[...] = z[:, W_SSD:W_SSD + W_UVCF].astype(BF16)
    zsc_ref[...] = z[:, W_SSD + W_UVCF:W_MAIN].astype(BF16)
    dt_ref[...] = jnp.dot(h, wdt_ref[...], preferred_element_type=F32)


def _in_projection(x2d, mod, g, w_main, w_dt, *, tm, rows_per_mod, mod_base):
    n = x2d.shape[0]
    row = lambda i: (i, 0)
    const = lambda i: (0, 0)
    return pl.pallas_call(
        _inproj_kernel,
        grid=(n // tm,),
        in_specs=[
            pl.BlockSpec((tm, D_MODEL), row),
            pl.BlockSpec((1, 6, D_MODEL), _mod_index(tm, rows_per_mod, mod_base)),
            pl.BlockSpec((1, D_MODEL), const),
            pl.BlockSpec((D_MODEL, W_MAIN), const),
            pl.BlockSpec((D_MODEL, LANES), const),
        ],
        out_specs=[
            pl.BlockSpec((tm, W_SSD), row),
            pl.BlockSpec((tm, W_UVCF), row),
            pl.BlockSpec((tm, W_SC), row),
            pl.BlockSpec((tm, LANES), row),
        ],
        out_shape=[
            jax.ShapeDtypeStruct((n, W_SSD), BF16),
            jax.ShapeDtypeStruct((n, W_UVCF), BF16),
            jax.ShapeDtypeStruct((n, W_SC), BF16),
            jax.ShapeDtypeStruct((n, LANES), F32),
        ],
        compiler_params=_params("parallel"),
        name="in_projection",
    )(x2d, mod, g, w_main, w_dt)


def _stage_window(stage_ref, src, i, nblocks, rows):
    r0 = pl.multiple_of(i * rows, rows)
    stage_ref[HALO:HALO + rows, :] = src(r0, rows)

    @pl.when(i > 0)
    def _():
        stage_ref[0:HALO, :] = src(pl.multiple_of(r0 - HALO, HALO), HALO)

    @pl.when(i == 0)
    def _():
        stage_ref[0:HALO, :] = jnp.zeros((HALO, stage_ref.shape[1]), F32)

    @pl.when(i < nblocks - 1)
    def _():
        stage_ref[HALO + rows:2 * HALO + rows, :] = src(pl.multiple_of(r0 + rows, HALO), HALO)

    @pl.when(i == nblocks - 1)
    def _():
        stage_ref[HALO + rows:2 * HALO + rows, :] = jnp.zeros((HALO, stage_ref.shape[1]), F32)


def _dwconv_from_stage(stage_ref, w_ref, taps, rows, c0, width):
    acc = None
    for k in range(taps):
        lo = HALO + k - taps // 2
        term = stage_ref[lo:lo + rows, c0:c0 + width] * w_ref[k:k + 1, c0:c0 + width]
        acc = term if acc is None else acc + term
    return acc


def _dwconv_prerotated(stage_ref, rot_ref, w_ref, taps, rows):
    span = rows + 2 * HALO - SUBLANES
    for r in range(SUBLANES):
        rot_ref[r] = stage_ref[r:r + span, :]
    acc = None
    for k in range(taps):
        lo = HALO + k - taps // 2
        q8 = (lo // SUBLANES) * SUBLANES
        term = rot_ref[lo % SUBLANES, q8:q8 + rows, :] * w_ref[k:k + 1, :]
        acc = term if acc is None else acc + term
    return acc


def _ssd_kernel(z_ref, dt_ref, init_ref, convw_ref, convb_ref, dtp_ref, vec_ref,
                y_ref, fin_ref, stage_ref, xbc_ref, p_ref, dts_ref, pt_ref, dtt_ref, yacc_ref, st_ref, *, seq):
    nc = seq // CHUNK
    lane = lax.broadcasted_iota(jnp.int32, (CHUNK, LANES), 1)
    row_i = lax.broadcasted_iota(jnp.int32, (CHUNK, CHUNK), 0)
    col_i = lax.broadcasted_iota(jnp.int32, (CHUNK, CHUNK), 1)
    lower = col_i <= row_i
    upper = col_i >= row_i
    head0 = lane < HEAD_DIM

    dt_all = _softplus(dt_ref[0] + dtp_ref[0:1, :])
    adt = dt_all * (-jnp.exp(dtp_ref[1:2, :]))
    fwd_lane = (lane % (2 * SSD_HEADS)) < SSD_HEADS
    p_all = (jnp.dot(lower.astype(F32), jnp.where(fwd_lane, adt, 0.0), preferred_element_type=F32,
                     precision=lax.Precision.HIGHEST)
             + jnp.dot(upper.astype(F32), jnp.where(fwd_lane, 0.0, adt), preferred_element_type=F32,
                       precision=lax.Precision.HIGHEST))
    pt_ref[...] = p_all.T
    dtt_ref[...] = dt_all.T
    for c in range(nc):
        shift = (LANES - 2 * SSD_HEADS * c) % LANES
        p_ref[c * CHUNK:(c + 1) * CHUNK, :] = pltpu.roll(p_all, shift, axis=1) if shift else p_all
        dts_ref[c * CHUNK:(c + 1) * CHUNK, :] = pltpu.roll(dt_all, shift, axis=1) if shift else dt_all

    def prep(c, carry):
        r0 = pl.multiple_of(c * CHUNK, CHUNK)
        _stage_window(stage_ref, lambda s, n: z_ref[0, pl.ds(s, n), 0:SSD_CONV_CH].astype(F32),
                      c, nc, CHUNK)
        for c0 in range(0, SSD_CONV_CH, BRANCH):
            v = _dwconv_from_stage(stage_ref, convw_ref, 5, CHUNK, c0, BRANCH) + convb_ref[0:1, c0:c0 + BRANCH]
            xbc_ref[pl.ds(r0, CHUNK), c0:c0 + BRANCH] = _silu(v)
        return carry

    lax.fori_loop(0, nc, prep, 0)

    def col(m, j):
        return m[:, j:j + 1]

    def pair(m, j0, j1):
        return jnp.where(head0, col(m, j0), col(m, j1))

    st_ref[...] = init_ref[0, 0]

    def fwd(c, carry):
        r0 = pl.multiple_of(c * CHUNK, CHUNK)
        rows = pl.ds(r0, CHUNK)
        pm = p_ref[rows, :]
        dt = dts_ref[rows, :]
        head_rows = pl.ds(pl.multiple_of(c * 2 * SSD_HEADS, 2 * SSD_HEADS), 2 * SSD_HEADS)
        pm_t = pt_ref[head_rows, :]
        dt_t = dtt_ref[head_rows, :]
        for g in range(SSD_GROUPS):
            xg = xbc_ref[rows, g * LANES:(g + 1) * LANES]
            bg = xbc_ref[rows, BRANCH + g * SSD_STATE:BRANCH + (g + 1) * SSD_STATE].astype(BF16)
            cg = xbc_ref[rows, BRANCH + (SSD_GROUPS + g) * SSD_STATE:
                         BRANCH + (SSD_GROUPS + g + 1) * SSD_STATE].astype(BF16)
            gram = lax.dot_general(cg, bg, (((1,), (1,)), ((), ())), preferred_element_type=F32)
            s_prev = st_ref[g]
            y = jnp.dot(cg, s_prev.astype(BF16), preferred_element_type=F32)
            hf0, hf1 = 2 * g, 2 * g + 1
            hb0, hb1 = SSD_HEADS + hf0, SSD_HEADS + hf1
            y = y * jnp.exp(pair(pm, hf0, hf1))
            for hf, hb, mask in ((hf0, hb0, head0), (hf1, hb1, jnp.logical_not(head0))):
                dec_f = jnp.exp(jnp.where(lower, col(pm, hf) - pm_t[hf:hf + 1, :], -jnp.inf))
                dec_b = jnp.exp(jnp.where(upper, col(pm, hb) - pm_t[hb:hb + 1, :], -jnp.inf))
                m = gram * (dec_f * dt_t[hf:hf + 1, :] + dec_b * dt_t[hb:hb + 1, :])
                xh = jnp.where(mask, xg, 0.0).astype(BF16)
                y = y + jnp.dot(m.astype(BF16), xh, preferred_element_type=F32)
            yacc_ref[rows, g * LANES:(g + 1) * LANES] = y
            last = pm[CHUNK - 1:CHUNK, :]
            w_end = pair(dt, hf0, hf1) * jnp.exp(pair(last - pm, hf0, hf1))
            upd = lax.dot_general(bg, (xg * w_end).astype(BF16), (((0,), (0,)), ((), ())),
                                  preferred_element_type=F32)
            decay = jnp.exp(jnp.where(head0[0:1, :], last[:, hf0:hf0 + 1], last[:, hf1:hf1 + 1]))
            st_ref[g] = s_prev * decay + upd
        return carry

    lax.fori_loop(0, nc, fwd, 0)
    fin_ref[0, 0] = st_ref[...]

    st_ref[...] = init_ref[0, 1]
    d_skip = vec_ref[0:1, :]
    norm_g = vec_ref[1:2, :]

    def bwd(k, carry):
        c = nc - 1 - k
        r0 = pl.multiple_of(c * CHUNK, CHUNK)
        rows = pl.ds(r0, CHUNK)
        pm = p_ref[rows, :]
        dt = dts_ref[rows, :]
        for g in range(SSD_GROUPS):
            xg = xbc_ref[rows, g * LANES:(g + 1) * LANES]
            bg = xbc_ref[rows, BRANCH + g * SSD_STATE:BRANCH + (g + 1) * SSD_STATE].astype(BF16)
            cg = xbc_ref[rows, BRANCH + (SSD_GROUPS + g) * SSD_STATE:
                         BRANCH + (SSD_GROUPS + g + 1) * SSD_STATE].astype(BF16)
            hb0, hb1 = SSD_HEADS + 2 * g, SSD_HEADS + 2 * g + 1
            s_prev = st_ref[g]
            y = jnp.dot(cg, s_prev.astype(BF16), preferred_element_type=F32) * jnp.exp(pair(pm, hb0, hb1))
            yacc_ref[rows, g * LANES:(g + 1) * LANES] = (
                yacc_ref[rows, g * LANES:(g + 1) * LANES] + y + d_skip[:, g * LANES:(g + 1) * LANES] * xg)
            first = pm[0:1, :]
            w_end = pair(dt, hb0, hb1) * jnp.exp(pair(first - pm, hb0, hb1))
            upd = lax.dot_general(bg, (xg * w_end).astype(BF16), (((0,), (0,)), ((), ())),
                                  preferred_element_type=F32)
            decay = jnp.exp(jnp.where(head0[0:1, :], first[:, hb0:hb0 + 1], first[:, hb1:hb1 + 1]))
            st_ref[g] = s_prev * decay + upd
        gated = yacc_ref[rows, :] * _silu(z_ref[0, rows, SSD_CONV_CH:W_SSD].astype(F32))
        ms = jnp.mean(gated * gated, axis=-1, keepdims=True)
        y_ref[0, rows, :] = (gated * lax.rsqrt(ms + RMS_EPS) * norm_g).astype(BF16)
        return carry

    lax.fori_loop(0, nc, bwd, 0)
    fin_ref[0, 1] = st_ref[...]


def _ssd_mixer(z_ssd, dt, init, lp):
    bsz, seq, _ = z_ssd.shape
    nc = seq // CHUNK
    n_hd = 2 * SSD_HEADS
    assert nc * n_hd <= LANES
    dt = dt[:, :, :n_hd].reshape(bsz, nc, CHUNK, n_hd).transpose(0, 2, 1, 3).reshape(bsz, CHUNK, nc * n_hd)
    dt = jnp.pad(dt, ((0, 0), (0, 0), (0, LANES - nc * n_hd)))
    seq_map = lambda b: (b, 0, 0)
    st_map = lambda b: (b, 0, 0, 0, 0)
    const = lambda b: (0, 0)
    st_shape = (bsz, 2, SSD_GROUPS, SSD_STATE, 2 * HEAD_DIM)
    return pl.pallas_call(
        functools.partial(_ssd_kernel, seq=seq),
        grid=(bsz,),
        in_specs=[
            pl.BlockSpec((1, seq, W_SSD), seq_map),
            pl.BlockSpec((1, CHUNK, LANES), seq_map),
            pl.BlockSpec((1,) + st_shape[1:], st_map),
            pl.BlockSpec((5, SSD_CONV_CH), const),
            pl.BlockSpec((1, SSD_CONV_CH), const),
            pl.BlockSpec((2, LANES), const),
            pl.BlockSpec((2, BRANCH), const),
        ],
        out_specs=[
            pl.BlockSpec((1, seq, BRANCH), seq_map),
            pl.BlockSpec((1,) + st_shape[1:], st_map),
        ],
        out_shape=[
            jax.ShapeDtypeStruct((bsz, seq, BRANCH), BF16),
            jax.ShapeDtypeStruct(st_shape, F32),
        ],
        scratch_shapes=[
            pltpu.VMEM((CHUNK + 2 * HALO, SSD_CONV_CH), F32),
            pltpu.VMEM((seq, SSD_CONV_CH), F32),
            pltpu.VMEM((seq, LANES), F32),
            pltpu.VMEM((seq, LANES), F32),
            pltpu.VMEM((LANES, CHUNK), F32),
            pltpu.VMEM((LANES, CHUNK), F32),
            pltpu.VMEM((seq, BRANCH), F32),
            pltpu.VMEM((SSD_GROUPS, SSD_STATE, 2 * HEAD_DIM), F32),
        ],
        compiler_params=_params("parallel"),
        name="ssd_mixer",
    )(z_ssd, dt, init, lp["ssd_conv_w"], lp["ssd_conv_b"], lp["ssd_dtp"], lp["ssd_vec"])


def _local_mixers_kernel(zuvcf_ref, zsc_ref, ws_ref, bs_ref, scw_ref, cfw_ref, vec_ref,
                         y_ref, stage_ref, rot_ref, *, seq):
    nb = seq // CHUNK
    lane = lax.broadcasted_iota(jnp.int32, (CHUNK, BRANCH), 1)
    gm_g, gm_b = vec_ref[0:1, :], vec_ref[1:2, :]
    cf_b, cf_g, cf_nb = vec_ref[2:3, :], vec_ref[3:4, :], vec_ref[4:5, :]

    def layer_norm(v, g, b):
        mu = jnp.mean(v, axis=-1, keepdims=True)
        var = jnp.mean(jnp.square(v - mu), axis=-1, keepdims=True)
        return (v - mu) * lax.rsqrt(var + LN_EPS) * g + b

    def sc_src(s, n):
        zs = zsc_ref[0, pl.ds(s, n), :].astype(F32)
        return zs[:, BRANCH:2 * BRANCH] * zs[:, 2 * BRANCH:3 * BRANCH]

    def cf_src(s, n):
        zc = zuvcf_ref[0, pl.ds(s, n), 2 * BRANCH:4 * BRANCH].astype(F32)
        return zc[:, 0:BRANCH] * _sigmoid(zc[:, BRANCH:2 * BRANCH])

    def block(i, carry):
        r0 = pl.multiple_of(i * CHUNK, CHUNK)
        rows = pl.ds(r0, CHUNK)
        ge = _gelu_tanh(zuvcf_ref[0, rows, 0:2 * BRANCH].astype(F32))
        u = ge[:, 0:BRANCH]
        v = layer_norm(ge[:, BRANCH:2 * BRANCH], gm_g, gm_b)
        s = bs_ref[...]
        for h in range(BRANCH // HEAD_DIM):
            in_head = (lane >= h * HEAD_DIM) & (lane < (h + 1) * HEAD_DIM)
            s = s + jnp.dot(ws_ref[h].astype(BF16), jnp.where(in_head, v, 0.0).astype(BF16),
                            preferred_element_type=F32)
        y_ref[0, rows, 0:BRANCH] = (u * s).astype(BF16)
        _stage_window(stage_ref, sc_src, i, nb, CHUNK)
        gate = zsc_ref[0, rows, 0:BRANCH].astype(F32)
        y_ref[0, rows, BRANCH:2 * BRANCH] = (gate * _dwconv_from_stage(stage_ref, scw_ref, 3, CHUNK, 0, BRANCH)
                                             ).astype(BF16)
        _stage_window(stage_ref, cf_src, i, nb, CHUNK)
        cv = _dwconv_prerotated(stage_ref, rot_ref, cfw_ref, 31, CHUNK) + cf_b
        y_ref[0, rows, 2 * BRANCH:3 * BRANCH] = _silu(layer_norm(cv, cf_g, cf_nb)).astype(BF16)
        return carry

    lax.fori_loop(0, nb, block, 0)


def _local_mixers(z_uvcf, z_sc, lp):
    bsz, seq, _ = z_uvcf.shape
    seq_map = lambda b: (b, 0, 0)
    const2 = lambda b: (0, 0)
    return pl.pallas_call(
        functools.partial(_local_mixers_kernel, seq=seq),
        grid=(bsz,),
        in_specs=[
            pl.BlockSpec((1, seq, W_UVCF), seq_map),
            pl.BlockSpec((1, seq, W_SC), seq_map),
            pl.BlockSpec((BRANCH // HEAD_DIM, CHUNK, CHUNK), lambda b: (0, 0, 0)),
            pl.BlockSpec((CHUNK, BRANCH), const2),
            pl.BlockSpec((3, BRANCH), const2),
            pl.BlockSpec((31, BRANCH), const2),
            pl.BlockSpec((5, BRANCH), const2),
        ],
        out_specs=pl.BlockSpec((1, seq, 3 * BRANCH), seq_map),
        out_shape=jax.ShapeDtypeStruct((bsz, seq, 3 * BRANCH), BF16),
        scratch_shapes=[pltpu.VMEM((CHUNK + 2 * HALO, BRANCH), F32),
                        pltpu.VMEM((SUBLANES, CHUNK + 2 * HALO - SUBLANES, BRANCH), F32)],
        compiler_params=_params("parallel"),
        name="local_mixers",
    )(z_uvcf, z_sc, lp["gm_ws"], lp["gm_bs"], lp["sc_conv_w"], lp["cf_conv_w"], lp["local_vec"])


def _token_mixed(x_ref, mod_ref, ya_ref, yb_ref, wa_ref, wb_ref):
    proj = (jnp.dot(ya_ref[...], wa_ref[...], preferred_element_type=F32)
            + jnp.dot(yb_ref[...], wb_ref[...], preferred_element_type=F32))
    return x_ref[...] + mod_ref[0, 2:3, :] * proj


def _token_mixed_specs(tm, rows_per_mod, mod_base):
    row = lambda i, *_: (i, 0)
    const = lambda i, *_: (0, 0)
    mod_idx = _mod_index(tm, rows_per_mod, mod_base)
    return [
        pl.BlockSpec((tm, D_MODEL), row),
        pl.BlockSpec((1, 6, D_MODEL), lambda i, *_: mod_idx(i)),
        pl.BlockSpec((tm, BRANCH), row),
        pl.BlockSpec((tm, 3 * BRANCH), row),
        pl.BlockSpec((BRANCH, D_MODEL), const),
        pl.BlockSpec((3 * BRANCH, D_MODEL), const),
    ]


def _residual_out(x, gate, delta, fg_ref):
    out = x + gate * delta
    if fg_ref is not None:
        ms = jnp.mean(out * out, axis=-1, keepdims=True)
        out = out * lax.rsqrt(ms + RMS_EPS) * fg_ref[...]
    return out


def _swiglu_kernel(*refs, final_norm):
    x_ref, mod_ref, ya_ref, yb_ref, wa_ref, wb_ref, g_ref, w1_ref, w3_ref, w2_ref = refs[0:10]
    fg_ref = refs[10] if final_norm else None
    o_ref = refs[-1]
    x1 = _token_mixed(x_ref, mod_ref, ya_ref, yb_ref, wa_ref, wb_ref)
    h = _norm_modulate(x1, g_ref[...], mod_ref[0, 3:4, :], mod_ref[0, 4:5, :]).astype(BF16)
    a = jnp.dot(h, w1_ref[...], preferred_element_type=F32)
    b = jnp.dot(h, w3_ref[...], preferred_element_type=F32)
    y = jnp.dot((_silu(a) * b).astype(BF16), w2_ref[...], preferred_element_type=F32)
    o_ref[...] = _residual_out(x1, mod_ref[0, 5:6, :], y, fg_ref)


def _dense_channel_mixer(x2d, mod, y_ssd, y_local, w_a, w_b, g, w1, w3, w2, *, tm, rows_per_mod, mod_base,
                         final_g=None):
    n = x2d.shape[0]
    ff = w1.shape[1]
    const = lambda i: (0, 0)
    resident = dict(pipeline_mode=pl.Buffered(1))
    in_specs = _token_mixed_specs(tm, rows_per_mod, mod_base) + [
        pl.BlockSpec((1, D_MODEL), const),
        pl.BlockSpec((D_MODEL, ff), const, **resident),
        pl.BlockSpec((D_MODEL, ff), const, **resident),
        pl.BlockSpec((ff, D_MODEL), const, **resident),
    ]
    args = [x2d, mod, y_ssd, y_local, w_a, w_b, g, w1, w3, w2]
    if final_g is not None:
        in_specs.append(pl.BlockSpec((1, D_MODEL), const))
        args.append(final_g)
    return pl.pallas_call(
        functools.partial(_swiglu_kernel, final_norm=final_g is not None),
        grid=(n // tm,),
        in_specs=in_specs,
        out_specs=pl.BlockSpec((tm, D_MODEL), lambda i: (i, 0)),
        out_shape=jax.ShapeDtypeStruct((n, D_MODEL), F32),
        compiler_params=_params("parallel"),
        name="dense_channel_mixer",
    )(*args)


def _router_kernel(x_ref, mod_ref, ya_ref, yb_ref, wa_ref, wb_ref, g_ref, r_ref,
                   x1_ref, h_ref, comb_ref, rank_ref, rankt_ref, cnt_ref, carry_ref):
    @pl.when(pl.program_id(0) == 0)
    def _():
        carry_ref[...] = jnp.zeros_like(carry_ref)

    x1 = _token_mixed(x_ref, mod_ref, ya_ref, yb_ref, wa_ref, wb_ref)
    x1_ref[...] = x1
    h = _norm_modulate(x1, g_ref[...], mod_ref[0, 3:4, :], mod_ref[0, 4:5, :])
    h_hi = h.astype(BF16)
    h_ref[...] = h_hi
    h_lo = (h - h_hi.astype(F32)).astype(BF16)
    logits = (jnp.dot(h_hi, r_ref[0], preferred_element_type=F32)
              + jnp.dot(h_lo, r_ref[0], preferred_element_type=F32)
              + jnp.dot(h_hi, r_ref[1], preferred_element_type=F32))
    lane = lax.broadcasted_iota(jnp.int32, logits.shape, 1)
    logits = jnp.where(lane < N_EXPERTS, logits, -jnp.inf)
    m1 = jnp.max(logits, axis=-1, keepdims=True)
    i1 = jnp.min(jnp.where(logits == m1, lane, LANES), axis=-1, keepdims=True)
    rest = jnp.where(lane == i1, -jnp.inf, logits)
    m2 = jnp.max(rest, axis=-1, keepdims=True)
    i2 = jnp.min(jnp.where(rest == m2, lane, LANES), axis=-1, keepdims=True)
    e2 = jnp.exp(m2 - m1)
    denom = 1.0 + e2
    comb_ref[...] = jnp.where(lane == i1, 1.0 / denom, 0.0) + jnp.where(lane == i2, e2 / denom, 0.0)
    is_chosen = (lane == i1) | (lane == i2)
    chosen = jnp.where(is_chosen, 1.0, 0.0)
    tm = chosen.shape[0]
    earlier = lax.broadcasted_iota(jnp.int32, (tm, tm), 1) < lax.broadcasted_iota(jnp.int32, (tm, tm), 0)
    running = carry_ref[...]
    before = jnp.dot(jnp.where(earlier, 1.0, 0.0).astype(BF16), chosen.astype(BF16),
                     preferred_element_type=F32) + running.astype(F32)
    rank = jnp.where(is_chosen, before, -1.0)
    rank_ref[...] = rank
    rankt_ref[...] = rank.T[0:N_EXPERTS, :]
    chosen_i = jnp.where(is_chosen, 1, 0)
    for k in range(tm // TOK_TILE):
        running = running + jnp.sum(chosen_i[k * TOK_TILE:(k + 1) * TOK_TILE, :], axis=0, keepdims=True)
        cnt_ref[0, k:k + 1, :] = running
    carry_ref[...] = running


def _router(x2d, mod, y_ssd, y_local, w_a, w_b, g, router_pad, *, tm, rows_per_mod, mod_base):
    n = x2d.shape[0]
    row = lambda i: (i, 0)
    const = lambda i: (0, 0)
    per = tm // TOK_TILE
    return pl.pallas_call(
        _router_kernel,
        grid=(n // tm,),
        in_specs=_token_mixed_specs(tm, rows_per_mod, mod_base) + [
            pl.BlockSpec((1, D_MODEL), const),
            pl.BlockSpec((2, D_MODEL, LANES), lambda i: (0, 0, 0)),
        ],
        out_specs=[
            pl.BlockSpec((tm, D_MODEL), row),
            pl.BlockSpec((tm, D_MODEL), row),
            pl.BlockSpec((tm, LANES), row),
            pl.BlockSpec((tm, LANES), row),
            pl.BlockSpec((N_EXPERTS, tm), lambda i: (0, i)),
            pl.BlockSpec((1, per, LANES), lambda i: (i, 0, 0)),
        ],
        out_shape=[
            jax.ShapeDtypeStruct((n, D_MODEL), F32),
            jax.ShapeDtypeStruct((n, D_MODEL), BF16),
            jax.ShapeDtypeStruct((n, LANES), F32),
            jax.ShapeDtypeStruct((n, LANES), F32),
            jax.ShapeDtypeStruct((N_EXPERTS, n), F32),
            jax.ShapeDtypeStruct((n // tm, per, LANES), jnp.int32),
        ],
        scratch_shapes=[pltpu.VMEM((1, LANES), jnp.int32)],
        compiler_params=_params("arbitrary"),
        name="moe_router",
    )(x2d, mod, y_ssd, y_local, w_a, w_b, g, router_pad)


def _routing_tables(cnt, n_tokens):
    i32 = jnp.int32
    n_t = n_tokens // TOK_TILE
    cnt_end = cnt.reshape(n_t, LANES)[:, :N_EXPERTS]
    cnt_start = jnp.concatenate([jnp.zeros((1, N_EXPERTS), i32), cnt_end[:-1]], axis=0)
    tot = cnt_end[-1]
    nblk = (tot + ROW_BLOCK - 1) // ROW_BLOCK
    blk_cum = jnp.cumsum(nblk)
    blk_first = blk_cum - nblk
    n_valid = blk_cum[-1]
    nb = n_tokens * TOP_K // ROW_BLOCK + N_EXPERTS
    bi = jnp.arange(nb, dtype=i32)
    valid = bi < n_valid
    e_of = jnp.sum((bi[:, None] >= blk_cum[None, :]).astype(i32), axis=1)
    e_last = jnp.sum(((n_valid - 1) >= blk_cum).astype(i32))
    blk_e = jnp.minimum(jnp.where(valid, e_of, e_last), N_EXPERTS - 1)
    base = jnp.where(valid, (bi - blk_first[blk_e]) * ROW_BLOCK, 0)
    lo = base[:, None] + jnp.arange(N_SUB, dtype=i32)[None, :] * SUB_BLOCK
    hi = jnp.where(valid[:, None], jnp.minimum(lo + SUB_BLOCK, tot[blk_e][:, None]), lo)
    cs = cnt_start[:, blk_e]
    ce = cnt_end[:, blk_e]
    rel = (cs[:, :, None] < hi[None]) & (ce[:, :, None] > lo[None])
    tile_id = jnp.arange(n_t, dtype=i32)[:, None, None]
    t_first = jnp.min(jnp.where(rel, tile_id, n_t), axis=0)
    sub_hi = jnp.max(jnp.where(rel, tile_id, -1), axis=0)
    sub_lo = jnp.where(sub_hi >= 0, t_first, 0)
    u_lo = jnp.min(t_first, axis=1)
    u_n = jnp.maximum(jnp.max(sub_hi, axis=1) - u_lo + 1, 0)
    u_lo = jnp.where(u_n > 0, u_lo, 0)
    per = COMBINE_TM // TOK_TILE
    cs_c = cnt_start[0::per]
    cn_c = cnt_end[per - 1::per] - cs_c
    return dict(blk_e=blk_e.astype(i32), valid=valid.astype(i32), base=base.astype(i32),
                u_lo=u_lo.astype(i32), u_n=u_n.astype(i32),
                sub_lo=sub_lo.reshape(-1).astype(i32), sub_hi=sub_hi.reshape(-1).astype(i32),
                seg_rows=(blk_first * ROW_BLOCK).astype(i32),
                cs_c=cs_c.reshape(-1).astype(i32), cn_c=cn_c.reshape(-1).astype(i32), n_blocks=nb)


def _experts_kernel(be_ref, bv_ref, base_ref, ulo_ref, un_ref, slo_ref, shi_ref,
                    h_hbm, rt_hbm, w1_ref, w3_ref, w2_ref, y_ref, hbuf, rbuf, hsem, rsem):
    i = pl.program_id(0)
    slot = i % 2

    def tile_copies(s, k, t):
        tok = pl.multiple_of(t * TOK_TILE, TOK_TILE)
        return (pltpu.make_async_copy(h_hbm.at[pl.ds(tok, TOK_TILE)], hbuf.at[s, k], hsem.at[s, k]),
                pltpu.make_async_copy(rt_hbm.at[:, pl.ds(tok, TOK_TILE)], rbuf.at[s, k], rsem.at[s, k]))

    def for_block_tiles(b, s, action):
        for k in range(GATHER_TILES):
            @pl.when(k < un_ref[b])
            def _():
                for cp in tile_copies(s, k, ulo_ref[b] + k):
                    action(cp)

    @pl.when(i == 0)
    def _():
        for_block_tiles(0, 0, lambda cp: cp.start())

    for_block_tiles(i, slot, lambda cp: cp.wait())

    @pl.when(i + 1 < pl.num_programs(0))
    def _():
        for_block_tiles(i + 1, 1 - slot, lambda cp: cp.start())

    @pl.when(bv_ref[i] == 1)
    def _():
        e = be_ref[i]
        row_id = lax.broadcasted_iota(jnp.int32, (SUB_BLOCK, TOK_TILE), 0).astype(F32)
        expert_row = lax.broadcasted_iota(jnp.int32, (N_EXPERTS, TOK_TILE), 0) == e
        selected = []

        for sb in range(N_SUB):
            first_rank = (base_ref[i] + sb * SUB_BLOCK).astype(F32)

            def tile_body(t, carry):
                k = t - ulo_ref[i]
                kk = jnp.minimum(k, GATHER_TILES)

                @pl.when(k >= GATHER_TILES)
                def _():
                    tok = pl.multiple_of(t * TOK_TILE, TOK_TILE)
                    pltpu.sync_copy(h_hbm.at[pl.ds(tok, TOK_TILE)], hbuf.at[slot, GATHER_TILES])
                    pltpu.sync_copy(rt_hbm.at[:, pl.ds(tok, TOK_TILE)], rbuf.at[slot, GATHER_TILES])

                ranks = jnp.sum(jnp.where(expert_row, rbuf[slot, kk], 0.0), axis=0, keepdims=True)
                pick = jnp.where(ranks == first_rank + row_id, 1.0, 0.0).astype(BF16)
                return carry + jnp.dot(pick, hbuf[slot, kk], preferred_element_type=F32)

            selected.append(lax.fori_loop(slo_ref[N_SUB * i + sb], shi_ref[N_SUB * i + sb] + 1, tile_body,
                                          jnp.zeros((SUB_BLOCK, D_MODEL), F32)).astype(BF16))

        hg = jnp.concatenate(selected, axis=0)
        half = w1_ref.shape[2] // 2
        y = None
        for c in range(2):
            a = jnp.dot(hg, w1_ref[0, :, c * half:(c + 1) * half], preferred_element_type=F32)
            b = jnp.dot(hg, w3_ref[0, :, c * half:(c + 1) * half], preferred_element_type=F32)
            part = jnp.dot((_silu(a) * b).astype(BF16), w2_ref[0, c * half:(c + 1) * half, :],
                           preferred_element_type=F32)
            y = part if y is None else y + part
        y_ref[...] = y.astype(BF16)

    @pl.when(bv_ref[i] == 0)
    def _():
        y_ref[...] = jnp.zeros_like(y_ref)


def _moe_experts(h2, rank_t, w1, w3, w2, tb):
    nb = tb["n_blocks"]
    ff = w1.shape[2]
    w_map = lambda i, be, *_: (be[i], 0, 0)
    grid_spec = pltpu.PrefetchScalarGridSpec(
        num_scalar_prefetch=7,
        grid=(nb,),
        in_specs=[
            pl.BlockSpec(memory_space=pl.ANY),
            pl.BlockSpec(memory_space=pl.ANY),
            pl.BlockSpec((1, D_MODEL, ff), w_map, pipeline_mode=pl.Buffered(1)),
            pl.BlockSpec((1, D_MODEL, ff), w_map, pipeline_mode=pl.Buffered(1)),
            pl.BlockSpec((1, ff, D_MODEL), w_map, pipeline_mode=pl.Buffered(1)),
        ],
        out_specs=pl.BlockSpec((ROW_BLOCK, D_MODEL), lambda i, *_: (i, 0)),
        scratch_shapes=[
            pltpu.VMEM((2, GATHER_TILES + 1, TOK_TILE, D_MODEL), BF16),
            pltpu.VMEM((2, GATHER_TILES + 1, N_EXPERTS, TOK_TILE), F32),
            pltpu.SemaphoreType.DMA((2, GATHER_TILES)),
            pltpu.SemaphoreType.DMA((2, GATHER_TILES)),
        ],
    )
    return pl.pallas_call(
        _experts_kernel,
        grid_spec=grid_spec,
        out_shape=jax.ShapeDtypeStruct((nb * ROW_BLOCK, D_MODEL), BF16),
        compiler_params=_params("arbitrary"),
        name="moe_experts",
    )(tb["blk_e"], tb["valid"], tb["base"], tb["u_lo"], tb["u_n"], tb["sub_lo"], tb["sub_hi"],
      h2, rank_t, w1, w3, w2)


def _combine_kernel(*refs, final_norm, n_rows):
    cs_ref, cn_ref, seg_ref, x_ref, mod_ref, comb_ref, rank_ref, y_hbm = refs[0:8]
    fg_ref = refs[8] if final_norm else None
    o_ref, ybuf, ysem, acc_ref = refs[-4:]
    i = pl.program_id(0)
    slot = i % 2
    extra = N_EXPERTS

    def window(b, e):
        lo = seg_ref[e] + cs_ref[b * N_EXPERTS + e]
        cnt = cn_ref[b * N_EXPERTS + e]
        first = (lo // 16) * 16
        n_win = jnp.where(cnt > 0, (lo + cnt - first + COMBINE_WIN - 1) // COMBINE_WIN, 0)
        return first, n_win

    def win_start(first, j):
        return pl.multiple_of(jnp.minimum(first + j * COMBINE_WIN, n_rows - COMBINE_WIN), 16)

    def for_tile_windows(b, s, action):
        for e in range(N_EXPERTS):
            first, n_win = window(b, e)

            @pl.when(n_win > 0)
            def _():
                action(pltpu.make_async_copy(y_hbm.at[pl.ds(win_start(first, 0), COMBINE_WIN)],
                                             ybuf.at[s, e], ysem.at[s, e]))

    @pl.when(i == 0)
    def _():
        def clear(k, carry):
            ybuf[k // (N_EXPERTS + 1), k % (N_EXPERTS + 1)] = jnp.zeros((COMBINE_WIN, D_MODEL), BF16)
            return carry

        lax.fori_loop(0, 2 * (N_EXPERTS + 1), clear, 0)
        for_tile_windows(0, 0, lambda cp: cp.start())

    for_tile_windows(i, slot, lambda cp: cp.wait())

    @pl.when(i + 1 < pl.num_programs(0))
    def _():
        for_tile_windows(i + 1, 1 - slot, lambda cp: cp.start())

    col_id = lax.broadcasted_iota(jnp.int32, (COMBINE_TM, COMBINE_WIN), 1).astype(F32)

    def weights(e, first, j):
        pos_col = rank_ref[:, e:e + 1] + seg_ref[e].astype(F32)
        start = win_start(first, j).astype(F32)
        return jnp.where(pos_col == start + col_id, comb_ref[:, e:e + 1], 0.0).astype(BF16)

    firsts = [window(i, e) for e in range(N_EXPERTS)]
    pick_all = jnp.concatenate([weights(e, firsts[e][0], 0) for e in range(N_EXPERTS)], axis=1)
    y_all = ybuf[slot, 0:N_EXPERTS].reshape(N_EXPERTS * COMBINE_WIN, D_MODEL)
    acc_ref[...] = jnp.dot(pick_all, y_all, preferred_element_type=F32)

    for e in range(N_EXPERTS):
        first, n_win = firsts[e]

        def more(j, carry):
            pltpu.sync_copy(y_hbm.at[pl.ds(win_start(first, j), COMBINE_WIN)], ybuf.at[slot, extra])
            acc_ref[...] += jnp.dot(weights(e, first, j), ybuf[slot, extra], preferred_element_type=F32)
            return carry

        lax.fori_loop(1, jnp.maximum(n_win, 1), more, 0)

    o_ref[...] = _residual_out(x_ref[...], mod_ref[0, 5:6, :], acc_ref[...], fg_ref)


def _moe_combine(x2d, mod, comb, rank, y_sorted, tb, *, rows_per_mod, mod_base, final_g=None):
    n = x2d.shape[0]
    tm = COMBINE_TM
    row = lambda i, *_: (i, 0)
    const = lambda i, *_: (0, 0)
    mod_idx = _mod_index(tm, rows_per_mod, mod_base)
    in_specs = [
        pl.BlockSpec((tm, D_MODEL), row),
        pl.BlockSpec((1, 6, D_MODEL), lambda i, *_: mod_idx(i)),
        pl.BlockSpec((tm, LANES), row),
        pl.BlockSpec((tm, LANES), row),
        pl.BlockSpec(memory_space=pl.ANY),
    ]
    args = [x2d, mod, comb, rank, y_sorted]
    if final_g is not None:
        in_specs.append(pl.BlockSpec((1, D_MODEL), const))
        args.append(final_g)
    grid_spec = pltpu.PrefetchScalarGridSpec(
        num_scalar_prefetch=3,
        grid=(n // tm,),
        in_specs=in_specs,
        out_specs=pl.BlockSpec((tm, D_MODEL), row),
        scratch_shapes=[
            pltpu.VMEM((2, N_EXPERTS + 1, COMBINE_WIN, D_MODEL), BF16),
            pltpu.SemaphoreType.DMA((2, N_EXPERTS)),
            pltpu.VMEM((tm, D_MODEL), F32),
        ],
    )
    return pl.pallas_call(
        functools.partial(_combine_kernel, final_norm=final_g is not None, n_rows=y_sorted.shape[0]),
        grid_spec=grid_spec,
        out_shape=jax.ShapeDtypeStruct((n, D_MODEL), F32),
        compiler_params=_params("arbitrary"),
        name="moe_combine",
    )(tb["cs_c"], tb["cn_c"], tb["seg_rows"], *args)


def _moe_channel_mixer(x2d, mod, y_ssd, y_local, w_a, w_b, g, router_pad, w1, w3, w2, *, rows_per_mod, mod_base,
                       final_g=None):
    x1, h2, comb, rank, rank_t, cnt = _router(x2d, mod, y_ssd, y_local, w_a, w_b, g, router_pad, tm=ROUTER_TM,
                                              rows_per_mod=rows_per_mod, mod_base=mod_base)
    tb = _routing_tables(cnt, x2d.shape[0])
    y_sorted = _moe_experts(h2, rank_t, w1, w3, w2, tb)
    return _moe_combine(x1, mod, comb, rank, y_sorted, tb, rows_per_mod=rows_per_mod, mod_base=mod_base,
                        final_g=final_g)


def _layer_params(l, w_in, w_out, ssd_conv_w, ssd_conv_b, ssd_dt_bias, ssd_a_log, ssd_d, ssd_norm_g,
                  gm_norm_g, gm_norm_b, gm_ws, gm_bs, sc_conv_w, cf_conv_w, cf_conv_b, cf_norm_g, cf_norm_b):
    wi = w_in[l]
    w_main = jnp.concatenate(
        [wi[:, _C_XBC:_C_DT], wi[:, _C_Z:_C_UV], wi[:, _C_UV:_C_SC], wi[:, _C_CF:_C_END], wi[:, _C_SC:_C_CF]],
        axis=1).astype(BF16)
    w_dt = jnp.pad(wi[:, _C_DT:_C_Z], ((0, 0), (0, LANES - 2 * SSD_HEADS))).astype(BF16)
    return {
        "w_main": w_main,
        "w_dt": w_dt,
        "w_out_a": w_out[l, 0:BRANCH].astype(BF16),
        "w_out_b": w_out[l, BRANCH:].astype(BF16),
        "ssd_conv_w": ssd_conv_w[l],
        "ssd_conv_b": ssd_conv_b[l].reshape(1, -1),
        "ssd_dtp": jnp.stack([jnp.tile(ssd_dt_bias[l].reshape(-1), LANES // (2 * SSD_HEADS)),
                              jnp.tile(ssd_a_log[l].reshape(-1), LANES // (2 * SSD_HEADS))]),
        "ssd_vec": jnp.stack([jnp.repeat(ssd_d[l], HEAD_DIM), ssd_norm_g[l]]),
        "gm_ws": gm_ws[l],
        "gm_bs": jnp.repeat(gm_bs[l].T, HEAD_DIM, axis=1),
        "sc_conv_w": sc_conv_w[l],
        "cf_conv_w": cf_conv_w[l],
        "local_vec": jnp.stack([gm_norm_g[l], gm_norm_b[l], cf_conv_b[l], cf_norm_g[l], cf_norm_b[l]]),
    }


def kernel(x, c, ctx, c_ctx, mod_w, mod_b, norm1_g, norm2_g, w_in, w_out, ssd_conv_w, ssd_conv_b, ssd_dt_bias, ssd_a_log, ssd_d, ssd_norm_g, gm_norm_g, gm_norm_b, gm_ws, gm_bs, sc_conv_w, cf_conv_w, cf_conv_b, cf_norm_g, cf_norm_b, ffn_w1, ffn_w3, ffn_w2, moe_router, moe_w1, moe_w3, moe_w2, final_norm_g):
    bsz, seq, _ = x.shape
    ctx_len = ctx.shape[1]
    depth = w_in.shape[0]
    n_x, n_c = bsz * seq, bsz * ctx_len
    tm = 512

    mod = _modulation(jnp.concatenate([c, c_ctx[None, :]], axis=0), mod_w, mod_b)
    x_map = dict(tm=tm, rows_per_mod=seq, mod_base=0)
    c_map = dict(tm=tm, rows_per_mod=n_c, mod_base=bsz)
    zero_state = jnp.zeros((bsz, 2, SSD_GROUPS, SSD_STATE, 2 * HEAD_DIM), F32)

    xs = x.reshape(n_x, D_MODEL)
    xc = ctx.reshape(n_c, D_MODEL)
    for l in range(depth):
        last = l == depth - 1
        lp = _layer_params(l, w_in, w_out, ssd_conv_w, ssd_conv_b, ssd_dt_bias, ssd_a_log, ssd_d, ssd_norm_g,
                           gm_norm_g, gm_norm_b, gm_ws, gm_bs, sc_conv_w, cf_conv_w, cf_conv_b, cf_norm_g,
                           cf_norm_b)
        g1 = norm1_g[l].reshape(1, D_MODEL)
        g2 = norm2_g[l].reshape(1, D_MODEL)
        zc_ssd, zc_uvcf, zc_sc, dtc = _in_projection(xc, mod[l], g1, lp["w_main"], lp["w_dt"], **c_map)
        yc_ssd, states = _ssd_mixer(zc_ssd.reshape(bsz, ctx_len, W_SSD), dtc.reshape(bsz, ctx_len, LANES),
                                    zero_state, lp)
        z_ssd, z_uvcf, z_sc, dtx = _in_projection(xs, mod[l], g1, lp["w_main"], lp["w_dt"], **x_map)
        y_ssd, _ = _ssd_mixer(z_ssd.reshape(bsz, seq, W_SSD), dtx.reshape(bsz, seq, LANES), states, lp)
        y_loc = _local_mixers(z_uvcf.reshape(bsz, seq, W_UVCF), z_sc.reshape(bsz, seq, W_SC), lp)
        streams = [(xs, y_ssd.reshape(n_x, BRANCH), y_loc.reshape(n_x, 3 * BRANCH), x_map)]
        if not last:
            yc_loc = _local_mixers(zc_uvcf.reshape(bsz, ctx_len, W_UVCF), zc_sc.reshape(bsz, ctx_len, W_SC), lp)
            streams.append((xc, yc_ssd.reshape(n_c, BRANCH), yc_loc.reshape(n_c, 3 * BRANCH), c_map))
        fg = final_norm_g.reshape(1, D_MODEL) if last else None
        i = l // 2
        if l % 2 == 0:
            w1, w3, w2 = ffn_w1[i].astype(BF16), ffn_w3[i].astype(BF16), ffn_w2[i].astype(BF16)
            outs = [_dense_channel_mixer(t, mod[l], ya, yb, lp["w_out_a"], lp["w_out_b"], g2, w1, w3, w2,
                                         final_g=fg if t is xs else None, **m) for t, ya, yb, m in streams]
        else:
            w1, w3, w2 = moe_w1[i].astype(BF16), moe_w3[i].astype(BF16), moe_w2[i].astype(BF16)
            r_f32 = jnp.pad(moe_router[i], ((0, 0), (0, LANES - N_EXPERTS)))
            r_hi = r_f32.astype(BF16)
            r_pad = jnp.stack([r_hi, (r_f32 - r_hi.astype(F32)).astype(BF16)])
            outs = [_moe_channel_mixer(t, mod[l], ya, yb, lp["w_out_a"], lp["w_out_b"], g2, r_pad, w1, w3, w2,
                                       rows_per_mod=m["rows_per_mod"], mod_base=m["mod_base"],
                                       final_g=fg if t is xs else None) for t, ya, yb, m in streams]
        xs = outs[0]
        if not last:
            xc = outs[1]
    return xs.reshape(bsz, seq, D_MODEL)
```

```python
import functools

import jax
import jax.numpy as jnp
from jax import lax
from jax.experimental import pallas as pl
from jax.experimental.pallas import tpu as pltpu

F32 = jnp.float32
BF16 = jnp.bfloat16

LANES = 128
SUBLANES = 8
VMEM_LIMIT_BYTES = 56 * 1024 * 1024

D_MODEL = 1024
BRANCH = 256
HEAD_DIM = 64
SSD_HEADS = 4
SSD_GROUPS = 2
SSD_STATE = 128
CHUNK = 128
SSD_CONV_CH = BRANCH + 2 * SSD_GROUPS * SSD_STATE
N_EXPERTS = 8
TOP_K = 2
TOK_TILE = 256
ROW_BLOCK = 512
SUB_BLOCK = 256
N_SUB = ROW_BLOCK // SUB_BLOCK
GATHER_TILES = 11
WINDOW_TILES = 6
ROUTER_TM = 512
COMBINE_TM = 512
COMBINE_WIN = 256
HALO = 16
RMS_EPS = 1e-6
LN_EPS = 1e-5

_C_XBC = 0
_C_DT = _C_XBC + SSD_CONV_CH
_C_Z = _C_DT + 2 * SSD_HEADS
_C_UV = _C_Z + BRANCH
_C_SC = _C_UV + 2 * BRANCH
_C_CF = _C_SC + 3 * BRANCH
_C_END = _C_CF + 2 * BRANCH
W_SSD = SSD_CONV_CH + BRANCH
W_UVCF = 4 * BRANCH
W_SC = 3 * BRANCH
W_MAIN = W_SSD + W_UVCF + W_SC


def _params(*sem):
    return pltpu.CompilerParams(dimension_semantics=sem, vmem_limit_bytes=VMEM_LIMIT_BYTES)


def _sigmoid(v):
    return 0.5 + 0.5 * jnp.tanh(0.5 * v)


def _silu(v):
    half = 0.5 * v
    return half + half * jnp.tanh(half)


def _softplus(v):
    return jnp.maximum(v, 0.0) + jnp.log1p(jnp.exp(-jnp.abs(v)))


def _gelu_tanh(v):
    return 0.5 * v * (1.0 + jnp.tanh(0.7978845608028654 * (v + 0.044715 * (v * v * v))))


def _norm_modulate(x, g, shift, scale):
    ms = jnp.mean(x * x, axis=-1, keepdims=True)
    return (x * lax.rsqrt(ms + RMS_EPS) * g) * (1.0 + scale) + shift


def _mod_kernel(c_ref, w_ref, b_ref, o_ref):
    o_ref[0, 0] = jnp.dot(_silu(c_ref[...]), w_ref[0], preferred_element_type=F32,
                          precision=lax.Precision.HIGHEST) + b_ref[0, 0]


def _modulation(c_all, mod_w, mod_b):
    depth = mod_w.shape[0]
    rows = c_all.shape[0]
    out = pl.pallas_call(
        _mod_kernel,
        grid=(depth, 6),
        in_specs=[
            pl.BlockSpec((rows, D_MODEL), lambda l, j: (0, 0)),
            pl.BlockSpec((1, D_MODEL, D_MODEL), lambda l, j: (l, 0, j)),
            pl.BlockSpec((1, 1, 1, D_MODEL), lambda l, j: (l, j, 0, 0)),
        ],
        out_specs=pl.BlockSpec((1, 1, rows, D_MODEL), lambda l, j: (l, j, 0, 0)),
        out_shape=jax.ShapeDtypeStruct((depth, 6, rows, D_MODEL), F32),
        compiler_params=_params("arbitrary", "arbitrary"),
        name="modulation",
    )(c_all, mod_w, mod_b.reshape(depth, 6, 1, D_MODEL))
    return jnp.transpose(out, (0, 2, 1, 3))


def _mod_index(tm, rows_per_mod, mod_base):
    return lambda i: (mod_base + (i * tm) // rows_per_mod, 0, 0)


def _inproj_kernel(x_ref, mod_ref, g_ref, w_ref, wdt_ref, zssd_ref, zuvcf_ref, zsc_ref, dt_ref):
    h = _norm_modulate(x_ref[...], g_ref[...], mod_ref[0, 0:1, :], mod_ref[0, 1:2, :]).astype(BF16)
    z = jnp.dot(h, w_ref[...], preferred_element_type=F32)
    zssd_ref[...] = z[:, 0:W_SSD].astype(BF16)
    zuvcf_ref[...] = z[:, W_SSD:W_SSD + W_UVCF].astype(BF16)
    zsc_ref[...] = z[:, W_SSD + W_UVCF:W_MAIN].astype(BF16)
    dt_ref[...] = jnp.dot(h, wdt_ref[...], preferred_element_type=F32)


def _in_projection(x2d, mod, g, w_main, w_dt, *, tm, rows_per_mod, mod_base):
    n = x2d.shape[0]
    row = lambda i: (i, 0)
    const = lambda i: (0, 0)
    return pl.pallas_call(
        _inproj_kernel,
        grid=(n // tm,),
        in_specs=[
            pl.BlockSpec((tm, D_MODEL), row),
            pl.BlockSpec((1, 6, D_MODEL), _mod_index(tm, rows_per_mod, mod_base)),
            pl.BlockSpec((1, D_MODEL), const),
            pl.BlockSpec((D_MODEL, W_MAIN), const),
            pl.BlockSpec((D_MODEL, LANES), const),
        ],
        out_specs=[
            pl.BlockSpec((tm, W_SSD), row),
            pl.BlockSpec((tm, W_UVCF), row),
            pl.BlockSpec((tm, W_SC), row),
            pl.BlockSpec((tm, LANES), row),
        ],
        out_shape=[
            jax.ShapeDtypeStruct((n, W_SSD), BF16),
            jax.ShapeDtypeStruct((n, W_UVCF), BF16),
            jax.ShapeDtypeStruct((n, W_SC), BF16),
            jax.ShapeDtypeStruct((n, LANES), F32),
        ],
        compiler_params=_params("parallel"),
        name="in_projection",
    )(x2d, mod, g, w_main, w_dt)


def _stage_window(stage_ref, src, i, nblocks, rows):
    r0 = pl.multiple_of(i * rows, rows)
    stage_ref[HALO:HALO + rows, :] = src(r0, rows)

    @pl.when(i > 0)
    def _():
        stage_ref[0:HALO, :] = src(pl.multiple_of(r0 - HALO, HALO), HALO)

    @pl.when(i == 0)
    def _():
        stage_ref[0:HALO, :] = jnp.zeros((HALO, stage_ref.shape[1]), F32)

    @pl.when(i < nblocks - 1)
    def _():
        stage_ref[HALO + rows:2 * HALO + rows, :] = src(pl.multiple_of(r0 + rows, HALO), HALO)

    @pl.when(i == nblocks - 1)
    def _():
        stage_ref[HALO + rows:2 * HALO + rows, :] = jnp.zeros((HALO, stage_ref.shape[1]), F32)


def _dwconv_from_stage(stage_ref, w_ref, taps, rows, c0, width):
    acc = None
    for k in range(taps):
        lo = HALO + k - taps // 2
        term = stage_ref[lo:lo + rows, c0:c0 + width] * w_ref[k:k + 1, c0:c0 + width]
        acc = term if acc is None else acc + term
    return acc


def _dwconv_prerotated(stage_ref, rot_ref, w_ref, taps, rows):
    span = rows + 2 * HALO - SUBLANES
    for r in range(SUBLANES):
        rot_ref[r] = stage_ref[r:r + span, :]
    acc = None
    for k in range(taps):
        lo = HALO + k - taps // 2
        q8 = (lo // SUBLANES) * SUBLANES
        term = rot_ref[lo % SUBLANES, q8:q8 + rows, :] * w_ref[k:k + 1, :]
        acc = term if acc is None else acc + term
    return acc


def _ssd_kernel(z_ref, dt_ref, init_ref, convw_ref, convb_ref, dtp_ref, vec_ref,
                y_ref, fin_ref, stage_ref, xbc_ref, p_ref, dts_ref, pt_ref, dtt_ref, yacc_ref, st_ref, *, seq):
    nc = seq // CHUNK
    lane = lax.broadcasted_iota(jnp.int32, (CHUNK, LANES), 1)
    row_i = lax.broadcasted_iota(jnp.int32, (CHUNK, CHUNK), 0)
    col_i = lax.broadcasted_iota(jnp.int32, (CHUNK, CHUNK), 1)
    lower = col_i <= row_i
    upper = col_i >= row_i
    head0 = lane < HEAD_DIM

    dt_all = _softplus(dt_ref[0] + dtp_ref[0:1, :])
    adt = dt_all * (-jnp.exp(dtp_ref[1:2, :]))
    fwd_lane = (lane % (2 * SSD_HEADS)) < SSD_HEADS
    p_all = (jnp.dot(lower.astype(F32), jnp.where(fwd_lane, adt, 0.0), preferred_element_type=F32,
                     precision=lax.Precision.HIGHEST)
             + jnp.dot(upper.astype(F32), jnp.where(fwd_lane, 0.0, adt), preferred_element_type=F32,
                       precision=lax.Precision.HIGHEST))
    pt_ref[...] = p_all.T
    dtt_ref[...] = dt_all.T
    for c in range(nc):
        shift = (LANES - 2 * SSD_HEADS * c) % LANES
        p_ref[c * CHUNK:(c + 1) * CHUNK, :] = pltpu.roll(p_all, shift, axis=1) if shift else p_all
        dts_ref[c * CHUNK:(c + 1) * CHUNK, :] = pltpu.roll(dt_all, shift, axis=1) if shift else dt_all

    def prep(c, carry):
        r0 = pl.multiple_of(c * CHUNK, CHUNK)
        _stage_window(stage_ref, lambda s, n: z_ref[0, pl.ds(s, n), 0:SSD_CONV_CH].astype(F32),
                      c, nc, CHUNK)
        for c0 in range(0, SSD_CONV_CH, BRANCH):
            v = _dwconv_from_stage(stage_ref, convw_ref, 5, CHUNK, c0, BRANCH) + convb_ref[0:1, c0:c0 + BRANCH]
            xbc_ref[pl.ds(r0, CHUNK), c0:c0 + BRANCH] = _silu(v)
        return carry

    lax.fori_loop(0, nc, prep, 0)

    def spread(m, j):
        return jnp.broadcast_to(m[:, j:j + 1], (CHUNK, LANES))

    def pair(a0, a1):
        return jnp.where(head0, a0, a1)

    st_ref[...] = init_ref[0, 0]

    def fwd(c, carry):
        r0 = pl.multiple_of(c * CHUNK, CHUNK)
        rows = pl.ds(r0, CHUNK)
        pm = p_ref[rows, :]
        dt = dts_ref[rows, :]
        head_rows = pl.ds(pl.multiple_of(c * 2 * SSD_HEADS, 2 * SSD_HEADS), 2 * SSD_HEADS)
        pm_t = pt_ref[head_rows, :]
        dt_t = dtt_ref[head_rows, :]
        last = pm[CHUNK - 1:CHUNK, :]
        to_end = dt * jnp.exp(last - pm)
        for g in range(SSD_GROUPS):
            xg = xbc_ref[rows, g * LANES:(g + 1) * LANES]
            bg = xbc_ref[rows, BRANCH + g * SSD_STATE:BRANCH + (g + 1) * SSD_STATE].astype(BF16)
            cg = xbc_ref[rows, BRANCH + (SSD_GROUPS + g) * SSD_STATE:
                         BRANCH + (SSD_GROUPS + g + 1) * SSD_STATE].astype(BF16)
            gram = lax.dot_general(cg, bg, (((1,), (1,)), ((), ())), preferred_element_type=F32)
            s_prev = st_ref[g]
            y = jnp.dot(cg, s_prev.astype(BF16), preferred_element_type=F32)
            hf0, hf1 = 2 * g, 2 * g + 1
            hb0, hb1 = SSD_HEADS + hf0, SSD_HEADS + hf1
            p_f0, p_f1 = spread(pm, hf0), spread(pm, hf1)
            y = y * jnp.exp(pair(p_f0, p_f1))
            for p_f, hf, hb, mask in ((p_f0, hf0, hb0, head0), (p_f1, hf1, hb1, jnp.logical_not(head0))):
                dec_f = jnp.exp(jnp.where(lower, p_f - pm_t[hf:hf + 1, :], -jnp.inf))
                dec_b = jnp.exp(jnp.where(upper, spread(pm, hb) - pm_t[hb:hb + 1, :], -jnp.inf))
                m = gram * (dec_f * dt_t[hf:hf + 1, :] + dec_b * dt_t[hb:hb + 1, :])
                xh = jnp.where(mask, xg, 0.0).astype(BF16)
                y = y + jnp.dot(m.astype(BF16), xh, preferred_element_type=F32)
            yacc_ref[rows, g * LANES:(g + 1) * LANES] = y
            w_end = pair(spread(to_end, hf0), spread(to_end, hf1))
            upd = lax.dot_general(bg, (xg * w_end).astype(BF16), (((0,), (0,)), ((), ())),
                                  preferred_element_type=F32)
            decay = jnp.exp(jnp.where(head0[0:1, :], last[:, hf0:hf0 + 1], last[:, hf1:hf1 + 1]))
            st_ref[g] = s_prev * decay + upd
        return carry

    lax.fori_loop(0, nc, fwd, 0)
    fin_ref[0, 0] = st_ref[...]

    st_ref[...] = init_ref[0, 1]
    d_skip = vec_ref[0:1, :]
    norm_g = vec_ref[1:2, :]

    def bwd(k, carry):
        c = nc - 1 - k
        r0 = pl.multiple_of(c * CHUNK, CHUNK)
        rows = pl.ds(r0, CHUNK)
        pm = p_ref[rows, :]
        dt = dts_ref[rows, :]
        first = pm[0:1, :]
        to_end = dt * jnp.exp(first - pm)
        for g in range(SSD_GROUPS):
            xg = xbc_ref[rows, g * LANES:(g + 1) * LANES]
            bg = xbc_ref[rows, BRANCH + g * SSD_STATE:BRANCH + (g + 1) * SSD_STATE].astype(BF16)
            cg = xbc_ref[rows, BRANCH + (SSD_GROUPS + g) * SSD_STATE:
                         BRANCH + (SSD_GROUPS + g + 1) * SSD_STATE].astype(BF16)
            hb0, hb1 = SSD_HEADS + 2 * g, SSD_HEADS + 2 * g + 1
            s_prev = st_ref[g]
            y = (jnp.dot(cg, s_prev.astype(BF16), preferred_element_type=F32)
                 * jnp.exp(pair(spread(pm, hb0), spread(pm, hb1))))
            yacc_ref[rows, g * LANES:(g + 1) * LANES] = (
                yacc_ref[rows, g * LANES:(g + 1) * LANES] + y + d_skip[:, g * LANES:(g + 1) * LANES] * xg)
            w_end = pair(spread(to_end, hb0), spread(to_end, hb1))
            upd = lax.dot_general(bg, (xg * w_end).astype(BF16), (((0,), (0,)), ((), ())),
                                  preferred_element_type=F32)
            decay = jnp.exp(jnp.where(head0[0:1, :], first[:, hb0:hb0 + 1], first[:, hb1:hb1 + 1]))
            st_ref[g] = s_prev * decay + upd
        gated = yacc_ref[rows, :] * _silu(z_ref[0, rows, SSD_CONV_CH:W_SSD].astype(F32))
        ms = jnp.mean(gated * gated, axis=-1, keepdims=True)
        y_ref[0, rows, :] = (gated * lax.rsqrt(ms + RMS_EPS) * norm_g).astype(BF16)
        return carry

    lax.fori_loop(0, nc, bwd, 0)
    fin_ref[0, 1] = st_ref[...]


def _ssd_mixer(z_ssd, dt, init, lp):
    bsz, seq, _ = z_ssd.shape
    nc = seq // CHUNK
    n_hd = 2 * SSD_HEADS
    assert nc * n_hd <= LANES
    dt = dt[:, :, :n_hd].reshape(bsz, nc, CHUNK, n_hd).transpose(0, 2, 1, 3).reshape(bsz, CHUNK, nc * n_hd)
    dt = jnp.pad(dt, ((0, 0), (0, 0), (0, LANES - nc * n_hd)))
    seq_map = lambda b: (b, 0, 0)
    st_map = lambda b: (b, 0, 0, 0, 0)
    const = lambda b: (0, 0)
    st_shape = (bsz, 2, SSD_GROUPS, SSD_STATE, 2 * HEAD_DIM)
    return pl.pallas_call(
        functools.partial(_ssd_kernel, seq=seq),
        grid=(bsz,),
        in_specs=[
            pl.BlockSpec((1, seq, W_SSD), seq_map),
            pl.BlockSpec((1, CHUNK, LANES), seq_map),
            pl.BlockSpec((1,) + st_shape[1:], st_map),
            pl.BlockSpec((5, SSD_CONV_CH), const),
            pl.BlockSpec((1, SSD_CONV_CH), const),
            pl.BlockSpec((2, LANES), const),
            pl.BlockSpec((2, BRANCH), const),
        ],
        out_specs=[
            pl.BlockSpec((1, seq, BRANCH), seq_map),
            pl.BlockSpec((1,) + st_shape[1:], st_map),
        ],
        out_shape=[
            jax.ShapeDtypeStruct((bsz, seq, BRANCH), BF16),
            jax.ShapeDtypeStruct(st_shape, F32),
        ],
        scratch_shapes=[
            pltpu.VMEM((CHUNK + 2 * HALO, SSD_CONV_CH), F32),
            pltpu.VMEM((seq, SSD_CONV_CH), F32),
            pltpu.VMEM((seq, LANES), F32),
            pltpu.VMEM((seq, LANES), F32),
            pltpu.VMEM((LANES, CHUNK), F32),
            pltpu.VMEM((LANES, CHUNK), F32),
            pltpu.VMEM((seq, BRANCH), F32),
            pltpu.VMEM((SSD_GROUPS, SSD_STATE, 2 * HEAD_DIM), F32),
        ],
        compiler_params=_params("parallel"),
        name="ssd_mixer",
    )(z_ssd, dt, init, lp["ssd_conv_w"], lp["ssd_conv_b"], lp["ssd_dtp"], lp["ssd_vec"])


def _local_mixers_kernel(zuvcf_ref, zsc_ref, ws_ref, bs_ref, scw_ref, cfw_ref, vec_ref,
                         y_ref, stage_ref, rot_ref, *, seq):
    nb = seq // CHUNK
    lane = lax.broadcasted_iota(jnp.int32, (CHUNK, BRANCH), 1)
    gm_g, gm_b = vec_ref[0:1, :], vec_ref[1:2, :]
    cf_b, cf_g, cf_nb = vec_ref[2:3, :], vec_ref[3:4, :], vec_ref[4:5, :]

    def layer_norm(v, g, b):
        mu = jnp.mean(v, axis=-1, keepdims=True)
        var = jnp.mean(jnp.square(v - mu), axis=-1, keepdims=True)
        return (v - mu) * lax.rsqrt(var + LN_EPS) * g + b

    def sc_src(s, n):
        zs = zsc_ref[0, pl.ds(s, n), :].astype(F32)
        return zs[:, BRANCH:2 * BRANCH] * zs[:, 2 * BRANCH:3 * BRANCH]

    def cf_src(s, n):
        zc = zuvcf_ref[0, pl.ds(s, n), 2 * BRANCH:4 * BRANCH].astype(F32)
        return zc[:, 0:BRANCH] * _sigmoid(zc[:, BRANCH:2 * BRANCH])

    def block(i, carry):
        r0 = pl.multiple_of(i * CHUNK, CHUNK)
        rows = pl.ds(r0, CHUNK)
        ge = _gelu_tanh(zuvcf_ref[0, rows, 0:2 * BRANCH].astype(F32))
        u = ge[:, 0:BRANCH]
        v = layer_norm(ge[:, BRANCH:2 * BRANCH], gm_g, gm_b)
        s = bs_ref[...]
        for h in range(BRANCH // HEAD_DIM):
            in_head = (lane >= h * HEAD_DIM) & (lane < (h + 1) * HEAD_DIM)
            s = s + jnp.dot(ws_ref[h].astype(BF16), jnp.where(in_head, v, 0.0).astype(BF16),
                            preferred_element_type=F32)
        y_ref[0, rows, 0:BRANCH] = (u * s).astype(BF16)
        _stage_window(stage_ref, sc_src, i, nb, CHUNK)
        gate = zsc_ref[0, rows, 0:BRANCH].astype(F32)
        y_ref[0, rows, BRANCH:2 * BRANCH] = (gate * _dwconv_from_stage(stage_ref, scw_ref, 3, CHUNK, 0, BRANCH)
                                             ).astype(BF16)
        _stage_window(stage_ref, cf_src, i, nb, CHUNK)
        cv = _dwconv_prerotated(stage_ref, rot_ref, cfw_ref, 31, CHUNK) + cf_b
        y_ref[0, rows, 2 * BRANCH:3 * BRANCH] = _silu(layer_norm(cv, cf_g, cf_nb)).astype(BF16)
        return carry

    lax.fori_loop(0, nb, block, 0)


def _local_mixers(z_uvcf, z_sc, lp):
    bsz, seq, _ = z_uvcf.shape
    seq_map = lambda b: (b, 0, 0)
    const2 = lambda b: (0, 0)
    return pl.pallas_call(
        functools.partial(_local_mixers_kernel, seq=seq),
        grid=(bsz,),
        in_specs=[
            pl.BlockSpec((1, seq, W_UVCF), seq_map),
            pl.BlockSpec((1, seq, W_SC), seq_map),
            pl.BlockSpec((BRANCH // HEAD_DIM, CHUNK, CHUNK), lambda b: (0, 0, 0)),
            pl.BlockSpec((CHUNK, BRANCH), const2),
            pl.BlockSpec((3, BRANCH), const2),
            pl.BlockSpec((31, BRANCH), const2),
            pl.BlockSpec((5, BRANCH), const2),
        ],
        out_specs=pl.BlockSpec((1, seq, 3 * BRANCH), seq_map),
        out_shape=jax.ShapeDtypeStruct((bsz, seq, 3 * BRANCH), BF16),
        scratch_shapes=[pltpu.VMEM((CHUNK + 2 * HALO, BRANCH), F32),
                        pltpu.VMEM((SUBLANES, CHUNK + 2 * HALO - SUBLANES, BRANCH), F32)],
        compiler_params=_params("parallel"),
        name="local_mixers",
    )(z_uvcf, z_sc, lp["gm_ws"], lp["gm_bs"], lp["sc_conv_w"], lp["cf_conv_w"], lp["local_vec"])


def _token_mixed(x_ref, mod_ref, ya_ref, yb_ref, wa_ref, wb_ref):
    proj = (jnp.dot(ya_ref[...], wa_ref[...], preferred_element_type=F32)
            + jnp.dot(yb_ref[...], wb_ref[...], preferred_element_type=F32))
    return x_ref[...] + mod_ref[0, 2:3, :] * proj


def _token_mixed_specs(tm, rows_per_mod, mod_base):
    row = lambda i, *_: (i, 0)
    const = lambda i, *_: (0, 0)
    mod_idx = _mod_index(tm, rows_per_mod, mod_base)
    return [
        pl.BlockSpec((tm, D_MODEL), row),
        pl.BlockSpec((1, 6, D_MODEL), lambda i, *_: mod_idx(i)),
        pl.BlockSpec((tm, BRANCH), row),
        pl.BlockSpec((tm, 3 * BRANCH), row),
        pl.BlockSpec((BRANCH, D_MODEL), const),
        pl.BlockSpec((3 * BRANCH, D_MODEL), const),
    ]


def _residual_out(x, gate, delta, fg_ref):
    out = x + gate * delta
    if fg_ref is not None:
        ms = jnp.mean(out * out, axis=-1, keepdims=True)
        out = out * lax.rsqrt(ms + RMS_EPS) * fg_ref[...]
    return out


def _swiglu_kernel(*refs, final_norm):
    x_ref, mod_ref, ya_ref, yb_ref, wa_ref, wb_ref, g_ref, w1_ref, w3_ref, w2_ref = refs[0:10]
    fg_ref = refs[10] if final_norm else None
    o_ref = refs[-1]
    x1 = _token_mixed(x_ref, mod_ref, ya_ref, yb_ref, wa_ref, wb_ref)
    h = _norm_modulate(x1, g_ref[...], mod_ref[0, 3:4, :], mod_ref[0, 4:5, :]).astype(BF16)
    a = jnp.dot(h, w1_ref[...], preferred_element_type=F32)
    b = jnp.dot(h, w3_ref[...], preferred_element_type=F32)
    y = jnp.dot((_silu(a) * b).astype(BF16), w2_ref[...], preferred_element_type=F32)
    o_ref[...] = _residual_out(x1, mod_ref[0, 5:6, :], y, fg_ref)


def _dense_channel_mixer(x2d, mod, y_ssd, y_local, w_a, w_b, g, w1, w3, w2, *, tm, rows_per_mod, mod_base,
                         final_g=None):
    n = x2d.shape[0]
    ff = w1.shape[1]
    const = lambda i: (0, 0)
    resident = dict(pipeline_mode=pl.Buffered(1))
    in_specs = _token_mixed_specs(tm, rows_per_mod, mod_base) + [
        pl.BlockSpec((1, D_MODEL), const),
        pl.BlockSpec((D_MODEL, ff), const, **resident),
        pl.BlockSpec((D_MODEL, ff), const, **resident),
        pl.BlockSpec((ff, D_MODEL), const, **resident),
    ]
    args = [x2d, mod, y_ssd, y_local, w_a, w_b, g, w1, w3, w2]
    if final_g is not None:
        in_specs.append(pl.BlockSpec((1, D_MODEL), const))
        args.append(final_g)
    return pl.pallas_call(
        functools.partial(_swiglu_kernel, final_norm=final_g is not None),
        grid=(n // tm,),
        in_specs=in_specs,
        out_specs=pl.BlockSpec((tm, D_MODEL), lambda i: (i, 0)),
        out_shape=jax.ShapeDtypeStruct((n, D_MODEL), F32),
        compiler_params=_params("parallel"),
        name="dense_channel_mixer",
    )(*args)


def _router_kernel(x_ref, mod_ref, ya_ref, yb_ref, wa_ref, wb_ref, g_ref, r_ref,
                   x1_ref, h_ref, comb_ref, rank_ref, rankt_ref, cnt_ref, carry_ref):
    @pl.when(pl.program_id(0) == 0)
    def _():
        carry_ref[...] = jnp.zeros_like(carry_ref)

    x1 = _token_mixed(x_ref, mod_ref, ya_ref, yb_ref, wa_ref, wb_ref)
    x1_ref[...] = x1
    h = _norm_modulate(x1, g_ref[...], mod_ref[0, 3:4, :], mod_ref[0, 4:5, :])
    h_hi = h.astype(BF16)
    h_ref[...] = h_hi
    h_lo = (h - h_hi.astype(F32)).astype(BF16)
    logits = (jnp.dot(h_hi, r_ref[0], preferred_element_type=F32)
              + jnp.dot(h_lo, r_ref[0], preferred_element_type=F32)
              + jnp.dot(h_hi, r_ref[1], preferred_element_type=F32))
    lane = lax.broadcasted_iota(jnp.int32, logits.shape, 1)
    logits = jnp.where(lane < N_EXPERTS, logits, -jnp.inf)
    m1 = jnp.max(logits, axis=-1, keepdims=True)
    i1 = jnp.min(jnp.where(logits == m1, lane, LANES), axis=-1, keepdims=True)
    rest = jnp.where(lane == i1, -jnp.inf, logits)
    m2 = jnp.max(rest, axis=-1, keepdims=True)
    i2 = jnp.min(jnp.where(rest == m2, lane, LANES), axis=-1, keepdims=True)
    e2 = jnp.exp(m2 - m1)
    denom = 1.0 + e2
    comb_ref[...] = jnp.where(lane == i1, 1.0 / denom, 0.0) + jnp.where(lane == i2, e2 / denom, 0.0)
    is_chosen = (lane == i1) | (lane == i2)
    chosen = jnp.where(is_chosen, 1.0, 0.0)
    tm = chosen.shape[0]
    earlier = lax.broadcasted_iota(jnp.int32, (tm, tm), 1) < lax.broadcasted_iota(jnp.int32, (tm, tm), 0)
    running = carry_ref[...]
    before = jnp.dot(jnp.where(earlier, 1.0, 0.0).astype(BF16), chosen.astype(BF16),
                     preferred_element_type=F32) + running.astype(F32)
    rank = jnp.where(is_chosen, before, -1.0)
    rank_ref[...] = rank
    rankt_ref[...] = rank.T[0:N_EXPERTS, :]
    chosen_i = jnp.where(is_chosen, 1, 0)
    for k in range(tm // TOK_TILE):
        running = running + jnp.sum(chosen_i[k * TOK_TILE:(k + 1) * TOK_TILE, :], axis=0, keepdims=True)
        cnt_ref[0, k:k + 1, :] = running
    carry_ref[...] = running


def _router(x2d, mod, y_ssd, y_local, w_a, w_b, g, router_pad, *, tm, rows_per_mod, mod_base):
    n = x2d.shape[0]
    row = lambda i: (i, 0)
    const = lambda i: (0, 0)
    per = tm // TOK_TILE
    return pl.pallas_call(
        _router_kernel,
        grid=(n // tm,),
        in_specs=_token_mixed_specs(tm, rows_per_mod, mod_base) + [
            pl.BlockSpec((1, D_MODEL), const),
            pl.BlockSpec((2, D_MODEL, LANES), lambda i: (0, 0, 0)),
        ],
        out_specs=[
            pl.BlockSpec((tm, D_MODEL), row),
            pl.BlockSpec((tm, D_MODEL), row),
            pl.BlockSpec((tm, LANES), row),
            pl.BlockSpec((tm, LANES), row),
            pl.BlockSpec((N_EXPERTS, tm), lambda i: (0, i)),
            pl.BlockSpec((1, per, LANES), lambda i: (i, 0, 0)),
        ],
        out_shape=[
            jax.ShapeDtypeStruct((n, D_MODEL), F32),
            jax.ShapeDtypeStruct((n, D_MODEL), BF16),
            jax.ShapeDtypeStruct((n, LANES), F32),
            jax.ShapeDtypeStruct((n, LANES), F32),
            jax.ShapeDtypeStruct((N_EXPERTS, n), F32),
            jax.ShapeDtypeStruct((n // tm, per, LANES), jnp.int32),
        ],
        scratch_shapes=[pltpu.VMEM((1, LANES), jnp.int32)],
        compiler_params=_params("arbitrary"),
        name="moe_router",
    )(x2d, mod, y_ssd, y_local, w_a, w_b, g, router_pad)


def _routing_tables(cnt, n_tokens):
    i32 = jnp.int32
    n_t = n_tokens // TOK_TILE
    cnt_end = cnt.reshape(n_t, LANES)[:, :N_EXPERTS]
    cnt_start = jnp.concatenate([jnp.zeros((1, N_EXPERTS), i32), cnt_end[:-1]], axis=0)
    tot = cnt_end[-1]
    nblk = (tot + ROW_BLOCK - 1) // ROW_BLOCK
    blk_cum = jnp.cumsum(nblk)
    blk_first = blk_cum - nblk
    n_valid = blk_cum[-1]
    nb = n_tokens * TOP_K // ROW_BLOCK + N_EXPERTS
    bi = jnp.arange(nb, dtype=i32)
    valid = bi < n_valid
    e_of = jnp.sum((bi[:, None] >= blk_cum[None, :]).astype(i32), axis=1)
    e_last = jnp.sum(((n_valid - 1) >= blk_cum).astype(i32))
    blk_e = jnp.minimum(jnp.where(valid, e_of, e_last), N_EXPERTS - 1)
    base = jnp.where(valid, (bi - blk_first[blk_e]) * ROW_BLOCK, 0)
    lo = base[:, None] + jnp.arange(N_SUB, dtype=i32)[None, :] * SUB_BLOCK
    hi = jnp.where(valid[:, None], jnp.minimum(lo + SUB_BLOCK, tot[blk_e][:, None]), lo)
    cs = cnt_start[:, blk_e]
    ce = cnt_end[:, blk_e]
    rel = (cs[:, :, None] < hi[None]) & (ce[:, :, None] > lo[None])
    tile_id = jnp.arange(n_t, dtype=i32)[:, None, None]
    t_first = jnp.min(jnp.where(rel, tile_id, n_t), axis=0)
    sub_hi = jnp.max(jnp.where(rel, tile_id, -1), axis=0)
    sub_lo = jnp.where(sub_hi >= 0, t_first, 0)
    u_lo = jnp.min(t_first, axis=1)
    u_n = jnp.maximum(jnp.max(sub_hi, axis=1) - u_lo + 1, 0)
    u_lo = jnp.where(u_n > 0, u_lo, 0)
    per = COMBINE_TM // TOK_TILE
    cs_c = cnt_start[0::per]
    cn_c = cnt_end[per - 1::per] - cs_c
    return dict(blk_e=blk_e.astype(i32), valid=valid.astype(i32), base=base.astype(i32),
                u_lo=u_lo.astype(i32), u_n=u_n.astype(i32),
                sub_lo=sub_lo.reshape(-1).astype(i32), sub_hi=sub_hi.reshape(-1).astype(i32),
                seg_rows=(blk_first * ROW_BLOCK).astype(i32),
                cs_c=cs_c.reshape(-1).astype(i32), cn_c=cn_c.reshape(-1).astype(i32), n_blocks=nb)


def _experts_kernel(be_ref, bv_ref, base_ref, ulo_ref, un_ref, slo_ref, shi_ref,
                    h_hbm, rt_hbm, w1_ref, w3_ref, w2_ref, y_ref, hbuf, rbuf, hsem, rsem):
    i = pl.program_id(0)
    slot = i % 2

    def tile_copies(s, k, t):
        tok = pl.multiple_of(t * TOK_TILE, TOK_TILE)
        return (pltpu.make_async_copy(h_hbm.at[pl.ds(tok, TOK_TILE)], hbuf.at[s, k], hsem.at[s, k]),
                pltpu.make_async_copy(rt_hbm.at[:, pl.ds(tok, TOK_TILE)], rbuf.at[s, k], rsem.at[s, k]))

    def for_block_tiles(b, s, action):
        for k in range(GATHER_TILES):
            @pl.when(k < un_ref[b])
            def _():
                for cp in tile_copies(s, k, ulo_ref[b] + k):
                    action(cp)

    @pl.when(i == 0)
    def _():
        def clear(k, carry):
            hbuf[k // (GATHER_TILES + 1), k % (GATHER_TILES + 1)] = jnp.zeros((TOK_TILE, D_MODEL), BF16)
            rbuf[k // (GATHER_TILES + 1), k % (GATHER_TILES + 1)] = jnp.zeros((N_EXPERTS, TOK_TILE), F32)
            return carry

        lax.fori_loop(0, 2 * (GATHER_TILES + 1), clear, 0)
        for_block_tiles(0, 0, lambda cp: cp.start())

    for_block_tiles(i, slot, lambda cp: cp.wait())

    @pl.when(i + 1 < pl.num_programs(0))
    def _():
        for_block_tiles(i + 1, 1 - slot, lambda cp: cp.start())

    @pl.when(bv_ref[i] == 1)
    def _():
        e = be_ref[i]
        row_id = lax.broadcasted_iota(jnp.int32, (SUB_BLOCK, TOK_TILE), 0).astype(F32)
        expert_row = lax.broadcasted_iota(jnp.int32, (N_EXPERTS, TOK_TILE), 0) == e
        u_lo = ulo_ref[i]
        selected = []

        def tile_weights(ranks8, first_rank):
            ranks = jnp.sum(jnp.where(expert_row, ranks8, 0.0), axis=0, keepdims=True)
            return jnp.where(ranks == first_rank + row_id, 1.0, 0.0).astype(BF16)

        for sb in range(N_SUB):
            first_rank = (base_ref[i] + sb * SUB_BLOCK).astype(F32)
            t_lo, t_hi = slo_ref[N_SUB * i + sb], shi_ref[N_SUB * i + sb]
            k0 = jnp.clip(t_lo - u_lo, 0, GATHER_TILES + 1 - WINDOW_TILES)
            picks = []
            for j in range(WINDOW_TILES):
                t = u_lo + k0 + j
                live = (t >= t_lo) & (t <= t_hi) & (k0 + j < GATHER_TILES)
                picks.append(tile_weights(jnp.where(live, rbuf[slot, k0 + j], -1.0), first_rank))
            window = hbuf[slot, pl.ds(k0, WINDOW_TILES)].reshape(WINDOW_TILES * TOK_TILE, D_MODEL)
            rows = jnp.dot(jnp.concatenate(picks, axis=1), window, preferred_element_type=F32)

            def late_tile(t, carry):
                k = t - u_lo
                kk = jnp.minimum(k, GATHER_TILES)

                @pl.when(k >= GATHER_TILES)
                def _():
                    tok = pl.multiple_of(t * TOK_TILE, TOK_TILE)
                    pltpu.sync_copy(h_hbm.at[pl.ds(tok, TOK_TILE)], hbuf.at[slot, GATHER_TILES])
                    pltpu.sync_copy(rt_hbm.at[:, pl.ds(tok, TOK_TILE)], rbuf.at[slot, GATHER_TILES])

                return carry + jnp.dot(tile_weights(rbuf[slot, kk], first_rank), hbuf[slot, kk],
                                       preferred_element_type=F32)

            t_next = u_lo + jnp.minimum(k0 + WINDOW_TILES, GATHER_TILES)
            rows = lax.fori_loop(jnp.maximum(t_next, t_lo), t_hi + 1, late_tile, rows)
            selected.append(rows.astype(BF16))

        hg = jnp.concatenate(selected, axis=0)
        half = w1_ref.shape[2] // 2
        y = None
        for c in range(2):
            a = jnp.dot(hg, w1_ref[0, :, c * half:(c + 1) * half], preferred_element_type=F32)
            b = jnp.dot(hg, w3_ref[0, :, c * half:(c + 1) * half], preferred_element_type=F32)
            part = jnp.dot((_silu(a) * b).astype(BF16), w2_ref[0, c * half:(c + 1) * half, :],
                           preferred_element_type=F32)
            y = part if y is None else y + part
        y_ref[...] = y.astype(BF16)

    @pl.when(bv_ref[i] == 0)
    def _():
        y_ref[...] = jnp.zeros_like(y_ref)


def _moe_experts(h2, rank_t, w1, w3, w2, tb):
    nb = tb["n_blocks"]
    ff = w1.shape[2]
    w_map = lambda i, be, *_: (be[i], 0, 0)
    grid_spec = pltpu.PrefetchScalarGridSpec(
        num_scalar_prefetch=7,
        grid=(nb,),
        in_specs=[
            pl.BlockSpec(memory_space=pl.ANY),
            pl.BlockSpec(memory_space=pl.ANY),
            pl.BlockSpec((1, D_MODEL, ff), w_map, pipeline_mode=pl.Buffered(1)),
            pl.BlockSpec((1, D_MODEL, ff), w_map, pipeline_mode=pl.Buffered(1)),
            pl.BlockSpec((1, ff, D_MODEL), w_map, pipeline_mode=pl.Buffered(1)),
        ],
        out_specs=pl.BlockSpec((ROW_BLOCK, D_MODEL), lambda i, *_: (i, 0)),
        scratch_shapes=[
            pltpu.VMEM((2, GATHER_TILES + 1, TOK_TILE, D_MODEL), BF16),
            pltpu.VMEM((2, GATHER_TILES + 1, N_EXPERTS, TOK_TILE), F32),
            pltpu.SemaphoreType.DMA((2, GATHER_TILES)),
            pltpu.SemaphoreType.DMA((2, GATHER_TILES)),
        ],
    )
    return pl.pallas_call(
        _experts_kernel,
        grid_spec=grid_spec,
        out_shape=jax.ShapeDtypeStruct((nb * ROW_BLOCK, D_MODEL), BF16),
        compiler_params=_params("arbitrary"),
        name="moe_experts",
    )(tb["blk_e"], tb["valid"], tb["base"], tb["u_lo"], tb["u_n"], tb["sub_lo"], tb["sub_hi"],
      h2, rank_t, w1, w3, w2)


def _combine_kernel(*refs, final_norm, n_rows):
    cs_ref, cn_ref, seg_ref, x_ref, mod_ref, comb_ref, rank_ref, y_hbm = refs[0:8]
    fg_ref = refs[8] if final_norm else None
    o_ref, ybuf, ysem, acc_ref = refs[-4:]
    i = pl.program_id(0)
    slot = i % 2
    extra = N_EXPERTS

    def window(b, e):
        lo = seg_ref[e] + cs_ref[b * N_EXPERTS + e]
        cnt = cn_ref[b * N_EXPERTS + e]
        first = (lo // 16) * 16
        n_win = jnp.where(cnt > 0, (lo + cnt - first + COMBINE_WIN - 1) // COMBINE_WIN, 0)
        return first, n_win

    def win_start(first, j):
        return pl.multiple_of(jnp.minimum(first + j * COMBINE_WIN, n_rows - COMBINE_WIN), 16)

    def for_tile_windows(b, s, action):
        for e in range(N_EXPERTS):
            first, n_win = window(b, e)

            @pl.when(n_win > 0)
            def _():
                action(pltpu.make_async_copy(y_hbm.at[pl.ds(win_start(first, 0), COMBINE_WIN)],
                                             ybuf.at[s, e], ysem.at[s, e]))

    @pl.when(i == 0)
    def _():
        def clear(k, carry):
            ybuf[k // (N_EXPERTS + 1), k % (N_EXPERTS + 1)] = jnp.zeros((COMBINE_WIN, D_MODEL), BF16)
            return carry

        lax.fori_loop(0, 2 * (N_EXPERTS + 1), clear, 0)
        for_tile_windows(0, 0, lambda cp: cp.start())

    for_tile_windows(i, slot, lambda cp: cp.wait())

    @pl.when(i + 1 < pl.num_programs(0))
    def _():
        for_tile_windows(i + 1, 1 - slot, lambda cp: cp.start())

    col_id = lax.broadcasted_iota(jnp.int32, (COMBINE_TM, COMBINE_WIN), 1).astype(F32)

    def weights(e, first, j):
        pos_col = rank_ref[:, e:e + 1] + seg_ref[e].astype(F32)
        start = win_start(first, j).astype(F32)
        return jnp.where(pos_col == start + col_id, comb_ref[:, e:e + 1], 0.0).astype(BF16)

    firsts = [window(i, e) for e in range(N_EXPERTS)]
    pick_all = jnp.concatenate([weights(e, firsts[e][0], 0) for e in range(N_EXPERTS)], axis=1)
    y_all = ybuf[slot, 0:N_EXPERTS].reshape(N_EXPERTS * COMBINE_WIN, D_MODEL)
    acc_ref[...] = jnp.dot(pick_all, y_all, preferred_element_type=F32)

    for e in range(N_EXPERTS):
        first, n_win = firsts[e]

        def more(j, carry):
            pltpu.sync_copy(y_hbm.at[pl.ds(win_start(first, j), COMBINE_WIN)], ybuf.at[slot, extra])
            acc_ref[...] += jnp.dot(weights(e, first, j), ybuf[slot, extra], preferred_element_type=F32)
            return carry

        lax.fori_loop(1, jnp.maximum(n_win, 1), more, 0)

    o_ref[...] = _residual_out(x_ref[...], mod_ref[0, 5:6, :], acc_ref[...], fg_ref)


def _moe_combine(x2d, mod, comb, rank, y_sorted, tb, *, rows_per_mod, mod_base, final_g=None):
    n = x2d.shape[0]
    tm = COMBINE_TM
    row = lambda i, *_: (i, 0)
    const = lambda i, *_: (0, 0)
    mod_idx = _mod_index(tm, rows_per_mod, mod_base)
    in_specs = [
        pl.BlockSpec((tm, D_MODEL), row),
        pl.BlockSpec((1, 6, D_MODEL), lambda i, *_: mod_idx(i)),
        pl.BlockSpec((tm, LANES), row),
        pl.BlockSpec((tm, LANES), row),
        pl.BlockSpec(memory_space=pl.ANY),
    ]
    args = [x2d, mod, comb, rank, y_sorted]
    if final_g is not None:
        in_specs.append(pl.BlockSpec((1, D_MODEL), const))
        args.append(final_g)
    grid_spec = pltpu.PrefetchScalarGridSpec(
        num_scalar_prefetch=3,
        grid=(n // tm,),
        in_specs=in_specs,
        out_specs=pl.BlockSpec((tm, D_MODEL), row),
        scratch_shapes=[
            pltpu.VMEM((2, N_EXPERTS + 1, COMBINE_WIN, D_MODEL), BF16),
            pltpu.SemaphoreType.DMA((2, N_EXPERTS)),
            pltpu.VMEM((tm, D_MODEL), F32),
        ],
    )
    return pl.pallas_call(
        functools.partial(_combine_kernel, final_norm=final_g is not None, n_rows=y_sorted.shape[0]),
        grid_spec=grid_spec,
        out_shape=jax.ShapeDtypeStruct((n, D_MODEL), F32),
        compiler_params=_params("arbitrary"),
        name="moe_combine",
    )(tb["cs_c"], tb["cn_c"], tb["seg_rows"], *args)


def _moe_channel_mixer(x2d, mod, y_ssd, y_local, w_a, w_b, g, router_pad, w1, w3, w2, *, rows_per_mod, mod_base,
                       final_g=None):
    x1, h2, comb, rank, rank_t, cnt = _router(x2d, mod, y_ssd, y_local, w_a, w_b, g, router_pad, tm=ROUTER_TM,
                                              rows_per_mod=rows_per_mod, mod_base=mod_base)
    tb = _routing_tables(cnt, x2d.shape[0])
    y_sorted = _moe_experts(h2, rank_t, w1, w3, w2, tb)
    return _moe_combine(x1, mod, comb, rank, y_sorted, tb, rows_per_mod=rows_per_mod, mod_base=mod_base,
                        final_g=final_g)


def _layer_params(l, w_in, w_out, ssd_conv_w, ssd_conv_b, ssd_dt_bias, ssd_a_log, ssd_d, ssd_norm_g,
                  gm_norm_g, gm_norm_b, gm_ws, gm_bs, sc_conv_w, cf_conv_w, cf_conv_b, cf_norm_g, cf_norm_b):
    wi = w_in[l]
    w_main = jnp.concatenate(
        [wi[:, _C_XBC:_C_DT], wi[:, _C_Z:_C_UV], wi[:, _C_UV:_C_SC], wi[:, _C_CF:_C_END], wi[:, _C_SC:_C_CF]],
        axis=1).astype(BF16)
    w_dt = jnp.pad(wi[:, _C_DT:_C_Z], ((0, 0), (0, LANES - 2 * SSD_HEADS))).astype(BF16)
    return {
        "w_main": w_main,
        "w_dt": w_dt,
        "w_out_a": w_out[l, 0:BRANCH].astype(BF16),
        "w_out_b": w_out[l, BRANCH:].astype(BF16),
        "ssd_conv_w": ssd_conv_w[l],
        "ssd_conv_b": ssd_conv_b[l].reshape(1, -1),
        "ssd_dtp": jnp.stack([jnp.tile(ssd_dt_bias[l].reshape(-1), LANES // (2 * SSD_HEADS)),
                              jnp.tile(ssd_a_log[l].reshape(-1), LANES // (2 * SSD_HEADS))]),
        "ssd_vec": jnp.stack([jnp.repeat(ssd_d[l], HEAD_DIM), ssd_norm_g[l]]),
        "gm_ws": gm_ws[l],
        "gm_bs": jnp.repeat(gm_bs[l].T, HEAD_DIM, axis=1),
        "sc_conv_w": sc_conv_w[l],
        "cf_conv_w": cf_conv_w[l],
        "local_vec": jnp.stack([gm_norm_g[l], gm_norm_b[l], cf_conv_b[l], cf_norm_g[l], cf_norm_b[l]]),
    }


def kernel(x, c, ctx, c_ctx, mod_w, mod_b, norm1_g, norm2_g, w_in, w_out, ssd_conv_w, ssd_conv_b, ssd_dt_bias, ssd_a_log, ssd_d, ssd_norm_g, gm_norm_g, gm_norm_b, gm_ws, gm_bs, sc_conv_w, cf_conv_w, cf_conv_b, cf_norm_g, cf_norm_b, ffn_w1, ffn_w3, ffn_w2, moe_router, moe_w1, moe_w3, moe_w2, final_norm_g):
    bsz, seq, _ = x.shape
    ctx_len = ctx.shape[1]
    depth = w_in.shape[0]
    n_x, n_c = bsz * seq, bsz * ctx_len
    tm = 512

    mod = _modulation(jnp.concatenate([c, c_ctx[None, :]], axis=0), mod_w, mod_b)
    x_map = dict(tm=tm, rows_per_mod=seq, mod_base=0)
    c_map = dict(tm=tm, rows_per_mod=n_c, mod_base=bsz)
    zero_state = jnp.zeros((bsz, 2, SSD_GROUPS, SSD_STATE, 2 * HEAD_DIM), F32)

    xs = x.reshape(n_x, D_MODEL)
    xc = ctx.reshape(n_c, D_MODEL)
    for l in range(depth):
        last = l == depth - 1
        lp = _layer_params(l, w_in, w_out, ssd_conv_w, ssd_conv_b, ssd_dt_bias, ssd_a_log, ssd_d, ssd_norm_g,
                           gm_norm_g, gm_norm_b, gm_ws, gm_bs, sc_conv_w, cf_conv_w, cf_conv_b, cf_norm_g,
                           cf_norm_b)
        g1 = norm1_g[l].reshape(1, D_MODEL)
        g2 = norm2_g[l].reshape(1, D_MODEL)
        zc_ssd, zc_uvcf, zc_sc, dtc = _in_projection(xc, mod[l], g1, lp["w_main"], lp["w_dt"], **c_map)
        yc_ssd, states = _ssd_mixer(zc_ssd.reshape(bsz, ctx_len, W_SSD), dtc.reshape(bsz, ctx_len, LANES),
                                    zero_state, lp)
        z_ssd, z_uvcf, z_sc, dtx = _in_projection(xs, mod[l], g1, lp["w_main"], lp["w_dt"], **x_map)
        y_ssd, _ = _ssd_mixer(z_ssd.reshape(bsz, seq, W_SSD), dtx.reshape(bsz, seq, LANES), states, lp)
        y_loc = _local_mixers(z_uvcf.reshape(bsz, seq, W_UVCF), z_sc.reshape(bsz, seq, W_SC), lp)
        streams = [(xs, y_ssd.reshape(n_x, BRANCH), y_loc.reshape(n_x, 3 * BRANCH), x_map)]
        if not last:
            yc_loc = _local_mixers(zc_uvcf.reshape(bsz, ctx_len, W_UVCF), zc_sc.reshape(bsz, ctx_len, W_SC), lp)
            streams.append((xc, yc_ssd.reshape(n_c, BRANCH), yc_loc.reshape(n_c, 3 * BRANCH), c_map))
        fg = final_norm_g.reshape(1, D_MODEL) if last else None
        i = l // 2
        if l % 2 == 0:
            w1, w3, w2 = ffn_w1[i].astype(BF16), ffn_w3[i].astype(BF16), ffn_w2[i].astype(BF16)
            outs = [_dense_channel_mixer(t, mod[l], ya, yb, lp["w_out_a"], lp["w_out_b"], g2, w1, w3, w2,
                                         final_g=fg if t is xs else None, **m) for t, ya, yb, m in streams]
        else:
            w1, w3, w2 = moe_w1[i].astype(BF16), moe_w3[i].astype(BF16), moe_w2[i].astype(BF16)
            r_f32 = jnp.pad(moe_router[i], ((0, 0), (0, LANES - N_EXPERTS)))
            r_hi = r_f32.astype(BF16)
            r_pad = jnp.stack([r_hi, (r_f32 - r_hi.astype(F32)).astype(BF16)])
            outs = [_moe_channel_mixer(t, mod[l], ya, yb, lp["w_out_a"], lp["w_out_b"], g2, r_pad, w1, w3, w2,
                                       rows_per_mod=m["rows_per_mod"], mod_base=m["mod_base"],
                                       final_g=fg if t is xs else None) for t, ya, yb, m in streams]
        xs = outs[0]
        if not last:
            xc = outs[1]
    return xs.reshape(bsz, seq, D_MODEL)
```

```python
import functools

import jax
import jax.numpy as jnp
from jax import lax
from jax.experimental import pallas as pl
from jax.experimental.pallas import tpu as pltpu

F32 = jnp.float32
BF16 = jnp.bfloat16

LANES = 128
SUBLANES = 8
VMEM_LIMIT_BYTES = 56 * 1024 * 1024

D_MODEL = 1024
BRANCH = 256
HEAD_DIM = 64
SSD_HEADS = 4
SSD_GROUPS = 2
SSD_STATE = 128
CHUNK = 128
SSD_CONV_CH = BRANCH + 2 * SSD_GROUPS * SSD_STATE
N_EXPERTS = 8
TOP_K = 2
TOK_TILE = 256
ROW_BLOCK = 512
SUB_BLOCK = 256
N_SUB = ROW_BLOCK // SUB_BLOCK
GATHER_TILES = 11
WINDOW_TILES = 6
ROUTER_TM = 512
COMBINE_TM = 512
COMBINE_WIN = 256
HALO = 16
RMS_EPS = 1e-6
LN_EPS = 1e-5

_C_XBC = 0
_C_DT = _C_XBC + SSD_CONV_CH
_C_Z = _C_DT + 2 * SSD_HEADS
_C_UV = _C_Z + BRANCH
_C_SC = _C_UV + 2 * BRANCH
_C_CF = _C_SC + 3 * BRANCH
_C_END = _C_CF + 2 * BRANCH
W_SSD = SSD_CONV_CH + BRANCH
W_UVCF = 4 * BRANCH
W_SC = 3 * BRANCH
W_MAIN = W_SSD + W_UVCF + W_SC


def _params(*sem):
    return pltpu.CompilerParams(dimension_semantics=sem, vmem_limit_bytes=VMEM_LIMIT_BYTES)


def _sigmoid(v):
    return 0.5 + 0.5 * jnp.tanh(0.5 * v)


def _silu(v):
    half = 0.5 * v
    return half + half * jnp.tanh(half)


def _softplus(v):
    return jnp.maximum(v, 0.0) + jnp.log1p(jnp.exp(-jnp.abs(v)))


def _gelu_tanh(v):
    return 0.5 * v * (1.0 + jnp.tanh(0.7978845608028654 * (v + 0.044715 * (v * v * v))))


def _norm_modulate(x, g, shift, scale):
    ms = jnp.mean(x * x, axis=-1, keepdims=True)
    return (x * lax.rsqrt(ms + RMS_EPS) * g) * (1.0 + scale) + shift


def _mod_kernel(c_ref, w_ref, b_ref, o_ref):
    o_ref[0, 0] = jnp.dot(_silu(c_ref[...]), w_ref[0], preferred_element_type=F32,
                          precision=lax.Precision.HIGHEST) + b_ref[0, 0]


def _modulation(c_all, mod_w, mod_b):
    depth = mod_w.shape[0]
    rows = c_all.shape[0]
    out = pl.pallas_call(
        _mod_kernel,
        grid=(depth, 6),
        in_specs=[
            pl.BlockSpec((rows, D_MODEL), lambda l, j: (0, 0)),
            pl.BlockSpec((1, D_MODEL, D_MODEL), lambda l, j: (l, 0, j)),
            pl.BlockSpec((1, 1, 1, D_MODEL), lambda l, j: (l, j, 0, 0)),
        ],
        out_specs=pl.BlockSpec((1, 1, rows, D_MODEL), lambda l, j: (l, j, 0, 0)),
        out_shape=jax.ShapeDtypeStruct((depth, 6, rows, D_MODEL), F32),
        compiler_params=_params("arbitrary", "arbitrary"),
        name="modulation",
    )(c_all, mod_w, mod_b.reshape(depth, 6, 1, D_MODEL))
    return jnp.transpose(out, (0, 2, 1, 3))


def _mod_index(tm, rows_per_mod, mod_base):
    return lambda i: (mod_base + (i * tm) // rows_per_mod, 0, 0)


def _fill_stage(stage_ref, prev, main, nxt):
    rows = main.shape[0]
    stage_ref[0:HALO, :] = prev
    stage_ref[HALO:HALO + rows, :] = main
    stage_ref[HALO + rows:2 * HALO + rows, :] = nxt


def _dwconv_rows(stage_ref, w_ref, taps, r, rows, c0, width):
    acc = None
    for k in range(taps):
        lo = HALO + r + k - taps // 2
        term = stage_ref[lo:lo + rows, c0:c0 + width] * w_ref[k:k + 1, c0:c0 + width]
        acc = term if acc is None else acc + term
    return acc


def _front_kernel(x_ref, xp_ref, xn_ref, mod_ref, g_ref, w_ref, wdt_ref, convw_ref, convb_ref,
                  ws_ref, bs_ref, scw_ref, cfw_ref, vec_ref,
                  zssd_ref, yloc_ref, dt_ref, ssd_stage, sc_stage, cf_stage, rot_ref, *, tm, seq):
    i = pl.program_id(0)
    tiles_per_seq = seq // tm
    keep_prev = jnp.where(i % tiles_per_seq == 0, 0.0, 1.0)
    keep_next = jnp.where(i % tiles_per_seq == tiles_per_seq - 1, 0.0, 1.0)
    g, shift, scale = g_ref[...], mod_ref[0, 0:1, :], mod_ref[0, 1:2, :]
    h = _norm_modulate(x_ref[...], g, shift, scale).astype(BF16)
    h_halo = _norm_modulate(jnp.concatenate([xp_ref[...], xn_ref[...]], axis=0), g, shift, scale).astype(BF16)
    z = jnp.dot(h, w_ref[...], preferred_element_type=F32)
    z_halo = jnp.dot(h_halo, w_ref[...], preferred_element_type=F32)
    zp = z_halo[0:HALO] * keep_prev
    zn = z_halo[HALO:2 * HALO] * keep_next
    dt_ref[...] = jnp.dot(h, wdt_ref[...], preferred_element_type=F32)

    _fill_stage(ssd_stage, zp[:, 0:SSD_CONV_CH], z[:, 0:SSD_CONV_CH], zn[:, 0:SSD_CONV_CH])
    for r in range(0, tm, CHUNK):
        for c0 in range(0, SSD_CONV_CH, BRANCH):
            v = _dwconv_rows(ssd_stage, convw_ref, 5, r, CHUNK, c0, BRANCH) + convb_ref[0:1, c0:c0 + BRANCH]
            zssd_ref[r:r + CHUNK, c0:c0 + BRANCH] = _silu(v).astype(BF16)
    zssd_ref[:, SSD_CONV_CH:W_SSD] = z[:, SSD_CONV_CH:W_SSD].astype(BF16)

    lane = lax.broadcasted_iota(jnp.int32, (CHUNK, BRANCH), 1)
    gm_g, gm_b = vec_ref[0:1, :], vec_ref[1:2, :]
    cf_b, cf_g, cf_nb = vec_ref[2:3, :], vec_ref[3:4, :], vec_ref[4:5, :]
    c_uv, c_cf, c_sc = W_SSD, W_SSD + 2 * BRANCH, W_SSD + W_UVCF

    def layer_norm(v, gain, bias):
        mu = jnp.mean(v, axis=-1, keepdims=True)
        var = jnp.mean(jnp.square(v - mu), axis=-1, keepdims=True)
        return (v - mu) * lax.rsqrt(var + LN_EPS) * gain + bias

    def sc_in(t):
        return t[:, c_sc + BRANCH:c_sc + 2 * BRANCH] * t[:, c_sc + 2 * BRANCH:c_sc + 3 * BRANCH]

    def cf_in(t):
        return t[:, c_cf:c_cf + BRANCH] * _sigmoid(t[:, c_cf + BRANCH:c_cf + 2 * BRANCH])

    _fill_stage(sc_stage, sc_in(zp), sc_in(z), sc_in(zn))
    _fill_stage(cf_stage, cf_in(zp), cf_in(z), cf_in(zn))
    span = tm + 2 * HALO - SUBLANES
    for phase in range(SUBLANES):
        rot_ref[phase] = cf_stage[phase:phase + span, :]

    for r in range(0, tm, CHUNK):
        ge = _gelu_tanh(z[r:r + CHUNK, c_uv:c_uv + 2 * BRANCH])
        u = ge[:, 0:BRANCH]
        v = layer_norm(ge[:, BRANCH:2 * BRANCH], gm_g, gm_b)
        s = bs_ref[...]
        for hd in range(BRANCH // HEAD_DIM):
            in_head = (lane >= hd * HEAD_DIM) & (lane < (hd + 1) * HEAD_DIM)
            s = s + jnp.dot(ws_ref[hd].astype(BF16), jnp.where(in_head, v, 0.0).astype(BF16),
                            preferred_element_type=F32)
        yloc_ref[r:r + CHUNK, 0:BRANCH] = (u * s).astype(BF16)
        gate = z[r:r + CHUNK, c_sc:c_sc + BRANCH]
        yloc_ref[r:r + CHUNK, BRANCH:2 * BRANCH] = (gate * _dwconv_rows(sc_stage, scw_ref, 3, r, CHUNK, 0, BRANCH)
                                                    ).astype(BF16)
        cv = cf_b
        for k in range(31):
            lo = HALO + r + k - 31 // 2
            q8 = (lo // SUBLANES) * SUBLANES
            cv = cv + rot_ref[lo % SUBLANES, q8:q8 + CHUNK, :] * cfw_ref[k:k + 1, :]
        yloc_ref[r:r + CHUNK, 2 * BRANCH:3 * BRANCH] = _silu(layer_norm(cv, cf_g, cf_nb)).astype(BF16)


def _token_mixer_front(x2d, mod, g, lp, *, tm, seq, rows_per_mod, mod_base):
    n = x2d.shape[0]
    per = tm // HALO
    row = lambda i: (i, 0)
    const = lambda i: (0, 0)
    span = tm + 2 * HALO
    return pl.pallas_call(
        functools.partial(_front_kernel, tm=tm, seq=seq),
        grid=(n // tm,),
        in_specs=[
            pl.BlockSpec((tm, D_MODEL), row),
            pl.BlockSpec((HALO, D_MODEL), lambda i: (jnp.maximum(i * per - 1, 0), 0)),
            pl.BlockSpec((HALO, D_MODEL), lambda i: (jnp.minimum((i + 1) * per, n // HALO - 1), 0)),
            pl.BlockSpec((1, 6, D_MODEL), _mod_index(tm, rows_per_mod, mod_base)),
            pl.BlockSpec((1, D_MODEL), const),
            pl.BlockSpec((D_MODEL, W_MAIN), const),
            pl.BlockSpec((D_MODEL, LANES), const),
            pl.BlockSpec((5, SSD_CONV_CH), const),
            pl.BlockSpec((1, SSD_CONV_CH), const),
            pl.BlockSpec((BRANCH // HEAD_DIM, CHUNK, CHUNK), lambda i: (0, 0, 0)),
            pl.BlockSpec((CHUNK, BRANCH), const),
            pl.BlockSpec((3, BRANCH), const),
            pl.BlockSpec((31, BRANCH), const),
            pl.BlockSpec((5, BRANCH), const),
        ],
        out_specs=[
            pl.BlockSpec((tm, W_SSD), row),
            pl.BlockSpec((tm, 3 * BRANCH), row),
            pl.BlockSpec((tm, LANES), row),
        ],
        out_shape=[
            jax.ShapeDtypeStruct((n, W_SSD), BF16),
            jax.ShapeDtypeStruct((n, 3 * BRANCH), BF16),
            jax.ShapeDtypeStruct((n, LANES), F32),
        ],
        scratch_shapes=[
            pltpu.VMEM((span, SSD_CONV_CH), F32),
            pltpu.VMEM((span, BRANCH), F32),
            pltpu.VMEM((span, BRANCH), F32),
            pltpu.VMEM((SUBLANES, span - SUBLANES, BRANCH), F32),
        ],
        compiler_params=_params("parallel"),
        name="token_mixer_front",
    )(x2d, x2d, x2d, mod, g, lp["w_main"], lp["w_dt"], lp["ssd_conv_w"], lp["ssd_conv_b"],
      lp["gm_ws"], lp["gm_bs"], lp["sc_conv_w"], lp["cf_conv_w"], lp["local_vec"])


def _ssd_kernel(z_ref, dt_ref, init_ref, dtp_ref, vec_ref,
                y_ref, fin_ref, p_ref, dts_ref, pt_ref, dtt_ref, yacc_ref, st_ref, *, seq):
    nc = seq // CHUNK
    lane = lax.broadcasted_iota(jnp.int32, (CHUNK, LANES), 1)
    row_i = lax.broadcasted_iota(jnp.int32, (CHUNK, CHUNK), 0)
    col_i = lax.broadcasted_iota(jnp.int32, (CHUNK, CHUNK), 1)
    lower = col_i <= row_i
    upper = col_i >= row_i
    head0 = lane < HEAD_DIM

    dt_all = _softplus(dt_ref[0] + dtp_ref[0:1, :])
    adt = dt_all * (-jnp.exp(dtp_ref[1:2, :]))
    fwd_lane = (lane % (2 * SSD_HEADS)) < SSD_HEADS
    p_all = (jnp.dot(lower.astype(F32), jnp.where(fwd_lane, adt, 0.0), preferred_element_type=F32,
                     precision=lax.Precision.HIGHEST)
             + jnp.dot(upper.astype(F32), jnp.where(fwd_lane, 0.0, adt), preferred_element_type=F32,
                       precision=lax.Precision.HIGHEST))
    pt_ref[...] = p_all.T
    dtt_ref[...] = dt_all.T
    for c in range(nc):
        shift = (LANES - 2 * SSD_HEADS * c) % LANES
        p_ref[c * CHUNK:(c + 1) * CHUNK, :] = pltpu.roll(p_all, shift, axis=1) if shift else p_all
        dts_ref[c * CHUNK:(c + 1) * CHUNK, :] = pltpu.roll(dt_all, shift, axis=1) if shift else dt_all

    def group_inputs(rows, g):
        xg = z_ref[0, rows, g * LANES:(g + 1) * LANES].astype(F32)
        bg = z_ref[0, rows, BRANCH + g * SSD_STATE:BRANCH + (g + 1) * SSD_STATE]
        cg = z_ref[0, rows, BRANCH + (SSD_GROUPS + g) * SSD_STATE:BRANCH + (SSD_GROUPS + g + 1) * SSD_STATE]
        return xg, bg, cg

    def spread(m, j):
        return jnp.broadcast_to(m[:, j:j + 1], (CHUNK, LANES))

    def pair(a0, a1):
        return jnp.where(head0, a0, a1)

    st_ref[...] = init_ref[0, 0]

    def fwd(c, carry):
        r0 = pl.multiple_of(c * CHUNK, CHUNK)
        rows = pl.ds(r0, CHUNK)
        pm = p_ref[rows, :]
        dt = dts_ref[rows, :]
        head_rows = pl.ds(pl.multiple_of(c * 2 * SSD_HEADS, 2 * SSD_HEADS), 2 * SSD_HEADS)
        pm_t = pt_ref[head_rows, :]
        dt_t = dtt_ref[head_rows, :]
        last = pm[CHUNK - 1:CHUNK, :]
        to_end = dt * jnp.exp(last - pm)
        for g in range(SSD_GROUPS):
            xg, bg, cg = group_inputs(rows, g)
            gram =lax.dot_general(cg, bg, (((1,), (1,)), ((), ())), preferred_element_type=F32)
            s_prev = st_ref[g]
            y = jnp.dot(cg, s_prev.astype(BF16), preferred_element_type=F32)
            hf0, hf1 = 2 * g, 2 * g + 1
            hb0, hb1 = SSD_HEADS + hf0, SSD_HEADS + hf1
            p_f0, p_f1 = spread(pm, hf0), spread(pm, hf1)
            y = y * jnp.exp(pair(p_f0, p_f1))
            for p_f, hf, hb, mask in ((p_f0, hf0, hb0, head0), (p_f1, hf1, hb1, jnp.logical_not(head0))):
                dec_f = jnp.exp(jnp.where(lower, p_f - pm_t[hf:hf + 1, :], -jnp.inf))
                dec_b = jnp.exp(jnp.where(upper, spread(pm, hb) - pm_t[hb:hb + 1, :], -jnp.inf))
                m = gram * (dec_f * dt_t[hf:hf + 1, :] + dec_b * dt_t[hb:hb + 1, :])
                xh = jnp.where(mask, xg, 0.0).astype(BF16)
                y = y + jnp.dot(m.astype(BF16), xh, preferred_element_type=F32)
            yacc_ref[rows, g * LANES:(g + 1) * LANES] = y
            w_end = pair(spread(to_end, hf0), spread(to_end, hf1))
            upd = lax.dot_general(bg, (xg * w_end).astype(BF16), (((0,), (0,)), ((), ())),
                                  preferred_element_type=F32)
            decay = jnp.exp(jnp.where(head0[0:1, :], last[:, hf0:hf0 + 1], last[:, hf1:hf1 + 1]))
            st_ref[g] = s_prev * decay + upd
        return carry

    lax.fori_loop(0, nc, fwd, 0)
    fin_ref[0, 0] = st_ref[...]

    st_ref[...] = init_ref[0, 1]
    d_skip = vec_ref[0:1, :]
    norm_g = vec_ref[1:2, :]

    def bwd(k, carry):
        c = nc - 1 - k
        r0 = pl.multiple_of(c * CHUNK, CHUNK)
        rows = pl.ds(r0, CHUNK)
        pm = p_ref[rows, :]
        dt = dts_ref[rows, :]
        first = pm[0:1, :]
        to_end = dt * jnp.exp(first - pm)
        for g in range(SSD_GROUPS):
            xg, bg, cg = group_inputs(rows, g)
            hb0, hb1 = SSD_HEADS + 2 * g, SSD_HEADS + 2 * g + 1
            s_prev = st_ref[g]
            y = (jnp.dot(cg, s_prev.astype(BF16), preferred_element_type=F32)
                 * jnp.exp(pair(spread(pm, hb0), spread(pm, hb1))))
            yacc_ref[rows, g * LANES:(g + 1) * LANES] = (
                yacc_ref[rows, g * LANES:(g + 1) * LANES] + y + d_skip[:, g * LANES:(g + 1) * LANES] * xg)
            w_end = pair(spread(to_end, hb0), spread(to_end, hb1))
            upd = lax.dot_general(bg, (xg * w_end).astype(BF16), (((0,), (0,)), ((), ())),
                                  preferred_element_type=F32)
            decay = jnp.exp(jnp.where(head0[0:1, :], first[:, hb0:hb0 + 1], first[:, hb1:hb1 + 1]))
            st_ref[g] = s_prev * decay + upd
        gated = yacc_ref[rows, :] * _silu(z_ref[0, rows, SSD_CONV_CH:W_SSD].astype(F32))
        ms = jnp.mean(gated * gated, axis=-1, keepdims=True)
        y_ref[0, rows, :] = (gated * lax.rsqrt(ms + RMS_EPS) * norm_g).astype(BF16)
        return carry

    lax.fori_loop(0, nc, bwd, 0)
    fin_ref[0, 1] = st_ref[...]


def _ssd_mixer(z_ssd, dt, init, lp):
    bsz, seq, _ = z_ssd.shape
    nc = seq // CHUNK
    n_hd = 2 * SSD_HEADS
    assert nc * n_hd <= LANES
    dt = dt[:, :, :n_hd].reshape(bsz, nc, CHUNK, n_hd).transpose(0, 2, 1, 3).reshape(bsz, CHUNK, nc * n_hd)
    dt = jnp.pad(dt, ((0, 0), (0, 0), (0, LANES - nc * n_hd)))
    seq_map = lambda b: (b, 0, 0)
    st_map = lambda b: (b, 0, 0, 0, 0)
    const = lambda b: (0, 0)
    st_shape = (bsz, 2, SSD_GROUPS, SSD_STATE, 2 * HEAD_DIM)
    return pl.pallas_call(
        functools.partial(_ssd_kernel, seq=seq),
        grid=(bsz,),
        in_specs=[
            pl.BlockSpec((1, seq, W_SSD), seq_map),
            pl.BlockSpec((1, CHUNK, LANES), seq_map),
            pl.BlockSpec((1,) + st_shape[1:], st_map),
            pl.BlockSpec((2, LANES), const),
            pl.BlockSpec((2, BRANCH), const),
        ],
        out_specs=[
            pl.BlockSpec((1, seq, BRANCH), seq_map),
            pl.BlockSpec((1,) + st_shape[1:], st_map),
        ],
        out_shape=[
            jax.ShapeDtypeStruct((bsz, seq, BRANCH), BF16),
            jax.ShapeDtypeStruct(st_shape, F32),
        ],
        scratch_shapes=[
            pltpu.VMEM((seq, LANES), F32),
            pltpu.VMEM((seq, LANES), F32),
            pltpu.VMEM((LANES, CHUNK), F32),
            pltpu.VMEM((LANES, CHUNK), F32),
            pltpu.VMEM((seq, BRANCH), F32),
            pltpu.VMEM((SSD_GROUPS, SSD_STATE, 2 * HEAD_DIM), F32),
        ],
        compiler_params=_params("parallel"),
        name="ssd_mixer",
    )(z_ssd, dt, init, lp["ssd_dtp"], lp["ssd_vec"])


def _token_mixed(x_ref, mod_ref, ya_ref, yb_ref, wa_ref, wb_ref):
    proj = (jnp.dot(ya_ref[...], wa_ref[...], preferred_element_type=F32)
            + jnp.dot(yb_ref[...], wb_ref[...], preferred_element_type=F32))
    return x_ref[...] + mod_ref[0, 2:3, :] * proj


def _token_mixed_specs(tm, rows_per_mod, mod_base):
    row = lambda i, *_: (i, 0)
    const = lambda i, *_: (0, 0)
    mod_idx = _mod_index(tm, rows_per_mod, mod_base)
    return [
        pl.BlockSpec((tm, D_MODEL), row),
        pl.BlockSpec((1, 6, D_MODEL), lambda i, *_: mod_idx(i)),
        pl.BlockSpec((tm, BRANCH), row),
        pl.BlockSpec((tm, 3 * BRANCH), row),
        pl.BlockSpec((BRANCH, D_MODEL), const),
        pl.BlockSpec((3 * BRANCH, D_MODEL), const),
    ]


def _residual_out(x, gate, delta, fg_ref):
    out = x + gate * delta
    if fg_ref is not None:
        ms = jnp.mean(out * out, axis=-1, keepdims=True)
        out = out * lax.rsqrt(ms + RMS_EPS) * fg_ref[...]
    return out


def _swiglu_kernel(*refs, final_norm):
    x_ref, mod_ref, ya_ref, yb_ref, wa_ref, wb_ref, g_ref, w1_ref, w3_ref, w2_ref = refs[0:10]
    fg_ref = refs[10] if final_norm else None
    o_ref = refs[-1]
    x1 = _token_mixed(x_ref, mod_ref, ya_ref, yb_ref, wa_ref, wb_ref)
    h = _norm_modulate(x1, g_ref[...], mod_ref[0, 3:4, :], mod_ref[0, 4:5, :]).astype(BF16)
    a = jnp.dot(h, w1_ref[...], preferred_element_type=F32)
    b = jnp.dot(h, w3_ref[...], preferred_element_type=F32)
    y = jnp.dot((_silu(a) * b).astype(BF16), w2_ref[...], preferred_element_type=F32)
    o_ref[...] = _residual_out(x1, mod_ref[0, 5:6, :], y, fg_ref)


def _dense_channel_mixer(x2d, mod, y_ssd, y_local, w_a, w_b, g, w1, w3, w2, *, tm, rows_per_mod, mod_base,
                         final_g=None):
    n = x2d.shape[0]
    ff = w1.shape[1]
    const = lambda i: (0, 0)
    resident = dict(pipeline_mode=pl.Buffered(1))
    in_specs = _token_mixed_specs(tm, rows_per_mod, mod_base) + [
        pl.BlockSpec((1, D_MODEL), const),
        pl.BlockSpec((D_MODEL, ff), const, **resident),
        pl.BlockSpec((D_MODEL, ff), const, **resident),
        pl.BlockSpec((ff, D_MODEL), const, **resident),
    ]
    args = [x2d, mod, y_ssd, y_local, w_a, w_b, g, w1, w3, w2]
    if final_g is not None:
        in_specs.append(pl.BlockSpec((1, D_MODEL), const))
        args.append(final_g)
    return pl.pallas_call(
        functools.partial(_swiglu_kernel, final_norm=final_g is not None),
        grid=(n // tm,),
        in_specs=in_specs,
        out_specs=pl.BlockSpec((tm, D_MODEL), lambda i: (i, 0)),
        out_shape=jax.ShapeDtypeStruct((n, D_MODEL), F32),
        compiler_params=_params("parallel"),
        name="dense_channel_mixer",
    )(*args)


def _router_kernel(x_ref, mod_ref, ya_ref, yb_ref, wa_ref, wb_ref, g_ref, r_ref,
                   x1_ref, h_ref, comb_ref, rank_ref, rankt_ref, cnt_ref, carry_ref):
    @pl.when(pl.program_id(0) == 0)
    def _():
        carry_ref[...] = jnp.zeros_like(carry_ref)

    x1 = _token_mixed(x_ref, mod_ref, ya_ref, yb_ref, wa_ref, wb_ref)
    x1_ref[...] = x1
    h = _norm_modulate(x1, g_ref[...], mod_ref[0, 3:4, :], mod_ref[0, 4:5, :])
    h_hi = h.astype(BF16)
    h_ref[...] = h_hi
    h_lo = (h - h_hi.astype(F32)).astype(BF16)
    logits = (jnp.dot(h_hi, r_ref[0], preferred_element_type=F32)
              + jnp.dot(h_lo, r_ref[0], preferred_element_type=F32)
              + jnp.dot(h_hi, r_ref[1], preferred_element_type=F32))
    lane = lax.broadcasted_iota(jnp.int32, logits.shape, 1)
    logits = jnp.where(lane < N_EXPERTS, logits, -jnp.inf)
    m1 = jnp.max(logits, axis=-1, keepdims=True)
    i1 = jnp.min(jnp.where(logits == m1, lane, LANES), axis=-1, keepdims=True)
    rest = jnp.where(lane == i1, -jnp.inf, logits)
    m2 = jnp.max(rest, axis=-1, keepdims=True)
    i2 = jnp.min(jnp.where(rest == m2, lane, LANES), axis=-1, keepdims=True)
    e2 = jnp.exp(m2 - m1)
    denom = 1.0 + e2
    comb_ref[...] = jnp.where(lane == i1, 1.0 / denom, 0.0) + jnp.where(lane == i2, e2 / denom, 0.0)
    is_chosen = (lane == i1) | (lane == i2)
    chosen = jnp.where(is_chosen, 1.0, 0.0)
    tm = chosen.shape[0]
    earlier = lax.broadcasted_iota(jnp.int32, (tm, tm), 1) < lax.broadcasted_iota(jnp.int32, (tm, tm), 0)
    running = carry_ref[...]
    before = jnp.dot(jnp.where(earlier, 1.0, 0.0).astype(BF16), chosen.astype(BF16),
                     preferred_element_type=F32) + running.astype(F32)
    rank = jnp.where(is_chosen, before, -1.0)
    rank_ref[...] = rank
    rankt_ref[...] = rank.T[0:N_EXPERTS, :]
    chosen_i = jnp.where(is_chosen, 1, 0)
    for k in range(tm // TOK_TILE):
        running = running + jnp.sum(chosen_i[k * TOK_TILE:(k + 1) * TOK_TILE, :], axis=0, keepdims=True)
        cnt_ref[0, k:k + 1, :] = running
    carry_ref[...] = running


def _router(x2d, mod, y_ssd, y_local, w_a, w_b, g, router_pad, *, tm, rows_per_mod, mod_base):
    n = x2d.shape[0]
    row = lambda i: (i, 0)
    const = lambda i: (0, 0)
    per = tm // TOK_TILE
    return pl.pallas_call(
        _router_kernel,
        grid=(n // tm,),
        in_specs=_token_mixed_specs(tm, rows_per_mod, mod_base) + [
            pl.BlockSpec((1, D_MODEL), const),
            pl.BlockSpec((2, D_MODEL, LANES), lambda i: (0, 0, 0)),
        ],
        out_specs=[
            pl.BlockSpec((tm, D_MODEL), row),
            pl.BlockSpec((tm, D_MODEL), row),
            pl.BlockSpec((tm, LANES), row),
            pl.BlockSpec((tm, LANES), row),
            pl.BlockSpec((N_EXPERTS, tm), lambda i: (0, i)),
            pl.BlockSpec((1, per, LANES), lambda i: (i, 0, 0)),
        ],
        out_shape=[
            jax.ShapeDtypeStruct((n, D_MODEL), F32),
            jax.ShapeDtypeStruct((n, D_MODEL), BF16),
            jax.ShapeDtypeStruct((n, LANES), F32),
            jax.ShapeDtypeStruct((n, LANES), F32),
            jax.ShapeDtypeStruct((N_EXPERTS, n), F32),
            jax.ShapeDtypeStruct((n // tm, per, LANES), jnp.int32),
        ],
        scratch_shapes=[pltpu.VMEM((1, LANES), jnp.int32)],
        compiler_params=_params("arbitrary"),
        name="moe_router",
    )(x2d, mod, y_ssd, y_local, w_a, w_b, g, router_pad)


def _routing_tables(cnt, n_tokens):
    i32 = jnp.int32
    n_t = n_tokens // TOK_TILE
    cnt_end = cnt.reshape(n_t, LANES)[:, :N_EXPERTS]
    cnt_start = jnp.concatenate([jnp.zeros((1, N_EXPERTS), i32), cnt_end[:-1]], axis=0)
    tot = cnt_end[-1]
    nblk = (tot + ROW_BLOCK - 1) // ROW_BLOCK
    blk_cum = jnp.cumsum(nblk)
    blk_first = blk_cum - nblk
    n_valid = blk_cum[-1]
    nb = n_tokens * TOP_K // ROW_BLOCK + N_EXPERTS
    bi = jnp.arange(nb, dtype=i32)
    valid = bi < n_valid
    e_of = jnp.sum((bi[:, None] >= blk_cum[None, :]).astype(i32), axis=1)
    e_last = jnp.sum(((n_valid - 1) >= blk_cum).astype(i32))
    blk_e = jnp.minimum(jnp.where(valid, e_of, e_last), N_EXPERTS - 1)
    base = jnp.where(valid, (bi - blk_first[blk_e]) * ROW_BLOCK, 0)
    lo = base[:, None] + jnp.arange(N_SUB, dtype=i32)[None, :] * SUB_BLOCK
    hi = jnp.where(valid[:, None], jnp.minimum(lo + SUB_BLOCK, tot[blk_e][:, None]), lo)
    cs = cnt_start[:, blk_e]
    ce = cnt_end[:, blk_e]
    rel = (cs[:, :, None] < hi[None]) & (ce[:, :, None] > lo[None])
    tile_id = jnp.arange(n_t, dtype=i32)[:, None, None]
    t_first = jnp.min(jnp.where(rel, tile_id, n_t), axis=0)
    sub_hi = jnp.max(jnp.where(rel, tile_id, -1), axis=0)
    sub_lo = jnp.where(sub_hi >= 0, t_first, 0)
    u_lo = jnp.min(t_first, axis=1)
    u_n = jnp.maximum(jnp.max(sub_hi, axis=1) - u_lo + 1, 0)
    u_lo = jnp.where(u_n > 0, u_lo, 0)
    per = COMBINE_TM // TOK_TILE
    cs_c = cnt_start[0::per]
    cn_c = cnt_end[per - 1::per] - cs_c
    return dict(blk_e=blk_e.astype(i32), valid=valid.astype(i32), base=base.astype(i32),
                u_lo=u_lo.astype(i32), u_n=u_n.astype(i32),
                sub_lo=sub_lo.reshape(-1).astype(i32), sub_hi=sub_hi.reshape(-1).astype(i32),
                seg_rows=(blk_first * ROW_BLOCK).astype(i32),
                cs_c=cs_c.reshape(-1).astype(i32), cn_c=cn_c.reshape(-1).astype(i32), n_blocks=nb)


def _experts_kernel(be_ref, bv_ref, base_ref, ulo_ref, un_ref, slo_ref, shi_ref,
                    h_hbm, rt_hbm, w1_ref, w3_ref, w2_ref, y_ref, hbuf, rbuf, hsem, rsem):
    i = pl.program_id(0)
    slot = i % 2

    def tile_copies(s, k, t):
        tok = pl.multiple_of(t * TOK_TILE, TOK_TILE)
        return (pltpu.make_async_copy(h_hbm.at[pl.ds(tok, TOK_TILE)], hbuf.at[s, k], hsem.at[s, k]),
                pltpu.make_async_copy(rt_hbm.at[:, pl.ds(tok, TOK_TILE)], rbuf.at[s, k], rsem.at[s, k]))

    def for_block_tiles(b, s, action):
        for k in range(GATHER_TILES):
            @pl.when(k < un_ref[b])
            def _():
                for cp in tile_copies(s, k, ulo_ref[b] + k):
                    action(cp)

    @pl.when(i == 0)
    def _():
        def clear(k, carry):
            hbuf[k // (GATHER_TILES + 1), k % (GATHER_TILES + 1)] = jnp.zeros((TOK_TILE, D_MODEL), BF16)
            rbuf[k // (GATHER_TILES + 1), k % (GATHER_TILES + 1)] = jnp.zeros((N_EXPERTS, TOK_TILE), F32)
            return carry

        lax.fori_loop(0, 2 * (GATHER_TILES + 1), clear, 0)
        for_block_tiles(0, 0, lambda cp: cp.start())

    for_block_tiles(i, slot, lambda cp: cp.wait())

    @pl.when(i + 1 < pl.num_programs(0))
    def _():
        for_block_tiles(i + 1, 1 - slot, lambda cp: cp.start())

    @pl.when(bv_ref[i] == 1)
    def _():
        e = be_ref[i]
        row_id = lax.broadcasted_iota(jnp.int32, (SUB_BLOCK, TOK_TILE), 0).astype(F32)
        expert_row = lax.broadcasted_iota(jnp.int32, (N_EXPERTS, TOK_TILE), 0) == e
        u_lo = ulo_ref[i]
        selected = []

        def tile_weights(ranks8, first_rank):
            ranks = jnp.sum(jnp.where(expert_row, ranks8, 0.0), axis=0, keepdims=True)
            return jnp.where(ranks == first_rank + row_id, 1.0, 0.0).astype(BF16)

        for sb in range(N_SUB):
            first_rank = (base_ref[i] + sb * SUB_BLOCK).astype(F32)
            t_lo, t_hi = slo_ref[N_SUB * i + sb], shi_ref[N_SUB * i + sb]
            k0 = jnp.clip(t_lo - u_lo, 0, GATHER_TILES + 1 - WINDOW_TILES)
            picks = []
            for j in range(WINDOW_TILES):
                t = u_lo + k0 + j
                live = (t >= t_lo) & (t <= t_hi) & (k0 + j < GATHER_TILES)
                picks.append(tile_weights(jnp.where(live, rbuf[slot, k0 + j], -1.0), first_rank))
            window = hbuf[slot, pl.ds(k0, WINDOW_TILES)].reshape(WINDOW_TILES * TOK_TILE, D_MODEL)
            rows = jnp.dot(jnp.concatenate(picks, axis=1), window, preferred_element_type=F32)

            def late_tile(t, carry):
                k = t - u_lo
                kk = jnp.minimum(k, GATHER_TILES)

                @pl.when(k >= GATHER_TILES)
                def _():
                    tok = pl.multiple_of(t * TOK_TILE, TOK_TILE)
                    pltpu.sync_copy(h_hbm.at[pl.ds(tok, TOK_TILE)], hbuf.at[slot, GATHER_TILES])
                    pltpu.sync_copy(rt_hbm.at[:, pl.ds(tok, TOK_TILE)], rbuf.at[slot, GATHER_TILES])

                return carry + jnp.dot(tile_weights(rbuf[slot, kk], first_rank), hbuf[slot, kk],
                                       preferred_element_type=F32)

            t_next = u_lo + jnp.minimum(k0 + WINDOW_TILES, GATHER_TILES)
            rows = lax.fori_loop(jnp.maximum(t_next, t_lo), t_hi + 1, late_tile, rows)
            selected.append(rows.astype(BF16))

        hg = jnp.concatenate(selected, axis=0)
        half = w1_ref.shape[2] // 2
        y = None
        for c in range(2):
            a = jnp.dot(hg, w1_ref[0, :, c * half:(c + 1) * half], preferred_element_type=F32)
            b = jnp.dot(hg, w3_ref[0, :, c * half:(c + 1) * half], preferred_element_type=F32)
            part = jnp.dot((_silu(a) * b).astype(BF16), w2_ref[0, c * half:(c + 1) * half, :],
                           preferred_element_type=F32)
            y = part if y is None else y + part
        y_ref[...] = y.astype(BF16)

    @pl.when(bv_ref[i] == 0)
    def _():
        y_ref[...] = jnp.zeros_like(y_ref)


def _moe_experts(h2, rank_t, w1, w3, w2, tb):
    nb = tb["n_blocks"]
    ff = w1.shape[2]
    w_map = lambda i, be, *_: (be[i], 0, 0)
    grid_spec = pltpu.PrefetchScalarGridSpec(
        num_scalar_prefetch=7,
        grid=(nb,),
        in_specs=[
            pl.BlockSpec(memory_space=pl.ANY),
            pl.BlockSpec(memory_space=pl.ANY),
            pl.BlockSpec((1, D_MODEL, ff), w_map, pipeline_mode=pl.Buffered(1)),
            pl.BlockSpec((1, D_MODEL, ff), w_map, pipeline_mode=pl.Buffered(1)),
            pl.BlockSpec((1, ff, D_MODEL), w_map, pipeline_mode=pl.Buffered(1)),
        ],
        out_specs=pl.BlockSpec((ROW_BLOCK, D_MODEL), lambda i, *_: (i, 0)),
        scratch_shapes=[
            pltpu.VMEM((2, GATHER_TILES + 1, TOK_TILE, D_MODEL), BF16),
            pltpu.VMEM((2, GATHER_TILES + 1, N_EXPERTS, TOK_TILE), F32),
            pltpu.SemaphoreType.DMA((2, GATHER_TILES)),
            pltpu.SemaphoreType.DMA((2, GATHER_TILES)),
        ],
    )
    return pl.pallas_call(
        _experts_kernel,
        grid_spec=grid_spec,
        out_shape=jax.ShapeDtypeStruct((nb * ROW_BLOCK, D_MODEL), BF16),
        compiler_params=_params("arbitrary"),
        name="moe_experts",
    )(tb["blk_e"], tb["valid"], tb["base"], tb["u_lo"], tb["u_n"], tb["sub_lo"], tb["sub_hi"],
      h2, rank_t, w1, w3, w2)


def _combine_kernel(*refs, final_norm, n_rows):
    cs_ref, cn_ref, seg_ref, x_ref, mod_ref, comb_ref, rank_ref, y_hbm = refs[0:8]
    fg_ref = refs[8] if final_norm else None
    o_ref, ybuf, ysem, acc_ref = refs[-4:]
    i = pl.program_id(0)
    slot = i % 2
    extra = N_EXPERTS

    def window(b, e):
        lo = seg_ref[e] + cs_ref[b * N_EXPERTS + e]
        cnt = cn_ref[b * N_EXPERTS + e]
        first = (lo // 16) * 16
        n_win = jnp.where(cnt > 0, (lo + cnt - first + COMBINE_WIN - 1) // COMBINE_WIN, 0)
        return first, n_win

    def win_start(first, j):
        return pl.multiple_of(jnp.minimum(first + j * COMBINE_WIN, n_rows - COMBINE_WIN), 16)

    def for_tile_windows(b, s, action):
        for e in range(N_EXPERTS):
            first, n_win = window(b, e)

            @pl.when(n_win > 0)
            def _():
                action(pltpu.make_async_copy(y_hbm.at[pl.ds(win_start(first, 0), COMBINE_WIN)],
                                             ybuf.at[s, e], ysem.at[s, e]))

    @pl.when(i == 0)
    def _():
        def clear(k, carry):
            ybuf[k // (N_EXPERTS + 1), k % (N_EXPERTS + 1)] = jnp.zeros((COMBINE_WIN, D_MODEL), BF16)
            return carry

        lax.fori_loop(0, 2 * (N_EXPERTS + 1), clear, 0)
        for_tile_windows(0, 0, lambda cp: cp.start())

    for_tile_windows(i, slot, lambda cp: cp.wait())

    @pl.when(i + 1 < pl.num_programs(0))
    def _():
        for_tile_windows(i + 1, 1 - slot, lambda cp: cp.start())

    col_id = lax.broadcasted_iota(jnp.int32, (COMBINE_TM, COMBINE_WIN), 1).astype(F32)

    def weights(e, first, j):
        pos_col = rank_ref[:, e:e + 1] + seg_ref[e].astype(F32)
        start = win_start(first, j).astype(F32)
        return jnp.where(pos_col == start + col_id, comb_ref[:, e:e + 1], 0.0).astype(BF16)

    firsts = [window(i, e) for e in range(N_EXPERTS)]
    pick_all = jnp.concatenate([weights(e, firsts[e][0], 0) for e in range(N_EXPERTS)], axis=1)
    y_all = ybuf[slot, 0:N_EXPERTS].reshape(N_EXPERTS * COMBINE_WIN, D_MODEL)
    acc_ref[...] = jnp.dot(pick_all, y_all, preferred_element_type=F32)

    for e in range(N_EXPERTS):
        first, n_win = firsts[e]

        def more(j, carry):
            pltpu.sync_copy(y_hbm.at[pl.ds(win_start(first, j), COMBINE_WIN)], ybuf.at[slot, extra])
            acc_ref[...] += jnp.dot(weights(e, first, j), ybuf[slot, extra], preferred_element_type=F32)
            return carry

        lax.fori_loop(1, jnp.maximum(n_win, 1), more, 0)

    o_ref[...] = _residual_out(x_ref[...], mod_ref[0, 5:6, :], acc_ref[...], fg_ref)


def _moe_combine(x2d, mod, comb, rank, y_sorted, tb, *, rows_per_mod, mod_base, final_g=None):
    n = x2d.shape[0]
    tm = COMBINE_TM
    row = lambda i, *_: (i, 0)
    const = lambda i, *_: (0, 0)
    mod_idx = _mod_index(tm, rows_per_mod, mod_base)
    in_specs = [
        pl.BlockSpec((tm, D_MODEL), row),
        pl.BlockSpec((1, 6, D_MODEL), lambda i, *_: mod_idx(i)),
        pl.BlockSpec((tm, LANES), row),
        pl.BlockSpec((tm, LANES), row),
        pl.BlockSpec(memory_space=pl.ANY),
    ]
    args = [x2d, mod, comb, rank, y_sorted]
    if final_g is not None:
        in_specs.append(pl.BlockSpec((1, D_MODEL), const))
        args.append(final_g)
    grid_spec = pltpu.PrefetchScalarGridSpec(
        num_scalar_prefetch=3,
        grid=(n // tm,),
        in_specs=in_specs,
        out_specs=pl.BlockSpec((tm, D_MODEL), row),
        scratch_shapes=[
            pltpu.VMEM((2, N_EXPERTS + 1, COMBINE_WIN, D_MODEL), BF16),
            pltpu.SemaphoreType.DMA((2, N_EXPERTS)),
            pltpu.VMEM((tm, D_MODEL), F32),
        ],
    )
    return pl.pallas_call(
        functools.partial(_combine_kernel, final_norm=final_g is not None, n_rows=y_sorted.shape[0]),
        grid_spec=grid_spec,
        out_shape=jax.ShapeDtypeStruct((n, D_MODEL), F32),
        compiler_params=_params("arbitrary"),
        name="moe_combine",
    )(tb["cs_c"], tb["cn_c"], tb["seg_rows"], *args)


def _moe_channel_mixer(x2d, mod, y_ssd, y_local, w_a, w_b, g, router_pad, w1, w3, w2, *, rows_per_mod, mod_base,
                       final_g=None):
    x1, h2, comb, rank, rank_t, cnt = _router(x2d, mod, y_ssd, y_local, w_a, w_b, g, router_pad, tm=ROUTER_TM,
                                              rows_per_mod=rows_per_mod, mod_base=mod_base)
    tb = _routing_tables(cnt, x2d.shape[0])
    y_sorted = _moe_experts(h2, rank_t, w1, w3, w2, tb)
    return _moe_combine(x1, mod, comb, rank, y_sorted, tb, rows_per_mod=rows_per_mod, mod_base=mod_base,
                        final_g=final_g)


def _layer_params(l, w_in, w_out, ssd_conv_w, ssd_conv_b, ssd_dt_bias, ssd_a_log, ssd_d, ssd_norm_g,
                  gm_norm_g, gm_norm_b, gm_ws, gm_bs, sc_conv_w, cf_conv_w, cf_conv_b, cf_norm_g, cf_norm_b):
    wi = w_in[l]
    w_main = jnp.concatenate(
        [wi[:, _C_XBC:_C_DT], wi[:, _C_Z:_C_UV], wi[:, _C_UV:_C_SC], wi[:, _C_CF:_C_END], wi[:, _C_SC:_C_CF]],
        axis=1).astype(BF16)
    w_dt = jnp.pad(wi[:, _C_DT:_C_Z], ((0, 0), (0, LANES - 2 * SSD_HEADS))).astype(BF16)
    return {
        "w_main": w_main,
        "w_dt": w_dt,
        "w_out_a": w_out[l, 0:BRANCH].astype(BF16),
        "w_out_b": w_out[l, BRANCH:].astype(BF16),
        "ssd_conv_w": ssd_conv_w[l],
        "ssd_conv_b": ssd_conv_b[l].reshape(1, -1),
        "ssd_dtp": jnp.stack([jnp.tile(ssd_dt_bias[l].reshape(-1), LANES // (2 * SSD_HEADS)),
                              jnp.tile(ssd_a_log[l].reshape(-1), LANES // (2 * SSD_HEADS))]),
        "ssd_vec": jnp.stack([jnp.repeat(ssd_d[l], HEAD_DIM), ssd_norm_g[l]]),
        "gm_ws": gm_ws[l],
        "gm_bs": jnp.repeat(gm_bs[l].T, HEAD_DIM, axis=1),
        "sc_conv_w": sc_conv_w[l],
        "cf_conv_w": cf_conv_w[l],
        "local_vec": jnp.stack([gm_norm_g[l], gm_norm_b[l], cf_conv_b[l], cf_norm_g[l], cf_norm_b[l]]),
    }


def kernel(x, c, ctx, c_ctx, mod_w, mod_b, norm1_g, norm2_g, w_in, w_out, ssd_conv_w, ssd_conv_b, ssd_dt_bias, ssd_a_log, ssd_d, ssd_norm_g, gm_norm_g, gm_norm_b, gm_ws, gm_bs, sc_conv_w, cf_conv_w, cf_conv_b, cf_norm_g, cf_norm_b, ffn_w1, ffn_w3, ffn_w2, moe_router, moe_w1, moe_w3, moe_w2, final_norm_g):
    bsz, seq, _ = x.shape
    ctx_len = ctx.shape[1]
    depth = w_in.shape[0]
    n_x, n_c = bsz * seq, bsz * ctx_len
    tm = 512

    mod = _modulation(jnp.concatenate([c, c_ctx[None, :]], axis=0), mod_w, mod_b)
    x_map = dict(tm=tm, rows_per_mod=seq, mod_base=0)
    c_map = dict(tm=tm, rows_per_mod=n_c, mod_base=bsz)
    zero_state = jnp.zeros((bsz, 2, SSD_GROUPS, SSD_STATE, 2 * HEAD_DIM), F32)

    xs = x.reshape(n_x, D_MODEL)
    xc = ctx.reshape(n_c, D_MODEL)
    for l in range(depth):
        last = l == depth - 1
        lp = _layer_params(l, w_in, w_out, ssd_conv_w, ssd_conv_b, ssd_dt_bias, ssd_a_log, ssd_d, ssd_norm_g,
                           gm_norm_g, gm_norm_b, gm_ws, gm_bs, sc_conv_w, cf_conv_w, cf_conv_b, cf_norm_g,
                           cf_norm_b)
        g1 = norm1_g[l].reshape(1, D_MODEL)
        g2 = norm2_g[l].reshape(1, D_MODEL)
        zc_ssd, yc_loc, dtc = _token_mixer_front(xc, mod[l], g1, lp, tm=min(tm, ctx_len), seq=ctx_len,
                                                 rows_per_mod=n_c, mod_base=bsz)
        yc_ssd, states = _ssd_mixer(zc_ssd.reshape(bsz, ctx_len, W_SSD), dtc.reshape(bsz, ctx_len, LANES),
                                    zero_state, lp)
        z_ssd, y_loc, dtx = _token_mixer_front(xs, mod[l], g1, lp, tm=tm, seq=seq, rows_per_mod=seq, mod_base=0)
        y_ssd, _ = _ssd_mixer(z_ssd.reshape(bsz, seq, W_SSD), dtx.reshape(bsz, seq, LANES), states, lp)
        streams = [(xs, y_ssd.reshape(n_x, BRANCH), y_loc, x_map)]
        if not last:
            streams.append((xc, yc_ssd.reshape(n_c, BRANCH), yc_loc, c_map))
        fg = final_norm_g.reshape(1, D_MODEL) if last else None
        i = l // 2
        if l % 2 == 0:
            w1, w3, w2 = ffn_w1[i].astype(BF16), ffn_w3[i].astype(BF16), ffn_w2[i].astype(BF16)
            outs = [_dense_channel_mixer(t, mod[l], ya, yb, lp["w_out_a"], lp["w_out_b"], g2, w1, w3, w2,
                                         final_g=fg if t is xs else None, **m) for t, ya, yb, m in streams]
        else:
            w1, w3, w2 = moe_w1[i].astype(BF16), moe_w3[i].astype(BF16), moe_w2[i].astype(BF16)
            r_f32 = jnp.pad(moe_router[i], ((0, 0), (0, LANES - N_EXPERTS)))
            r_hi = r_f32.astype(BF16)
            r_pad = jnp.stack([r_hi, (r_f32 - r_hi.astype(F32)).astype(BF16)])
            outs = [_moe_channel_mixer(t, mod[l], ya, yb, lp["w_out_a"], lp["w_out_b"], g2, r_pad, w1, w3, w2,
                                       rows_per_mod=m["rows_per_mod"], mod_base=m["mod_base"],
                                       final_g=fg if t is xs else None) for t, ya, yb, m in streams]
        xs = outs[0]
        if not last:
            xc = outs[1]
    return xs.reshape(bsz, seq, D_MODEL)
```

```python
import functools

import jax
import jax.numpy as jnp
from jax import lax
from jax.experimental import pallas as pl
from jax.experimental.pallas import tpu as pltpu

F32 = jnp.float32
BF16 = jnp.bfloat16

LANES = 128
SUBLANES = 8
VMEM_LIMIT_BYTES = 56 * 1024 * 1024

D_MODEL = 1024
BRANCH = 256
HEAD_DIM = 64
SSD_HEADS = 4
SSD_GROUPS = 2
SSD_STATE = 128
CHUNK = 128
SSD_SEQS_PER_STEP = 2
SSD_CONV_CH = BRANCH + 2 * SSD_GROUPS * SSD_STATE
N_EXPERTS = 8
TOP_K = 2
TOK_TILE = 256
ROW_BLOCK = 512
SUB_BLOCK = 256
N_SUB = ROW_BLOCK // SUB_BLOCK
GATHER_TILES = 11
WINDOW_TILES = 5
ROUTER_TM = 512
COMBINE_TM = 512
COMBINE_WIN = 256
HALO = 16
RMS_EPS = 1e-6
LN_EPS = 1e-5

_C_XBC = 0
_C_DT = _C_XBC + SSD_CONV_CH
_C_Z = _C_DT + 2 * SSD_HEADS
_C_UV = _C_Z + BRANCH
_C_SC = _C_UV + 2 * BRANCH
_C_CF = _C_SC + 3 * BRANCH
_C_END = _C_CF + 2 * BRANCH
W_SSD = SSD_CONV_CH + BRANCH
W_UVCF = 4 * BRANCH
W_SC = 3 * BRANCH
W_MAIN = W_SSD + W_UVCF + W_SC


def _params(*sem):
    return pltpu.CompilerParams(dimension_semantics=sem, vmem_limit_bytes=VMEM_LIMIT_BYTES)


def _sigmoid(v):
    return 0.5 + 0.5 * jnp.tanh(0.5 * v)


def _silu(v):
    half = 0.5 * v
    return half + half * jnp.tanh(half)


def _softplus(v):
    return jnp.maximum(v, 0.0) + jnp.log1p(jnp.exp(-jnp.abs(v)))


def _gelu_tanh(v):
    return 0.5 * v * (1.0 + jnp.tanh(0.7978845608028654 * (v + 0.044715 * (v * v * v))))


def _norm_modulate(x, g, shift, scale):
    ms = jnp.mean(x * x, axis=-1, keepdims=True)
    return (x * lax.rsqrt(ms + RMS_EPS) * g) * (1.0 + scale) + shift


def _mod_kernel(c_ref, w_ref, b_ref, o_ref):
    o_ref[0, 0] = jnp.dot(_silu(c_ref[...]), w_ref[0], preferred_element_type=F32,
                          precision=lax.Precision.HIGHEST) + b_ref[0, 0]


def _modulation(c_all, mod_w, mod_b):
    depth = mod_w.shape[0]
    rows = c_all.shape[0]
    out = pl.pallas_call(
        _mod_kernel,
        grid=(depth, 6),
        in_specs=[
            pl.BlockSpec((rows, D_MODEL), lambda l, j: (0, 0)),
            pl.BlockSpec((1, D_MODEL, D_MODEL), lambda l, j: (l, 0, j)),
            pl.BlockSpec((1, 1, 1, D_MODEL), lambda l, j: (l, j, 0, 0)),
        ],
        out_specs=pl.BlockSpec((1, 1, rows, D_MODEL), lambda l, j: (l, j, 0, 0)),
        out_shape=jax.ShapeDtypeStruct((depth, 6, rows, D_MODEL), F32),
        compiler_params=_params("arbitrary", "arbitrary"),
        name="modulation",
    )(c_all, mod_w, mod_b.reshape(depth, 6, 1, D_MODEL))
    return jnp.transpose(out, (0, 2, 1, 3))


def _mod_index(tm, rows_per_mod, mod_base):
    return lambda i: (mod_base + (i * tm) // rows_per_mod, 0, 0)


def _fill_stage(stage_ref, prev, main, nxt):
    rows = main.shape[0]
    stage_ref[0:HALO, :] = prev
    stage_ref[HALO:HALO + rows, :] = main
    stage_ref[HALO + rows:2 * HALO + rows, :] = nxt


def _dwconv_rows(stage_ref, w_ref, taps, r, rows, c0, width):
    acc = None
    for k in range(taps):
        lo = HALO + r + k - taps // 2
        term = stage_ref[lo:lo + rows, c0:c0 + width] * w_ref[k:k + 1, c0:c0 + width]
        acc = term if acc is None else acc + term
    return acc


def _front_kernel(x_ref, xp_ref, xn_ref, mod_ref, g_ref, w_ref, wdt_ref, convw_ref, convb_ref,
                  ws_ref, bs_ref, scw_ref, cfw_ref, vec_ref,
                  zssd_ref, yloc_ref, dt_ref, ssd_stage, sc_stage, cf_stage, rot_ref, *, tm, seq):
    i = pl.program_id(0)
    tiles_per_seq = seq // tm
    keep_prev = jnp.where(i % tiles_per_seq == 0, 0.0, 1.0)
    keep_next = jnp.where(i % tiles_per_seq == tiles_per_seq - 1, 0.0, 1.0)
    g, shift, scale = g_ref[...], mod_ref[0, 0:1, :], mod_ref[0, 1:2, :]
    h = _norm_modulate(x_ref[...], g, shift, scale).astype(BF16)
    h_halo = _norm_modulate(jnp.concatenate([xp_ref[...], xn_ref[...]], axis=0), g, shift, scale).astype(BF16)
    z = jnp.dot(h, w_ref[...], preferred_element_type=F32)
    z_halo = jnp.dot(h_halo, w_ref[...], preferred_element_type=F32)
    zp = z_halo[0:HALO] * keep_prev
    zn = z_halo[HALO:2 * HALO] * keep_next
    dt_ref[...] = jnp.dot(h, wdt_ref[...], preferred_element_type=F32)

    _fill_stage(ssd_stage, zp[:, 0:SSD_CONV_CH], z[:, 0:SSD_CONV_CH], zn[:, 0:SSD_CONV_CH])
    for r in range(0, tm, CHUNK):
        for c0 in range(0, SSD_CONV_CH, BRANCH):
            v = _dwconv_rows(ssd_stage, convw_ref, 5, r, CHUNK, c0, BRANCH) + convb_ref[0:1, c0:c0 + BRANCH]
            zssd_ref[r:r + CHUNK, c0:c0 + BRANCH] = _silu(v).astype(BF16)
    zssd_ref[:, SSD_CONV_CH:W_SSD] = z[:, SSD_CONV_CH:W_SSD].astype(BF16)

    lane = lax.broadcasted_iota(jnp.int32, (CHUNK, BRANCH), 1)
    gm_g, gm_b = vec_ref[0:1, :], vec_ref[1:2, :]
    cf_b, cf_g, cf_nb = vec_ref[2:3, :], vec_ref[3:4, :], vec_ref[4:5, :]
    c_uv, c_cf, c_sc = W_SSD, W_SSD + 2 * BRANCH, W_SSD + W_UVCF

    def layer_norm(v, gain, bias):
        mu = jnp.mean(v, axis=-1, keepdims=True)
        var = jnp.mean(jnp.square(v - mu), axis=-1, keepdims=True)
        return (v - mu) * lax.rsqrt(var + LN_EPS) * gain + bias

    def sc_in(t):
        return t[:, c_sc + BRANCH:c_sc + 2 * BRANCH] * t[:, c_sc + 2 * BRANCH:c_sc + 3 * BRANCH]

    def cf_in(t):
        return t[:, c_cf:c_cf + BRANCH] * _sigmoid(t[:, c_cf + BRANCH:c_cf + 2 * BRANCH])

    _fill_stage(sc_stage, sc_in(zp), sc_in(z), sc_in(zn))
    _fill_stage(cf_stage, cf_in(zp), cf_in(z), cf_in(zn))
    span = tm + 2 * HALO - SUBLANES
    for phase in range(SUBLANES):
        rot_ref[phase] = cf_stage[phase:phase + span, :]

    for r in range(0, tm, CHUNK):
        ge = _gelu_tanh(z[r:r + CHUNK, c_uv:c_uv + 2 * BRANCH])
        u = ge[:, 0:BRANCH]
        v = layer_norm(ge[:, BRANCH:2 * BRANCH], gm_g, gm_b)
        s = bs_ref[...]
        for hd in range(BRANCH // HEAD_DIM):
            in_head = (lane >= hd * HEAD_DIM) & (lane < (hd + 1) * HEAD_DIM)
            s = s + jnp.dot(ws_ref[hd].astype(BF16), jnp.where(in_head, v, 0.0).astype(BF16),
                            preferred_element_type=F32)
        yloc_ref[r:r + CHUNK, 0:BRANCH] = (u * s).astype(BF16)
        gate = z[r:r + CHUNK, c_sc:c_sc + BRANCH]
        yloc_ref[r:r + CHUNK, BRANCH:2 * BRANCH] = (gate * _dwconv_rows(sc_stage, scw_ref, 3, r, CHUNK, 0, BRANCH)
                                                    ).astype(BF16)
        cv = cf_b
        for k in range(31):
            lo = HALO + r + k - 31 // 2
            q8 = (lo // SUBLANES) * SUBLANES
            cv = cv + rot_ref[lo % SUBLANES, q8:q8 + CHUNK, :] * cfw_ref[k:k + 1, :]
        yloc_ref[r:r + CHUNK, 2 * BRANCH:3 * BRANCH] = _silu(layer_norm(cv, cf_g, cf_nb)).astype(BF16)


def _token_mixer_front(x2d, mod, g, lp, *, tm, seq, rows_per_mod, mod_base):
    n = x2d.shape[0]
    per = tm // HALO
    row = lambda i: (i, 0)
    const = lambda i: (0, 0)
    span = tm + 2 * HALO
    return pl.pallas_call(
        functools.partial(_front_kernel, tm=tm, seq=seq),
        grid=(n // tm,),
        in_specs=[
            pl.BlockSpec((tm, D_MODEL), row),
            pl.BlockSpec((HALO, D_MODEL), lambda i: (jnp.maximum(i * per - 1, 0), 0)),
            pl.BlockSpec((HALO, D_MODEL), lambda i: (jnp.minimum((i + 1) * per, n // HALO - 1), 0)),
            pl.BlockSpec((1, 6, D_MODEL), _mod_index(tm, rows_per_mod, mod_base)),
            pl.BlockSpec((1, D_MODEL), const),
            pl.BlockSpec((D_MODEL, W_MAIN), const),
            pl.BlockSpec((D_MODEL, LANES), const),
            pl.BlockSpec((5, SSD_CONV_CH), const),
            pl.BlockSpec((1, SSD_CONV_CH), const),
            pl.BlockSpec((BRANCH // HEAD_DIM, CHUNK, CHUNK), lambda i: (0, 0, 0)),
            pl.BlockSpec((CHUNK, BRANCH), const),
            pl.BlockSpec((3, BRANCH), const),
            pl.BlockSpec((31, BRANCH), const),
            pl.BlockSpec((5, BRANCH), const),
        ],
        out_specs=[
            pl.BlockSpec((tm, W_SSD), row),
            pl.BlockSpec((tm, 3 * BRANCH), row),
            pl.BlockSpec((tm, LANES), row),
        ],
        out_shape=[
            jax.ShapeDtypeStruct((n, W_SSD), BF16),
            jax.ShapeDtypeStruct((n, 3 * BRANCH), BF16),
            jax.ShapeDtypeStruct((n, LANES), F32),
        ],
        scratch_shapes=[
            pltpu.VMEM((span, SSD_CONV_CH), F32),
            pltpu.VMEM((span, BRANCH), F32),
            pltpu.VMEM((span, BRANCH), F32),
            pltpu.VMEM((SUBLANES, span - SUBLANES, BRANCH), F32),
        ],
        compiler_params=_params("parallel"),
        name="token_mixer_front",
    )(x2d, x2d, x2d, mod, g, lp["w_main"], lp["w_dt"], lp["ssd_conv_w"], lp["ssd_conv_b"],
      lp["gm_ws"], lp["gm_bs"], lp["sc_conv_w"], lp["cf_conv_w"], lp["local_vec"])


def _ssd_kernel(z_ref, dt_ref, init_ref, dtp_ref, vec_ref,
                y_ref, fin_ref, p_ref, dts_ref, pt_ref, dtt_ref, yacc_ref, st_ref, *, seq, n_seq):
    nc = seq // CHUNK
    lane = lax.broadcasted_iota(jnp.int32, (CHUNK, LANES), 1)
    row_i = lax.broadcasted_iota(jnp.int32, (CHUNK, CHUNK), 0)
    col_i = lax.broadcasted_iota(jnp.int32, (CHUNK, CHUNK), 1)
    lower = col_i <= row_i
    upper = col_i >= row_i
    head0 = lane < HEAD_DIM

    fwd_lane = (lane % (2 * SSD_HEADS)) < SSD_HEADS
    for q in range(n_seq):
        dt_all = _softplus(dt_ref[q] + dtp_ref[0:1, :])
        adt = dt_all * (-jnp.exp(dtp_ref[1:2, :]))
        p_all = (jnp.dot(lower.astype(F32), jnp.where(fwd_lane, adt, 0.0), preferred_element_type=F32,
                         precision=lax.Precision.HIGHEST)
                 + jnp.dot(upper.astype(F32), jnp.where(fwd_lane, 0.0, adt), preferred_element_type=F32,
                           precision=lax.Precision.HIGHEST))
        pt_ref[q] = p_all.T
        dtt_ref[q] = dt_all.T
        for c in range(nc):
            shift = (LANES - 2 * SSD_HEADS * c) % LANES
            p_ref[q, c * CHUNK:(c + 1) * CHUNK, :] = pltpu.roll(p_all, shift, axis=1) if shift else p_all
            dts_ref[q, c * CHUNK:(c + 1) * CHUNK, :] = pltpu.roll(dt_all, shift, axis=1) if shift else dt_all

    def group_inputs(q, rows, g):
        xg = z_ref[q, rows, g * LANES:(g + 1) * LANES].astype(F32)
        bg = z_ref[q, rows, BRANCH + g * SSD_STATE:BRANCH + (g + 1) * SSD_STATE]
        cg = z_ref[q, rows, BRANCH + (SSD_GROUPS + g) * SSD_STATE:BRANCH + (SSD_GROUPS + g + 1) * SSD_STATE]
        return xg, bg, cg

    def spread(m, j):
        return jnp.broadcast_to(m[:, j:j + 1], (CHUNK, LANES))

    def pair(a0, a1):
        return jnp.where(head0, a0, a1)

    st_ref[...] = init_ref[:, 0]

    def fwd_chunk(q, c):
        r0 = pl.multiple_of(c * CHUNK, CHUNK)
        rows = pl.ds(r0, CHUNK)
        pm = p_ref[q, rows, :]
        dt = dts_ref[q, rows, :]
        head_rows = pl.ds(pl.multiple_of(c * 2 * SSD_HEADS, 2 * SSD_HEADS), 2 * SSD_HEADS)
        pm_t = pt_ref[q, head_rows, :]
        dt_t = dtt_ref[q, head_rows, :]
        last = pm[CHUNK - 1:CHUNK, :]
        to_end = dt * jnp.exp(last - pm)
        for g in range(SSD_GROUPS):
            xg, bg, cg = group_inputs(q, rows, g)
            gram = lax.dot_general(cg, bg, (((1,), (1,)), ((), ())), preferred_element_type=F32)
            s_prev = st_ref[q, g]
            y = jnp.dot(cg, s_prev.astype(BF16), preferred_element_type=F32)
            hf0, hf1 = 2 * g, 2 * g + 1
            hb0, hb1 = SSD_HEADS + hf0, SSD_HEADS + hf1
            p_f0, p_f1 = spread(pm, hf0), spread(pm, hf1)
            y = y * jnp.exp(pair(p_f0, p_f1))
            for p_f, hf, hb, mask in ((p_f0, hf0, hb0, head0), (p_f1, hf1, hb1, jnp.logical_not(head0))):
                dec_f = jnp.exp(jnp.where(lower, p_f - pm_t[hf:hf + 1, :], -jnp.inf))
                dec_b = jnp.exp(jnp.where(upper, spread(pm, hb) - pm_t[hb:hb + 1, :], -jnp.inf))
                m = gram * (dec_f * dt_t[hf:hf + 1, :] + dec_b * dt_t[hb:hb + 1, :])
                xh = jnp.where(mask, xg, 0.0).astype(BF16)
                y = y + jnp.dot(m.astype(BF16), xh, preferred_element_type=F32)
            yacc_ref[q, rows, g * LANES:(g + 1) * LANES] = y
            w_end = pair(spread(to_end, hf0), spread(to_end, hf1))
            upd = lax.dot_general(bg, (xg * w_end).astype(BF16), (((0,), (0,)), ((), ())),
                                  preferred_element_type=F32)
            decay = jnp.exp(jnp.where(head0[0:1, :], last[:, hf0:hf0 + 1], last[:, hf1:hf1 + 1]))
            st_ref[q, g] = s_prev * decay + upd

    def fwd(c, carry):
        for q in range(n_seq):
            fwd_chunk(q, c)
        return carry

    lax.fori_loop(0, nc, fwd, 0)
    fin_ref[:, 0] = st_ref[...]

    st_ref[...] = init_ref[:, 1]
    d_skip = vec_ref[0:1, :]
    norm_g = vec_ref[1:2, :]

    def bwd_chunk(q, c):
        r0 = pl.multiple_of(c * CHUNK, CHUNK)
        rows = pl.ds(r0, CHUNK)
        pm = p_ref[q, rows, :]
        dt = dts_ref[q, rows, :]
        first = pm[0:1, :]
        to_end = dt * jnp.exp(first - pm)
        for g in range(SSD_GROUPS):
            xg, bg, cg = group_inputs(q, rows, g)
            hb0, hb1 = SSD_HEADS + 2 * g, SSD_HEADS + 2 * g + 1
            s_prev = st_ref[q, g]
            y = (jnp.dot(cg, s_prev.astype(BF16), preferred_element_type=F32)
                 * jnp.exp(pair(spread(pm, hb0), spread(pm, hb1))))
            yacc_ref[q, rows, g * LANES:(g + 1) * LANES] = (
                yacc_ref[q, rows, g * LANES:(g + 1) * LANES] + y + d_skip[:, g * LANES:(g + 1) * LANES] * xg)
            w_end = pair(spread(to_end, hb0), spread(to_end, hb1))
            upd = lax.dot_general(bg, (xg * w_end).astype(BF16), (((0,), (0,)), ((), ())),
                                  preferred_element_type=F32)
            decay = jnp.exp(jnp.where(head0[0:1, :], first[:, hb0:hb0 + 1], first[:, hb1:hb1 + 1]))
            st_ref[q, g] = s_prev * decay + upd
        gated = yacc_ref[q, rows, :] * _silu(z_ref[q, rows, SSD_CONV_CH:W_SSD].astype(F32))
        ms = jnp.mean(gated * gated, axis=-1, keepdims=True)
        y_ref[q, rows, :] = (gated * lax.rsqrt(ms + RMS_EPS) * norm_g).astype(BF16)

    def bwd(k, carry):
        for q in range(n_seq):
            bwd_chunk(q, nc - 1 - k)
        return carry

    lax.fori_loop(0, nc, bwd, 0)
    fin_ref[:, 1] = st_ref[...]


def _ssd_mixer(z_ssd, dt, init, lp):
    bsz, seq, _ = z_ssd.shape
    nc = seq // CHUNK
    n_hd = 2 * SSD_HEADS
    n_seq = SSD_SEQS_PER_STEP if bsz % SSD_SEQS_PER_STEP == 0 else 1
    assert nc * n_hd <= LANES
    dt = dt[:, :, :n_hd].reshape(bsz, nc, CHUNK, n_hd).transpose(0, 2, 1, 3).reshape(bsz, CHUNK, nc * n_hd)
    dt = jnp.pad(dt, ((0, 0), (0, 0), (0, LANES - nc * n_hd)))
    seq_map = lambda b: (b, 0, 0)
    st_map = lambda b: (b, 0, 0, 0, 0)
    const = lambda b: (0, 0)
    st_shape = (bsz, 2, SSD_GROUPS, SSD_STATE, 2 * HEAD_DIM)
    return pl.pallas_call(
        functools.partial(_ssd_kernel, seq=seq, n_seq=n_seq),
        grid=(bsz // n_seq,),
        in_specs=[
            pl.BlockSpec((n_seq, seq, W_SSD), seq_map),
            pl.BlockSpec((n_seq, CHUNK, LANES), seq_map),
            pl.BlockSpec((n_seq,) + st_shape[1:], st_map),
            pl.BlockSpec((2, LANES), const),
            pl.BlockSpec((2, BRANCH), const),
        ],
        out_specs=[
            pl.BlockSpec((n_seq, seq, BRANCH), seq_map),
            pl.BlockSpec((n_seq,) + st_shape[1:], st_map),
        ],
        out_shape=[
            jax.ShapeDtypeStruct((bsz, seq, BRANCH), BF16),
            jax.ShapeDtypeStruct(st_shape, F32),
        ],
        scratch_shapes=[
            pltpu.VMEM((n_seq, seq, LANES), F32),
            pltpu.VMEM((n_seq, seq, LANES), F32),
            pltpu.VMEM((n_seq, LANES, CHUNK), F32),
            pltpu.VMEM((n_seq, LANES, CHUNK), F32),
            pltpu.VMEM((n_seq, seq, BRANCH), F32),
            pltpu.VMEM((n_seq, SSD_GROUPS, SSD_STATE, 2 * HEAD_DIM), F32),
        ],
        compiler_params=_params("parallel"),
        name="ssd_mixer",
    )(z_ssd, dt, init, lp["ssd_dtp"], lp["ssd_vec"])


def _token_mixed(x_ref, mod_ref, ya_ref, yb_ref, wa_ref, wb_ref):
    proj = (jnp.dot(ya_ref[...], wa_ref[...], preferred_element_type=F32)
            + jnp.dot(yb_ref[...], wb_ref[...], preferred_element_type=F32))
    return x_ref[...] + mod_ref[0, 2:3, :] * proj


def _token_mixed_specs(tm, rows_per_mod, mod_base):
    row = lambda i, *_: (i, 0)
    const = lambda i, *_: (0, 0)
    mod_idx = _mod_index(tm, rows_per_mod, mod_base)
    return [
        pl.BlockSpec((tm, D_MODEL), row),
        pl.BlockSpec((1, 6, D_MODEL), lambda i, *_: mod_idx(i)),
        pl.BlockSpec((tm, BRANCH), row),
        pl.BlockSpec((tm, 3 * BRANCH), row),
        pl.BlockSpec((BRANCH, D_MODEL), const),
        pl.BlockSpec((3 * BRANCH, D_MODEL), const),
    ]


def _residual_out(x, gate, delta, fg_ref):
    out = x + gate * delta
    if fg_ref is not None:
        ms = jnp.mean(out * out, axis=-1, keepdims=True)
        out = out * lax.rsqrt(ms + RMS_EPS) * fg_ref[...]
    return out


def _swiglu_kernel(*refs, final_norm):
    x_ref, mod_ref, ya_ref, yb_ref, wa_ref, wb_ref, g_ref, w1_ref, w3_ref, w2_ref = refs[0:10]
    fg_ref = refs[10] if final_norm else None
    o_ref = refs[-1]
    x1 = _token_mixed(x_ref, mod_ref, ya_ref, yb_ref, wa_ref, wb_ref)
    h = _norm_modulate(x1, g_ref[...], mod_ref[0, 3:4, :], mod_ref[0, 4:5, :]).astype(BF16)
    a = jnp.dot(h, w1_ref[...], preferred_element_type=F32)
    b = jnp.dot(h, w3_ref[...], preferred_element_type=F32)
    y = jnp.dot((_silu(a) * b).astype(BF16), w2_ref[...], preferred_element_type=F32)
    o_ref[...] = _residual_out(x1, mod_ref[0, 5:6, :], y, fg_ref)


def _dense_channel_mixer(x2d, mod, y_ssd, y_local, w_a, w_b, g, w1, w3, w2, *, tm, rows_per_mod, mod_base,
                         final_g=None):
    n = x2d.shape[0]
    ff = w1.shape[1]
    const = lambda i: (0, 0)
    resident = dict(pipeline_mode=pl.Buffered(1))
    in_specs = _token_mixed_specs(tm, rows_per_mod, mod_base) + [
        pl.BlockSpec((1, D_MODEL), const),
        pl.BlockSpec((D_MODEL, ff), const, **resident),
        pl.BlockSpec((D_MODEL, ff), const, **resident),
        pl.BlockSpec((ff, D_MODEL), const, **resident),
    ]
    args = [x2d, mod, y_ssd, y_local, w_a, w_b, g, w1, w3, w2]
    if final_g is not None:
        in_specs.append(pl.BlockSpec((1, D_MODEL), const))
        args.append(final_g)
    return pl.pallas_call(
        functools.partial(_swiglu_kernel, final_norm=final_g is not None),
        grid=(n // tm,),
        in_specs=in_specs,
        out_specs=pl.BlockSpec((tm, D_MODEL), lambda i: (i, 0)),
        out_shape=jax.ShapeDtypeStruct((n, D_MODEL), F32),
        compiler_params=_params("parallel"),
        name="dense_channel_mixer",
    )(*args)


def _router_kernel(x_ref, mod_ref, ya_ref, yb_ref, wa_ref, wb_ref, g_ref, r_ref,
                   x1_ref, h_ref, comb_ref, rank_ref, rankt_ref, cnt_ref, carry_ref):
    @pl.when(pl.program_id(0) == 0)
    def _():
        carry_ref[...] = jnp.zeros_like(carry_ref)

    x1 = _token_mixed(x_ref, mod_ref, ya_ref, yb_ref, wa_ref, wb_ref)
    x1_ref[...] = x1
    h = _norm_modulate(x1, g_ref[...], mod_ref[0, 3:4, :], mod_ref[0, 4:5, :])
    h_hi = h.astype(BF16)
    h_ref[...] = h_hi
    h_lo = (h - h_hi.astype(F32)).astype(BF16)
    parts = jnp.dot(jnp.concatenate([h_hi, h_lo], axis=0), r_ref[...], preferred_element_type=F32)
    parts = parts[0:h.shape[0], :] + parts[h.shape[0]:, :]
    logits = parts + pltpu.roll(parts, LANES - N_EXPERTS, axis=1)
    lane = lax.broadcasted_iota(jnp.int32, logits.shape, 1)
    logits = jnp.where(lane < N_EXPERTS, logits, -jnp.inf)
    m1 = jnp.max(logits, axis=-1, keepdims=True)
    i1 = jnp.min(jnp.where(logits == m1, lane, LANES), axis=-1, keepdims=True)
    rest = jnp.where(lane == i1, -jnp.inf, logits)
    m2 = jnp.max(rest, axis=-1, keepdims=True)
    i2 = jnp.min(jnp.where(rest == m2, lane, LANES), axis=-1, keepdims=True)
    e2 = jnp.exp(m2 - m1)
    denom = 1.0 + e2
    comb_ref[...] = jnp.where(lane == i1, 1.0 / denom, 0.0) + jnp.where(lane == i2, e2 / denom, 0.0)
    is_chosen = (lane == i1) | (lane == i2)
    chosen = jnp.where(is_chosen, 1.0, 0.0)
    tm = chosen.shape[0]
    earlier = lax.broadcasted_iota(jnp.int32, (tm, tm), 1) < lax.broadcasted_iota(jnp.int32, (tm, tm), 0)
    running = carry_ref[...]
    before = jnp.dot(jnp.where(earlier, 1.0, 0.0).astype(BF16), chosen.astype(BF16),
                     preferred_element_type=F32) + running.astype(F32)
    rank = jnp.where(is_chosen, before, -1.0)
    rank_ref[...] = rank
    rankt_ref[...] = rank.T[0:N_EXPERTS, :]
    chosen_i = jnp.where(is_chosen, 1, 0)
    for k in range(tm // TOK_TILE):
        running = running + jnp.sum(chosen_i[k * TOK_TILE:(k + 1) * TOK_TILE, :], axis=0, keepdims=True)
        cnt_ref[0, k:k + 1, :] = running
    carry_ref[...] = running


def _router(x2d, mod, y_ssd, y_local, w_a, w_b, g, router_pad, *, tm, rows_per_mod, mod_base):
    n = x2d.shape[0]
    row = lambda i: (i, 0)
    const = lambda i: (0, 0)
    per = tm // TOK_TILE
    return pl.pallas_call(
        _router_kernel,
        grid=(n // tm,),
        in_specs=_token_mixed_specs(tm, rows_per_mod, mod_base) + [
            pl.BlockSpec((1, D_MODEL), const),
            pl.BlockSpec((D_MODEL, LANES), const),
        ],
        out_specs=[
            pl.BlockSpec((tm, D_MODEL), row),
            pl.BlockSpec((tm, D_MODEL), row),
            pl.BlockSpec((tm, LANES), row),
            pl.BlockSpec((tm, LANES), row),
            pl.BlockSpec((N_EXPERTS, tm), lambda i: (0, i)),
            pl.BlockSpec((1, per, LANES), lambda i: (i, 0, 0)),
        ],
        out_shape=[
            jax.ShapeDtypeStruct((n, D_MODEL), F32),
            jax.ShapeDtypeStruct((n, D_MODEL), BF16),
            jax.ShapeDtypeStruct((n, LANES), F32),
            jax.ShapeDtypeStruct((n, LANES), F32),
            jax.ShapeDtypeStruct((N_EXPERTS, n), F32),
            jax.ShapeDtypeStruct((n // tm, per, LANES), jnp.int32),
        ],
        scratch_shapes=[pltpu.VMEM((1, LANES), jnp.int32)],
        compiler_params=_params("arbitrary"),
        name="moe_router",
    )(x2d, mod, y_ssd, y_local, w_a, w_b, g, router_pad)


def _routing_tables(cnt, n_tokens):
    i32 = jnp.int32
    n_t = n_tokens // TOK_TILE
    cnt_end = cnt.reshape(n_t, LANES)[:, :N_EXPERTS]
    cnt_start = jnp.concatenate([jnp.zeros((1, N_EXPERTS), i32), cnt_end[:-1]], axis=0)
    tot = cnt_end[-1]
    nblk = (tot + ROW_BLOCK - 1) // ROW_BLOCK
    blk_cum = jnp.cumsum(nblk)
    blk_first = blk_cum - nblk
    n_valid = blk_cum[-1]
    nb = n_tokens * TOP_K // ROW_BLOCK + N_EXPERTS
    bi = jnp.arange(nb, dtype=i32)
    valid = bi < n_valid
    e_of = jnp.sum((bi[:, None] >= blk_cum[None, :]).astype(i32), axis=1)
    e_last = jnp.sum(((n_valid - 1) >= blk_cum).astype(i32))
    blk_e = jnp.minimum(jnp.where(valid, e_of, e_last), N_EXPERTS - 1)
    base = jnp.where(valid, (bi - blk_first[blk_e]) * ROW_BLOCK, 0)
    lo = base[:, None] + jnp.arange(N_SUB, dtype=i32)[None, :] * SUB_BLOCK
    hi = jnp.where(valid[:, None], jnp.minimum(lo + SUB_BLOCK, tot[blk_e][:, None]), lo)
    cs = cnt_start[:, blk_e]
    ce = cnt_end[:, blk_e]
    rel = (cs[:, :, None] < hi[None]) & (ce[:, :, None] > lo[None])
    tile_id = jnp.arange(n_t, dtype=i32)[:, None, None]
    t_first = jnp.min(jnp.where(rel, tile_id, n_t), axis=0)
    sub_hi = jnp.max(jnp.where(rel, tile_id, -1), axis=0)
    sub_lo = jnp.where(sub_hi >= 0, t_first, 0)
    u_lo = jnp.min(t_first, axis=1)
    u_n = jnp.maximum(jnp.max(sub_hi, axis=1) - u_lo + 1, 0)
    u_lo = jnp.where(u_n > 0, u_lo, 0)
    per = COMBINE_TM // TOK_TILE
    cs_c = cnt_start[0::per]
    cn_c = cnt_end[per - 1::per] - cs_c
    return dict(blk_e=blk_e.astype(i32), valid=valid.astype(i32), base=base.astype(i32),
                u_lo=u_lo.astype(i32), u_n=u_n.astype(i32),
                sub_lo=sub_lo.reshape(-1).astype(i32), sub_hi=sub_hi.reshape(-1).astype(i32),
                seg_rows=(blk_first * ROW_BLOCK).astype(i32),
                cs_c=cs_c.reshape(-1).astype(i32), cn_c=cn_c.reshape(-1).astype(i32), n_blocks=nb)


def _experts_kernel(be_ref, bv_ref, base_ref, ulo_ref, un_ref, slo_ref, shi_ref,
                    h_hbm, rt_hbm, w1_ref, w3_ref, w2_ref, y_ref, hbuf, rbuf, hsem, rsem):
    i = pl.program_id(0)
    slot = i % 2

    def tile_copies(s, k, t):
        tok = pl.multiple_of(t * TOK_TILE, TOK_TILE)
        return (pltpu.make_async_copy(h_hbm.at[pl.ds(tok, TOK_TILE)], hbuf.at[s, k], hsem.at[s, k]),
                pltpu.make_async_copy(rt_hbm.at[:, pl.ds(tok, TOK_TILE)], rbuf.at[s, k], rsem.at[s, k]))

    def for_block_tiles(b, s, action):
        for k in range(GATHER_TILES):
            @pl.when(k < un_ref[b])
            def _():
                for cp in tile_copies(s, k, ulo_ref[b] + k):
                    action(cp)

    @pl.when(i == 0)
    def _():
        def clear(k, carry):
            hbuf[k // (GATHER_TILES + 1), k % (GATHER_TILES + 1)] = jnp.zeros((TOK_TILE, D_MODEL), BF16)
            rbuf[k // (GATHER_TILES + 1), k % (GATHER_TILES + 1)] = jnp.zeros((N_EXPERTS, TOK_TILE), F32)
            return carry

        lax.fori_loop(0, 2 * (GATHER_TILES + 1), clear, 0)
        for_block_tiles(0, 0, lambda cp: cp.start())

    for_block_tiles(i, slot, lambda cp: cp.wait())

    @pl.when(i + 1 < pl.num_programs(0))
    def _():
        for_block_tiles(i + 1, 1 - slot, lambda cp: cp.start())

    @pl.when(bv_ref[i] == 1)
    def _():
        e = be_ref[i]
        row_id = lax.broadcasted_iota(jnp.int32, (SUB_BLOCK, TOK_TILE), 0).astype(F32)
        expert_row = lax.broadcasted_iota(jnp.int32, (N_EXPERTS, TOK_TILE), 0) == e
        u_lo = ulo_ref[i]
        selected = []

        def tile_weights(ranks8, first_rank):
            ranks = jnp.sum(jnp.where(expert_row, ranks8, 0.0), axis=0, keepdims=True)
            return jnp.where(ranks == first_rank + row_id, 1.0, 0.0).astype(BF16)

        for sb in range(N_SUB):
            first_rank = (base_ref[i] + sb * SUB_BLOCK).astype(F32)
            t_lo, t_hi = slo_ref[N_SUB * i + sb], shi_ref[N_SUB * i + sb]
            k0 = jnp.clip(t_lo - u_lo, 0, GATHER_TILES + 1 - WINDOW_TILES)
            picks = []
            for j in range(WINDOW_TILES):
                t = u_lo + k0 + j
                live = (t >= t_lo) & (t <= t_hi) & (k0 + j < GATHER_TILES)
                picks.append(tile_weights(jnp.where(live, rbuf[slot, k0 + j], -1.0), first_rank))
            window = hbuf[slot, pl.ds(k0, WINDOW_TILES)].reshape(WINDOW_TILES * TOK_TILE, D_MODEL)
            rows = jnp.dot(jnp.concatenate(picks, axis=1), window, preferred_element_type=F32)

            def late_tile(t, carry):
                k = t - u_lo
                kk = jnp.minimum(k, GATHER_TILES)

                @pl.when(k >= GATHER_TILES)
                def _():
                    tok = pl.multiple_of(t * TOK_TILE, TOK_TILE)
                    pltpu.sync_copy(h_hbm.at[pl.ds(tok, TOK_TILE)], hbuf.at[slot, GATHER_TILES])
                    pltpu.sync_copy(rt_hbm.at[:, pl.ds(tok, TOK_TILE)], rbuf.at[slot, GATHER_TILES])

                return carry + jnp.dot(tile_weights(rbuf[slot, kk], first_rank), hbuf[slot, kk],
                                       preferred_element_type=F32)

            t_next = u_lo + jnp.minimum(k0 + WINDOW_TILES, GATHER_TILES)
            rows = lax.fori_loop(jnp.maximum(t_next, t_lo), t_hi + 1, late_tile, rows)
            selected.append(rows.astype(BF16))

        hg = jnp.concatenate(selected, axis=0)
        half = w1_ref.shape[2] // 2
        y = None
        for c in range(2):
            a = jnp.dot(hg, w1_ref[0, :, c * half:(c + 1) * half], preferred_element_type=F32)
            b = jnp.dot(hg, w3_ref[0, :, c * half:(c + 1) * half], preferred_element_type=F32)
            part = jnp.dot((_silu(a) * b).astype(BF16), w2_ref[0, c * half:(c + 1) * half, :],
                           preferred_element_type=F32)
            y = part if y is None else y + part
        y_ref[...] = y.astype(BF16)

    @pl.when(bv_ref[i] == 0)
    def _():
        y_ref[...] = jnp.zeros_like(y_ref)


def _moe_experts(h2, rank_t, w1, w3, w2, tb):
    nb = tb["n_blocks"]
    ff = w1.shape[2]
    w_map = lambda i, be, *_: (be[i], 0, 0)
    grid_spec = pltpu.PrefetchScalarGridSpec(
        num_scalar_prefetch=7,
        grid=(nb,),
        in_specs=[
            pl.BlockSpec(memory_space=pl.ANY),
            pl.BlockSpec(memory_space=pl.ANY),
            pl.BlockSpec((1, D_MODEL, ff), w_map, pipeline_mode=pl.Buffered(1)),
            pl.BlockSpec((1, D_MODEL, ff), w_map, pipeline_mode=pl.Buffered(1)),
            pl.BlockSpec((1, ff, D_MODEL), w_map, pipeline_mode=pl.Buffered(1)),
        ],
        out_specs=pl.BlockSpec((ROW_BLOCK, D_MODEL), lambda i, *_: (i, 0)),
        scratch_shapes=[
            pltpu.VMEM((2, GATHER_TILES + 1, TOK_TILE, D_MODEL), BF16),
            pltpu.VMEM((2, GATHER_TILES + 1, N_EXPERTS, TOK_TILE), F32),
            pltpu.SemaphoreType.DMA((2, GATHER_TILES)),
            pltpu.SemaphoreType.DMA((2, GATHER_TILES)),
        ],
    )
    return pl.pallas_call(
        _experts_kernel,
        grid_spec=grid_spec,
        out_shape=jax.ShapeDtypeStruct((nb * ROW_BLOCK, D_MODEL), BF16),
        compiler_params=_params("arbitrary"),
        name="moe_experts",
    )(tb["blk_e"], tb["valid"], tb["base"], tb["u_lo"], tb["u_n"], tb["sub_lo"], tb["sub_hi"],
      h2, rank_t, w1, w3, w2)


def _combine_kernel(*refs, final_norm, n_rows):
    cs_ref, cn_ref, seg_ref, x_ref, mod_ref, comb_ref, rank_ref, y_hbm = refs[0:8]
    fg_ref = refs[8] if final_norm else None
    o_ref, ybuf, ysem, acc_ref = refs[-4:]
    i = pl.program_id(0)
    slot = i % 2
    extra = N_EXPERTS

    def window(b, e):
        lo = seg_ref[e] + cs_ref[b * N_EXPERTS + e]
        cnt = cn_ref[b * N_EXPERTS + e]
        first = (lo // 16) * 16
        n_win = jnp.where(cnt > 0, (lo + cnt - first + COMBINE_WIN - 1) // COMBINE_WIN, 0)
        return first, n_win

    def win_start(first, j):
        return pl.multiple_of(jnp.minimum(first + j * COMBINE_WIN, n_rows - COMBINE_WIN), 16)

    def for_tile_windows(b, s, action):
        for e in range(N_EXPERTS):
            first, n_win = window(b, e)

            @pl.when(n_win > 0)
            def _():
                action(pltpu.make_async_copy(y_hbm.at[pl.ds(win_start(first, 0), COMBINE_WIN)],
                                             ybuf.at[s, e], ysem.at[s, e]))

    @pl.when(i == 0)
    def _():
        def clear(k, carry):
            ybuf[k // (N_EXPERTS + 1), k % (N_EXPERTS + 1)] = jnp.zeros((COMBINE_WIN, D_MODEL), BF16)
            return carry

        lax.fori_loop(0, 2 * (N_EXPERTS + 1), clear, 0)
        for_tile_windows(0, 0, lambda cp: cp.start())

    for_tile_windows(i, slot, lambda cp: cp.wait())

    @pl.when(i + 1 < pl.num_programs(0))
    def _():
        for_tile_windows(i + 1, 1 - slot, lambda cp: cp.start())

    col_id = lax.broadcasted_iota(jnp.int32, (COMBINE_TM, COMBINE_WIN), 1).astype(F32)

    def weights(e, first, j):
        pos_col = rank_ref[:, e:e + 1] + seg_ref[e].astype(F32)
        start = win_start(first, j).astype(F32)
        return jnp.where(pos_col == start + col_id, comb_ref[:, e:e + 1], 0.0).astype(BF16)

    firsts = [window(i, e) for e in range(N_EXPERTS)]
    pick_all = jnp.concatenate([weights(e, firsts[e][0], 0) for e in range(N_EXPERTS)], axis=1)
    y_all = ybuf[slot, 0:N_EXPERTS].reshape(N_EXPERTS * COMBINE_WIN, D_MODEL)
    acc_ref[...] = jnp.dot(pick_all, y_all, preferred_element_type=F32)

    for e in range(N_EXPERTS):
        first, n_win = firsts[e]

        def more(j, carry):
            pltpu.sync_copy(y_hbm.at[pl.ds(win_start(first, j), COMBINE_WIN)], ybuf.at[slot, extra])
            acc_ref[...] += jnp.dot(weights(e, first, j), ybuf[slot, extra], preferred_element_type=F32)
            return carry

        lax.fori_loop(1, jnp.maximum(n_win, 1), more, 0)

    o_ref[...] = _residual_out(x_ref[...], mod_ref[0, 5:6, :], acc_ref[...], fg_ref)


def _moe_combine(x2d, mod, comb, rank, y_sorted, tb, *, rows_per_mod, mod_base, final_g=None):
    n = x2d.shape[0]
    tm = COMBINE_TM
    row = lambda i, *_: (i, 0)
    const = lambda i, *_: (0, 0)
    mod_idx = _mod_index(tm, rows_per_mod, mod_base)
    in_specs = [
        pl.BlockSpec((tm, D_MODEL), row),
        pl.BlockSpec((1, 6, D_MODEL), lambda i, *_: mod_idx(i)),
        pl.BlockSpec((tm, LANES), row),
        pl.BlockSpec((tm, LANES), row),
        pl.BlockSpec(memory_space=pl.ANY),
    ]
    args = [x2d, mod, comb, rank, y_sorted]
    if final_g is not None:
        in_specs.append(pl.BlockSpec((1, D_MODEL), const))
        args.append(final_g)
    grid_spec = pltpu.PrefetchScalarGridSpec(
        num_scalar_prefetch=3,
        grid=(n // tm,),
        in_specs=in_specs,
        out_specs=pl.BlockSpec((tm, D_MODEL), row),
        scratch_shapes=[
            pltpu.VMEM((2, N_EXPERTS + 1, COMBINE_WIN, D_MODEL), BF16),
            pltpu.SemaphoreType.DMA((2, N_EXPERTS)),
            pltpu.VMEM((tm, D_MODEL), F32),
        ],
    )
    return pl.pallas_call(
        functools.partial(_combine_kernel, final_norm=final_g is not None, n_rows=y_sorted.shape[0]),
        grid_spec=grid_spec,
        out_shape=jax.ShapeDtypeStruct((n, D_MODEL), F32),
        compiler_params=_params("arbitrary"),
        name="moe_combine",
    )(tb["cs_c"], tb["cn_c"], tb["seg_rows"], *args)


def _moe_channel_mixer(x2d, mod, y_ssd, y_local, w_a, w_b, g, router_pad, w1, w3, w2, *, rows_per_mod, mod_base,
                       final_g=None):
    x1, h2, comb, rank, rank_t, cnt = _router(x2d, mod, y_ssd, y_local, w_a, w_b, g, router_pad, tm=ROUTER_TM,
                                              rows_per_mod=rows_per_mod, mod_base=mod_base)
    tb = _routing_tables(cnt, x2d.shape[0])
    y_sorted = _moe_experts(h2, rank_t, w1, w3, w2, tb)
    return _moe_combine(x1, mod, comb, rank, y_sorted, tb, rows_per_mod=rows_per_mod, mod_base=mod_base,
                        final_g=final_g)


def _layer_params(l, w_in, w_out, ssd_conv_w, ssd_conv_b, ssd_dt_bias, ssd_a_log, ssd_d, ssd_norm_g,
                  gm_norm_g, gm_norm_b, gm_ws, gm_bs, sc_conv_w, cf_conv_w, cf_conv_b, cf_norm_g, cf_norm_b):
    wi = w_in[l]
    w_main = jnp.concatenate(
        [wi[:, _C_XBC:_C_DT], wi[:, _C_Z:_C_UV], wi[:, _C_UV:_C_SC], wi[:, _C_CF:_C_END], wi[:, _C_SC:_C_CF]],
        axis=1).astype(BF16)
    w_dt = jnp.pad(wi[:, _C_DT:_C_Z], ((0, 0), (0, LANES - 2 * SSD_HEADS))).astype(BF16)
    return {
        "w_main": w_main,
        "w_dt": w_dt,
        "w_out_a": w_out[l, 0:BRANCH].astype(BF16),
        "w_out_b": w_out[l, BRANCH:].astype(BF16),
        "ssd_conv_w": ssd_conv_w[l],
        "ssd_conv_b": ssd_conv_b[l].reshape(1, -1),
        "ssd_dtp": jnp.stack([jnp.tile(ssd_dt_bias[l].reshape(-1), LANES // (2 * SSD_HEADS)),
                              jnp.tile(ssd_a_log[l].reshape(-1), LANES // (2 * SSD_HEADS))]),
        "ssd_vec": jnp.stack([jnp.repeat(ssd_d[l], HEAD_DIM), ssd_norm_g[l]]),
        "gm_ws": gm_ws[l],
        "gm_bs": jnp.repeat(gm_bs[l].T, HEAD_DIM, axis=1),
        "sc_conv_w": sc_conv_w[l],
        "cf_conv_w": cf_conv_w[l],
        "local_vec": jnp.stack([gm_norm_g[l], gm_norm_b[l], cf_conv_b[l], cf_norm_g[l], cf_norm_b[l]]),
    }


def kernel(x, c, ctx, c_ctx, mod_w, mod_b, norm1_g, norm2_g, w_in, w_out, ssd_conv_w, ssd_conv_b, ssd_dt_bias, ssd_a_log, ssd_d, ssd_norm_g, gm_norm_g, gm_norm_b, gm_ws, gm_bs, sc_conv_w, cf_conv_w, cf_conv_b, cf_norm_g, cf_norm_b, ffn_w1, ffn_w3, ffn_w2, moe_router, moe_w1, moe_w3, moe_w2, final_norm_g):
    bsz, seq, _ = x.shape
    ctx_len = ctx.shape[1]
    depth = w_in.shape[0]
    n_x, n_c = bsz * seq, bsz * ctx_len
    tm = 512

    mod = _modulation(jnp.concatenate([c, c_ctx[None, :]], axis=0), mod_w, mod_b)
    x_map = dict(tm=tm, rows_per_mod=seq, mod_base=0)
    c_map = dict(tm=tm, rows_per_mod=n_c, mod_base=bsz)
    zero_state = jnp.zeros((bsz, 2, SSD_GROUPS, SSD_STATE, 2 * HEAD_DIM), F32)

    xs = x.reshape(n_x, D_MODEL)
    xc = ctx.reshape(n_c, D_MODEL)
    for l in range(depth):
        last = l == depth - 1
        lp = _layer_params(l, w_in, w_out, ssd_conv_w, ssd_conv_b, ssd_dt_bias, ssd_a_log, ssd_d, ssd_norm_g,
                           gm_norm_g, gm_norm_b, gm_ws, gm_bs, sc_conv_w, cf_conv_w, cf_conv_b, cf_norm_g,
                           cf_norm_b)
        g1 = norm1_g[l].reshape(1, D_MODEL)
        g2 = norm2_g[l].reshape(1, D_MODEL)
        zc_ssd, yc_loc, dtc = _token_mixer_front(xc, mod[l], g1, lp, tm=min(tm, ctx_len), seq=ctx_len,
                                                 rows_per_mod=n_c, mod_base=bsz)
        yc_ssd, states = _ssd_mixer(zc_ssd.reshape(bsz, ctx_len, W_SSD), dtc.reshape(bsz, ctx_len, LANES),
                                    zero_state, lp)
        z_ssd, y_loc, dtx = _token_mixer_front(xs, mod[l], g1, lp, tm=tm, seq=seq, rows_per_mod=seq, mod_base=0)
        y_ssd, _ = _ssd_mixer(z_ssd.reshape(bsz, seq, W_SSD), dtx.reshape(bsz, seq, LANES), states, lp)
        streams = [(xs, y_ssd.reshape(n_x, BRANCH), y_loc, x_map)]
        if not last:
            streams.append((xc, yc_ssd.reshape(n_c, BRANCH), yc_loc, c_map))
        fg = final_norm_g.reshape(1, D_MODEL) if last else None
        i = l // 2
        if l % 2 == 0:
            w1, w3, w2 = ffn_w1[i].astype(BF16), ffn_w3[i].astype(BF16), ffn_w2[i].astype(BF16)
            outs = [_dense_channel_mixer(t, mod[l], ya, yb, lp["w_out_a"], lp["w_out_b"], g2, w1, w3, w2,
                                         final_g=fg if t is xs else None, **m) for t, ya, yb, m in streams]
        else:
            w1, w3, w2 = moe_w1[i].astype(BF16), moe_w3[i].astype(BF16), moe_w2[i].astype(BF16)
            r_hi = moe_router[i].astype(BF16)
            r_lo = (moe_router[i] - r_hi.astype(F32)).astype(BF16)
            r_pad = jnp.pad(jnp.concatenate([r_hi, r_lo], axis=1), ((0, 0), (0, LANES - 2 * N_EXPERTS)))
            outs = [_moe_channel_mixer(t, mod[l], ya, yb, lp["w_out_a"], lp["w_out_b"], g2, r_pad, w1, w3, w2,
                                       rows_per_mod=m["rows_per_mod"], mod_base=m["mod_base"],
                                       final_g=fg if t is xs else None) for t, ya, yb, m in streams]
        xs = outs[0]
        if not last:
            xc = outs[1]
    return xs.reshape(bsz, seq, D_MODEL)
```

```python
import functools

import jax
import jax.numpy as jnp
from jax import lax
from jax.experimental import pallas as pl
from jax.experimental.pallas import tpu as pltpu

F32 = jnp.float32
BF16 = jnp.bfloat16

LANES = 128
SUBLANES = 8
VMEM_LIMIT_BYTES = 56 * 1024 * 1024

D_MODEL = 1024
BRANCH = 256
HEAD_DIM = 64
SSD_HEADS = 4
SSD_GROUPS = 2
SSD_STATE = 128
CHUNK = 128
SSD_SEQS_PER_STEP = 2
SSD_CONV_CH = BRANCH + 2 * SSD_GROUPS * SSD_STATE
N_EXPERTS = 8
TOP_K = 2
TOK_TILE = 256
ROW_BLOCK = 512
SUB_BLOCK = 256
N_SUB = ROW_BLOCK // SUB_BLOCK
GATHER_TILES = 11
WINDOW_TILES = 5
ROUTER_TM = 512
COMBINE_TM = 512
COMBINE_WIN = 256
HALO = 16
RMS_EPS = 1e-6
LN_EPS = 1e-5

_C_XBC = 0
_C_DT = _C_XBC + SSD_CONV_CH
_C_Z = _C_DT + 2 * SSD_HEADS
_C_UV = _C_Z + BRANCH
_C_SC = _C_UV + 2 * BRANCH
_C_CF = _C_SC + 3 * BRANCH
_C_END = _C_CF + 2 * BRANCH
W_SSD = SSD_CONV_CH + BRANCH
W_UVCF = 4 * BRANCH
W_SC = 3 * BRANCH
W_MAIN = W_SSD + W_UVCF + W_SC


def _params(*sem):
    return pltpu.CompilerParams(dimension_semantics=sem, vmem_limit_bytes=VMEM_LIMIT_BYTES)


def _sigmoid(v):
    return 0.5 + 0.5 * jnp.tanh(0.5 * v)


def _silu(v):
    half = 0.5 * v
    return half + half * jnp.tanh(half)


def _softplus(v):
    return jnp.maximum(v, 0.0) + jnp.log1p(jnp.exp(-jnp.abs(v)))


def _gelu_tanh(v):
    return 0.5 * v * (1.0 + jnp.tanh(0.7978845608028654 * (v + 0.044715 * (v * v * v))))


def _norm_modulate(x, g, shift, scale):
    ms = jnp.mean(x * x, axis=-1, keepdims=True)
    return (x * lax.rsqrt(ms + RMS_EPS) * g) * (1.0 + scale) + shift


def _mod_kernel(c_ref, w_ref, b_ref, o_ref):
    o_ref[0, 0] = jnp.dot(_silu(c_ref[...]), w_ref[0], preferred_element_type=F32,
                          precision=lax.Precision.HIGHEST) + b_ref[0, 0]


def _modulation(c_all, mod_w, mod_b):
    depth = mod_w.shape[0]
    rows = c_all.shape[0]
    out = pl.pallas_call(
        _mod_kernel,
        grid=(depth, 6),
        in_specs=[
            pl.BlockSpec((rows, D_MODEL), lambda l, j: (0, 0)),
            pl.BlockSpec((1, D_MODEL, D_MODEL), lambda l, j: (l, 0, j)),
            pl.BlockSpec((1, 1, 1, D_MODEL), lambda l, j: (l, j, 0, 0)),
        ],
        out_specs=pl.BlockSpec((1, 1, rows, D_MODEL), lambda l, j: (l, j, 0, 0)),
        out_shape=jax.ShapeDtypeStruct((depth, 6, rows, D_MODEL), F32),
        compiler_params=_params("arbitrary", "arbitrary"),
        name="modulation",
    )(c_all, mod_w, mod_b.reshape(depth, 6, 1, D_MODEL))
    return jnp.transpose(out, (0, 2, 1, 3))


def _mod_index(tm, rows_per_mod, mod_base):
    return lambda i: (mod_base + (i * tm) // rows_per_mod, 0, 0)


def _fill_stage(stage_ref, prev, main, nxt):
    rows = main.shape[0]
    stage_ref[0:HALO, :] = prev
    stage_ref[HALO:HALO + rows, :] = main
    stage_ref[HALO + rows:2 * HALO + rows, :] = nxt


def _dwconv_rows(stage_ref, w_ref, taps, r, rows, c0, width):
    acc = None
    for k in range(taps):
        lo = HALO + r + k - taps // 2
        term = stage_ref[lo:lo + rows, c0:c0 + width] * w_ref[k:k + 1, c0:c0 + width]
        acc = term if acc is None else acc + term
    return acc


def _front_kernel(x_ref, xp_ref, xn_ref, mod_ref, g_ref, w_ref, wdt_ref, convw_ref, convb_ref,
                  ws_ref, bs_ref, scw_ref, cfw_ref, vec_ref,
                  zssd_ref, yloc_ref, dt_ref, ssd_stage, sc_stage, cf_stage, rot_ref, *, tm, seq):
    i = pl.program_id(0)
    tiles_per_seq = seq // tm
    keep_prev = jnp.where(i % tiles_per_seq == 0, 0.0, 1.0)
    keep_next = jnp.where(i % tiles_per_seq == tiles_per_seq - 1, 0.0, 1.0)
    g, shift, scale = g_ref[...], mod_ref[0, 0:1, :], mod_ref[0, 1:2, :]
    h = _norm_modulate(x_ref[...], g, shift, scale).astype(BF16)
    h_halo = _norm_modulate(jnp.concatenate([xp_ref[...], xn_ref[...]], axis=0), g, shift, scale).astype(BF16)
    z = jnp.dot(h, w_ref[...], preferred_element_type=F32)
    z_halo = jnp.dot(h_halo, w_ref[...], preferred_element_type=F32)
    zp = z_halo[0:HALO] * keep_prev
    zn = z_halo[HALO:2 * HALO] * keep_next
    dt_ref[...] = jnp.dot(h, wdt_ref[...], preferred_element_type=F32)

    _fill_stage(ssd_stage, zp[:, 0:SSD_CONV_CH], z[:, 0:SSD_CONV_CH], zn[:, 0:SSD_CONV_CH])
    for r in range(0, tm, CHUNK):
        for c0 in range(0, SSD_CONV_CH, BRANCH):
            v = _dwconv_rows(ssd_stage, convw_ref, 5, r, CHUNK, c0, BRANCH) + convb_ref[0:1, c0:c0 + BRANCH]
            zssd_ref[r:r + CHUNK, c0:c0 + BRANCH] = _silu(v).astype(BF16)
    zssd_ref[:, SSD_CONV_CH:W_SSD] = z[:, SSD_CONV_CH:W_SSD].astype(BF16)

    lane = lax.broadcasted_iota(jnp.int32, (CHUNK, BRANCH), 1)
    gm_g, gm_b = vec_ref[0:1, :], vec_ref[1:2, :]
    cf_b, cf_g, cf_nb = vec_ref[2:3, :], vec_ref[3:4, :], vec_ref[4:5, :]
    c_uv, c_cf, c_sc = W_SSD, W_SSD + 2 * BRANCH, W_SSD + W_UVCF

    def layer_norm(v, gain, bias):
        mu = jnp.mean(v, axis=-1, keepdims=True)
        var = jnp.mean(jnp.square(v - mu), axis=-1, keepdims=True)
        return (v - mu) * lax.rsqrt(var + LN_EPS) * gain + bias

    def sc_in(t):
        return t[:, c_sc + BRANCH:c_sc + 2 * BRANCH] * t[:, c_sc + 2 * BRANCH:c_sc + 3 * BRANCH]

    def cf_in(t):
        return t[:, c_cf:c_cf + BRANCH] * _sigmoid(t[:, c_cf + BRANCH:c_cf + 2 * BRANCH])

    _fill_stage(sc_stage, sc_in(zp), sc_in(z), sc_in(zn))
    _fill_stage(cf_stage, cf_in(zp), cf_in(z), cf_in(zn))
    span = tm + 2 * HALO - SUBLANES
    for phase in range(SUBLANES):
        rot_ref[phase] = cf_stage[phase:phase + span, :]

    for r in range(0, tm, CHUNK):
        ge = _gelu_tanh(z[r:r + CHUNK, c_uv:c_uv + 2 * BRANCH])
        u = ge[:, 0:BRANCH]
        v = layer_norm(ge[:, BRANCH:2 * BRANCH], gm_g, gm_b)
        s = bs_ref[...]
        for hd in range(BRANCH // HEAD_DIM):
            in_head = (lane >= hd * HEAD_DIM) & (lane < (hd + 1) * HEAD_DIM)
            s = s + jnp.dot(ws_ref[hd].astype(BF16), jnp.where(in_head, v, 0.0).astype(BF16),
                            preferred_element_type=F32)
        yloc_ref[r:r + CHUNK, 0:BRANCH] = (u * s).astype(BF16)
        gate = z[r:r + CHUNK, c_sc:c_sc + BRANCH]
        yloc_ref[r:r + CHUNK, BRANCH:2 * BRANCH] = (gate * _dwconv_rows(sc_stage, scw_ref, 3, r, CHUNK, 0, BRANCH)
                                                    ).astype(BF16)
        cv = cf_b
        for k in range(31):
            lo = HALO + r + k - 31 // 2
            q8 = (lo // SUBLANES) * SUBLANES
            cv = cv + rot_ref[lo % SUBLANES, q8:q8 + CHUNK, :] * cfw_ref[k:k + 1, :]
        yloc_ref[r:r + CHUNK, 2 * BRANCH:3 * BRANCH] = _silu(layer_norm(cv, cf_g, cf_nb)).astype(BF16)


def _token_mixer_front(x2d, mod, g, lp, *, tm, seq, rows_per_mod, mod_base):
    n = x2d.shape[0]
    per = tm // HALO
    row = lambda i: (i, 0)
    const = lambda i: (0, 0)
    span = tm + 2 * HALO
    return pl.pallas_call(
        functools.partial(_front_kernel, tm=tm, seq=seq),
        grid=(n // tm,),
        in_specs=[
            pl.BlockSpec((tm, D_MODEL), row),
            pl.BlockSpec((HALO, D_MODEL), lambda i: (jnp.maximum(i * per - 1, 0), 0)),
            pl.BlockSpec((HALO, D_MODEL), lambda i: (jnp.minimum((i + 1) * per, n // HALO - 1), 0)),
            pl.BlockSpec((1, 6, D_MODEL), _mod_index(tm, rows_per_mod, mod_base)),
            pl.BlockSpec((1, D_MODEL), const),
            pl.BlockSpec((D_MODEL, W_MAIN), const),
            pl.BlockSpec((D_MODEL, LANES), const),
            pl.BlockSpec((5, SSD_CONV_CH), const),
            pl.BlockSpec((1, SSD_CONV_CH), const),
            pl.BlockSpec((BRANCH // HEAD_DIM, CHUNK, CHUNK), lambda i: (0, 0, 0)),
            pl.BlockSpec((CHUNK, BRANCH), const),
            pl.BlockSpec((3, BRANCH), const),
            pl.BlockSpec((31, BRANCH), const),
            pl.BlockSpec((5, BRANCH), const),
        ],
        out_specs=[
            pl.BlockSpec((tm, W_SSD), row),
            pl.BlockSpec((tm, 3 * BRANCH), row),
            pl.BlockSpec((tm, LANES), row),
        ],
        out_shape=[
            jax.ShapeDtypeStruct((n, W_SSD), BF16),
            jax.ShapeDtypeStruct((n, 3 * BRANCH), BF16),
            jax.ShapeDtypeStruct((n, LANES), F32),
        ],
        scratch_shapes=[
            pltpu.VMEM((span, SSD_CONV_CH), F32),
            pltpu.VMEM((span, BRANCH), F32),
            pltpu.VMEM((span, BRANCH), F32),
            pltpu.VMEM((SUBLANES, span - SUBLANES, BRANCH), F32),
        ],
        compiler_params=_params("parallel"),
        name="token_mixer_front",
    )(x2d, x2d, x2d, mod, g, lp["w_main"], lp["w_dt"], lp["ssd_conv_w"], lp["ssd_conv_b"],
      lp["gm_ws"], lp["gm_bs"], lp["sc_conv_w"], lp["cf_conv_w"], lp["local_vec"])


def _ssd_kernel(z_ref, dt_ref, init_ref, dtp_ref, vec_ref,
                y_ref, fin_ref, p_ref, dts_ref, pt_ref, dtt_ref, yacc_ref, st_ref, *, seq, n_seq):
    nc = seq // CHUNK
    lane = lax.broadcasted_iota(jnp.int32, (CHUNK, LANES), 1)
    row_i = lax.broadcasted_iota(jnp.int32, (CHUNK, CHUNK), 0)
    col_i = lax.broadcasted_iota(jnp.int32, (CHUNK, CHUNK), 1)
    lower = col_i <= row_i
    upper = col_i >= row_i
    head0 = lane < HEAD_DIM

    fwd_lane = (lane % (2 * SSD_HEADS)) < SSD_HEADS
    for q in range(n_seq):
        dt_all = _softplus(dt_ref[q] + dtp_ref[0:1, :])
        adt = dt_all * (-jnp.exp(dtp_ref[1:2, :]))
        p_all = (jnp.dot(lower.astype(F32), jnp.where(fwd_lane, adt, 0.0), preferred_element_type=F32,
                         precision=lax.Precision.HIGHEST)
                 + jnp.dot(upper.astype(F32), jnp.where(fwd_lane, 0.0, adt), preferred_element_type=F32,
                           precision=lax.Precision.HIGHEST))
        pt_ref[q] = p_all.T
        dtt_ref[q] = dt_all.T
        for c in range(nc):
            shift = (LANES - 2 * SSD_HEADS * c) % LANES
            p_ref[q, c * CHUNK:(c + 1) * CHUNK, :] = pltpu.roll(p_all, shift, axis=1) if shift else p_all
            dts_ref[q, c * CHUNK:(c + 1) * CHUNK, :] = pltpu.roll(dt_all, shift, axis=1) if shift else dt_all

    def group_inputs(q, rows, g):
        xg = z_ref[q, rows, g * LANES:(g + 1) * LANES].astype(F32)
        bg = z_ref[q, rows, BRANCH + g * SSD_STATE:BRANCH + (g + 1) * SSD_STATE]
        cg = z_ref[q, rows, BRANCH + (SSD_GROUPS + g) * SSD_STATE:BRANCH + (SSD_GROUPS + g + 1) * SSD_STATE]
        return xg, bg, cg

    def spread(m, j):
        return jnp.broadcast_to(m[:, j:j + 1], (CHUNK, LANES))

    def pair(a0, a1):
        return jnp.where(head0, a0, a1)

    st_ref[...] = init_ref[:, 0]

    def fwd_chunk(q, c):
        r0 = pl.multiple_of(c * CHUNK, CHUNK)
        rows = pl.ds(r0, CHUNK)
        pm = p_ref[q, rows, :]
        dt = dts_ref[q, rows, :]
        head_rows = pl.ds(pl.multiple_of(c * 2 * SSD_HEADS, 2 * SSD_HEADS), 2 * SSD_HEADS)
        pm_t = pt_ref[q, head_rows, :]
        dt_t = dtt_ref[q, head_rows, :]
        last = pm[CHUNK - 1:CHUNK, :]
        to_end = dt * jnp.exp(last - pm)
        for g in range(SSD_GROUPS):
            xg, bg, cg = group_inputs(q, rows, g)
            gram = lax.dot_general(cg, bg, (((1,), (1,)), ((), ())), preferred_element_type=F32)
            s_prev = st_ref[q, g]
            y = jnp.dot(cg, s_prev.astype(BF16), preferred_element_type=F32)
            hf0, hf1 = 2 * g, 2 * g + 1
            hb0, hb1 = SSD_HEADS + hf0, SSD_HEADS + hf1
            p_f0, p_f1 = spread(pm, hf0), spread(pm, hf1)
            y = y * jnp.exp(pair(p_f0, p_f1))
            for p_f, hf, hb, mask in ((p_f0, hf0, hb0, head0), (p_f1, hf1, hb1, jnp.logical_not(head0))):
                dec_f = jnp.exp(jnp.where(lower, p_f - pm_t[hf:hf + 1, :], -jnp.inf))
                dec_b = jnp.exp(jnp.where(upper, spread(pm, hb) - pm_t[hb:hb + 1, :], -jnp.inf))
                m = gram * (dec_f * dt_t[hf:hf + 1, :] + dec_b * dt_t[hb:hb + 1, :])
                xh = jnp.where(mask, xg, 0.0).astype(BF16)
                y = y + jnp.dot(m.astype(BF16), xh, preferred_element_type=F32)
            yacc_ref[q, rows, g * LANES:(g + 1) * LANES] = y
            w_end = pair(spread(to_end, hf0), spread(to_end, hf1))
            upd = lax.dot_general(bg, (xg * w_end).astype(BF16), (((0,), (0,)), ((), ())),
                                  preferred_element_type=F32)
            decay = jnp.exp(jnp.where(head0[0:1, :], last[:, hf0:hf0 + 1], last[:, hf1:hf1 + 1]))
            st_ref[q, g] = s_prev * decay + upd

    def fwd(c, carry):
        for q in range(n_seq):
            fwd_chunk(q, c)
        return carry

    lax.fori_loop(0, nc, fwd, 0)
    fin_ref[:, 0] = st_ref[...]

    st_ref[...] = init_ref[:, 1]
    d_skip = vec_ref[0:1, :]
    norm_g = vec_ref[1:2, :]

    def bwd_chunk(q, c):
        r0 = pl.multiple_of(c * CHUNK, CHUNK)
        rows = pl.ds(r0, CHUNK)
        pm = p_ref[q, rows, :]
        dt = dts_ref[q, rows, :]
        first = pm[0:1, :]
        to_end = dt * jnp.exp(first - pm)
        for g in range(SSD_GROUPS):
            xg, bg, cg = group_inputs(q, rows, g)
            hb0, hb1 = SSD_HEADS + 2 * g, SSD_HEADS + 2 * g + 1
            s_prev = st_ref[q, g]
            y = (jnp.dot(cg, s_prev.astype(BF16), preferred_element_type=F32)
                 * jnp.exp(pair(spread(pm, hb0), spread(pm, hb1))))
            yacc_ref[q, rows, g * LANES:(g + 1) * LANES] = (
                yacc_ref[q, rows, g * LANES:(g + 1) * LANES] + y + d_skip[:, g * LANES:(g + 1) * LANES] * xg)
            w_end = pair(spread(to_end, hb0), spread(to_end, hb1))
            upd = lax.dot_general(bg, (xg * w_end).astype(BF16), (((0,), (0,)), ((), ())),
                                  preferred_element_type=F32)
            decay = jnp.exp(jnp.where(head0[0:1, :], first[:, hb0:hb0 + 1], first[:, hb1:hb1 + 1]))
            st_ref[q, g] = s_prev * decay + upd
        gated = yacc_ref[q, rows, :] * _silu(z_ref[q, rows, SSD_CONV_CH:W_SSD].astype(F32))
        ms = jnp.mean(gated * gated, axis=-1, keepdims=True)
        y_ref[q, rows, :] = (gated * lax.rsqrt(ms + RMS_EPS) * norm_g).astype(BF16)

    def bwd(k, carry):
        for q in range(n_seq):
            bwd_chunk(q, nc - 1 - k)
        return carry

    lax.fori_loop(0, nc, bwd, 0)
    fin_ref[:, 1] = st_ref[...]


def _ssd_mixer(z_ssd, dt, init, lp):
    bsz, seq, _ = z_ssd.shape
    nc = seq // CHUNK
    n_hd = 2 * SSD_HEADS
    n_seq = SSD_SEQS_PER_STEP if bsz % SSD_SEQS_PER_STEP == 0 else 1
    assert nc * n_hd <= LANES
    dt = dt[:, :, :n_hd].reshape(bsz, nc, CHUNK, n_hd).transpose(0, 2, 1, 3).reshape(bsz, CHUNK, nc * n_hd)
    dt = jnp.pad(dt, ((0, 0), (0, 0), (0, LANES - nc * n_hd)))
    seq_map = lambda b: (b, 0, 0)
    st_map = lambda b: (b, 0, 0, 0, 0)
    const = lambda b: (0, 0)
    st_shape = (bsz, 2, SSD_GROUPS, SSD_STATE, 2 * HEAD_DIM)
    return pl.pallas_call(
        functools.partial(_ssd_kernel, seq=seq, n_seq=n_seq),
        grid=(bsz // n_seq,),
        in_specs=[
            pl.BlockSpec((n_seq, seq, W_SSD), seq_map),
            pl.BlockSpec((n_seq, CHUNK, LANES), seq_map),
            pl.BlockSpec((n_seq,) + st_shape[1:], st_map),
            pl.BlockSpec((2, LANES), const),
            pl.BlockSpec((2, BRANCH), const),
        ],
        out_specs=[
            pl.BlockSpec((n_seq, seq, BRANCH), seq_map),
            pl.BlockSpec((n_seq,) + st_shape[1:], st_map),
        ],
        out_shape=[
            jax.ShapeDtypeStruct((bsz, seq, BRANCH), BF16),
            jax.ShapeDtypeStruct(st_shape, F32),
        ],
        scratch_shapes=[
            pltpu.VMEM((n_seq, seq, LANES), F32),
            pltpu.VMEM((n_seq, seq, LANES), F32),
            pltpu.VMEM((n_seq, LANES, CHUNK), F32),
            pltpu.VMEM((n_seq, LANES, CHUNK), F32),
            pltpu.VMEM((n_seq, seq, BRANCH), F32),
            pltpu.VMEM((n_seq, SSD_GROUPS, SSD_STATE, 2 * HEAD_DIM), F32),
        ],
        compiler_params=_params("parallel"),
        name="ssd_mixer",
    )(z_ssd, dt, init, lp["ssd_dtp"], lp["ssd_vec"])


def _token_mixed(x_ref, mod_ref, ya_ref, yb_ref, wa_ref, wb_ref):
    proj = (jnp.dot(ya_ref[...], wa_ref[...], preferred_element_type=F32)
            + jnp.dot(yb_ref[...], wb_ref[...], preferred_element_type=F32))
    return x_ref[...] + mod_ref[0, 2:3, :] * proj


def _token_mixed_specs(tm, rows_per_mod, mod_base):
    row = lambda i, *_: (i, 0)
    const = lambda i, *_: (0, 0)
    mod_idx = _mod_index(tm, rows_per_mod, mod_base)
    return [
        pl.BlockSpec((tm, D_MODEL), row),
        pl.BlockSpec((1, 6, D_MODEL), lambda i, *_: mod_idx(i)),
        pl.BlockSpec((tm, BRANCH), row),
        pl.BlockSpec((tm, 3 * BRANCH), row),
        pl.BlockSpec((BRANCH, D_MODEL), const),
        pl.BlockSpec((3 * BRANCH, D_MODEL), const),
    ]


def _residual_out(x, gate, delta, fg_ref):
    out = x + gate * delta
    if fg_ref is not None:
        ms = jnp.mean(out * out, axis=-1, keepdims=True)
        out = out * lax.rsqrt(ms + RMS_EPS) * fg_ref[...]
    return out


def _cast_slices(srcs, dsts):
    for src, dst in zip(srcs, dsts):
        dst[...] = src[...].astype(BF16)


def _cast_specs(casts, steps):
    row = lambda i, *_: (i, 0)
    blocks = [(arr.shape[0] // steps, arr.shape[1]) for arr in casts]
    return ([pl.BlockSpec(b, row) for b in blocks], [pl.BlockSpec(b, row) for b in blocks],
            [jax.ShapeDtypeStruct(arr.shape, BF16) for arr in casts])


def _swiglu_kernel(*refs, final_norm, n_casts):
    x_ref, mod_ref, ya_ref, yb_ref, wa_ref, wb_ref, g_ref, w1_ref, w3_ref, w2_ref = refs[0:10]
    pos = 10
    fg_ref = None
    if final_norm:
        fg_ref = refs[pos]
        pos += 1
    cast_in = refs[pos:pos + n_casts]
    o_ref = refs[pos + n_casts]
    cast_out = refs[pos + n_casts + 1:pos + 2 * n_casts + 1]
    x1 = _token_mixed(x_ref, mod_ref, ya_ref, yb_ref, wa_ref, wb_ref)
    h = _norm_modulate(x1, g_ref[...], mod_ref[0, 3:4, :], mod_ref[0, 4:5, :]).astype(BF16)
    a = jnp.dot(h, w1_ref[...], preferred_element_type=F32)
    b = jnp.dot(h, w3_ref[...], preferred_element_type=F32)
    y = jnp.dot((_silu(a) * b).astype(BF16), w2_ref[...], preferred_element_type=F32)
    o_ref[...] = _residual_out(x1, mod_ref[0, 5:6, :], y, fg_ref)
    _cast_slices(cast_in, cast_out)


def _dense_channel_mixer(x2d, mod, y_ssd, y_local, w_a, w_b, g, w1, w3, w2, *, tm, rows_per_mod, mod_base,
                         final_g=None, casts=()):
    n = x2d.shape[0]
    steps = n // tm
    ff = w1.shape[1]
    const = lambda i: (0, 0)
    row = lambda i: (i, 0)
    resident = dict(pipeline_mode=pl.Buffered(1))
    in_specs = _token_mixed_specs(tm, rows_per_mod, mod_base) + [
        pl.BlockSpec((1, D_MODEL), const),
        pl.BlockSpec((D_MODEL, ff), const, **resident),
        pl.BlockSpec((D_MODEL, ff), const, **resident),
        pl.BlockSpec((ff, D_MODEL), const, **resident),
    ]
    args = [x2d, mod, y_ssd, y_local, w_a, w_b, g, w1, w3, w2]
    if final_g is not None:
        in_specs.append(pl.BlockSpec((1, D_MODEL), const))
        args.append(final_g)
    cast_in, cast_out, cast_shape = _cast_specs(casts, steps)
    in_specs += cast_in
    args += list(casts)
    out_specs = [pl.BlockSpec((tm, D_MODEL), row)] + cast_out
    out_shape = [jax.ShapeDtypeStruct((n, D_MODEL), F32)] + cast_shape
    outs = pl.pallas_call(
        functools.partial(_swiglu_kernel, final_norm=final_g is not None, n_casts=len(casts)),
        grid=(steps,),
        in_specs=in_specs,
        out_specs=out_specs,
        out_shape=out_shape,
        compiler_params=_params("parallel"),
        name="dense_channel_mixer",
    )(*args)
    return outs[0], list(outs[1:])


def _can_cast_in_steps(arr2d, steps):
    return arr2d.shape[0] % steps == 0 and (arr2d.shape[0] // steps) % (2 * SUBLANES) == 0


def _router_kernel(*refs, n_casts):
    x_ref, mod_ref, ya_ref, yb_ref, wa_ref, wb_ref, g_ref, r_ref = refs[0:8]
    cast_in = refs[8:8 + n_casts]
    x1_ref, h_ref, comb_ref, rank_ref, rankt_ref, cnt_ref = refs[8 + n_casts:14 + n_casts]
    cast_out = refs[14 + n_casts:14 + 2 * n_casts]
    carry_ref = refs[-1]
    _cast_slices(cast_in, cast_out)

    @pl.when(pl.program_id(0) == 0)
    def _():
        carry_ref[...] = jnp.zeros_like(carry_ref)

    x1 = _token_mixed(x_ref, mod_ref, ya_ref, yb_ref, wa_ref, wb_ref)
    x1_ref[...] = x1
    h = _norm_modulate(x1, g_ref[...], mod_ref[0, 3:4, :], mod_ref[0, 4:5, :])
    h_hi = h.astype(BF16)
    h_ref[...] = h_hi
    h_lo = (h - h_hi.astype(F32)).astype(BF16)
    parts = jnp.dot(jnp.concatenate([h_hi, h_lo], axis=0), r_ref[...], preferred_element_type=F32)
    parts = parts[0:h.shape[0], :] + parts[h.shape[0]:, :]
    logits = parts + pltpu.roll(parts, LANES - N_EXPERTS, axis=1)
    lane = lax.broadcasted_iota(jnp.int32, logits.shape, 1)
    logits = jnp.where(lane < N_EXPERTS, logits, -jnp.inf)
    m1 = jnp.max(logits, axis=-1, keepdims=True)
    i1 = jnp.min(jnp.where(logits == m1, lane, LANES), axis=-1, keepdims=True)
    rest = jnp.where(lane == i1, -jnp.inf, logits)
    m2 = jnp.max(rest, axis=-1, keepdims=True)
    i2 = jnp.min(jnp.where(rest == m2, lane, LANES), axis=-1, keepdims=True)
    e2 = jnp.exp(m2 - m1)
    denom = 1.0 + e2
    comb_ref[...] = jnp.where(lane == i1, 1.0 / denom, 0.0) + jnp.where(lane == i2, e2 / denom, 0.0)
    is_chosen = (lane == i1) | (lane == i2)
    chosen = jnp.where(is_chosen, 1.0, 0.0)
    tm = chosen.shape[0]
    earlier = lax.broadcasted_iota(jnp.int32, (tm, tm), 1) < lax.broadcasted_iota(jnp.int32, (tm, tm), 0)
    running = carry_ref[...]
    before = jnp.dot(jnp.where(earlier, 1.0, 0.0).astype(BF16), chosen.astype(BF16),
                     preferred_element_type=F32) + running.astype(F32)
    rank = jnp.where(is_chosen, before, -1.0)
    rank_ref[...] = rank
    rankt_ref[...] = rank.T[0:N_EXPERTS, :]
    chosen_i = jnp.where(is_chosen, 1, 0)
    for k in range(tm // TOK_TILE):
        running = running + jnp.sum(chosen_i[k * TOK_TILE:(k + 1) * TOK_TILE, :], axis=0, keepdims=True)
        cnt_ref[0, k:k + 1, :] = running
    carry_ref[...] = running


def _router(x2d, mod, y_ssd, y_local, w_a, w_b, g, router_pad, *, tm, rows_per_mod, mod_base, casts=()):
    n = x2d.shape[0]
    row = lambda i: (i, 0)
    const = lambda i: (0, 0)
    per = tm // TOK_TILE
    cast_in, cast_out, cast_shape = _cast_specs(casts, n // tm)
    return pl.pallas_call(
        functools.partial(_router_kernel, n_casts=len(casts)),
        grid=(n // tm,),
        in_specs=_token_mixed_specs(tm, rows_per_mod, mod_base) + [
            pl.BlockSpec((1, D_MODEL), const),
            pl.BlockSpec((D_MODEL, LANES), const),
        ] + cast_in,
        out_specs=[
            pl.BlockSpec((tm, D_MODEL), row),
            pl.BlockSpec((tm, D_MODEL), row),
            pl.BlockSpec((tm, LANES), row),
            pl.BlockSpec((tm, LANES), row),
            pl.BlockSpec((N_EXPERTS, tm), lambda i: (0, i)),
            pl.BlockSpec((1, per, LANES), lambda i: (i, 0, 0)),
        ] + cast_out,
        out_shape=[
            jax.ShapeDtypeStruct((n, D_MODEL), F32),
            jax.ShapeDtypeStruct((n, D_MODEL), BF16),
            jax.ShapeDtypeStruct((n, LANES), F32),
            jax.ShapeDtypeStruct((n, LANES), F32),
            jax.ShapeDtypeStruct((N_EXPERTS, n), F32),
            jax.ShapeDtypeStruct((n // tm, per, LANES), jnp.int32),
        ] + cast_shape,
        scratch_shapes=[pltpu.VMEM((1, LANES), jnp.int32)],
        compiler_params=_params("arbitrary"),
        name="moe_router",
    )(x2d, mod, y_ssd, y_local, w_a, w_b, g, router_pad, *casts)


def _routing_tables(cnt, n_tokens):
    i32 = jnp.int32
    n_t = n_tokens // TOK_TILE
    cnt_end = cnt.reshape(n_t, LANES)[:, :N_EXPERTS]
    cnt_start = jnp.concatenate([jnp.zeros((1, N_EXPERTS), i32), cnt_end[:-1]], axis=0)
    tot = cnt_end[-1]
    nblk = (tot + ROW_BLOCK - 1) // ROW_BLOCK
    blk_cum = jnp.cumsum(nblk)
    blk_first = blk_cum - nblk
    n_valid = blk_cum[-1]
    nb = n_tokens * TOP_K // ROW_BLOCK + N_EXPERTS
    bi = jnp.arange(nb, dtype=i32)
    valid = bi < n_valid
    e_of = jnp.sum((bi[:, None] >= blk_cum[None, :]).astype(i32), axis=1)
    e_last = jnp.sum(((n_valid - 1) >= blk_cum).astype(i32))
    blk_e = jnp.minimum(jnp.where(valid, e_of, e_last), N_EXPERTS - 1)
    base = jnp.where(valid, (bi - blk_first[blk_e]) * ROW_BLOCK, 0)
    lo = base[:, None] + jnp.arange(N_SUB, dtype=i32)[None, :] * SUB_BLOCK
    hi = jnp.where(valid[:, None], jnp.minimum(lo + SUB_BLOCK, tot[blk_e][:, None]), lo)
    cs = cnt_start[:, blk_e]
    ce = cnt_end[:, blk_e]
    rel = (cs[:, :, None] < hi[None]) & (ce[:, :, None] > lo[None])
    tile_id = jnp.arange(n_t, dtype=i32)[:, None, None]
    t_first = jnp.min(jnp.where(rel, tile_id, n_t), axis=0)
    sub_hi = jnp.max(jnp.where(rel, tile_id, -1), axis=0)
    sub_lo = jnp.where(sub_hi >= 0, t_first, 0)
    u_lo = jnp.min(t_first, axis=1)
    u_n = jnp.maximum(jnp.max(sub_hi, axis=1) - u_lo + 1, 0)
    u_lo = jnp.where(u_n > 0, u_lo, 0)
    per = COMBINE_TM // TOK_TILE
    cs_c = cnt_start[0::per]
    cn_c = cnt_end[per - 1::per] - cs_c
    return dict(blk_e=blk_e.astype(i32), valid=valid.astype(i32), base=base.astype(i32),
                u_lo=u_lo.astype(i32), u_n=u_n.astype(i32),
                sub_lo=sub_lo.reshape(-1).astype(i32), sub_hi=sub_hi.reshape(-1).astype(i32),
                seg_rows=(blk_first * ROW_BLOCK).astype(i32),
                cs_c=cs_c.reshape(-1).astype(i32), cn_c=cn_c.reshape(-1).astype(i32), n_blocks=nb)


def _experts_kernel(be_ref, bv_ref, base_ref, ulo_ref, un_ref, slo_ref, shi_ref,
                    h_hbm, rt_hbm, w1_ref, w3_ref, w2_ref, y_ref, hbuf, rbuf, hsem, rsem):
    i = pl.program_id(0)
    slot = i % 2

    def tile_copies(s, k, t):
        tok = pl.multiple_of(t * TOK_TILE, TOK_TILE)
        return (pltpu.make_async_copy(h_hbm.at[pl.ds(tok, TOK_TILE)], hbuf.at[s, k], hsem.at[s, k]),
                pltpu.make_async_copy(rt_hbm.at[:, pl.ds(tok, TOK_TILE)], rbuf.at[s, k], rsem.at[s, k]))

    def for_block_tiles(b, s, action):
        for k in range(GATHER_TILES):
            @pl.when(k < un_ref[b])
            def _():
                for cp in tile_copies(s, k, ulo_ref[b] + k):
                    action(cp)

    @pl.when(i == 0)
    def _():
        def clear(k, carry):
            hbuf[k // (GATHER_TILES + 1), k % (GATHER_TILES + 1)] = jnp.zeros((TOK_TILE, D_MODEL), BF16)
            rbuf[k // (GATHER_TILES + 1), k % (GATHER_TILES + 1)] = jnp.zeros((N_EXPERTS, TOK_TILE), F32)
            return carry

        lax.fori_loop(0, 2 * (GATHER_TILES + 1), clear, 0)
        for_block_tiles(0, 0, lambda cp: cp.start())

    for_block_tiles(i, slot, lambda cp: cp.wait())

    @pl.when(i + 1 < pl.num_programs(0))
    def _():
        for_block_tiles(i + 1, 1 - slot, lambda cp: cp.start())

    @pl.when(bv_ref[i] == 1)
    def _():
        e = be_ref[i]
        row_id = lax.broadcasted_iota(jnp.int32, (SUB_BLOCK, TOK_TILE), 0).astype(F32)
        expert_row = lax.broadcasted_iota(jnp.int32, (N_EXPERTS, TOK_TILE), 0) == e
        u_lo = ulo_ref[i]
        selected = []

        def tile_weights(ranks8, first_rank):
            ranks = jnp.sum(jnp.where(expert_row, ranks8, 0.0), axis=0, keepdims=True)
            return jnp.where(ranks == first_rank + row_id, 1.0, 0.0).astype(BF16)

        for sb in range(N_SUB):
            first_rank = (base_ref[i] + sb * SUB_BLOCK).astype(F32)
            t_lo, t_hi = slo_ref[N_SUB * i + sb], shi_ref[N_SUB * i + sb]
            k0 = jnp.clip(t_lo - u_lo, 0, GATHER_TILES + 1 - WINDOW_TILES)
            picks = []
            for j in range(WINDOW_TILES):
                t = u_lo + k0 + j
                live = (t >= t_lo) & (t <= t_hi) & (k0 + j < GATHER_TILES)
                picks.append(tile_weights(jnp.where(live, rbuf[slot, k0 + j], -1.0), first_rank))
            window = hbuf[slot, pl.ds(k0, WINDOW_TILES)].reshape(WINDOW_TILES * TOK_TILE, D_MODEL)
            rows = jnp.dot(jnp.concatenate(picks, axis=1), window, preferred_element_type=F32)

            def late_tile(t, carry):
                k = t - u_lo
                kk = jnp.minimum(k, GATHER_TILES)

                @pl.when(k >= GATHER_TILES)
                def _():
                    tok = pl.multiple_of(t * TOK_TILE, TOK_TILE)
                    pltpu.sync_copy(h_hbm.at[pl.ds(tok, TOK_TILE)], hbuf.at[slot, GATHER_TILES])
                    pltpu.sync_copy(rt_hbm.at[:, pl.ds(tok, TOK_TILE)], rbuf.at[slot, GATHER_TILES])

                return carry + jnp.dot(tile_weights(rbuf[slot, kk], first_rank), hbuf[slot, kk],
                                       preferred_element_type=F32)

            t_next = u_lo + jnp.minimum(k0 + WINDOW_TILES, GATHER_TILES)
            rows = lax.fori_loop(jnp.maximum(t_next, t_lo), t_hi + 1, late_tile, rows)
            selected.append(rows.astype(BF16))

        hg = jnp.concatenate(selected, axis=0)
        half = w1_ref.shape[2] // 2
        y = None
        for c in range(2):
            a = jnp.dot(hg, w1_ref[0, :, c * half:(c + 1) * half], preferred_element_type=F32)
            b = jnp.dot(hg, w3_ref[0, :, c * half:(c + 1) * half], preferred_element_type=F32)
            part = jnp.dot((_silu(a) * b).astype(BF16), w2_ref[0, c * half:(c + 1) * half, :],
                           preferred_element_type=F32)
            y = part if y is None else y + part
        y_ref[...] = y.astype(BF16)

    @pl.when(bv_ref[i] == 0)
    def _():
        y_ref[...] = jnp.zeros_like(y_ref)


def _moe_experts(h2, rank_t, w1, w3, w2, tb):
    nb = tb["n_blocks"]
    ff = w1.shape[2]
    w_map = lambda i, be, *_: (be[i], 0, 0)
    grid_spec = pltpu.PrefetchScalarGridSpec(
        num_scalar_prefetch=7,
        grid=(nb,),
        in_specs=[
            pl.BlockSpec(memory_space=pl.ANY),
            pl.BlockSpec(memory_space=pl.ANY),
            pl.BlockSpec((1, D_MODEL, ff), w_map, pipeline_mode=pl.Buffered(1)),
            pl.BlockSpec((1, D_MODEL, ff), w_map, pipeline_mode=pl.Buffered(1)),
            pl.BlockSpec((1, ff, D_MODEL), w_map, pipeline_mode=pl.Buffered(1)),
        ],
        out_specs=pl.BlockSpec((ROW_BLOCK, D_MODEL), lambda i, *_: (i, 0)),
        scratch_shapes=[
            pltpu.VMEM((2, GATHER_TILES + 1, TOK_TILE, D_MODEL), BF16),
            pltpu.VMEM((2, GATHER_TILES + 1, N_EXPERTS, TOK_TILE), F32),
            pltpu.SemaphoreType.DMA((2, GATHER_TILES)),
            pltpu.SemaphoreType.DMA((2, GATHER_TILES)),
        ],
    )
    return pl.pallas_call(
        _experts_kernel,
        grid_spec=grid_spec,
        out_shape=jax.ShapeDtypeStruct((nb * ROW_BLOCK, D_MODEL), BF16),
        compiler_params=_params("arbitrary"),
        name="moe_experts",
    )(tb["blk_e"], tb["valid"], tb["base"], tb["u_lo"], tb["u_n"], tb["sub_lo"], tb["sub_hi"],
      h2, rank_t, w1, w3, w2)


def _combine_kernel(*refs, final_norm, n_rows):
    cs_ref, cn_ref, seg_ref, x_ref, mod_ref, comb_ref, rank_ref, y_hbm = refs[0:8]
    fg_ref = refs[8] if final_norm else None
    o_ref, ybuf, ysem, acc_ref = refs[-4:]
    i = pl.program_id(0)
    slot = i % 2
    extra = N_EXPERTS

    def window(b, e):
        lo = seg_ref[e] + cs_ref[b * N_EXPERTS + e]
        cnt = cn_ref[b * N_EXPERTS + e]
        first = (lo // 16) * 16
        n_win = jnp.where(cnt > 0, (lo + cnt - first + COMBINE_WIN - 1) // COMBINE_WIN, 0)
        return first, n_win

    def win_start(first, j):
        return pl.multiple_of(jnp.minimum(first + j * COMBINE_WIN, n_rows - COMBINE_WIN), 16)

    def for_tile_windows(b, s, action):
        for e in range(N_EXPERTS):
            first, n_win = window(b, e)

            @pl.when(n_win > 0)
            def _():
                action(pltpu.make_async_copy(y_hbm.at[pl.ds(win_start(first, 0), COMBINE_WIN)],
                                             ybuf.at[s, e], ysem.at[s, e]))

    @pl.when(i == 0)
    def _():
        def clear(k, carry):
            ybuf[k // (N_EXPERTS + 1), k % (N_EXPERTS + 1)] = jnp.zeros((COMBINE_WIN, D_MODEL), BF16)
            return carry

        lax.fori_loop(0, 2 * (N_EXPERTS + 1), clear, 0)
        for_tile_windows(0, 0, lambda cp: cp.start())

    for_tile_windows(i, slot, lambda cp: cp.wait())

    @pl.when(i + 1 < pl.num_programs(0))
    def _():
        for_tile_windows(i + 1, 1 - slot, lambda cp: cp.start())

    col_id = lax.broadcasted_iota(jnp.int32, (COMBINE_TM, COMBINE_WIN), 1).astype(F32)

    def weights(e, first, j):
        pos_col = rank_ref[:, e:e + 1] + seg_ref[e].astype(F32)
        start = win_start(first, j).astype(F32)
        return jnp.where(pos_col == start + col_id, comb_ref[:, e:e + 1], 0.0).astype(BF16)

    firsts = [window(i, e) for e in range(N_EXPERTS)]
    pick_all = jnp.concatenate([weights(e, firsts[e][0], 0) for e in range(N_EXPERTS)], axis=1)
    y_all = ybuf[slot, 0:N_EXPERTS].reshape(N_EXPERTS * COMBINE_WIN, D_MODEL)
    acc_ref[...] = jnp.dot(pick_all, y_all, preferred_element_type=F32)

    for e in range(N_EXPERTS):
        first, n_win = firsts[e]

        def more(j, carry):
            pltpu.sync_copy(y_hbm.at[pl.ds(win_start(first, j), COMBINE_WIN)], ybuf.at[slot, extra])
            acc_ref[...] += jnp.dot(weights(e, first, j), ybuf[slot, extra], preferred_element_type=F32)
            return carry

        lax.fori_loop(1, jnp.maximum(n_win, 1), more, 0)

    o_ref[...] = _residual_out(x_ref[...], mod_ref[0, 5:6, :], acc_ref[...], fg_ref)


def _moe_combine(x2d, mod, comb, rank, y_sorted, tb, *, rows_per_mod, mod_base, final_g=None):
    n = x2d.shape[0]
    tm = COMBINE_TM
    row = lambda i, *_: (i, 0)
    const = lambda i, *_: (0, 0)
    mod_idx = _mod_index(tm, rows_per_mod, mod_base)
    in_specs = [
        pl.BlockSpec((tm, D_MODEL), row),
        pl.BlockSpec((1, 6, D_MODEL), lambda i, *_: mod_idx(i)),
        pl.BlockSpec((tm, LANES), row),
        pl.BlockSpec((tm, LANES), row),
        pl.BlockSpec(memory_space=pl.ANY),
    ]
    args = [x2d, mod, comb, rank, y_sorted]
    if final_g is not None:
        in_specs.append(pl.BlockSpec((1, D_MODEL), const))
        args.append(final_g)
    grid_spec = pltpu.PrefetchScalarGridSpec(
        num_scalar_prefetch=3,
        grid=(n // tm,),
        in_specs=in_specs,
        out_specs=pl.BlockSpec((tm, D_MODEL), row),
        scratch_shapes=[
            pltpu.VMEM((2, N_EXPERTS + 1, COMBINE_WIN, D_MODEL), BF16),
            pltpu.SemaphoreType.DMA((2, N_EXPERTS)),
            pltpu.VMEM((tm, D_MODEL), F32),
        ],
    )
    return pl.pallas_call(
        functools.partial(_combine_kernel, final_norm=final_g is not None, n_rows=y_sorted.shape[0]),
        grid_spec=grid_spec,
        out_shape=jax.ShapeDtypeStruct((n, D_MODEL), F32),
        compiler_params=_params("arbitrary"),
        name="moe_combine",
    )(tb["cs_c"], tb["cn_c"], tb["seg_rows"], *args)


def _to_bf16_in(w, steps):
    if w.dtype == BF16:
        return None, w
    flat = w.reshape(-1, w.shape[-1])
    if _can_cast_in_steps(flat, steps):
        return flat, None
    return None, w.astype(BF16)


def _moe_channel_mixer(x2d, mod, y_ssd, y_local, w_a, w_b, g, router_pad, w1, w3, w2, *, rows_per_mod, mod_base,
                       final_g=None):
    w1 = w1.astype(BF16)
    w3 = w3.astype(BF16)
    w2_flat, w2_bf16 = _to_bf16_in(w2, x2d.shape[0] // ROUTER_TM)
    x1, h2, comb, rank, rank_t, cnt, *cast = _router(
        x2d, mod, y_ssd, y_local, w_a, w_b, g, router_pad, tm=ROUTER_TM, rows_per_mod=rows_per_mod,
        mod_base=mod_base, casts=() if w2_flat is None else (w2_flat,))
    if w2_bf16 is None:
        w2_bf16 = cast[0].reshape(w2.shape)
    w2 = w2_bf16
    tb = _routing_tables(cnt, x2d.shape[0])
    y_sorted = _moe_experts(h2, rank_t, w1, w3, w2, tb)
    return _moe_combine(x1, mod, comb, rank, y_sorted, tb, rows_per_mod=rows_per_mod, mod_base=mod_base,
                        final_g=final_g)


def _layer_params(l, w_in, w_out, ssd_conv_w, ssd_conv_b, ssd_dt_bias, ssd_a_log, ssd_d, ssd_norm_g,
                  gm_norm_g, gm_norm_b, gm_ws, gm_bs, sc_conv_w, cf_conv_w, cf_conv_b, cf_norm_g, cf_norm_b):
    wi = w_in[l]
    w_main = jnp.concatenate(
        [wi[:, _C_XBC:_C_DT], wi[:, _C_Z:_C_UV], wi[:, _C_UV:_C_SC], wi[:, _C_CF:_C_END], wi[:, _C_SC:_C_CF]],
        axis=1).astype(BF16)
    w_dt = jnp.pad(wi[:, _C_DT:_C_Z], ((0, 0), (0, LANES - 2 * SSD_HEADS))).astype(BF16)
    return {
        "w_main": w_main,
        "w_dt": w_dt,
        "w_out_a": w_out[l, 0:BRANCH].astype(BF16),
        "w_out_b": w_out[l, BRANCH:].astype(BF16),
        "ssd_conv_w": ssd_conv_w[l],
        "ssd_conv_b": ssd_conv_b[l].reshape(1, -1),
        "ssd_dtp": jnp.stack([jnp.tile(ssd_dt_bias[l].reshape(-1), LANES // (2 * SSD_HEADS)),
                              jnp.tile(ssd_a_log[l].reshape(-1), LANES // (2 * SSD_HEADS))]),
        "ssd_vec": jnp.stack([jnp.repeat(ssd_d[l], HEAD_DIM), ssd_norm_g[l]]),
        "gm_ws": gm_ws[l],
        "gm_bs": jnp.repeat(gm_bs[l].T, HEAD_DIM, axis=1),
        "sc_conv_w": sc_conv_w[l],
        "cf_conv_w": cf_conv_w[l],
        "local_vec": jnp.stack([gm_norm_g[l], gm_norm_b[l], cf_conv_b[l], cf_norm_g[l], cf_norm_b[l]]),
    }


def kernel(x, c, ctx, c_ctx, mod_w, mod_b, norm1_g, norm2_g, w_in, w_out, ssd_conv_w, ssd_conv_b, ssd_dt_bias, ssd_a_log, ssd_d, ssd_norm_g, gm_norm_g, gm_norm_b, gm_ws, gm_bs, sc_conv_w, cf_conv_w, cf_conv_b, cf_norm_g, cf_norm_b, ffn_w1, ffn_w3, ffn_w2, moe_router, moe_w1, moe_w3, moe_w2, final_norm_g):
    bsz, seq, _ = x.shape
    ctx_len = ctx.shape[1]
    depth = w_in.shape[0]
    n_x, n_c = bsz * seq, bsz * ctx_len
    tm = 512

    mod = _modulation(jnp.concatenate([c, c_ctx[None, :]], axis=0), mod_w, mod_b)
    x_map = dict(tm=tm, rows_per_mod=seq, mod_base=0)
    c_map = dict(tm=tm, rows_per_mod=n_c, mod_base=bsz)
    zero_state = jnp.zeros((bsz, 2, SSD_GROUPS, SSD_STATE, 2 * HEAD_DIM), F32)

    xs = x.reshape(n_x, D_MODEL)
    xc = ctx.reshape(n_c, D_MODEL)
    moe_bf16 = {}
    for l in range(depth):
        last = l == depth - 1
        lp = _layer_params(l, w_in, w_out, ssd_conv_w, ssd_conv_b, ssd_dt_bias, ssd_a_log, ssd_d, ssd_norm_g,
                           gm_norm_g, gm_norm_b, gm_ws, gm_bs, sc_conv_w, cf_conv_w, cf_conv_b, cf_norm_g,
                           cf_norm_b)
        g1 = norm1_g[l].reshape(1, D_MODEL)
        g2 = norm2_g[l].reshape(1, D_MODEL)
        zc_ssd, yc_loc, dtc = _token_mixer_front(xc, mod[l], g1, lp, tm=min(tm, ctx_len), seq=ctx_len,
                                                 rows_per_mod=n_c, mod_base=bsz)
        yc_ssd, states = _ssd_mixer(zc_ssd.reshape(bsz, ctx_len, W_SSD), dtc.reshape(bsz, ctx_len, LANES),
                                    zero_state, lp)
        z_ssd, y_loc, dtx = _token_mixer_front(xs, mod[l], g1, lp, tm=tm, seq=seq, rows_per_mod=seq, mod_base=0)
        y_ssd, _ = _ssd_mixer(z_ssd.reshape(bsz, seq, W_SSD), dtx.reshape(bsz, seq, LANES), states, lp)
        streams = [(xs, y_ssd.reshape(n_x, BRANCH), y_loc, x_map)]
        if not last:
            streams.append((xc, yc_ssd.reshape(n_c, BRANCH), yc_loc, c_map))
        fg = final_norm_g.reshape(1, D_MODEL) if last else None
        i = l // 2
        if l % 2 == 0:
            w1, w3, w2 = ffn_w1[i].astype(BF16), ffn_w3[i].astype(BF16), ffn_w2[i].astype(BF16)
            casts = []
            if l + 1 < depth:
                casts = [w.reshape(-1, w.shape[-1]) for w in (moe_w1[i], moe_w3[i])]
                if not all(_can_cast_in_steps(w, n_x // tm) for w in casts):
                    casts = []
            outs = []
            for t, ya, yb, m in streams:
                out, cast = _dense_channel_mixer(t, mod[l], ya, yb, lp["w_out_a"], lp["w_out_b"], g2, w1, w3, w2,
                                                 final_g=fg if t is xs else None, casts=casts if t is xs else (),
                                                 **m)
                outs.append(out)
                if cast:
                    moe_bf16[i] = [c.reshape(moe_w1[i].shape) for c in cast]
        else:
            w1, w3 = moe_bf16.get(i) or (moe_w1[i], moe_w3[i])
            w2 = moe_w2[i]
            r_hi = moe_router[i].astype(BF16)
            r_lo = (moe_router[i] - r_hi.astype(F32)).astype(BF16)
            r_pad = jnp.pad(jnp.concatenate([r_hi, r_lo], axis=1), ((0, 0), (0, LANES - 2 * N_EXPERTS)))
            outs = [_moe_channel_mixer(t, mod[l], ya, yb, lp["w_out_a"], lp["w_out_b"], g2, r_pad, w1, w3, w2,
                                       rows_per_mod=m["rows_per_mod"], mod_base=m["mod_base"],
                                       final_g=fg if t is xs else None) for t, ya, yb, m in streams]
        xs = outs[0]
        if not last:
            xc = outs[1]
    return xs.reshape(bsz, seq, D_MODEL)
```

```python
import functools

import jax
import jax.numpy as jnp
from jax import lax
from jax.experimental import pallas as pl
from jax.experimental.pallas import tpu as pltpu

F32 = jnp.float32
BF16 = jnp.bfloat16

LANES = 128
SUBLANES = 8
VMEM_LIMIT_BYTES = 56 * 1024 * 1024

D_MODEL = 1024
BRANCH = 256
HEAD_DIM = 64
SSD_HEADS = 4
SSD_GROUPS = 2
SSD_STATE = 128
CHUNK = 128
SSD_SEQS_PER_STEP = 2
SSD_CONV_CH = BRANCH + 2 * SSD_GROUPS * SSD_STATE
N_EXPERTS = 8
TOP_K = 2
TOK_TILE = 256
ROW_BLOCK = 512
SUB_BLOCK = 256
N_SUB = ROW_BLOCK // SUB_BLOCK
GATHER_TILES = 11
WINDOW_TILES = 5
ROUTER_TM = 512
COMBINE_TM = 512
COMBINE_WIN = 256
HALO = 16
RMS_EPS = 1e-6
LN_EPS = 1e-5

_C_XBC = 0
_C_DT = _C_XBC + SSD_CONV_CH
_C_Z = _C_DT + 2 * SSD_HEADS
_C_UV = _C_Z + BRANCH
_C_SC = _C_UV + 2 * BRANCH
_C_CF = _C_SC + 3 * BRANCH
_C_END = _C_CF + 2 * BRANCH
W_SSD = SSD_CONV_CH + BRANCH
W_UVCF = 4 * BRANCH
W_SC = 3 * BRANCH
W_MAIN = W_SSD + W_UVCF + W_SC


def _params(*sem):
    return pltpu.CompilerParams(dimension_semantics=sem, vmem_limit_bytes=VMEM_LIMIT_BYTES)


def _sigmoid(v):
    return 0.5 + 0.5 * jnp.tanh(0.5 * v)


def _silu(v):
    half = 0.5 * v
    return half + half * jnp.tanh(half)


def _softplus(v):
    return jnp.maximum(v, 0.0) + jnp.log1p(jnp.exp(-jnp.abs(v)))


def _gelu_tanh(v):
    return 0.5 * v * (1.0 + jnp.tanh(0.7978845608028654 * (v + 0.044715 * (v * v * v))))


def _norm_modulate(x, g, shift, scale):
    ms = jnp.mean(x * x, axis=-1, keepdims=True)
    return (x * lax.rsqrt(ms + RMS_EPS) * g) * (1.0 + scale) + shift


def _mod_kernel(c_ref, w_ref, b_ref, o_ref):
    o_ref[0, 0] = jnp.dot(_silu(c_ref[...]), w_ref[0], preferred_element_type=F32,
                          precision=lax.Precision.HIGHEST) + b_ref[0, 0]


def _modulation(c_all, mod_w, mod_b):
    depth = mod_w.shape[0]
    rows = c_all.shape[0]
    out = pl.pallas_call(
        _mod_kernel,
        grid=(depth, 6),
        in_specs=[
            pl.BlockSpec((rows, D_MODEL), lambda l, j: (0, 0)),
            pl.BlockSpec((1, D_MODEL, D_MODEL), lambda l, j: (l, 0, j)),
            pl.BlockSpec((1, 1, 1, D_MODEL), lambda l, j: (l, j, 0, 0)),
        ],
        out_specs=pl.BlockSpec((1, 1, rows, D_MODEL), lambda l, j: (l, j, 0, 0)),
        out_shape=jax.ShapeDtypeStruct((depth, 6, rows, D_MODEL), F32),
        compiler_params=_params("arbitrary", "arbitrary"),
        name="modulation",
    )(c_all, mod_w, mod_b.reshape(depth, 6, 1, D_MODEL))
    return jnp.transpose(out, (0, 2, 1, 3))


def _mod_index(tm, rows_per_mod, mod_base):
    return lambda i: (mod_base + (i * tm) // rows_per_mod, 0, 0)


def _fill_stage(stage_ref, prev, main, nxt):
    rows = main.shape[0]
    stage_ref[0:HALO, :] = prev
    stage_ref[HALO:HALO + rows, :] = main
    stage_ref[HALO + rows:2 * HALO + rows, :] = nxt


def _dwconv_rows(stage_ref, w_ref, taps, r, rows, c0, width):
    acc = None
    for k in range(taps):
        lo = HALO + r + k - taps // 2
        term = stage_ref[lo:lo + rows, c0:c0 + width] * w_ref[k:k + 1, c0:c0 + width]
        acc = term if acc is None else acc + term
    return acc


def _front_kernel(*refs, tm, seq, local):
    x_ref, xp_ref, xn_ref, mod_ref, g_ref, w_ref, wdt_ref, convw_ref, convb_ref = refs[0:9]
    if local:
        (ws_ref, bs_ref, scw_ref, cfw_ref, vec_ref,
         zssd_ref, yloc_ref, dt_ref, ssd_stage, sc_stage, cf_stage, rot_ref) = refs[9:]
    else:
        zssd_ref, dt_ref, ssd_stage = refs[9:]
    i = pl.program_id(0)
    tiles_per_seq = seq // tm
    keep_prev = jnp.where(i % tiles_per_seq == 0, 0.0, 1.0)
    keep_next = jnp.where(i % tiles_per_seq == tiles_per_seq - 1, 0.0, 1.0)
    g, shift, scale = g_ref[...], mod_ref[0, 0:1, :], mod_ref[0, 1:2, :]
    h = _norm_modulate(x_ref[...], g, shift, scale).astype(BF16)
    h_halo = _norm_modulate(jnp.concatenate([xp_ref[...], xn_ref[...]], axis=0), g, shift, scale).astype(BF16)
    z = jnp.dot(h, w_ref[...], preferred_element_type=F32)
    z_halo = jnp.dot(h_halo, w_ref[...], preferred_element_type=F32)
    zp = z_halo[0:HALO] * keep_prev
    zn = z_halo[HALO:2 * HALO] * keep_next
    dt_ref[...] = jnp.dot(h, wdt_ref[...], preferred_element_type=F32)

    _fill_stage(ssd_stage, zp[:, 0:SSD_CONV_CH], z[:, 0:SSD_CONV_CH], zn[:, 0:SSD_CONV_CH])
    for r in range(0, tm, CHUNK):
        for c0 in range(0, SSD_CONV_CH, BRANCH):
            v = _dwconv_rows(ssd_stage, convw_ref, 5, r, CHUNK, c0, BRANCH) + convb_ref[0:1, c0:c0 + BRANCH]
            zssd_ref[r:r + CHUNK, c0:c0 + BRANCH] = _silu(v).astype(BF16)
    zssd_ref[:, SSD_CONV_CH:W_SSD] = z[:, SSD_CONV_CH:W_SSD].astype(BF16)
    if not local:
        return

    lane = lax.broadcasted_iota(jnp.int32, (CHUNK, BRANCH), 1)
    gm_g, gm_b = vec_ref[0:1, :], vec_ref[1:2, :]
    cf_b, cf_g, cf_nb = vec_ref[2:3, :], vec_ref[3:4, :], vec_ref[4:5, :]
    c_uv, c_cf, c_sc = W_SSD, W_SSD + 2 * BRANCH, W_SSD + W_UVCF

    def layer_norm(v, gain, bias):
        mu = jnp.mean(v, axis=-1, keepdims=True)
        var = jnp.mean(jnp.square(v - mu), axis=-1, keepdims=True)
        return (v - mu) * lax.rsqrt(var + LN_EPS) * gain + bias

    def sc_in(t):
        return t[:, c_sc + BRANCH:c_sc + 2 * BRANCH] * t[:, c_sc + 2 * BRANCH:c_sc + 3 * BRANCH]

    def cf_in(t):
        return t[:, c_cf:c_cf + BRANCH] * _sigmoid(t[:, c_cf + BRANCH:c_cf + 2 * BRANCH])

    _fill_stage(sc_stage, sc_in(zp), sc_in(z), sc_in(zn))
    _fill_stage(cf_stage, cf_in(zp), cf_in(z), cf_in(zn))
    span = tm + 2 * HALO - SUBLANES
    for phase in range(SUBLANES):
        rot_ref[phase] = cf_stage[phase:phase + span, :]

    for r in range(0, tm, CHUNK):
        ge = _gelu_tanh(z[r:r + CHUNK, c_uv:c_uv + 2 * BRANCH])
        u = ge[:, 0:BRANCH]
        v = layer_norm(ge[:, BRANCH:2 * BRANCH], gm_g, gm_b)
        s = bs_ref[...]
        for hd in range(BRANCH // HEAD_DIM):
            in_head = (lane >= hd * HEAD_DIM) & (lane < (hd + 1) * HEAD_DIM)
            s = s + jnp.dot(ws_ref[hd].astype(BF16), jnp.where(in_head, v, 0.0).astype(BF16),
                            preferred_element_type=F32)
        yloc_ref[r:r + CHUNK, 0:BRANCH] = (u * s).astype(BF16)
        gate = z[r:r + CHUNK, c_sc:c_sc + BRANCH]
        yloc_ref[r:r + CHUNK, BRANCH:2 * BRANCH] = (gate * _dwconv_rows(sc_stage, scw_ref, 3, r, CHUNK, 0, BRANCH)
                                                    ).astype(BF16)
        cv = cf_b
        for k in range(31):
            lo = HALO + r + k - 31 // 2
            q8 = (lo // SUBLANES) * SUBLANES
            cv = cv + rot_ref[lo % SUBLANES, q8:q8 + CHUNK, :] * cfw_ref[k:k + 1, :]
        yloc_ref[r:r + CHUNK, 2 * BRANCH:3 * BRANCH] = _silu(layer_norm(cv, cf_g, cf_nb)).astype(BF16)


def _token_mixer_front(x2d, mod, g, lp, *, tm, seq, rows_per_mod, mod_base, local=True):
    n = x2d.shape[0]
    per = tm // HALO
    row = lambda i: (i, 0)
    const = lambda i: (0, 0)
    span = tm + 2 * HALO
    in_specs = [
        pl.BlockSpec((tm, D_MODEL), row),
        pl.BlockSpec((HALO, D_MODEL), lambda i: (jnp.maximum(i * per - 1, 0), 0)),
        pl.BlockSpec((HALO, D_MODEL), lambda i: (jnp.minimum((i + 1) * per, n // HALO - 1), 0)),
        pl.BlockSpec((1, 6, D_MODEL), _mod_index(tm, rows_per_mod, mod_base)),
        pl.BlockSpec((1, D_MODEL), const),
        pl.BlockSpec((D_MODEL, W_MAIN if local else W_SSD), const),
        pl.BlockSpec((D_MODEL, LANES), const),
        pl.BlockSpec((5, SSD_CONV_CH), const),
        pl.BlockSpec((1, SSD_CONV_CH), const),
    ]
    args = [x2d, x2d, x2d, mod, g, lp["w_main"], lp["w_dt"], lp["ssd_conv_w"], lp["ssd_conv_b"]]
    out_specs = [pl.BlockSpec((tm, W_SSD), row)]
    out_shape = [jax.ShapeDtypeStruct((n, W_SSD), BF16)]
    scratch = [pltpu.VMEM((span, SSD_CONV_CH), F32)]
    if local:
        in_specs += [
            pl.BlockSpec((BRANCH // HEAD_DIM, CHUNK, CHUNK), lambda i: (0, 0, 0)),
            pl.BlockSpec((CHUNK, BRANCH), const),
            pl.BlockSpec((3, BRANCH), const),
            pl.BlockSpec((31, BRANCH), const),
            pl.BlockSpec((5, BRANCH), const),
        ]
        args += [lp["gm_ws"], lp["gm_bs"], lp["sc_conv_w"], lp["cf_conv_w"], lp["local_vec"]]
        out_specs.append(pl.BlockSpec((tm, 3 * BRANCH), row))
        out_shape.append(jax.ShapeDtypeStruct((n, 3 * BRANCH), BF16))
        scratch += [
            pltpu.VMEM((span, BRANCH), F32),
            pltpu.VMEM((span, BRANCH), F32),
            pltpu.VMEM((SUBLANES, span - SUBLANES, BRANCH), F32),
        ]
    out_specs.append(pl.BlockSpec((tm, LANES), row))
    out_shape.append(jax.ShapeDtypeStruct((n, LANES), F32))
    outs = pl.pallas_call(
        functools.partial(_front_kernel, tm=tm, seq=seq, local=local),
        grid=(n // tm,),
        in_specs=in_specs,
        out_specs=out_specs,
        out_shape=out_shape,
        scratch_shapes=scratch,
        compiler_params=_params("parallel"),
        name="token_mixer_front",
    )(*args)
    return (outs[0], outs[1], outs[2]) if local else (outs[0], None, outs[1])


def _ssd_kernel(z_ref, dt_ref, init_ref, dtp_ref, vec_ref,
                y_ref, fin_ref, p_ref, dts_ref, pt_ref, dtt_ref, yacc_ref, st_ref, *, seq, n_seq):
    nc = seq // CHUNK
    lane = lax.broadcasted_iota(jnp.int32, (CHUNK, LANES), 1)
    row_i = lax.broadcasted_iota(jnp.int32, (CHUNK, CHUNK), 0)
    col_i = lax.broadcasted_iota(jnp.int32, (CHUNK, CHUNK), 1)
    lower = col_i <= row_i
    upper = col_i >= row_i
    head0 = lane < HEAD_DIM

    fwd_lane = (lane % (2 * SSD_HEADS)) < SSD_HEADS
    for q in range(n_seq):
        dt_all = _softplus(dt_ref[q] + dtp_ref[0:1, :])
        adt = dt_all * (-jnp.exp(dtp_ref[1:2, :]))
        p_all = (jnp.dot(lower.astype(F32), jnp.where(fwd_lane, adt, 0.0), preferred_element_type=F32,
                         precision=lax.Precision.HIGHEST)
                 + jnp.dot(upper.astype(F32), jnp.where(fwd_lane, 0.0, adt), preferred_element_type=F32,
                           precision=lax.Precision.HIGHEST))
        pt_ref[q] = p_all.T
        dtt_ref[q] = dt_all.T
        for c in range(nc):
            shift = (LANES - 2 * SSD_HEADS * c) % LANES
            p_ref[q, c * CHUNK:(c + 1) * CHUNK, :] = pltpu.roll(p_all, shift, axis=1) if shift else p_all
            dts_ref[q, c * CHUNK:(c + 1) * CHUNK, :] = pltpu.roll(dt_all, shift, axis=1) if shift else dt_all

    def group_inputs(q, rows, g):
        xg = z_ref[q, rows, g * LANES:(g + 1) * LANES].astype(F32)
        bg = z_ref[q, rows, BRANCH + g * SSD_STATE:BRANCH + (g + 1) * SSD_STATE]
        cg = z_ref[q, rows, BRANCH + (SSD_GROUPS + g) * SSD_STATE:BRANCH + (SSD_GROUPS + g + 1) * SSD_STATE]
        return xg, bg, cg

    def spread(m, j):
        return jnp.broadcast_to(m[:, j:j + 1], (CHUNK, LANES))

    def pair(a0, a1):
        return jnp.where(head0, a0, a1)

    st_ref[...] = init_ref[:, 0]

    def fwd_chunk(q, c):
        r0 = pl.multiple_of(c * CHUNK, CHUNK)
        rows = pl.ds(r0, CHUNK)
        pm = p_ref[q, rows, :]
        dt = dts_ref[q, rows, :]
        head_rows = pl.ds(pl.multiple_of(c * 2 * SSD_HEADS, 2 * SSD_HEADS), 2 * SSD_HEADS)
        pm_t = pt_ref[q, head_rows, :]
        dt_t = dtt_ref[q, head_rows, :]
        last = pm[CHUNK - 1:CHUNK, :]
        to_end = dt * jnp.exp(last - pm)
        for g in range(SSD_GROUPS):
            xg, bg, cg = group_inputs(q, rows, g)
            gram = lax.dot_general(cg, bg, (((1,), (1,)), ((), ())), preferred_element_type=F32)
            s_prev = st_ref[q, g]
            y = jnp.dot(cg, s_prev.astype(BF16), preferred_element_type=F32)
            hf0, hf1 = 2 * g, 2 * g + 1
            hb0, hb1 = SSD_HEADS + hf0, SSD_HEADS + hf1
            p_f0, p_f1 = spread(pm, hf0), spread(pm, hf1)
            y = y * jnp.exp(pair(p_f0, p_f1))
            for p_f, hf, hb, mask in ((p_f0, hf0, hb0, head0), (p_f1, hf1, hb1, jnp.logical_not(head0))):
                dec_f = jnp.exp(jnp.where(lower, p_f - pm_t[hf:hf + 1, :], -jnp.inf))
                dec_b = jnp.exp(jnp.where(upper, spread(pm, hb) - pm_t[hb:hb + 1, :], -jnp.inf))
                m = gram * (dec_f * dt_t[hf:hf + 1, :] + dec_b * dt_t[hb:hb + 1, :])
                xh = jnp.where(mask, xg, 0.0).astype(BF16)
                y = y + jnp.dot(m.astype(BF16), xh, preferred_element_type=F32)
            yacc_ref[q, rows, g * LANES:(g + 1) * LANES] = y
            w_end = pair(spread(to_end, hf0), spread(to_end, hf1))
            upd = lax.dot_general(bg, (xg * w_end).astype(BF16), (((0,), (0,)), ((), ())),
                                  preferred_element_type=F32)
            decay = jnp.exp(jnp.where(head0[0:1, :], last[:, hf0:hf0 + 1], last[:, hf1:hf1 + 1]))
            st_ref[q, g] = s_prev * decay + upd

    def fwd(c, carry):
        for q in range(n_seq):
            fwd_chunk(q, c)
        return carry

    lax.fori_loop(0, nc, fwd, 0, unroll=2)
    fin_ref[:, 0] = st_ref[...]

    st_ref[...] = init_ref[:, 1]
    d_skip = vec_ref[0:1, :]
    norm_g = vec_ref[1:2, :]

    def bwd_chunk(q, c):
        r0 = pl.multiple_of(c * CHUNK, CHUNK)
        rows = pl.ds(r0, CHUNK)
        pm = p_ref[q, rows, :]
        dt = dts_ref[q, rows, :]
        first = pm[0:1, :]
        to_end = dt * jnp.exp(first - pm)
        for g in range(SSD_GROUPS):
            xg, bg, cg = group_inputs(q, rows, g)
            hb0, hb1 = SSD_HEADS + 2 * g, SSD_HEADS + 2 * g + 1
            s_prev = st_ref[q, g]
            y = (jnp.dot(cg, s_prev.astype(BF16), preferred_element_type=F32)
                 * jnp.exp(pair(spread(pm, hb0), spread(pm, hb1))))
            yacc_ref[q, rows, g * LANES:(g + 1) * LANES] = (
                yacc_ref[q, rows, g * LANES:(g + 1) * LANES] + y + d_skip[:, g * LANES:(g + 1) * LANES] * xg)
            w_end = pair(spread(to_end, hb0), spread(to_end, hb1))
            upd = lax.dot_general(bg, (xg * w_end).astype(BF16), (((0,), (0,)), ((), ())),
                                  preferred_element_type=F32)
            decay = jnp.exp(jnp.where(head0[0:1, :], first[:, hb0:hb0 + 1], first[:, hb1:hb1 + 1]))
            st_ref[q, g] = s_prev * decay + upd
        gated = yacc_ref[q, rows, :] * _silu(z_ref[q, rows, SSD_CONV_CH:W_SSD].astype(F32))
        ms = jnp.mean(gated * gated, axis=-1, keepdims=True)
        y_ref[q, rows, :] = (gated * lax.rsqrt(ms + RMS_EPS) * norm_g).astype(BF16)

    def bwd(k, carry):
        for q in range(n_seq):
            bwd_chunk(q, nc - 1 - k)
        return carry

    lax.fori_loop(0, nc, bwd, 0)
    fin_ref[:, 1] = st_ref[...]


def _ssd_mixer(z_ssd, dt, init, lp):
    bsz, seq, _ = z_ssd.shape
    nc = seq // CHUNK
    n_hd = 2 * SSD_HEADS
    n_seq = SSD_SEQS_PER_STEP if bsz % SSD_SEQS_PER_STEP == 0 else 1
    assert nc * n_hd <= LANES
    dt = dt[:, :, :n_hd].reshape(bsz, nc, CHUNK, n_hd).transpose(0, 2, 1, 3).reshape(bsz, CHUNK, nc * n_hd)
    dt = jnp.pad(dt, ((0, 0), (0, 0), (0, LANES - nc * n_hd)))
    seq_map = lambda b: (b, 0, 0)
    st_map = lambda b: (b, 0, 0, 0, 0)
    const = lambda b: (0, 0)
    st_shape = (bsz, 2, SSD_GROUPS, SSD_STATE, 2 * HEAD_DIM)
    return pl.pallas_call(
        functools.partial(_ssd_kernel, seq=seq, n_seq=n_seq),
        grid=(bsz // n_seq,),
        in_specs=[
            pl.BlockSpec((n_seq, seq, W_SSD), seq_map),
            pl.BlockSpec((n_seq, CHUNK, LANES), seq_map),
            pl.BlockSpec((n_seq,) + st_shape[1:], st_map),
            pl.BlockSpec((2, LANES), const),
            pl.BlockSpec((2, BRANCH), const),
        ],
        out_specs=[
            pl.BlockSpec((n_seq, seq, BRANCH), seq_map),
            pl.BlockSpec((n_seq,) + st_shape[1:], st_map),
        ],
        out_shape=[
            jax.ShapeDtypeStruct((bsz, seq, BRANCH), BF16),
            jax.ShapeDtypeStruct(st_shape, F32),
        ],
        scratch_shapes=[
            pltpu.VMEM((n_seq, seq, LANES), F32),
            pltpu.VMEM((n_seq, seq, LANES), F32),
            pltpu.VMEM((n_seq, LANES, CHUNK), F32),
            pltpu.VMEM((n_seq, LANES, CHUNK), F32),
            pltpu.VMEM((n_seq, seq, BRANCH), F32),
            pltpu.VMEM((n_seq, SSD_GROUPS, SSD_STATE, 2 * HEAD_DIM), F32),
        ],
        compiler_params=_params("parallel"),
        name="ssd_mixer",
    )(z_ssd, dt, init, lp["ssd_dtp"], lp["ssd_vec"])


def _token_mixed(x_ref, mod_ref, ya_ref, yb_ref, wa_ref, wb_ref):
    proj = (jnp.dot(ya_ref[...], wa_ref[...], preferred_element_type=F32)
            + jnp.dot(yb_ref[...], wb_ref[...], preferred_element_type=F32))
    return x_ref[...] + mod_ref[0, 2:3, :] * proj


def _token_mixed_specs(tm, rows_per_mod, mod_base):
    row = lambda i, *_: (i, 0)
    const = lambda i, *_: (0, 0)
    mod_idx = _mod_index(tm, rows_per_mod, mod_base)
    return [
        pl.BlockSpec((tm, D_MODEL), row),
        pl.BlockSpec((1, 6, D_MODEL), lambda i, *_: mod_idx(i)),
        pl.BlockSpec((tm, BRANCH), row),
        pl.BlockSpec((tm, 3 * BRANCH), row),
        pl.BlockSpec((BRANCH, D_MODEL), const),
        pl.BlockSpec((3 * BRANCH, D_MODEL), const),
    ]


def _residual_out(x, gate, delta, fg_ref):
    out = x + gate * delta
    if fg_ref is not None:
        ms = jnp.mean(out * out, axis=-1, keepdims=True)
        out = out * lax.rsqrt(ms + RMS_EPS) * fg_ref[...]
    return out


def _cast_slices(srcs, dsts):
    for src, dst in zip(srcs, dsts):
        dst[...] = src[...].astype(BF16)


def _cast_specs(casts, steps):
    row = lambda i, *_: (i, 0)
    blocks = [(arr.shape[0] // steps, arr.shape[1]) for arr in casts]
    return ([pl.BlockSpec(b, row) for b in blocks], [pl.BlockSpec(b, row) for b in blocks],
            [jax.ShapeDtypeStruct(arr.shape, BF16) for arr in casts])


def _swiglu_kernel(*refs, final_norm, n_casts):
    x_ref, mod_ref, ya_ref, yb_ref, wa_ref, wb_ref, g_ref, w1_ref, w3_ref, w2_ref = refs[0:10]
    pos = 10
    fg_ref = None
    if final_norm:
        fg_ref = refs[pos]
        pos += 1
    cast_in = refs[pos:pos + n_casts]
    o_ref = refs[pos + n_casts]
    cast_out = refs[pos + n_casts + 1:pos + 2 * n_casts + 1]
    x1 = _token_mixed(x_ref, mod_ref, ya_ref, yb_ref, wa_ref, wb_ref)
    h = _norm_modulate(x1, g_ref[...], mod_ref[0, 3:4, :], mod_ref[0, 4:5, :]).astype(BF16)
    a = jnp.dot(h, w1_ref[...], preferred_element_type=F32)
    b = jnp.dot(h, w3_ref[...], preferred_element_type=F32)
    y = jnp.dot((_silu(a) * b).astype(BF16), w2_ref[...], preferred_element_type=F32)
    o_ref[...] = _residual_out(x1, mod_ref[0, 5:6, :], y, fg_ref)
    _cast_slices(cast_in, cast_out)


def _dense_channel_mixer(x2d, mod, y_ssd, y_local, w_a, w_b, g, w1, w3, w2, *, tm, rows_per_mod, mod_base,
                         final_g=None, casts=()):
    n = x2d.shape[0]
    steps = n // tm
    ff = w1.shape[1]
    const = lambda i: (0, 0)
    row = lambda i: (i, 0)
    resident = dict(pipeline_mode=pl.Buffered(1))
    in_specs = _token_mixed_specs(tm, rows_per_mod, mod_base) + [
        pl.BlockSpec((1, D_MODEL), const),
        pl.BlockSpec((D_MODEL, ff), const, **resident),
        pl.BlockSpec((D_MODEL, ff), const, **resident),
        pl.BlockSpec((ff, D_MODEL), const, **resident),
    ]
    args = [x2d, mod, y_ssd, y_local, w_a, w_b, g, w1, w3, w2]
    if final_g is not None:
        in_specs.append(pl.BlockSpec((1, D_MODEL), const))
        args.append(final_g)
    cast_in, cast_out, cast_shape = _cast_specs(casts, steps)
    in_specs += cast_in
    args += list(casts)
    out_specs = [pl.BlockSpec((tm, D_MODEL), row)] + cast_out
    out_shape = [jax.ShapeDtypeStruct((n, D_MODEL), F32)] + cast_shape
    outs = pl.pallas_call(
        functools.partial(_swiglu_kernel, final_norm=final_g is not None, n_casts=len(casts)),
        grid=(steps,),
        in_specs=in_specs,
        out_specs=out_specs,
        out_shape=out_shape,
        compiler_params=_params("parallel"),
        name="dense_channel_mixer",
    )(*args)
    return outs[0], list(outs[1:])


def _can_cast_in_steps(arr2d, steps):
    return arr2d.shape[0] % steps == 0 and (arr2d.shape[0] // steps) % (2 * SUBLANES) == 0


def _router_kernel(*refs, n_casts):
    x_ref, mod_ref, ya_ref, yb_ref, wa_ref, wb_ref, g_ref, r_ref = refs[0:8]
    cast_in = refs[8:8 + n_casts]
    x1_ref, h_ref, comb_ref, rank_ref, rankt_ref, cnt_ref = refs[8 + n_casts:14 + n_casts]
    cast_out = refs[14 + n_casts:14 + 2 * n_casts]
    carry_ref = refs[-1]
    _cast_slices(cast_in, cast_out)

    @pl.when(pl.program_id(0) == 0)
    def _():
        carry_ref[...] = jnp.zeros_like(carry_ref)

    x1 = _token_mixed(x_ref, mod_ref, ya_ref, yb_ref, wa_ref, wb_ref)
    x1_ref[...] = x1
    h = _norm_modulate(x1, g_ref[...], mod_ref[0, 3:4, :], mod_ref[0, 4:5, :])
    h_hi = h.astype(BF16)
    h_ref[...] = h_hi
    h_lo = (h - h_hi.astype(F32)).astype(BF16)
    parts = jnp.dot(jnp.concatenate([h_hi, h_lo], axis=0), r_ref[...], preferred_element_type=F32)
    parts = parts[0:h.shape[0], :] + parts[h.shape[0]:, :]
    logits = parts + pltpu.roll(parts, LANES - N_EXPERTS, axis=1)
    lane = lax.broadcasted_iota(jnp.int32, logits.shape, 1).astype(F32)
    logits = jnp.where(lane < N_EXPERTS, logits, -jnp.inf)
    m1 = jnp.max(logits, axis=-1, keepdims=True)
    i1 = jnp.min(jnp.where(logits == m1, lane, float(LANES)), axis=-1, keepdims=True)
    rest = jnp.where(lane == i1, -jnp.inf, logits)
    m2 = jnp.max(rest, axis=-1, keepdims=True)
    i2 = jnp.min(jnp.where(rest == m2, lane, float(LANES)), axis=-1, keepdims=True)
    e2 = jnp.exp(m2 - m1)
    denom = 1.0 + e2
    comb_ref[...] = jnp.where(lane == i1, 1.0 / denom, 0.0) + jnp.where(lane == i2, e2 / denom, 0.0)
    is_chosen = (lane == i1) | (lane == i2)
    chosen = jnp.where(is_chosen, 1.0, 0.0)
    tm = chosen.shape[0]
    earlier = lax.broadcasted_iota(jnp.int32, (tm, tm), 1) < lax.broadcasted_iota(jnp.int32, (tm, tm), 0)
    running = carry_ref[...]
    before = jnp.dot(jnp.where(earlier, 1.0, 0.0).astype(BF16), chosen.astype(BF16),
                     preferred_element_type=F32) + running.astype(F32)
    rank = jnp.where(is_chosen, before, -1.0)
    rank_ref[...] = rank
    rankt_ref[...] = rank.T[0:N_EXPERTS, :]
    chosen_i = jnp.where(is_chosen, 1, 0)
    for k in range(tm // TOK_TILE):
        running = running + jnp.sum(chosen_i[k * TOK_TILE:(k + 1) * TOK_TILE, :], axis=0, keepdims=True)
        cnt_ref[0, k:k + 1, :] = running
    carry_ref[...] = running


def _router(x2d, mod, y_ssd, y_local, w_a, w_b, g, router_pad, *, tm, rows_per_mod, mod_base, casts=()):
    n = x2d.shape[0]
    row = lambda i: (i, 0)
    const = lambda i: (0, 0)
    per = tm // TOK_TILE
    cast_in, cast_out, cast_shape = _cast_specs(casts, n // tm)
    return pl.pallas_call(
        functools.partial(_router_kernel, n_casts=len(casts)),
        grid=(n // tm,),
        in_specs=_token_mixed_specs(tm, rows_per_mod, mod_base) + [
            pl.BlockSpec((1, D_MODEL), const),
            pl.BlockSpec((D_MODEL, LANES), const),
        ] + cast_in,
        out_specs=[
            pl.BlockSpec((tm, D_MODEL), row),
            pl.BlockSpec((tm, D_MODEL), row),
            pl.BlockSpec((tm, LANES), row),
            pl.BlockSpec((tm, LANES), row),
            pl.BlockSpec((N_EXPERTS, tm), lambda i: (0, i)),
            pl.BlockSpec((1, per, LANES), lambda i: (i, 0, 0)),
        ] + cast_out,
        out_shape=[
            jax.ShapeDtypeStruct((n, D_MODEL), F32),
            jax.ShapeDtypeStruct((n, D_MODEL), BF16),
            jax.ShapeDtypeStruct((n, LANES), F32),
            jax.ShapeDtypeStruct((n, LANES), F32),
            jax.ShapeDtypeStruct((N_EXPERTS, n), F32),
            jax.ShapeDtypeStruct((n // tm, per, LANES), jnp.int32),
        ] + cast_shape,
        scratch_shapes=[pltpu.VMEM((1, LANES), jnp.int32)],
        compiler_params=_params("arbitrary"),
        name="moe_router",
    )(x2d, mod, y_ssd, y_local, w_a, w_b, g, router_pad, *casts)


def _routing_tables(cnt, n_tokens):
    i32 = jnp.int32
    n_t = n_tokens // TOK_TILE
    cnt_end = cnt.reshape(n_t, LANES)[:, :N_EXPERTS]
    cnt_start = jnp.concatenate([jnp.zeros((1, N_EXPERTS), i32), cnt_end[:-1]], axis=0)
    tot = cnt_end[-1]
    nblk = (tot + ROW_BLOCK - 1) // ROW_BLOCK
    blk_cum = jnp.cumsum(nblk)
    blk_first = blk_cum - nblk
    n_valid = blk_cum[-1]
    nb = n_tokens * TOP_K // ROW_BLOCK + N_EXPERTS
    bi = jnp.arange(nb, dtype=i32)
    valid = bi < n_valid
    e_of = jnp.sum((bi[:, None] >= blk_cum[None, :]).astype(i32), axis=1)
    e_last = jnp.sum(((n_valid - 1) >= blk_cum).astype(i32))
    blk_e = jnp.minimum(jnp.where(valid, e_of, e_last), N_EXPERTS - 1)
    base = jnp.where(valid, (bi - blk_first[blk_e]) * ROW_BLOCK, 0)
    lo = base[:, None] + jnp.arange(N_SUB, dtype=i32)[None, :] * SUB_BLOCK
    hi = jnp.where(valid[:, None], jnp.minimum(lo + SUB_BLOCK, tot[blk_e][:, None]), lo)
    cs = cnt_start[:, blk_e]
    ce = cnt_end[:, blk_e]
    rel = (cs[:, :, None] < hi[None]) & (ce[:, :, None] > lo[None])
    tile_id = jnp.arange(n_t, dtype=i32)[:, None, None]
    t_first = jnp.min(jnp.where(rel, tile_id, n_t), axis=0)
    sub_hi = jnp.max(jnp.where(rel, tile_id, -1), axis=0)
    sub_lo = jnp.where(sub_hi >= 0, t_first, 0)
    u_lo = jnp.min(t_first, axis=1)
    u_n = jnp.maximum(jnp.max(sub_hi, axis=1) - u_lo + 1, 0)
    u_lo = jnp.where(u_n > 0, u_lo, 0)
    per = COMBINE_TM // TOK_TILE
    cs_c = cnt_start[0::per]
    cn_c = cnt_end[per - 1::per] - cs_c
    return dict(blk_e=blk_e.astype(i32), valid=valid.astype(i32), base=base.astype(i32),
                u_lo=u_lo.astype(i32), u_n=u_n.astype(i32),
                sub_lo=sub_lo.reshape(-1).astype(i32), sub_hi=sub_hi.reshape(-1).astype(i32),
                seg_rows=(blk_first * ROW_BLOCK).astype(i32),
                cs_c=cs_c.reshape(-1).astype(i32), cn_c=cn_c.reshape(-1).astype(i32), n_blocks=nb)


def _experts_kernel(be_ref, bv_ref, base_ref, ulo_ref, un_ref, slo_ref, shi_ref,
                    h_hbm, rt_hbm, w1_ref, w3_ref, w2_ref, y_ref, hbuf, rbuf, hsem, rsem):
    i = pl.program_id(0)
    slot = i % 2

    def tile_copies(s, k, t):
        tok = pl.multiple_of(t * TOK_TILE, TOK_TILE)
        return (pltpu.make_async_copy(h_hbm.at[pl.ds(tok, TOK_TILE)], hbuf.at[s, k], hsem.at[s, k]),
                pltpu.make_async_copy(rt_hbm.at[:, pl.ds(tok, TOK_TILE)], rbuf.at[s, k], rsem.at[s, k]))

    def for_block_tiles(b, s, action):
        for k in range(GATHER_TILES):
            @pl.when(k < un_ref[b])
            def _():
                for cp in tile_copies(s, k, ulo_ref[b] + k):
                    action(cp)

    @pl.when(i == 0)
    def _():
        def clear(k, carry):
            hbuf[k // (GATHER_TILES + 1), k % (GATHER_TILES + 1)] = jnp.zeros((TOK_TILE, D_MODEL), BF16)
            rbuf[k // (GATHER_TILES + 1), k % (GATHER_TILES + 1)] = jnp.zeros((N_EXPERTS, TOK_TILE), F32)
            return carry

        lax.fori_loop(0, 2 * (GATHER_TILES + 1), clear, 0)
        for_block_tiles(0, 0, lambda cp: cp.start())

    for_block_tiles(i, slot, lambda cp: cp.wait())

    @pl.when(i + 1 < pl.num_programs(0))
    def _():
        for_block_tiles(i + 1, 1 - slot, lambda cp: cp.start())

    @pl.when(bv_ref[i] == 1)
    def _():
        e = be_ref[i]
        row_id = lax.broadcasted_iota(jnp.int32, (SUB_BLOCK, TOK_TILE), 0).astype(F32)
        expert_row = lax.broadcasted_iota(jnp.int32, (N_EXPERTS, TOK_TILE), 0) == e
        u_lo = ulo_ref[i]
        selected = []

        def tile_weights(ranks8, first_rank):
            ranks = jnp.sum(jnp.where(expert_row, ranks8, 0.0), axis=0, keepdims=True)
            return jnp.where(ranks == first_rank + row_id, 1.0, 0.0).astype(BF16)

        for sb in range(N_SUB):
            first_rank = (base_ref[i] + sb * SUB_BLOCK).astype(F32)
            t_lo, t_hi = slo_ref[N_SUB * i + sb], shi_ref[N_SUB * i + sb]
            k0 = jnp.clip(t_lo - u_lo, 0, GATHER_TILES + 1 - WINDOW_TILES)
            picks = []
            for j in range(WINDOW_TILES):
                t = u_lo + k0 + j
                live = (t >= t_lo) & (t <= t_hi) & (k0 + j < GATHER_TILES)
                picks.append(tile_weights(jnp.where(live, rbuf[slot, k0 + j], -1.0), first_rank))
            window = hbuf[slot, pl.ds(k0, WINDOW_TILES)].reshape(WINDOW_TILES * TOK_TILE, D_MODEL)
            rows = jnp.dot(jnp.concatenate(picks, axis=1), window, preferred_element_type=F32)

            def late_tile(t, carry):
                k = t - u_lo
                kk = jnp.minimum(k, GATHER_TILES)

                @pl.when(k >= GATHER_TILES)
                def _():
                    tok = pl.multiple_of(t * TOK_TILE, TOK_TILE)
                    pltpu.sync_copy(h_hbm.at[pl.ds(tok, TOK_TILE)], hbuf.at[slot, GATHER_TILES])
                    pltpu.sync_copy(rt_hbm.at[:, pl.ds(tok, TOK_TILE)], rbuf.at[slot, GATHER_TILES])

                return carry + jnp.dot(tile_weights(rbuf[slot, kk], first_rank), hbuf[slot, kk],
                                       preferred_element_type=F32)

            t_next = u_lo + jnp.minimum(k0 + WINDOW_TILES, GATHER_TILES)
            rows = lax.fori_loop(jnp.maximum(t_next, t_lo), t_hi + 1, late_tile, rows)
            selected.append(rows.astype(BF16))

        hg = jnp.concatenate(selected, axis=0)
        half = w1_ref.shape[2] // 2
        y = None
        for c in range(2):
            a = jnp.dot(hg, w1_ref[0, :, c * half:(c + 1) * half], preferred_element_type=F32)
            b = jnp.dot(hg, w3_ref[0, :, c * half:(c + 1) * half], preferred_element_type=F32)
            part = jnp.dot((_silu(a) * b).astype(BF16), w2_ref[0, c * half:(c + 1) * half, :],
                           preferred_element_type=F32)
            y = part if y is None else y + part
        y_ref[...] = y.astype(BF16)

    @pl.when(bv_ref[i] == 0)
    def _():
        y_ref[...] = jnp.zeros_like(y_ref)


def _moe_experts(h2, rank_t, w1, w3, w2, tb):
    nb = tb["n_blocks"]
    ff = w1.shape[2]
    w_map = lambda i, be, *_: (be[i], 0, 0)
    grid_spec = pltpu.PrefetchScalarGridSpec(
        num_scalar_prefetch=7,
        grid=(nb,),
        in_specs=[
            pl.BlockSpec(memory_space=pl.ANY),
            pl.BlockSpec(memory_space=pl.ANY),
            pl.BlockSpec((1, D_MODEL, ff), w_map, pipeline_mode=pl.Buffered(1)),
            pl.BlockSpec((1, D_MODEL, ff), w_map, pipeline_mode=pl.Buffered(1)),
            pl.BlockSpec((1, ff, D_MODEL), w_map, pipeline_mode=pl.Buffered(1)),
        ],
        out_specs=pl.BlockSpec((ROW_BLOCK, D_MODEL), lambda i, *_: (i, 0)),
        scratch_shapes=[
            pltpu.VMEM((2, GATHER_TILES + 1, TOK_TILE, D_MODEL), BF16),
            pltpu.VMEM((2, GATHER_TILES + 1, N_EXPERTS, TOK_TILE), F32),
            pltpu.SemaphoreType.DMA((2, GATHER_TILES)),
            pltpu.SemaphoreType.DMA((2, GATHER_TILES)),
        ],
    )
    return pl.pallas_call(
        _experts_kernel,
        grid_spec=grid_spec,
        out_shape=jax.ShapeDtypeStruct((nb * ROW_BLOCK, D_MODEL), BF16),
        compiler_params=_params("arbitrary"),
        name="moe_experts",
    )(tb["blk_e"], tb["valid"], tb["base"], tb["u_lo"], tb["u_n"], tb["sub_lo"], tb["sub_hi"],
      h2, rank_t, w1, w3, w2)


def _combine_kernel(*refs, final_norm, n_rows):
    cs_ref, cn_ref, seg_ref, x_ref, mod_ref, comb_ref, rank_ref, y_hbm = refs[0:8]
    fg_ref = refs[8] if final_norm else None
    o_ref, ybuf, ysem, acc_ref = refs[-4:]
    i = pl.program_id(0)
    slot = i % 2
    extra = N_EXPERTS

    def window(b, e):
        lo = seg_ref[e] + cs_ref[b * N_EXPERTS + e]
        cnt = cn_ref[b * N_EXPERTS + e]
        first = (lo // 16) * 16
        n_win = jnp.where(cnt > 0, (lo + cnt - first + COMBINE_WIN - 1) // COMBINE_WIN, 0)
        return first, n_win

    def win_start(first, j):
        return pl.multiple_of(jnp.minimum(first + j * COMBINE_WIN, n_rows - COMBINE_WIN), 16)

    def for_tile_windows(b, s, action):
        for e in range(N_EXPERTS):
            first, n_win = window(b, e)

            @pl.when(n_win > 0)
            def _():
                action(pltpu.make_async_copy(y_hbm.at[pl.ds(win_start(first, 0), COMBINE_WIN)],
                                             ybuf.at[s, e], ysem.at[s, e]))

    @pl.when(i == 0)
    def _():
        def clear(k, carry):
            ybuf[k // (N_EXPERTS + 1), k % (N_EXPERTS + 1)] = jnp.zeros((COMBINE_WIN, D_MODEL), BF16)
            return carry

        lax.fori_loop(0, 2 * (N_EXPERTS + 1), clear, 0)
        for_tile_windows(0, 0, lambda cp: cp.start())

    for_tile_windows(i, slot, lambda cp: cp.wait())

    @pl.when(i + 1 < pl.num_programs(0))
    def _():
        for_tile_windows(i + 1, 1 - slot, lambda cp: cp.start())

    col_id = lax.broadcasted_iota(jnp.int32, (COMBINE_TM, COMBINE_WIN), 1).astype(F32)

    def weights(e, first, j):
        pos_col = rank_ref[:, e:e + 1] + seg_ref[e].astype(F32)
        start = win_start(first, j).astype(F32)
        return jnp.where(pos_col == start + col_id, comb_ref[:, e:e + 1], 0.0).astype(BF16)

    firsts = [window(i, e) for e in range(N_EXPERTS)]
    pick_all = jnp.concatenate([weights(e, firsts[e][0], 0) for e in range(N_EXPERTS)], axis=1)
    y_all = ybuf[slot, 0:N_EXPERTS].reshape(N_EXPERTS * COMBINE_WIN, D_MODEL)
    acc_ref[...] = jnp.dot(pick_all, y_all, preferred_element_type=F32)

    for e in range(N_EXPERTS):
        first, n_win = firsts[e]

        def more(j, carry):
            pltpu.sync_copy(y_hbm.at[pl.ds(win_start(first, j), COMBINE_WIN)], ybuf.at[slot, extra])
            acc_ref[...] += jnp.dot(weights(e, first, j), ybuf[slot, extra], preferred_element_type=F32)
            return carry

        lax.fori_loop(1, jnp.maximum(n_win, 1), more, 0)

    o_ref[...] = _residual_out(x_ref[...], mod_ref[0, 5:6, :], acc_ref[...], fg_ref)


def _moe_combine(x2d, mod, comb, rank, y_sorted, tb, *, rows_per_mod, mod_base, final_g=None):
    n = x2d.shape[0]
    tm = COMBINE_TM
    row = lambda i, *_: (i, 0)
    const = lambda i, *_: (0, 0)
    mod_idx = _mod_index(tm, rows_per_mod, mod_base)
    in_specs = [
        pl.BlockSpec((tm, D_MODEL), row),
        pl.BlockSpec((1, 6, D_MODEL), lambda i, *_: mod_idx(i)),
        pl.BlockSpec((tm, LANES), row),
        pl.BlockSpec((tm, LANES), row),
        pl.BlockSpec(memory_space=pl.ANY),
    ]
    args = [x2d, mod, comb, rank, y_sorted]
    if final_g is not None:
        in_specs.append(pl.BlockSpec((1, D_MODEL), const))
        args.append(final_g)
    grid_spec = pltpu.PrefetchScalarGridSpec(
        num_scalar_prefetch=3,
        grid=(n // tm,),
        in_specs=in_specs,
        out_specs=pl.BlockSpec((tm, D_MODEL), row),
        scratch_shapes=[
            pltpu.VMEM((2, N_EXPERTS + 1, COMBINE_WIN, D_MODEL), BF16),
            pltpu.SemaphoreType.DMA((2, N_EXPERTS)),
            pltpu.VMEM((tm, D_MODEL), F32),
        ],
    )
    return pl.pallas_call(
        functools.partial(_combine_kernel, final_norm=final_g is not None, n_rows=y_sorted.shape[0]),
        grid_spec=grid_spec,
        out_shape=jax.ShapeDtypeStruct((n, D_MODEL), F32),
        compiler_params=_params("arbitrary"),
        name="moe_combine",
    )(tb["cs_c"], tb["cn_c"], tb["seg_rows"], *args)


def _to_bf16_in(w, steps):
    if w.dtype == BF16:
        return None, w
    flat = w.reshape(-1, w.shape[-1])
    if _can_cast_in_steps(flat, steps):
        return flat, None
    return None, w.astype(BF16)


def _moe_channel_mixer(x2d, mod, y_ssd, y_local, w_a, w_b, g, router_pad, w1, w3, w2, *, rows_per_mod, mod_base,
                       final_g=None):
    w1 = w1.astype(BF16)
    w3 = w3.astype(BF16)
    w2_flat, w2_bf16 = _to_bf16_in(w2, x2d.shape[0] // ROUTER_TM)
    x1, h2, comb, rank, rank_t, cnt, *cast = _router(
        x2d, mod, y_ssd, y_local, w_a, w_b, g, router_pad, tm=ROUTER_TM, rows_per_mod=rows_per_mod,
        mod_base=mod_base, casts=() if w2_flat is None else (w2_flat,))
    if w2_bf16 is None:
        w2_bf16 = cast[0].reshape(w2.shape)
    w2 = w2_bf16
    tb = _routing_tables(cnt, x2d.shape[0])
    y_sorted = _moe_experts(h2, rank_t, w1, w3, w2, tb)
    return _moe_combine(x1, mod, comb, rank, y_sorted, tb, rows_per_mod=rows_per_mod, mod_base=mod_base,
                        final_g=final_g)


def _layer_params(l, w_in, w_out, ssd_conv_w, ssd_conv_b, ssd_dt_bias, ssd_a_log, ssd_d, ssd_norm_g,
                  gm_norm_g, gm_norm_b, gm_ws, gm_bs, sc_conv_w, cf_conv_w, cf_conv_b, cf_norm_g, cf_norm_b):
    wi = w_in[l]
    w_main = jnp.concatenate(
        [wi[:, _C_XBC:_C_DT], wi[:, _C_Z:_C_UV], wi[:, _C_UV:_C_SC], wi[:, _C_CF:_C_END], wi[:, _C_SC:_C_CF]],
        axis=1).astype(BF16)
    w_dt = jnp.pad(wi[:, _C_DT:_C_Z], ((0, 0), (0, LANES - 2 * SSD_HEADS))).astype(BF16)
    return {
        "w_main": w_main,
        "w_dt": w_dt,
        "w_out_a": w_out[l, 0:BRANCH].astype(BF16),
        "w_out_b": w_out[l, BRANCH:].astype(BF16),
        "ssd_conv_w": ssd_conv_w[l],
        "ssd_conv_b": ssd_conv_b[l].reshape(1, -1),
        "ssd_dtp": jnp.stack([jnp.tile(ssd_dt_bias[l].reshape(-1), LANES // (2 * SSD_HEADS)),
                              jnp.tile(ssd_a_log[l].reshape(-1), LANES // (2 * SSD_HEADS))]),
        "ssd_vec": jnp.stack([jnp.repeat(ssd_d[l], HEAD_DIM), ssd_norm_g[l]]),
        "gm_ws": gm_ws[l],
        "gm_bs": jnp.repeat(gm_bs[l].T, HEAD_DIM, axis=1),
        "sc_conv_w": sc_conv_w[l],
        "cf_conv_w": cf_conv_w[l],
        "local_vec": jnp.stack([gm_norm_g[l], gm_norm_b[l], cf_conv_b[l], cf_norm_g[l], cf_norm_b[l]]),
    }


def kernel(x, c, ctx, c_ctx, mod_w, mod_b, norm1_g, norm2_g, w_in, w_out, ssd_conv_w, ssd_conv_b, ssd_dt_bias, ssd_a_log, ssd_d, ssd_norm_g, gm_norm_g, gm_norm_b, gm_ws, gm_bs, sc_conv_w, cf_conv_w, cf_conv_b, cf_norm_g, cf_norm_b, ffn_w1, ffn_w3, ffn_w2, moe_router, moe_w1, moe_w3, moe_w2, final_norm_g):
    bsz, seq, _ = x.shape
    ctx_len = ctx.shape[1]
    depth = w_in.shape[0]
    n_x, n_c = bsz * seq, bsz * ctx_len
    tm = 512

    mod = _modulation(jnp.concatenate([c, c_ctx[None, :]], axis=0), mod_w, mod_b)
    x_map = dict(tm=tm, rows_per_mod=seq, mod_base=0)
    c_map = dict(tm=tm, rows_per_mod=n_c, mod_base=bsz)
    zero_state = jnp.zeros((bsz, 2, SSD_GROUPS, SSD_STATE, 2 * HEAD_DIM), F32)

    xs = x.reshape(n_x, D_MODEL)
    xc = ctx.reshape(n_c, D_MODEL)
    moe_bf16 = {}
    for l in range(depth):
        last = l == depth - 1
        lp = _layer_params(l, w_in, w_out, ssd_conv_w, ssd_conv_b, ssd_dt_bias, ssd_a_log, ssd_d, ssd_norm_g,
                           gm_norm_g, gm_norm_b, gm_ws, gm_bs, sc_conv_w, cf_conv_w, cf_conv_b, cf_norm_g,
                           cf_norm_b)
        g1 = norm1_g[l].reshape(1, D_MODEL)
        g2 = norm2_g[l].reshape(1, D_MODEL)
        zc_ssd, yc_loc, dtc = _token_mixer_front(xc, mod[l], g1, lp, tm=min(tm, ctx_len), seq=ctx_len,
                                                 rows_per_mod=n_c, mod_base=bsz, local=not last)
        yc_ssd, states = _ssd_mixer(zc_ssd.reshape(bsz, ctx_len, W_SSD), dtc.reshape(bsz, ctx_len, LANES),
                                    zero_state, lp)
        z_ssd, y_loc, dtx = _token_mixer_front(xs, mod[l], g1, lp, tm=tm, seq=seq, rows_per_mod=seq, mod_base=0)
        y_ssd, _ = _ssd_mixer(z_ssd.reshape(bsz, seq, W_SSD), dtx.reshape(bsz, seq, LANES), states, lp)
        streams = [(xs, y_ssd.reshape(n_x, BRANCH), y_loc, x_map)]
        if not last:
            streams.append((xc, yc_ssd.reshape(n_c, BRANCH), yc_loc, c_map))
        fg = final_norm_g.reshape(1, D_MODEL) if last else None
        i = l // 2
        if l % 2 == 0:
            w1, w3, w2 = ffn_w1[i].astype(BF16), ffn_w3[i].astype(BF16), ffn_w2[i].astype(BF16)
            casts = []
            if l + 1 < depth:
                casts = [w.reshape(-1, w.shape[-1]) for w in (moe_w1[i], moe_w3[i])]
                if not all(_can_cast_in_steps(w, n_x // tm) for w in casts):
                    casts = []
            outs = []
            for t, ya, yb, m in streams:
                out, cast = _dense_channel_mixer(t, mod[l], ya, yb, lp["w_out_a"], lp["w_out_b"], g2, w1, w3, w2,
                                                 final_g=fg if t is xs else None, casts=casts if t is xs else (),
                                                 **m)
                outs.append(out)
                if cast:
                    moe_bf16[i] = [c.reshape(moe_w1[i].shape) for c in cast]
        else:
            w1, w3 = moe_bf16.get(i) or (moe_w1[i], moe_w3[i])
            w2 = moe_w2[i]
            r_hi = moe_router[i].astype(BF16)
            r_lo = (moe_router[i] - r_hi.astype(F32)).astype(BF16)
            r_pad = jnp.pad(jnp.concatenate([r_hi, r_lo], axis=1), ((0, 0), (0, LANES - 2 * N_EXPERTS)))
            outs = [_moe_channel_mixer(t, mod[l], ya, yb, lp["w_out_a"], lp["w_out_b"], g2, r_pad, w1, w3, w2,
                                       rows_per_mod=m["rows_per_mod"], mod_base=m["mod_base"],
                                       final_g=fg if t is xs else None) for t, ya, yb, m in streams]
        xs = outs[0]
        if not last:
            xc = outs[1]
    return xs.reshape(bsz, seq, D_MODEL)
```

```python
import functools

import jax
import jax.numpy as jnp
from jax import lax
from jax.experimental import pallas as pl
from jax.experimental.pallas import tpu as pltpu

F32 = jnp.float32
BF16 = jnp.bfloat16

LANES = 128
SUBLANES = 8
VMEM_LIMIT_BYTES = 56 * 1024 * 1024

D_MODEL = 1024
BRANCH = 256
HEAD_DIM = 64
SSD_HEADS = 4
SSD_GROUPS = 2
SSD_STATE = 128
CHUNK = 128
SSD_ROWS_PER_STEP = 4096
SSD_MAX_SEQS_PER_STEP = 4
SSD_CONV_CH = BRANCH + 2 * SSD_GROUPS * SSD_STATE
N_EXPERTS = 8
TOP_K = 2
TOK_TILE = 256
ROW_BLOCK = 512
SUB_BLOCK = 256
N_SUB = ROW_BLOCK // SUB_BLOCK
GATHER_TILES = 11
WINDOW_TILES = 5
ROUTER_TM = 512
COMBINE_TM = 512
COMBINE_WIN = 256
HALO = 16
RMS_EPS = 1e-6
LN_EPS = 1e-5

_C_XBC = 0
_C_DT = _C_XBC + SSD_CONV_CH
_C_Z = _C_DT + 2 * SSD_HEADS
_C_UV = _C_Z + BRANCH
_C_SC = _C_UV + 2 * BRANCH
_C_CF = _C_SC + 3 * BRANCH
_C_END = _C_CF + 2 * BRANCH
W_SSD = SSD_CONV_CH + BRANCH
W_UVCF = 4 * BRANCH
W_SC = 3 * BRANCH
W_MAIN = W_SSD + W_UVCF + W_SC


def _params(*sem):
    return pltpu.CompilerParams(dimension_semantics=sem, vmem_limit_bytes=VMEM_LIMIT_BYTES)


def _sigmoid(v):
    return 0.5 + 0.5 * jnp.tanh(0.5 * v)


def _silu(v):
    half = 0.5 * v
    return half + half * jnp.tanh(half)


def _softplus(v):
    return jnp.maximum(v, 0.0) + jnp.log1p(jnp.exp(-jnp.abs(v)))


def _gelu_tanh(v):
    return 0.5 * v * (1.0 + jnp.tanh(0.7978845608028654 * (v + 0.044715 * (v * v * v))))


def _norm_modulate(x, g, shift, scale):
    ms = jnp.mean(x * x, axis=-1, keepdims=True)
    return (x * lax.rsqrt(ms + RMS_EPS) * g) * (1.0 + scale) + shift


def _mod_kernel(c_ref, w_ref, b_ref, o_ref):
    o_ref[0, 0] = jnp.dot(_silu(c_ref[...]), w_ref[0], preferred_element_type=F32,
                          precision=lax.Precision.HIGHEST) + b_ref[0, 0]


def _modulation(c_all, mod_w, mod_b):
    depth = mod_w.shape[0]
    rows = c_all.shape[0]
    out = pl.pallas_call(
        _mod_kernel,
        grid=(depth, 6),
        in_specs=[
            pl.BlockSpec((rows, D_MODEL), lambda l, j: (0, 0)),
            pl.BlockSpec((1, D_MODEL, D_MODEL), lambda l, j: (l, 0, j)),
            pl.BlockSpec((1, 1, 1, D_MODEL), lambda l, j: (l, j, 0, 0)),
        ],
        out_specs=pl.BlockSpec((1, 1, rows, D_MODEL), lambda l, j: (l, j, 0, 0)),
        out_shape=jax.ShapeDtypeStruct((depth, 6, rows, D_MODEL), F32),
        compiler_params=_params("arbitrary", "arbitrary"),
        name="modulation",
    )(c_all, mod_w, mod_b.reshape(depth, 6, 1, D_MODEL))
    return jnp.transpose(out, (0, 2, 1, 3))


def _mod_index(tm, rows_per_mod, mod_base):
    return lambda i: (mod_base + (i * tm) // rows_per_mod, 0, 0)


def _fill_stage(stage_ref, prev, main, nxt):
    rows = main.shape[0]
    stage_ref[0:HALO, :] = prev
    stage_ref[HALO:HALO + rows, :] = main
    stage_ref[HALO + rows:2 * HALO + rows, :] = nxt


def _dwconv_rows(stage_ref, w_ref, taps, r, rows, c0, width):
    acc = None
    for k in range(taps):
        lo = HALO + r + k - taps // 2
        term = stage_ref[lo:lo + rows, c0:c0 + width] * w_ref[k:k + 1, c0:c0 + width]
        acc = term if acc is None else acc + term
    return acc


def _front_kernel(*refs, tm, seq, local):
    x_ref, xp_ref, xn_ref, mod_ref, g_ref, w_ref, wdt_ref, convw_ref, convb_ref = refs[0:9]
    if local:
        (ws_ref, bs_ref, scw_ref, cfw_ref, vec_ref,
         zssd_ref, yloc_ref, dt_ref, ssd_stage, sc_stage, cf_stage, rot_ref) = refs[9:]
    else:
        zssd_ref, dt_ref, ssd_stage = refs[9:]
    i = pl.program_id(0)
    tiles_per_seq = seq // tm
    keep_prev = jnp.where(i % tiles_per_seq == 0, 0.0, 1.0)
    keep_next = jnp.where(i % tiles_per_seq == tiles_per_seq - 1, 0.0, 1.0)
    g, shift, scale = g_ref[...], mod_ref[0, 0:1, :], mod_ref[0, 1:2, :]
    h = _norm_modulate(x_ref[...], g, shift, scale).astype(BF16)
    h_halo = _norm_modulate(jnp.concatenate([xp_ref[...], xn_ref[...]], axis=0), g, shift, scale).astype(BF16)
    z = jnp.dot(h, w_ref[...], preferred_element_type=F32)
    z_halo = jnp.dot(h_halo, w_ref[...], preferred_element_type=F32)
    zp = z_halo[0:HALO] * keep_prev
    zn = z_halo[HALO:2 * HALO] * keep_next
    dt_ref[...] = jnp.dot(h, wdt_ref[...], preferred_element_type=F32)

    _fill_stage(ssd_stage, zp[:, 0:SSD_CONV_CH], z[:, 0:SSD_CONV_CH], zn[:, 0:SSD_CONV_CH])
    for r in range(0, tm, CHUNK):
        for c0 in range(0, SSD_CONV_CH, BRANCH):
            v = _dwconv_rows(ssd_stage, convw_ref, 5, r, CHUNK, c0, BRANCH) + convb_ref[0:1, c0:c0 + BRANCH]
            zssd_ref[r:r + CHUNK, c0:c0 + BRANCH] = _silu(v).astype(BF16)
    zssd_ref[:, SSD_CONV_CH:W_SSD] = z[:, SSD_CONV_CH:W_SSD].astype(BF16)
    if local:
        _local_mixers_tile(z[:, W_SSD:W_MAIN], zp[:, W_SSD:W_MAIN], zn[:, W_SSD:W_MAIN], ws_ref, bs_ref, scw_ref,
                           cfw_ref, vec_ref, sc_stage, cf_stage, rot_ref, yloc_ref, tm)


def _local_mixers_tile(z, zp, zn, ws_ref, bs_ref, scw_ref, cfw_ref, vec_ref, sc_stage, cf_stage, rot_ref,
                       yloc_ref, tm):
    lane = lax.broadcasted_iota(jnp.int32, (CHUNK, BRANCH), 1)
    gm_g, gm_b = vec_ref[0:1, :], vec_ref[1:2, :]
    cf_b, cf_g, cf_nb = vec_ref[2:3, :], vec_ref[3:4, :], vec_ref[4:5, :]
    c_uv, c_cf, c_sc = 0, 2 * BRANCH, W_UVCF

    def layer_norm(v, gain, bias):
        mu = jnp.mean(v, axis=-1, keepdims=True)
        var = jnp.mean(jnp.square(v - mu), axis=-1, keepdims=True)
        return (v - mu) * lax.rsqrt(var + LN_EPS) * gain + bias

    def sc_in(t):
        return t[:, c_sc + BRANCH:c_sc + 2 * BRANCH] * t[:, c_sc + 2 * BRANCH:c_sc + 3 * BRANCH]

    def cf_in(t):
        return t[:, c_cf:c_cf + BRANCH] * _sigmoid(t[:, c_cf + BRANCH:c_cf + 2 * BRANCH])

    _fill_stage(sc_stage, sc_in(zp), sc_in(z), sc_in(zn))
    _fill_stage(cf_stage, cf_in(zp), cf_in(z), cf_in(zn))
    span = tm + 2 * HALO - SUBLANES
    for phase in range(SUBLANES):
        rot_ref[phase] = cf_stage[phase:phase + span, :]

    for r in range(0, tm, CHUNK):
        ge = _gelu_tanh(z[r:r + CHUNK, c_uv:c_uv + 2 * BRANCH])
        u = ge[:, 0:BRANCH]
        v = layer_norm(ge[:, BRANCH:2 * BRANCH], gm_g, gm_b)
        s = bs_ref[...]
        for hd in range(BRANCH // HEAD_DIM):
            in_head = (lane >= hd * HEAD_DIM) & (lane < (hd + 1) * HEAD_DIM)
            s = s + jnp.dot(ws_ref[hd].astype(BF16), jnp.where(in_head, v, 0.0).astype(BF16),
                            preferred_element_type=F32)
        yloc_ref[r:r + CHUNK, 0:BRANCH] = (u * s).astype(BF16)
        gate = z[r:r + CHUNK, c_sc:c_sc + BRANCH]
        yloc_ref[r:r + CHUNK, BRANCH:2 * BRANCH] = (gate * _dwconv_rows(sc_stage, scw_ref, 3, r, CHUNK, 0, BRANCH)
                                                    ).astype(BF16)
        cv = cf_b
        for k in range(31):
            lo = HALO + r + k - 31 // 2
            q8 = (lo // SUBLANES) * SUBLANES
            cv = cv + rot_ref[lo % SUBLANES, q8:q8 + CHUNK, :] * cfw_ref[k:k + 1, :]
        yloc_ref[r:r + CHUNK, 2 * BRANCH:3 * BRANCH] = _silu(layer_norm(cv, cf_g, cf_nb)).astype(BF16)


def _local_mixer_param_specs():
    const = lambda i, *_: (0, 0)
    return [
        pl.BlockSpec((BRANCH // HEAD_DIM, CHUNK, CHUNK), lambda i, *_: (0, 0, 0)),
        pl.BlockSpec((CHUNK, BRANCH), const),
        pl.BlockSpec((3, BRANCH), const),
        pl.BlockSpec((31, BRANCH), const),
        pl.BlockSpec((5, BRANCH), const),
    ]


def _local_mixer_params(lp):
    return [lp["gm_ws"], lp["gm_bs"], lp["sc_conv_w"], lp["cf_conv_w"], lp["local_vec"]]


def _local_mixer_scratch(tm):
    span = tm + 2 * HALO
    return [pltpu.VMEM((span, BRANCH), F32), pltpu.VMEM((span, BRANCH), F32),
            pltpu.VMEM((SUBLANES, span - SUBLANES, BRANCH), F32)]


def _token_mixer_front(x2d, mod, g, lp, *, tm, seq, rows_per_mod, mod_base, local=True):
    n = x2d.shape[0]
    steps = n // tm
    per = tm // HALO
    row = lambda i: (i, 0)
    const = lambda i: (0, 0)
    in_specs = [
        pl.BlockSpec((tm, D_MODEL), row),
        pl.BlockSpec((HALO, D_MODEL), lambda i: (jnp.maximum(i * per - 1, 0), 0)),
        pl.BlockSpec((HALO, D_MODEL), lambda i: (jnp.minimum((i + 1) * per, n // HALO - 1), 0)),
        pl.BlockSpec((1, 6, D_MODEL), _mod_index(tm, rows_per_mod, mod_base)),
        pl.BlockSpec((1, D_MODEL), const),
        pl.BlockSpec((D_MODEL, W_MAIN if local else W_SSD), const),
        pl.BlockSpec((D_MODEL, LANES), const),
        pl.BlockSpec((5, SSD_CONV_CH), const),
        pl.BlockSpec((1, SSD_CONV_CH), const),
    ]
    args = [x2d, x2d, x2d, mod, g, lp["w_main"], lp["w_dt"], lp["ssd_conv_w"], lp["ssd_conv_b"]]
    out_specs = [pl.BlockSpec((tm, W_SSD), row)]
    out_shape = [jax.ShapeDtypeStruct((n, W_SSD), BF16)]
    scratch = [pltpu.VMEM((tm + 2 * HALO, SSD_CONV_CH), F32)]
    if local:
        in_specs += _local_mixer_param_specs()
        args += _local_mixer_params(lp)
        out_specs.append(pl.BlockSpec((tm, 3 * BRANCH), row))
        out_shape.append(jax.ShapeDtypeStruct((n, 3 * BRANCH), BF16))
        scratch += _local_mixer_scratch(tm)
    out_specs.append(pl.BlockSpec((tm, LANES), row))
    out_shape.append(jax.ShapeDtypeStruct((n, LANES), F32))
    outs = pl.pallas_call(
        functools.partial(_front_kernel, tm=tm, seq=seq, local=local),
        grid=(steps,),
        in_specs=in_specs,
        out_specs=out_specs,
        out_shape=out_shape,
        scratch_shapes=scratch,
        compiler_params=_params("parallel"),
        name="token_mixer_front",
    )(*args)
    return (outs[0], outs[1], outs[2]) if local else (outs[0], None, outs[1])


def _ssd_kernel(z_ref, dt_ref, init_ref, dtp_ref, vec_ref,
                y_ref, fin_ref, p_ref, dts_ref, pt_ref, dtt_ref, yacc_ref, st_ref, *, seq, n_seq):
    nc = seq // CHUNK
    lane = lax.broadcasted_iota(jnp.int32, (CHUNK, LANES), 1)
    row_i = lax.broadcasted_iota(jnp.int32, (CHUNK, CHUNK), 0)
    col_i = lax.broadcasted_iota(jnp.int32, (CHUNK, CHUNK), 1)
    lower = col_i <= row_i
    upper = col_i >= row_i
    head0 = lane < HEAD_DIM

    fwd_lane = (lane % (2 * SSD_HEADS)) < SSD_HEADS
    for q in range(n_seq):
        dt_all = _softplus(dt_ref[q] + dtp_ref[0:1, :])
        adt = dt_all * (-jnp.exp(dtp_ref[1:2, :]))
        p_all = (jnp.dot(lower.astype(F32), jnp.where(fwd_lane, adt, 0.0), preferred_element_type=F32,
                         precision=lax.Precision.HIGHEST)
                 + jnp.dot(upper.astype(F32), jnp.where(fwd_lane, 0.0, adt), preferred_element_type=F32,
                           precision=lax.Precision.HIGHEST))
        pt_ref[q] = p_all.T
        dtt_ref[q] = dt_all.T
        for c in range(nc):
            shift = (LANES - 2 * SSD_HEADS * c) % LANES
            p_ref[q, c * CHUNK:(c + 1) * CHUNK, :] = pltpu.roll(p_all, shift, axis=1) if shift else p_all
            dts_ref[q, c * CHUNK:(c + 1) * CHUNK, :] = pltpu.roll(dt_all, shift, axis=1) if shift else dt_all

    def group_inputs(q, rows, g):
        xg = z_ref[q, rows, g * LANES:(g + 1) * LANES].astype(F32)
        bg = z_ref[q, rows, BRANCH + g * SSD_STATE:BRANCH + (g + 1) * SSD_STATE]
        cg = z_ref[q, rows, BRANCH + (SSD_GROUPS + g) * SSD_STATE:BRANCH + (SSD_GROUPS + g + 1) * SSD_STATE]
        return xg, bg, cg

    def spread(m, j):
        return jnp.broadcast_to(m[:, j:j + 1], (CHUNK, LANES))

    def pair(a0, a1):
        return jnp.where(head0, a0, a1)

    st_ref[...] = init_ref[:, 0]

    def fwd_chunk(q, c):
        r0 = pl.multiple_of(c * CHUNK, CHUNK)
        rows = pl.ds(r0, CHUNK)
        pm = p_ref[q, rows, :]
        dt = dts_ref[q, rows, :]
        head_rows = pl.ds(pl.multiple_of(c * 2 * SSD_HEADS, 2 * SSD_HEADS), 2 * SSD_HEADS)
        pm_t = pt_ref[q, head_rows, :]
        dt_t = dtt_ref[q, head_rows, :]
        last = pm[CHUNK - 1:CHUNK, :]
        to_end = dt * jnp.exp(last - pm)
        for g in range(SSD_GROUPS):
            xg, bg, cg = group_inputs(q, rows, g)
            gram = lax.dot_general(cg, bg, (((1,), (1,)), ((), ())), preferred_element_type=F32)
            s_prev = st_ref[q, g]
            y = jnp.dot(cg, s_prev.astype(BF16), preferred_element_type=F32)
            hf0, hf1 = 2 * g, 2 * g + 1
            hb0, hb1 = SSD_HEADS + hf0, SSD_HEADS + hf1
            p_f0, p_f1 = spread(pm, hf0), spread(pm, hf1)
            y = y * jnp.exp(pair(p_f0, p_f1))
            for p_f, hf, hb, mask in ((p_f0, hf0, hb0, head0), (p_f1, hf1, hb1, jnp.logical_not(head0))):
                dec_f = jnp.exp(jnp.where(lower, p_f - pm_t[hf:hf + 1, :], -jnp.inf))
                dec_b = jnp.exp(jnp.where(upper, spread(pm, hb) - pm_t[hb:hb + 1, :], -jnp.inf))
                m = gram * (dec_f * dt_t[hf:hf + 1, :] + dec_b * dt_t[hb:hb + 1, :])
                xh = jnp.where(mask, xg, 0.0).astype(BF16)
                y = y + jnp.dot(m.astype(BF16), xh, preferred_element_type=F32)
            yacc_ref[q, rows, g * LANES:(g + 1) * LANES] = y
            w_end = pair(spread(to_end, hf0), spread(to_end, hf1))
            upd = lax.dot_general(bg, (xg * w_end).astype(BF16), (((0,), (0,)), ((), ())),
                                  preferred_element_type=F32)
            decay = jnp.exp(jnp.where(head0[0:1, :], last[:, hf0:hf0 + 1], last[:, hf1:hf1 + 1]))
            st_ref[q, g] = s_prev * decay + upd

    def fwd(c, carry):
        for q in range(n_seq):
            fwd_chunk(q, c)
        return carry

    lax.fori_loop(0, nc, fwd, 0, unroll=2)
    fin_ref[:, 0] = st_ref[...]

    st_ref[...] = init_ref[:, 1]
    d_skip = vec_ref[0:1, :]
    norm_g = vec_ref[1:2, :]

    def bwd_chunk(q, c):
        r0 = pl.multiple_of(c * CHUNK, CHUNK)
        rows = pl.ds(r0, CHUNK)
        pm = p_ref[q, rows, :]
        dt = dts_ref[q, rows, :]
        first = pm[0:1, :]
        to_end = dt * jnp.exp(first - pm)
        for g in range(SSD_GROUPS):
            xg, bg, cg = group_inputs(q, rows, g)
            hb0, hb1 = SSD_HEADS + 2 * g, SSD_HEADS + 2 * g + 1
            s_prev = st_ref[q, g]
            y = (jnp.dot(cg, s_prev.astype(BF16), preferred_element_type=F32)
                 * jnp.exp(pair(spread(pm, hb0), spread(pm, hb1))))
            yacc_ref[q, rows, g * LANES:(g + 1) * LANES] = (
                yacc_ref[q, rows, g * LANES:(g + 1) * LANES] + y + d_skip[:, g * LANES:(g + 1) * LANES] * xg)
            w_end = pair(spread(to_end, hb0), spread(to_end, hb1))
            upd = lax.dot_general(bg, (xg * w_end).astype(BF16), (((0,), (0,)), ((), ())),
                                  preferred_element_type=F32)
            decay = jnp.exp(jnp.where(head0[0:1, :], first[:, hb0:hb0 + 1], first[:, hb1:hb1 + 1]))
            st_ref[q, g] = s_prev * decay + upd
        gated = yacc_ref[q, rows, :] * _silu(z_ref[q, rows, SSD_CONV_CH:W_SSD].astype(F32))
        ms = jnp.mean(gated * gated, axis=-1, keepdims=True)
        y_ref[q, rows, :] = (gated * lax.rsqrt(ms + RMS_EPS) * norm_g).astype(BF16)

    def bwd(k, carry):
        for q in range(n_seq):
            bwd_chunk(q, nc - 1 - k)
        return carry

    lax.fori_loop(0, nc, bwd, 0)
    fin_ref[:, 1] = st_ref[...]


def _ssd_mixer(z_ssd, dt, init, lp):
    bsz, seq, _ = z_ssd.shape
    nc = seq // CHUNK
    n_hd = 2 * SSD_HEADS
    n_seq = max(1, min(SSD_MAX_SEQS_PER_STEP, SSD_ROWS_PER_STEP // seq))
    while bsz % n_seq:
        n_seq -= 1
    assert nc * n_hd <= LANES
    dt = dt[:, :, :n_hd].reshape(bsz, nc, CHUNK, n_hd).transpose(0, 2, 1, 3).reshape(bsz, CHUNK, nc * n_hd)
    dt = jnp.pad(dt, ((0, 0), (0, 0), (0, LANES - nc * n_hd)))
    seq_map = lambda b: (b, 0, 0)
    st_map = lambda b: (b, 0, 0, 0, 0)
    const = lambda b: (0, 0)
    st_shape = (bsz, 2, SSD_GROUPS, SSD_STATE, 2 * HEAD_DIM)
    return pl.pallas_call(
        functools.partial(_ssd_kernel, seq=seq, n_seq=n_seq),
        grid=(bsz // n_seq,),
        in_specs=[
            pl.BlockSpec((n_seq, seq, W_SSD), seq_map),
            pl.BlockSpec((n_seq, CHUNK, LANES), seq_map),
            pl.BlockSpec((n_seq,) + st_shape[1:], st_map),
            pl.BlockSpec((2, LANES), const),
            pl.BlockSpec((2, BRANCH), const),
        ],
        out_specs=[
            pl.BlockSpec((n_seq, seq, BRANCH), seq_map),
            pl.BlockSpec((n_seq,) + st_shape[1:], st_map),
        ],
        out_shape=[
            jax.ShapeDtypeStruct((bsz, seq, BRANCH), BF16),
            jax.ShapeDtypeStruct(st_shape, F32),
        ],
        scratch_shapes=[
            pltpu.VMEM((n_seq, seq, LANES), F32),
            pltpu.VMEM((n_seq, seq, LANES), F32),
            pltpu.VMEM((n_seq, LANES, CHUNK), F32),
            pltpu.VMEM((n_seq, LANES, CHUNK), F32),
            pltpu.VMEM((n_seq, seq, BRANCH), F32),
            pltpu.VMEM((n_seq, SSD_GROUPS, SSD_STATE, 2 * HEAD_DIM), F32),
        ],
        compiler_params=_params("parallel"),
        name="ssd_mixer",
    )(z_ssd, dt, init, lp["ssd_dtp"], lp["ssd_vec"])


def _token_mixed(x_ref, mod_ref, ya_ref, yb_ref, wa_ref, wb_ref):
    proj = (jnp.dot(ya_ref[...], wa_ref[...], preferred_element_type=F32)
            + jnp.dot(yb_ref[...], wb_ref[...], preferred_element_type=F32))
    return x_ref[...] + mod_ref[0, 2:3, :] * proj


def _token_mixed_specs(tm, rows_per_mod, mod_base):
    row = lambda i, *_: (i, 0)
    const = lambda i, *_: (0, 0)
    mod_idx = _mod_index(tm, rows_per_mod, mod_base)
    return [
        pl.BlockSpec((tm, D_MODEL), row),
        pl.BlockSpec((1, 6, D_MODEL), lambda i, *_: mod_idx(i)),
        pl.BlockSpec((tm, BRANCH), row),
        pl.BlockSpec((tm, 3 * BRANCH), row),
        pl.BlockSpec((BRANCH, D_MODEL), const),
        pl.BlockSpec((3 * BRANCH, D_MODEL), const),
    ]


def _residual_out(x, gate, delta, fg_ref):
    out = x + gate * delta
    if fg_ref is not None:
        ms = jnp.mean(out * out, axis=-1, keepdims=True)
        out = out * lax.rsqrt(ms + RMS_EPS) * fg_ref[...]
    return out


def _cast_slices(srcs, dsts):
    for src, dst in zip(srcs, dsts):
        dst[...] = src[...].astype(BF16)


def _cast_specs(casts, steps):
    row = lambda i, *_: (i, 0)
    blocks = [(arr.shape[0] // steps, arr.shape[1]) for arr in casts]
    return ([pl.BlockSpec(b, row) for b in blocks], [pl.BlockSpec(b, row) for b in blocks],
            [jax.ShapeDtypeStruct(arr.shape, BF16) for arr in casts])


def _swiglu_kernel(*refs, final_norm, n_casts):
    x_ref, mod_ref, ya_ref, yb_ref, wa_ref, wb_ref, g_ref, w1_ref, w3_ref, w2_ref = refs[0:10]
    pos = 10
    fg_ref = None
    if final_norm:
        fg_ref = refs[pos]
        pos += 1
    cast_in = refs[pos:pos + n_casts]
    o_ref = refs[pos + n_casts]
    cast_out = refs[pos + n_casts + 1:pos + 2 * n_casts + 1]
    x1 = _token_mixed(x_ref, mod_ref, ya_ref, yb_ref, wa_ref, wb_ref)
    h = _norm_modulate(x1, g_ref[...], mod_ref[0, 3:4, :], mod_ref[0, 4:5, :]).astype(BF16)
    a = jnp.dot(h, w1_ref[...], preferred_element_type=F32)
    b = jnp.dot(h, w3_ref[...], preferred_element_type=F32)
    y = jnp.dot((_silu(a) * b).astype(BF16), w2_ref[...], preferred_element_type=F32)
    o_ref[...] = _residual_out(x1, mod_ref[0, 5:6, :], y, fg_ref)
    _cast_slices(cast_in, cast_out)


def _dense_channel_mixer(x2d, mod, y_ssd, y_local, w_a, w_b, g, w1, w3, w2, *, tm, rows_per_mod, mod_base,
                         final_g=None, casts=()):
    n = x2d.shape[0]
    steps = n // tm
    ff = w1.shape[1]
    const = lambda i: (0, 0)
    row = lambda i: (i, 0)
    resident = dict(pipeline_mode=pl.Buffered(1))
    in_specs = _token_mixed_specs(tm, rows_per_mod, mod_base) + [
        pl.BlockSpec((1, D_MODEL), const),
        pl.BlockSpec((D_MODEL, ff), const, **resident),
        pl.BlockSpec((D_MODEL, ff), const, **resident),
        pl.BlockSpec((ff, D_MODEL), const, **resident),
    ]
    args = [x2d, mod, y_ssd, y_local, w_a, w_b, g, w1, w3, w2]
    if final_g is not None:
        in_specs.append(pl.BlockSpec((1, D_MODEL), const))
        args.append(final_g)
    cast_in, cast_out, cast_shape = _cast_specs(casts, steps)
    in_specs += cast_in
    args += list(casts)
    out_specs = [pl.BlockSpec((tm, D_MODEL), row)] + cast_out
    out_shape = [jax.ShapeDtypeStruct((n, D_MODEL), F32)] + cast_shape
    outs = pl.pallas_call(
        functools.partial(_swiglu_kernel, final_norm=final_g is not None, n_casts=len(casts)),
        grid=(steps,),
        in_specs=in_specs,
        out_specs=out_specs,
        out_shape=out_shape,
        compiler_params=_params("parallel"),
        name="dense_channel_mixer",
    )(*args)
    return outs[0], list(outs[1:])


def _can_cast_in_steps(arr2d, steps):
    return arr2d.shape[0] % steps == 0 and (arr2d.shape[0] // steps) % (2 * SUBLANES) == 0


def _router_kernel(*refs, n_casts):
    x_ref, mod_ref, ya_ref, yb_ref, wa_ref, wb_ref, g_ref, r_ref = refs[0:8]
    cast_in = refs[8:8 + n_casts]
    x1_ref, h_ref, comb_ref, rank_ref, rankt_ref, cnt_ref = refs[8 + n_casts:14 + n_casts]
    cast_out = refs[14 + n_casts:14 + 2 * n_casts]
    carry_ref = refs[-1]
    _cast_slices(cast_in, cast_out)

    @pl.when(pl.program_id(0) == 0)
    def _():
        carry_ref[...] = jnp.zeros_like(carry_ref)

    x1 = _token_mixed(x_ref, mod_ref, ya_ref, yb_ref, wa_ref, wb_ref)
    x1_ref[...] = x1
    h = _norm_modulate(x1, g_ref[...], mod_ref[0, 3:4, :], mod_ref[0, 4:5, :])
    h_hi = h.astype(BF16)
    h_ref[...] = h_hi
    h_lo = (h - h_hi.astype(F32)).astype(BF16)
    parts = jnp.dot(jnp.concatenate([h_hi, h_lo], axis=0), r_ref[...], preferred_element_type=F32)
    parts = parts[0:h.shape[0], :] + parts[h.shape[0]:, :]
    logits = parts + pltpu.roll(parts, LANES - N_EXPERTS, axis=1)
    lane = lax.broadcasted_iota(jnp.int32, logits.shape, 1).astype(F32)
    logits = jnp.where(lane < N_EXPERTS, logits, -jnp.inf)
    m1 = jnp.max(logits, axis=-1, keepdims=True)
    i1 = jnp.min(jnp.where(logits == m1, lane, float(LANES)), axis=-1, keepdims=True)
    rest = jnp.where(lane == i1, -jnp.inf, logits)
    m2 = jnp.max(rest, axis=-1, keepdims=True)
    i2 = jnp.min(jnp.where(rest == m2, lane, float(LANES)), axis=-1, keepdims=True)
    e2 = jnp.exp(m2 - m1)
    denom = 1.0 + e2
    comb_ref[...] = jnp.where(lane == i1, 1.0 / denom, 0.0) + jnp.where(lane == i2, e2 / denom, 0.0)
    is_chosen = (lane == i1) | (lane == i2)
    chosen = jnp.where(is_chosen, 1.0, 0.0)
    tm = chosen.shape[0]
    earlier = lax.broadcasted_iota(jnp.int32, (tm, tm), 1) < lax.broadcasted_iota(jnp.int32, (tm, tm), 0)
    running = carry_ref[...]
    before = jnp.dot(jnp.where(earlier, 1.0, 0.0).astype(BF16), chosen.astype(BF16),
                     preferred_element_type=F32) + running.astype(F32)
    rank = jnp.where(is_chosen, before, -1.0)
    rank_ref[...] = rank
    rankt_ref[...] = rank.T[0:N_EXPERTS, :]
    chosen_i = jnp.where(is_chosen, 1, 0)
    for k in range(tm // TOK_TILE):
        running = running + jnp.sum(chosen_i[k * TOK_TILE:(k + 1) * TOK_TILE, :], axis=0, keepdims=True)
        cnt_ref[0, k:k + 1, :] = running
    carry_ref[...] = running


def _router(x2d, mod, y_ssd, y_local, w_a, w_b, g, router_pad, *, tm, rows_per_mod, mod_base, casts=()):
    n = x2d.shape[0]
    row = lambda i: (i, 0)
    const = lambda i: (0, 0)
    per = tm // TOK_TILE
    cast_in, cast_out, cast_shape = _cast_specs(casts, n // tm)
    return pl.pallas_call(
        functools.partial(_router_kernel, n_casts=len(casts)),
        grid=(n // tm,),
        in_specs=_token_mixed_specs(tm, rows_per_mod, mod_base) + [
            pl.BlockSpec((1, D_MODEL), const),
            pl.BlockSpec((D_MODEL, LANES), const),
        ] + cast_in,
        out_specs=[
            pl.BlockSpec((tm, D_MODEL), row),
            pl.BlockSpec((tm, D_MODEL), row),
            pl.BlockSpec((tm, LANES), row),
            pl.BlockSpec((tm, LANES), row),
            pl.BlockSpec((N_EXPERTS, tm), lambda i: (0, i)),
            pl.BlockSpec((1, per, LANES), lambda i: (i, 0, 0)),
        ] + cast_out,
        out_shape=[
            jax.ShapeDtypeStruct((n, D_MODEL), F32),
            jax.ShapeDtypeStruct((n, D_MODEL), BF16),
            jax.ShapeDtypeStruct((n, LANES), F32),
            jax.ShapeDtypeStruct((n, LANES), F32),
            jax.ShapeDtypeStruct((N_EXPERTS, n), F32),
            jax.ShapeDtypeStruct((n // tm, per, LANES), jnp.int32),
        ] + cast_shape,
        scratch_shapes=[pltpu.VMEM((1, LANES), jnp.int32)],
        compiler_params=_params("arbitrary"),
        name="moe_router",
    )(x2d, mod, y_ssd, y_local, w_a, w_b, g, router_pad, *casts)


def _routing_tables(cnt, n_tokens):
    i32 = jnp.int32
    n_t = n_tokens // TOK_TILE
    cnt_end = cnt.reshape(n_t, LANES)[:, :N_EXPERTS]
    cnt_start = jnp.concatenate([jnp.zeros((1, N_EXPERTS), i32), cnt_end[:-1]], axis=0)
    tot = cnt_end[-1]
    nblk = (tot + ROW_BLOCK - 1) // ROW_BLOCK
    blk_cum = jnp.cumsum(nblk)
    blk_first = blk_cum - nblk
    n_valid = blk_cum[-1]
    nb = n_tokens * TOP_K // ROW_BLOCK + N_EXPERTS
    bi = jnp.arange(nb, dtype=i32)
    valid = bi < n_valid
    e_of = jnp.sum((bi[:, None] >= blk_cum[None, :]).astype(i32), axis=1)
    e_last = jnp.sum(((n_valid - 1) >= blk_cum).astype(i32))
    blk_e = jnp.minimum(jnp.where(valid, e_of, e_last), N_EXPERTS - 1)
    base = jnp.where(valid, (bi - blk_first[blk_e]) * ROW_BLOCK, 0)
    lo = base[:, None] + jnp.arange(N_SUB, dtype=i32)[None, :] * SUB_BLOCK
    hi = jnp.where(valid[:, None], jnp.minimum(lo + SUB_BLOCK, tot[blk_e][:, None]), lo)
    cs = cnt_start[:, blk_e]
    ce = cnt_end[:, blk_e]
    rel = (cs[:, :, None] < hi[None]) & (ce[:, :, None] > lo[None])
    tile_id = jnp.arange(n_t, dtype=i32)[:, None, None]
    t_first = jnp.min(jnp.where(rel, tile_id, n_t), axis=0)
    sub_hi = jnp.max(jnp.where(rel, tile_id, -1), axis=0)
    sub_lo = jnp.where(sub_hi >= 0, t_first, 0)
    u_lo = jnp.min(t_first, axis=1)
    u_n = jnp.maximum(jnp.max(sub_hi, axis=1) - u_lo + 1, 0)
    u_lo = jnp.where(u_n > 0, u_lo, 0)
    per = COMBINE_TM // TOK_TILE
    seg_rows = blk_first * ROW_BLOCK
    lo_c = seg_rows[None, :] + cnt_start[0::per]
    cnt_c = cnt_end[per - 1::per] - cnt_start[0::per]
    first_c = (lo_c // 16) * 16
    nwin_c = jnp.where(cnt_c > 0, (lo_c + cnt_c - first_c + COMBINE_WIN - 1) // COMBINE_WIN, 0)
    return dict(blk_e=blk_e.astype(i32), valid=valid.astype(i32), base=base.astype(i32),
                u_lo=u_lo.astype(i32), u_n=u_n.astype(i32),
                sub_lo=sub_lo.reshape(-1).astype(i32), sub_hi=sub_hi.reshape(-1).astype(i32),
                seg_rows=seg_rows.astype(i32), first_c=first_c.reshape(-1).astype(i32),
                nwin_c=nwin_c.reshape(-1).astype(i32), n_blocks=nb)


def _experts_kernel(be_ref, bv_ref, base_ref, ulo_ref, un_ref, slo_ref, shi_ref,
                    h_hbm, rt_hbm, w1_ref, w3_ref, w2_ref, y_ref, hbuf, rbuf, hsem, rsem):
    i = pl.program_id(0)
    slot = i % 2

    def tile_copies(s, k, t):
        tok = pl.multiple_of(t * TOK_TILE, TOK_TILE)
        return (pltpu.make_async_copy(h_hbm.at[pl.ds(tok, TOK_TILE)], hbuf.at[s, k], hsem.at[s, k]),
                pltpu.make_async_copy(rt_hbm.at[:, pl.ds(tok, TOK_TILE)], rbuf.at[s, k], rsem.at[s, k]))

    def for_block_tiles(b, s, action):
        for k in range(GATHER_TILES):
            @pl.when(k < un_ref[b])
            def _():
                for cp in tile_copies(s, k, ulo_ref[b] + k):
                    action(cp)

    @pl.when(i == 0)
    def _():
        def clear(k, carry):
            hbuf[k // (GATHER_TILES + 1), k % (GATHER_TILES + 1)] = jnp.zeros((TOK_TILE, D_MODEL), BF16)
            rbuf[k // (GATHER_TILES + 1), k % (GATHER_TILES + 1)] = jnp.zeros((N_EXPERTS, TOK_TILE), F32)
            return carry

        lax.fori_loop(0, 2 * (GATHER_TILES + 1), clear, 0)
        for_block_tiles(0, 0, lambda cp: cp.start())

    for_block_tiles(i, slot, lambda cp: cp.wait())

    @pl.when(i + 1 < pl.num_programs(0))
    def _():
        for_block_tiles(i + 1, 1 - slot, lambda cp: cp.start())

    @pl.when(bv_ref[i] == 1)
    def _():
        e = be_ref[i]
        row_id = lax.broadcasted_iota(jnp.int32, (SUB_BLOCK, TOK_TILE), 0).astype(F32)
        expert_row = lax.broadcasted_iota(jnp.int32, (N_EXPERTS, TOK_TILE), 0) == e
        u_lo = ulo_ref[i]
        selected = []

        def tile_weights(ranks8, first_rank):
            ranks = jnp.sum(jnp.where(expert_row, ranks8, 0.0), axis=0, keepdims=True)
            return jnp.where(ranks == first_rank + row_id, 1.0, 0.0).astype(BF16)

        for sb in range(N_SUB):
            first_rank = (base_ref[i] + sb * SUB_BLOCK).astype(F32)
            t_lo, t_hi = slo_ref[N_SUB * i + sb], shi_ref[N_SUB * i + sb]
            k0 = jnp.clip(t_lo - u_lo, 0, GATHER_TILES + 1 - WINDOW_TILES)
            picks = []
            for j in range(WINDOW_TILES):
                t = u_lo + k0 + j
                live = (t >= t_lo) & (t <= t_hi) & (k0 + j < GATHER_TILES)
                picks.append(tile_weights(jnp.where(live, rbuf[slot, k0 + j], -1.0), first_rank))
            window = hbuf[slot, pl.ds(k0, WINDOW_TILES)].reshape(WINDOW_TILES * TOK_TILE, D_MODEL)
            rows = jnp.dot(jnp.concatenate(picks, axis=1), window, preferred_element_type=F32)

            def late_tile(t, carry):
                k = t - u_lo
                kk = jnp.minimum(k, GATHER_TILES)

                @pl.when(k >= GATHER_TILES)
                def _():
                    tok = pl.multiple_of(t * TOK_TILE, TOK_TILE)
                    pltpu.sync_copy(h_hbm.at[pl.ds(tok, TOK_TILE)], hbuf.at[slot, GATHER_TILES])
                    pltpu.sync_copy(rt_hbm.at[:, pl.ds(tok, TOK_TILE)], rbuf.at[slot, GATHER_TILES])

                return carry + jnp.dot(tile_weights(rbuf[slot, kk], first_rank), hbuf[slot, kk],
                                       preferred_element_type=F32)

            t_next = u_lo + jnp.minimum(k0 + WINDOW_TILES, GATHER_TILES)
            rows = lax.fori_loop(jnp.maximum(t_next, t_lo), t_hi + 1, late_tile, rows)
            selected.append(rows.astype(BF16))

        hg = jnp.concatenate(selected, axis=0)
        half = w1_ref.shape[2] // 2
        y = None
        for c in range(2):
            a = jnp.dot(hg, w1_ref[0, :, c * half:(c + 1) * half], preferred_element_type=F32)
            b = jnp.dot(hg, w3_ref[0, :, c * half:(c + 1) * half], preferred_element_type=F32)
            part = jnp.dot((_silu(a) * b).astype(BF16), w2_ref[0, c * half:(c + 1) * half, :],
                           preferred_element_type=F32)
            y = part if y is None else y + part
        y_ref[...] = y.astype(BF16)

    @pl.when(bv_ref[i] == 0)
    def _():
        y_ref[...] = jnp.zeros_like(y_ref)


def _moe_experts(h2, rank_t, w1, w3, w2, tb):
    nb = tb["n_blocks"]
    ff = w1.shape[2]
    w_map = lambda i, be, *_: (be[i], 0, 0)
    grid_spec = pltpu.PrefetchScalarGridSpec(
        num_scalar_prefetch=7,
        grid=(nb,),
        in_specs=[
            pl.BlockSpec(memory_space=pl.ANY),
            pl.BlockSpec(memory_space=pl.ANY),
            pl.BlockSpec((1, D_MODEL, ff), w_map, pipeline_mode=pl.Buffered(1)),
            pl.BlockSpec((1, D_MODEL, ff), w_map, pipeline_mode=pl.Buffered(1)),
            pl.BlockSpec((1, ff, D_MODEL), w_map, pipeline_mode=pl.Buffered(1)),
        ],
        out_specs=pl.BlockSpec((ROW_BLOCK, D_MODEL), lambda i, *_: (i, 0)),
        scratch_shapes=[
            pltpu.VMEM((2, GATHER_TILES + 1, TOK_TILE, D_MODEL), BF16),
            pltpu.VMEM((2, GATHER_TILES + 1, N_EXPERTS, TOK_TILE), F32),
            pltpu.SemaphoreType.DMA((2, GATHER_TILES)),
            pltpu.SemaphoreType.DMA((2, GATHER_TILES)),
        ],
    )
    return pl.pallas_call(
        _experts_kernel,
        grid_spec=grid_spec,
        out_shape=jax.ShapeDtypeStruct((nb * ROW_BLOCK, D_MODEL), BF16),
        compiler_params=_params("arbitrary"),
        name="moe_experts",
    )(tb["blk_e"], tb["valid"], tb["base"], tb["u_lo"], tb["u_n"], tb["sub_lo"], tb["sub_hi"],
      h2, rank_t, w1, w3, w2)


def _combine_kernel(*refs, final_norm, n_rows):
    first_ref, nwin_ref, seg_ref, x_ref, mod_ref, comb_ref, rank_ref, y_hbm = refs[0:8]
    fg_ref = refs[8] if final_norm else None
    o_ref, ybuf, ysem, acc_ref = refs[-4:]
    i = pl.program_id(0)
    slot = i % 2
    extra = N_EXPERTS

    def window(b, e):
        return first_ref[b * N_EXPERTS + e], nwin_ref[b * N_EXPERTS + e]

    def win_start(first, j):
        return pl.multiple_of(jnp.minimum(first + j * COMBINE_WIN, n_rows - COMBINE_WIN), 16)

    def for_tile_windows(b, s, action):
        for e in range(N_EXPERTS):
            first, n_win = window(b, e)

            @pl.when(n_win > 0)
            def _():
                action(pltpu.make_async_copy(y_hbm.at[pl.ds(win_start(first, 0), COMBINE_WIN)],
                                             ybuf.at[s, e], ysem.at[s, e]))

    @pl.when(i == 0)
    def _():
        def clear(k, carry):
            ybuf[k // (N_EXPERTS + 1), k % (N_EXPERTS + 1)] = jnp.zeros((COMBINE_WIN, D_MODEL), BF16)
            return carry

        lax.fori_loop(0, 2 * (N_EXPERTS + 1), clear, 0)
        for_tile_windows(0, 0, lambda cp: cp.start())

    for_tile_windows(i, slot, lambda cp: cp.wait())

    @pl.when(i + 1 < pl.num_programs(0))
    def _():
        for_tile_windows(i + 1, 1 - slot, lambda cp: cp.start())

    col_id = lax.broadcasted_iota(jnp.int32, (COMBINE_TM, COMBINE_WIN), 1).astype(F32)

    def weights(e, first, j):
        pos_col = rank_ref[:, e:e + 1] + seg_ref[e].astype(F32)
        start = win_start(first, j).astype(F32)
        return jnp.where(pos_col == start + col_id, comb_ref[:, e:e + 1], 0.0).astype(BF16)

    firsts = [window(i, e) for e in range(N_EXPERTS)]
    pick_all = jnp.concatenate([weights(e, firsts[e][0], 0) for e in range(N_EXPERTS)], axis=1)
    y_all = ybuf[slot, 0:N_EXPERTS].reshape(N_EXPERTS * COMBINE_WIN, D_MODEL)
    acc_ref[...] = jnp.dot(pick_all, y_all, preferred_element_type=F32)

    for e in range(N_EXPERTS):
        first, n_win = firsts[e]

        def more(j, carry):
            pltpu.sync_copy(y_hbm.at[pl.ds(win_start(first, j), COMBINE_WIN)], ybuf.at[slot, extra])
            acc_ref[...] += jnp.dot(weights(e, first, j), ybuf[slot, extra], preferred_element_type=F32)
            return carry

        lax.fori_loop(1, jnp.maximum(n_win, 1), more, 0)

    o_ref[...] = _residual_out(x_ref[...], mod_ref[0, 5:6, :], acc_ref[...], fg_ref)


def _moe_combine(x2d, mod, comb, rank, y_sorted, tb, *, rows_per_mod, mod_base, final_g=None):
    n = x2d.shape[0]
    tm = COMBINE_TM
    row = lambda i, *_: (i, 0)
    const = lambda i, *_: (0, 0)
    mod_idx = _mod_index(tm, rows_per_mod, mod_base)
    in_specs = [
        pl.BlockSpec((tm, D_MODEL), row),
        pl.BlockSpec((1, 6, D_MODEL), lambda i, *_: mod_idx(i)),
        pl.BlockSpec((tm, LANES), row),
        pl.BlockSpec((tm, LANES), row),
        pl.BlockSpec(memory_space=pl.ANY),
    ]
    args = [x2d, mod, comb, rank, y_sorted]
    if final_g is not None:
        in_specs.append(pl.BlockSpec((1, D_MODEL), const))
        args.append(final_g)
    grid_spec = pltpu.PrefetchScalarGridSpec(
        num_scalar_prefetch=3,
        grid=(n // tm,),
        in_specs=in_specs,
        out_specs=pl.BlockSpec((tm, D_MODEL), row),
        scratch_shapes=[
            pltpu.VMEM((2, N_EXPERTS + 1, COMBINE_WIN, D_MODEL), BF16),
            pltpu.SemaphoreType.DMA((2, N_EXPERTS)),
            pltpu.VMEM((tm, D_MODEL), F32),
        ],
    )
    return pl.pallas_call(
        functools.partial(_combine_kernel, final_norm=final_g is not None, n_rows=y_sorted.shape[0]),
        grid_spec=grid_spec,
        out_shape=jax.ShapeDtypeStruct((n, D_MODEL), F32),
        compiler_params=_params("arbitrary"),
        name="moe_combine",
    )(tb["first_c"], tb["nwin_c"], tb["seg_rows"], *args)


def _to_bf16_in(w, steps):
    if w.dtype == BF16:
        return None, w
    flat = w.reshape(-1, w.shape[-1])
    if _can_cast_in_steps(flat, steps):
        return flat, None
    return None, w.astype(BF16)


def _moe_channel_mixer(x2d, mod, y_ssd, y_local, w_a, w_b, g, router_pad, w1, w3, w2, *, rows_per_mod, mod_base,
                       final_g=None):
    w1 = w1.astype(BF16)
    w3 = w3.astype(BF16)
    w2_flat, w2_bf16 = _to_bf16_in(w2, x2d.shape[0] // ROUTER_TM)
    x1, h2, comb, rank, rank_t, cnt, *cast = _router(
        x2d, mod, y_ssd, y_local, w_a, w_b, g, router_pad, tm=ROUTER_TM, rows_per_mod=rows_per_mod,
        mod_base=mod_base, casts=() if w2_flat is None else (w2_flat,))
    if w2_bf16 is None:
        w2_bf16 = cast[0].reshape(w2.shape)
    w2 = w2_bf16
    tb = _routing_tables(cnt, x2d.shape[0])
    y_sorted = _moe_experts(h2, rank_t, w1, w3, w2, tb)
    return _moe_combine(x1, mod, comb, rank, y_sorted, tb, rows_per_mod=rows_per_mod, mod_base=mod_base,
                        final_g=final_g)


def _layer_params(l, w_in, w_out, ssd_conv_w, ssd_conv_b, ssd_dt_bias, ssd_a_log, ssd_d, ssd_norm_g,
                  gm_norm_g, gm_norm_b, gm_ws, gm_bs, sc_conv_w, cf_conv_w, cf_conv_b, cf_norm_g, cf_norm_b):
    wi = w_in[l]
    w_main = jnp.concatenate(
        [wi[:, _C_XBC:_C_DT], wi[:, _C_Z:_C_UV], wi[:, _C_UV:_C_SC], wi[:, _C_CF:_C_END], wi[:, _C_SC:_C_CF]],
        axis=1).astype(BF16)
    w_dt = jnp.pad(wi[:, _C_DT:_C_Z], ((0, 0), (0, LANES - 2 * SSD_HEADS))).astype(BF16)
    return {
        "w_main": w_main,
        "w_dt": w_dt,
        "w_out_a": w_out[l, 0:BRANCH].astype(BF16),
        "w_out_b": w_out[l, BRANCH:].astype(BF16),
        "ssd_conv_w": ssd_conv_w[l],
        "ssd_conv_b": ssd_conv_b[l].reshape(1, -1),
        "ssd_dtp": jnp.stack([jnp.tile(ssd_dt_bias[l].reshape(-1), LANES // (2 * SSD_HEADS)),
                              jnp.tile(ssd_a_log[l].reshape(-1), LANES // (2 * SSD_HEADS))]),
        "ssd_vec": jnp.stack([jnp.repeat(ssd_d[l], HEAD_DIM), ssd_norm_g[l]]),
        "gm_ws": gm_ws[l],
        "gm_bs": jnp.repeat(gm_bs[l].T, HEAD_DIM, axis=1),
        "sc_conv_w": sc_conv_w[l],
        "cf_conv_w": cf_conv_w[l],
        "local_vec": jnp.stack([gm_norm_g[l], gm_norm_b[l], cf_conv_b[l], cf_norm_g[l], cf_norm_b[l]]),
    }


def kernel(x, c, ctx, c_ctx, mod_w, mod_b, norm1_g, norm2_g, w_in, w_out, ssd_conv_w, ssd_conv_b, ssd_dt_bias, ssd_a_log, ssd_d, ssd_norm_g, gm_norm_g, gm_norm_b, gm_ws, gm_bs, sc_conv_w, cf_conv_w, cf_conv_b, cf_norm_g, cf_norm_b, ffn_w1, ffn_w3, ffn_w2, moe_router, moe_w1, moe_w3, moe_w2, final_norm_g):
    bsz, seq, _ = x.shape
    ctx_len = ctx.shape[1]
    depth = w_in.shape[0]
    n_x, n_c = bsz * seq, bsz * ctx_len
    tm = 512

    mod = _modulation(jnp.concatenate([c, c_ctx[None, :]], axis=0), mod_w, mod_b)
    x_map = dict(tm=tm, rows_per_mod=seq, mod_base=0)
    c_map = dict(tm=tm, rows_per_mod=n_c, mod_base=bsz)
    zero_state = jnp.zeros((bsz, 2, SSD_GROUPS, SSD_STATE, 2 * HEAD_DIM), F32)

    xs = x.reshape(n_x, D_MODEL)
    xc = ctx.reshape(n_c, D_MODEL)
    moe_bf16 = {}
    for l in range(depth):
        last = l == depth - 1
        lp = _layer_params(l, w_in, w_out, ssd_conv_w, ssd_conv_b, ssd_dt_bias, ssd_a_log, ssd_d, ssd_norm_g,
                           gm_norm_g, gm_norm_b, gm_ws, gm_bs, sc_conv_w, cf_conv_w, cf_conv_b, cf_norm_g,
                           cf_norm_b)
        g1 = norm1_g[l].reshape(1, D_MODEL)
        g2 = norm2_g[l].reshape(1, D_MODEL)
        zc_ssd, yc_loc, dtc = _token_mixer_front(xc, mod[l], g1, lp, tm=min(tm, ctx_len), seq=ctx_len,
                                                 rows_per_mod=n_c, mod_base=bsz, local=not last)
        yc_ssd, states = _ssd_mixer(zc_ssd.reshape(bsz, ctx_len, W_SSD), dtc.reshape(bsz, ctx_len, LANES),
                                    zero_state, lp)
        z_ssd, y_loc, dtx = _token_mixer_front(xs, mod[l], g1, lp, tm=tm, seq=seq, rows_per_mod=seq, mod_base=0)
        y_ssd, _ = _ssd_mixer(z_ssd.reshape(bsz, seq, W_SSD), dtx.reshape(bsz, seq, LANES), states, lp)
        streams = [(xs, y_ssd.reshape(n_x, BRANCH), y_loc, x_map)]
        if not last:
            streams.append((xc, yc_ssd.reshape(n_c, BRANCH), yc_loc, c_map))
        fg = final_norm_g.reshape(1, D_MODEL) if last else None
        i = l // 2
        if l % 2 == 0:
            w1, w3, w2 = ffn_w1[i].astype(BF16), ffn_w3[i].astype(BF16), ffn_w2[i].astype(BF16)
            casts = []
            if l + 1 < depth:
                casts = [w.reshape(-1, w.shape[-1]) for w in (moe_w1[i], moe_w3[i])]
                if not all(_can_cast_in_steps(w, n_x // tm) for w in casts):
                    casts = []
            outs = []
            for t, ya, yb, m in streams:
                out, cast = _dense_channel_mixer(t, mod[l], ya, yb, lp["w_out_a"], lp["w_out_b"], g2, w1, w3, w2,
                                                 final_g=fg if t is xs else None, casts=casts if t is xs else (),
                                                 **m)
                outs.append(out)
                if cast:
                    moe_bf16[i] = [c.reshape(moe_w1[i].shape) for c in cast]
        else:
            w1, w3 = moe_bf16.get(i) or (moe_w1[i], moe_w3[i])
            w2 = moe_w2[i]
            r_hi = moe_router[i].astype(BF16)
            r_lo = (moe_router[i] - r_hi.astype(F32)).astype(BF16)
            r_pad = jnp.pad(jnp.concatenate([r_hi, r_lo], axis=1), ((0, 0), (0, LANES - 2 * N_EXPERTS)))
            outs = [_moe_channel_mixer(t, mod[l], ya, yb, lp["w_out_a"], lp["w_out_b"], g2, r_pad, w1, w3, w2,
                                       rows_per_mod=m["rows_per_mod"], mod_base=m["mod_base"],
                                       final_g=fg if t is xs else None) for t, ya, yb, m in streams]
        xs = outs[0]
        if not last:
            xc = outs[1]
    return xs.reshape(bsz, seq, D_MODEL)
```

```python
import functools

import jax
import jax.numpy as jnp
from jax import lax
from jax.experimental import pallas as pl
from jax.experimental.pallas import tpu as pltpu

F32 = jnp.float32
BF16 = jnp.bfloat16

LANES = 128
SUBLANES = 8
VMEM_LIMIT_BYTES = 56 * 1024 * 1024

D_MODEL = 1024
BRANCH = 256
HEAD_DIM = 64
SSD_HEADS = 4
SSD_GROUPS = 2
SSD_STATE = 128
CHUNK = 128
SSD_ROWS_PER_STEP = 4096
SSD_MAX_SEQS_PER_STEP = 4
SSD_CONV_CH = BRANCH + 2 * SSD_GROUPS * SSD_STATE
N_EXPERTS = 8
TOP_K = 2
TOK_TILE = 256
ROW_BLOCK = 512
SUB_BLOCK = 256
N_SUB = ROW_BLOCK // SUB_BLOCK
GATHER_TILES = 11
WINDOW_TILES = 5
ROUTER_TM = 512
COMBINE_TM = 512
COMBINE_WIN = 256
FRONT_TM = 1024
HALO = 16
RMS_EPS = 1e-6
LN_EPS = 1e-5

_C_XBC = 0
_C_DT = _C_XBC + SSD_CONV_CH
_C_Z = _C_DT + 2 * SSD_HEADS
_C_UV = _C_Z + BRANCH
_C_SC = _C_UV + 2 * BRANCH
_C_CF = _C_SC + 3 * BRANCH
_C_END = _C_CF + 2 * BRANCH
W_SSD = SSD_CONV_CH + BRANCH
W_UVCF = 4 * BRANCH
W_SC = 3 * BRANCH
W_MAIN = W_SSD + W_UVCF + W_SC


def _params(*sem):
    return pltpu.CompilerParams(dimension_semantics=sem, vmem_limit_bytes=VMEM_LIMIT_BYTES)


def _sigmoid(v):
    return 0.5 + 0.5 * jnp.tanh(0.5 * v)


def _silu(v):
    half = 0.5 * v
    return half + half * jnp.tanh(half)


def _softplus(v):
    return jnp.maximum(v, 0.0) + jnp.log1p(jnp.exp(-jnp.abs(v)))


def _gelu_tanh(v):
    return 0.5 * v * (1.0 + jnp.tanh(0.7978845608028654 * (v + 0.044715 * (v * v * v))))


def _norm_modulate(x, g, shift, scale):
    ms = jnp.mean(x * x, axis=-1, keepdims=True)
    return (x * lax.rsqrt(ms + RMS_EPS) * g) * (1.0 + scale) + shift


def _mod_kernel(c_ref, w_ref, b_ref, o_ref):
    o_ref[0, 0] = jnp.dot(_silu(c_ref[...]), w_ref[0], preferred_element_type=F32,
                          precision=lax.Precision.HIGHEST) + b_ref[0, 0]


def _modulation(c_all, mod_w, mod_b):
    depth = mod_w.shape[0]
    rows = c_all.shape[0]
    out = pl.pallas_call(
        _mod_kernel,
        grid=(depth, 6),
        in_specs=[
            pl.BlockSpec((rows, D_MODEL), lambda l, j: (0, 0)),
            pl.BlockSpec((1, D_MODEL, D_MODEL), lambda l, j: (l, 0, j)),
            pl.BlockSpec((1, 1, 1, D_MODEL), lambda l, j: (l, j, 0, 0)),
        ],
        out_specs=pl.BlockSpec((1, 1, rows, D_MODEL), lambda l, j: (l, j, 0, 0)),
        out_shape=jax.ShapeDtypeStruct((depth, 6, rows, D_MODEL), F32),
        compiler_params=_params("arbitrary", "arbitrary"),
        name="modulation",
    )(c_all, mod_w, mod_b.reshape(depth, 6, 1, D_MODEL))
    return jnp.transpose(out, (0, 2, 1, 3))


def _mod_index(tm, rows_per_mod, mod_base):
    return lambda i: (mod_base + (i * tm) // rows_per_mod, 0, 0)


def _fill_stage(stage_ref, prev, main, nxt):
    rows = main.shape[0]
    stage_ref[0:HALO, :] = prev
    stage_ref[HALO:HALO + rows, :] = main
    stage_ref[HALO + rows:2 * HALO + rows, :] = nxt


def _dwconv_rows(stage_ref, w_ref, taps, r, rows, c0, width):
    acc = None
    for k in range(taps):
        lo = HALO + r + k - taps // 2
        term = stage_ref[lo:lo + rows, c0:c0 + width] * w_ref[k:k + 1, c0:c0 + width]
        acc = term if acc is None else acc + term
    return acc


def _front_kernel(*refs, tm, seq, local):
    x_ref, xp_ref, xn_ref, mod_ref, g_ref, w_ref, wdt_ref, convw_ref, convb_ref = refs[0:9]
    if local:
        (ws_ref, bs_ref, scw_ref, cfw_ref, vec_ref,
         zssd_ref, yloc_ref, dt_ref, ssd_stage, sc_stage, cf_stage, rot_ref) = refs[9:]
    else:
        zssd_ref, dt_ref, ssd_stage = refs[9:]
    i = pl.program_id(0)
    tiles_per_seq = seq // tm
    keep_prev = jnp.where(i % tiles_per_seq == 0, 0.0, 1.0)
    keep_next = jnp.where(i % tiles_per_seq == tiles_per_seq - 1, 0.0, 1.0)
    g, shift, scale = g_ref[...], mod_ref[0, 0:1, :], mod_ref[0, 1:2, :]
    h = _norm_modulate(x_ref[...], g, shift, scale).astype(BF16)
    h_halo = _norm_modulate(jnp.concatenate([xp_ref[...], xn_ref[...]], axis=0), g, shift, scale).astype(BF16)
    z = jnp.dot(h, w_ref[...], preferred_element_type=F32)
    z_halo = jnp.dot(h_halo, w_ref[...], preferred_element_type=F32)
    zp = z_halo[0:HALO] * keep_prev
    zn = z_halo[HALO:2 * HALO] * keep_next
    dt_ref[...] = jnp.dot(h, wdt_ref[...], preferred_element_type=F32)

    _fill_stage(ssd_stage, zp[:, 0:SSD_CONV_CH], z[:, 0:SSD_CONV_CH], zn[:, 0:SSD_CONV_CH])
    for r in range(0, tm, CHUNK):
        for c0 in range(0, SSD_CONV_CH, BRANCH):
            v = _dwconv_rows(ssd_stage, convw_ref, 5, r, CHUNK, c0, BRANCH) + convb_ref[0:1, c0:c0 + BRANCH]
            zssd_ref[r:r + CHUNK, c0:c0 + BRANCH] = _silu(v).astype(BF16)
    zssd_ref[:, SSD_CONV_CH:W_SSD] = z[:, SSD_CONV_CH:W_SSD].astype(BF16)
    if local:
        _local_mixers_tile(z[:, W_SSD:W_MAIN], zp[:, W_SSD:W_MAIN], zn[:, W_SSD:W_MAIN], ws_ref, bs_ref, scw_ref,
                           cfw_ref, vec_ref, sc_stage, cf_stage, rot_ref, yloc_ref, tm)


def _local_mixers_tile(z, zp, zn, ws_ref, bs_ref, scw_ref, cfw_ref, vec_ref, sc_stage, cf_stage, rot_ref,
                       yloc_ref, tm):
    lane = lax.broadcasted_iota(jnp.int32, (CHUNK, BRANCH), 1)
    gm_g, gm_b = vec_ref[0:1, :], vec_ref[1:2, :]
    cf_b, cf_g, cf_nb = vec_ref[2:3, :], vec_ref[3:4, :], vec_ref[4:5, :]
    c_uv, c_cf, c_sc = 0, 2 * BRANCH, W_UVCF

    def layer_norm(v, gain, bias):
        mu = jnp.mean(v, axis=-1, keepdims=True)
        var = jnp.mean(jnp.square(v - mu), axis=-1, keepdims=True)
        return (v - mu) * lax.rsqrt(var + LN_EPS) * gain + bias

    def sc_in(t):
        return t[:, c_sc + BRANCH:c_sc + 2 * BRANCH] * t[:, c_sc + 2 * BRANCH:c_sc + 3 * BRANCH]

    def cf_in(t):
        return t[:, c_cf:c_cf + BRANCH] * _sigmoid(t[:, c_cf + BRANCH:c_cf + 2 * BRANCH])

    _fill_stage(sc_stage, sc_in(zp), sc_in(z), sc_in(zn))
    _fill_stage(cf_stage, cf_in(zp), cf_in(z), cf_in(zn))
    span = tm + 2 * HALO - SUBLANES
    for phase in range(SUBLANES):
        rot_ref[phase] = cf_stage[phase:phase + span, :]

    for r in range(0, tm, CHUNK):
        ge = _gelu_tanh(z[r:r + CHUNK, c_uv:c_uv + 2 * BRANCH])
        u = ge[:, 0:BRANCH]
        v = layer_norm(ge[:, BRANCH:2 * BRANCH], gm_g, gm_b)
        s = bs_ref[...]
        for hd in range(BRANCH // HEAD_DIM):
            in_head = (lane >= hd * HEAD_DIM) & (lane < (hd + 1) * HEAD_DIM)
            s = s + jnp.dot(ws_ref[hd].astype(BF16), jnp.where(in_head, v, 0.0).astype(BF16),
                            preferred_element_type=F32)
        yloc_ref[r:r + CHUNK, 0:BRANCH] = (u * s).astype(BF16)
        gate = z[r:r + CHUNK, c_sc:c_sc + BRANCH]
        yloc_ref[r:r + CHUNK, BRANCH:2 * BRANCH] = (gate * _dwconv_rows(sc_stage, scw_ref, 3, r, CHUNK, 0, BRANCH)
                                                    ).astype(BF16)
        cv = cf_b
        for k in range(31):
            lo = HALO + r + k - 31 // 2
            q8 = (lo // SUBLANES) * SUBLANES
            cv = cv + rot_ref[lo % SUBLANES, q8:q8 + CHUNK, :] * cfw_ref[k:k + 1, :]
        yloc_ref[r:r + CHUNK, 2 * BRANCH:3 * BRANCH] = _silu(layer_norm(cv, cf_g, cf_nb)).astype(BF16)


def _local_mixer_param_specs():
    const = lambda i, *_: (0, 0)
    return [
        pl.BlockSpec((BRANCH // HEAD_DIM, CHUNK, CHUNK), lambda i, *_: (0, 0, 0)),
        pl.BlockSpec((CHUNK, BRANCH), const),
        pl.BlockSpec((3, BRANCH), const),
        pl.BlockSpec((31, BRANCH), const),
        pl.BlockSpec((5, BRANCH), const),
    ]


def _local_mixer_params(lp):
    return [lp["gm_ws"], lp["gm_bs"], lp["sc_conv_w"], lp["cf_conv_w"], lp["local_vec"]]


def _local_mixer_scratch(tm):
    span = tm + 2 * HALO
    return [pltpu.VMEM((span, BRANCH), F32), pltpu.VMEM((span, BRANCH), F32),
            pltpu.VMEM((SUBLANES, span - SUBLANES, BRANCH), F32)]


def _token_mixer_front(x2d, mod, g, lp, *, tm, seq, rows_per_mod, mod_base, local=True):
    n = x2d.shape[0]
    steps = n // tm
    per = tm // HALO
    row = lambda i: (i, 0)
    const = lambda i: (0, 0)
    in_specs = [
        pl.BlockSpec((tm, D_MODEL), row),
        pl.BlockSpec((HALO, D_MODEL), lambda i: (jnp.maximum(i * per - 1, 0), 0)),
        pl.BlockSpec((HALO, D_MODEL), lambda i: (jnp.minimum((i + 1) * per, n // HALO - 1), 0)),
        pl.BlockSpec((1, 6, D_MODEL), _mod_index(tm, rows_per_mod, mod_base)),
        pl.BlockSpec((1, D_MODEL), const),
        pl.BlockSpec((D_MODEL, W_MAIN if local else W_SSD), const),
        pl.BlockSpec((D_MODEL, LANES), const),
        pl.BlockSpec((5, SSD_CONV_CH), const),
        pl.BlockSpec((1, SSD_CONV_CH), const),
    ]
    args = [x2d, x2d, x2d, mod, g, lp["w_main"], lp["w_dt"], lp["ssd_conv_w"], lp["ssd_conv_b"]]
    out_specs = [pl.BlockSpec((tm, W_SSD), row)]
    out_shape = [jax.ShapeDtypeStruct((n, W_SSD), BF16)]
    scratch = [pltpu.VMEM((tm + 2 * HALO, SSD_CONV_CH), F32)]
    if local:
        in_specs += _local_mixer_param_specs()
        args += _local_mixer_params(lp)
        out_specs.append(pl.BlockSpec((tm, 3 * BRANCH), row))
        out_shape.append(jax.ShapeDtypeStruct((n, 3 * BRANCH), BF16))
        scratch += _local_mixer_scratch(tm)
    out_specs.append(pl.BlockSpec((tm, LANES), row))
    out_shape.append(jax.ShapeDtypeStruct((n, LANES), F32))
    outs = pl.pallas_call(
        functools.partial(_front_kernel, tm=tm, seq=seq, local=local),
        grid=(steps,),
        in_specs=in_specs,
        out_specs=out_specs,
        out_shape=out_shape,
        scratch_shapes=scratch,
        compiler_params=_params("parallel"),
        name="token_mixer_front",
    )(*args)
    return (outs[0], outs[1], outs[2]) if local else (outs[0], None, outs[1])


def _ssd_kernel(z_ref, dt_ref, init_ref, dtp_ref, vec_ref,
                y_ref, fin_ref, p_ref, dts_ref, pt_ref, dtt_ref, yacc_ref, st_ref, *, seq, n_seq):
    nc = seq // CHUNK
    lane = lax.broadcasted_iota(jnp.int32, (CHUNK, LANES), 1)
    row_i = lax.broadcasted_iota(jnp.int32, (CHUNK, CHUNK), 0)
    col_i = lax.broadcasted_iota(jnp.int32, (CHUNK, CHUNK), 1)
    lower = col_i <= row_i
    upper = col_i >= row_i
    head0 = lane < HEAD_DIM

    fwd_lane = (lane % (2 * SSD_HEADS)) < SSD_HEADS
    for q in range(n_seq):
        dt_all = _softplus(dt_ref[q] + dtp_ref[0:1, :])
        adt = dt_all * (-jnp.exp(dtp_ref[1:2, :]))
        p_all = (jnp.dot(lower.astype(F32), jnp.where(fwd_lane, adt, 0.0), preferred_element_type=F32,
                         precision=lax.Precision.HIGHEST)
                 + jnp.dot(upper.astype(F32), jnp.where(fwd_lane, 0.0, adt), preferred_element_type=F32,
                           precision=lax.Precision.HIGHEST))
        pt_ref[q] = p_all.T
        dtt_ref[q] = dt_all.T
        for c in range(nc):
            shift = (LANES - 2 * SSD_HEADS * c) % LANES
            p_ref[q, c * CHUNK:(c + 1) * CHUNK, :] = pltpu.roll(p_all, shift, axis=1) if shift else p_all
            dts_ref[q, c * CHUNK:(c + 1) * CHUNK, :] = pltpu.roll(dt_all, shift, axis=1) if shift else dt_all

    def group_inputs(q, rows, g):
        xg = z_ref[q, rows, g * LANES:(g + 1) * LANES].astype(F32)
        bg = z_ref[q, rows, BRANCH + g * SSD_STATE:BRANCH + (g + 1) * SSD_STATE]
        cg = z_ref[q, rows, BRANCH + (SSD_GROUPS + g) * SSD_STATE:BRANCH + (SSD_GROUPS + g + 1) * SSD_STATE]
        return xg, bg, cg

    def spread(m, j):
        return jnp.broadcast_to(m[:, j:j + 1], (CHUNK, LANES))

    def pair(a0, a1):
        return jnp.where(head0, a0, a1)

    st_ref[...] = init_ref[:, 0]

    def fwd_chunk(q, c):
        r0 = pl.multiple_of(c * CHUNK, CHUNK)
        rows = pl.ds(r0, CHUNK)
        pm = p_ref[q, rows, :]
        dt = dts_ref[q, rows, :]
        head_rows = pl.ds(pl.multiple_of(c * 2 * SSD_HEADS, 2 * SSD_HEADS), 2 * SSD_HEADS)
        pm_t = pt_ref[q, head_rows, :]
        dt_t = dtt_ref[q, head_rows, :]
        last = pm[CHUNK - 1:CHUNK, :]
        to_end = dt * jnp.exp(last - pm)
        for g in range(SSD_GROUPS):
            xg, bg, cg = group_inputs(q, rows, g)
            gram = lax.dot_general(cg, bg, (((1,), (1,)), ((), ())), preferred_element_type=F32)
            s_prev = st_ref[q, g]
            y = jnp.dot(cg, s_prev.astype(BF16), preferred_element_type=F32)
            hf0, hf1 = 2 * g, 2 * g + 1
            hb0, hb1 = SSD_HEADS + hf0, SSD_HEADS + hf1
            p_f0, p_f1 = spread(pm, hf0), spread(pm, hf1)
            y = y * jnp.exp(pair(p_f0, p_f1))
            for p_f, hf, hb, mask in ((p_f0, hf0, hb0, head0), (p_f1, hf1, hb1, jnp.logical_not(head0))):
                dec_f = jnp.exp(jnp.where(lower, p_f - pm_t[hf:hf + 1, :], -jnp.inf))
                dec_b = jnp.exp(jnp.where(upper, spread(pm, hb) - pm_t[hb:hb + 1, :], -jnp.inf))
                m = gram * (dec_f * dt_t[hf:hf + 1, :] + dec_b * dt_t[hb:hb + 1, :])
                xh = jnp.where(mask, xg, 0.0).astype(BF16)
                y = y + jnp.dot(m.astype(BF16), xh, preferred_element_type=F32)
            yacc_ref[q, rows, g * LANES:(g + 1) * LANES] = y
            w_end = pair(spread(to_end, hf0), spread(to_end, hf1))
            upd = lax.dot_general(bg, (xg * w_end).astype(BF16), (((0,), (0,)), ((), ())),
                                  preferred_element_type=F32)
            decay = jnp.exp(jnp.where(head0[0:1, :], last[:, hf0:hf0 + 1], last[:, hf1:hf1 + 1]))
            st_ref[q, g] = s_prev * decay + upd

    def fwd(c, carry):
        for q in range(n_seq):
            fwd_chunk(q, c)
        return carry

    lax.fori_loop(0, nc, fwd, 0, unroll=2)
    fin_ref[:, 0] = st_ref[...]

    st_ref[...] = init_ref[:, 1]
    d_skip = vec_ref[0:1, :]
    norm_g = vec_ref[1:2, :]

    def bwd_chunk(q, c):
        r0 = pl.multiple_of(c * CHUNK, CHUNK)
        rows = pl.ds(r0, CHUNK)
        pm = p_ref[q, rows, :]
        dt = dts_ref[q, rows, :]
        first = pm[0:1, :]
        to_end = dt * jnp.exp(first - pm)
        for g in range(SSD_GROUPS):
            xg, bg, cg = group_inputs(q, rows, g)
            hb0, hb1 = SSD_HEADS + 2 * g, SSD_HEADS + 2 * g + 1
            s_prev = st_ref[q, g]
            y = (jnp.dot(cg, s_prev.astype(BF16), preferred_element_type=F32)
                 * jnp.exp(pair(spread(pm, hb0), spread(pm, hb1))))
            yacc_ref[q, rows, g * LANES:(g + 1) * LANES] = (
                yacc_ref[q, rows, g * LANES:(g + 1) * LANES] + y + d_skip[:, g * LANES:(g + 1) * LANES] * xg)
            w_end = pair(spread(to_end, hb0), spread(to_end, hb1))
            upd = lax.dot_general(bg, (xg * w_end).astype(BF16), (((0,), (0,)), ((), ())),
                                  preferred_element_type=F32)
            decay = jnp.exp(jnp.where(head0[0:1, :], first[:, hb0:hb0 + 1], first[:, hb1:hb1 + 1]))
            st_ref[q, g] = s_prev * decay + upd
        gated = yacc_ref[q, rows, :] * _silu(z_ref[q, rows, SSD_CONV_CH:W_SSD].astype(F32))
        ms = jnp.mean(gated * gated, axis=-1, keepdims=True)
        y_ref[q, rows, :] = (gated * lax.rsqrt(ms + RMS_EPS) * norm_g).astype(BF16)

    def bwd(k, carry):
        for q in range(n_seq):
            bwd_chunk(q, nc - 1 - k)
        return carry

    lax.fori_loop(0, nc, bwd, 0)
    fin_ref[:, 1] = st_ref[...]


def _ssd_mixer(z_ssd, dt, init, lp):
    bsz, seq, _ = z_ssd.shape
    nc = seq // CHUNK
    n_hd = 2 * SSD_HEADS
    n_seq = max(1, min(SSD_MAX_SEQS_PER_STEP, SSD_ROWS_PER_STEP // seq))
    while bsz % n_seq:
        n_seq -= 1
    assert nc * n_hd <= LANES
    dt = dt[:, :, :n_hd].reshape(bsz, nc, CHUNK, n_hd).transpose(0, 2, 1, 3).reshape(bsz, CHUNK, nc * n_hd)
    dt = jnp.pad(dt, ((0, 0), (0, 0), (0, LANES - nc * n_hd)))
    seq_map = lambda b: (b, 0, 0)
    st_map = lambda b: (b, 0, 0, 0, 0)
    const = lambda b: (0, 0)
    st_shape = (bsz, 2, SSD_GROUPS, SSD_STATE, 2 * HEAD_DIM)
    return pl.pallas_call(
        functools.partial(_ssd_kernel, seq=seq, n_seq=n_seq),
        grid=(bsz // n_seq,),
        in_specs=[
            pl.BlockSpec((n_seq, seq, W_SSD), seq_map),
            pl.BlockSpec((n_seq, CHUNK, LANES), seq_map),
            pl.BlockSpec((n_seq,) + st_shape[1:], st_map),
            pl.BlockSpec((2, LANES), const),
            pl.BlockSpec((2, BRANCH), const),
        ],
        out_specs=[
            pl.BlockSpec((n_seq, seq, BRANCH), seq_map),
            pl.BlockSpec((n_seq,) + st_shape[1:], st_map),
        ],
        out_shape=[
            jax.ShapeDtypeStruct((bsz, seq, BRANCH), BF16),
            jax.ShapeDtypeStruct(st_shape, F32),
        ],
        scratch_shapes=[
            pltpu.VMEM((n_seq, seq, LANES), F32),
            pltpu.VMEM((n_seq, seq, LANES), F32),
            pltpu.VMEM((n_seq, LANES, CHUNK), F32),
            pltpu.VMEM((n_seq, LANES, CHUNK), F32),
            pltpu.VMEM((n_seq, seq, BRANCH), F32),
            pltpu.VMEM((n_seq, SSD_GROUPS, SSD_STATE, 2 * HEAD_DIM), F32),
        ],
        compiler_params=_params("parallel"),
        name="ssd_mixer",
    )(z_ssd, dt, init, lp["ssd_dtp"], lp["ssd_vec"])


def _token_mixed(x_ref, mod_ref, ya_ref, yb_ref, wa_ref, wb_ref):
    proj = (jnp.dot(ya_ref[...], wa_ref[...], preferred_element_type=F32)
            + jnp.dot(yb_ref[...], wb_ref[...], preferred_element_type=F32))
    return x_ref[...] + mod_ref[0, 2:3, :] * proj


def _token_mixed_specs(tm, rows_per_mod, mod_base):
    row = lambda i, *_: (i, 0)
    const = lambda i, *_: (0, 0)
    mod_idx = _mod_index(tm, rows_per_mod, mod_base)
    return [
        pl.BlockSpec((tm, D_MODEL), row),
        pl.BlockSpec((1, 6, D_MODEL), lambda i, *_: mod_idx(i)),
        pl.BlockSpec((tm, BRANCH), row),
        pl.BlockSpec((tm, 3 * BRANCH), row),
        pl.BlockSpec((BRANCH, D_MODEL), const),
        pl.BlockSpec((3 * BRANCH, D_MODEL), const),
    ]


def _residual_out(x, gate, delta, fg_ref):
    out = x + gate * delta
    if fg_ref is not None:
        ms = jnp.mean(out * out, axis=-1, keepdims=True)
        out = out * lax.rsqrt(ms + RMS_EPS) * fg_ref[...]
    return out


def _cast_slices(srcs, dsts):
    for src, dst in zip(srcs, dsts):
        dst[...] = src[...].astype(BF16)


def _cast_specs(casts, steps):
    row = lambda i, *_: (i, 0)
    blocks = [(arr.shape[0] // steps, arr.shape[1]) for arr in casts]
    return ([pl.BlockSpec(b, row) for b in blocks], [pl.BlockSpec(b, row) for b in blocks],
            [jax.ShapeDtypeStruct(arr.shape, BF16) for arr in casts])


def _swiglu_kernel(*refs, final_norm, n_casts):
    x_ref, mod_ref, ya_ref, yb_ref, wa_ref, wb_ref, g_ref, w1_ref, w3_ref, w2_ref = refs[0:10]
    pos = 10
    fg_ref = None
    if final_norm:
        fg_ref = refs[pos]
        pos += 1
    cast_in = refs[pos:pos + n_casts]
    o_ref = refs[pos + n_casts]
    cast_out = refs[pos + n_casts + 1:pos + 2 * n_casts + 1]
    x1 = _token_mixed(x_ref, mod_ref, ya_ref, yb_ref, wa_ref, wb_ref)
    h = _norm_modulate(x1, g_ref[...], mod_ref[0, 3:4, :], mod_ref[0, 4:5, :]).astype(BF16)
    a = jnp.dot(h, w1_ref[...], preferred_element_type=F32)
    b = jnp.dot(h, w3_ref[...], preferred_element_type=F32)
    y = jnp.dot((_silu(a) * b).astype(BF16), w2_ref[...], preferred_element_type=F32)
    o_ref[...] = _residual_out(x1, mod_ref[0, 5:6, :], y, fg_ref)
    _cast_slices(cast_in, cast_out)


def _dense_channel_mixer(x2d, mod, y_ssd, y_local, w_a, w_b, g, w1, w3, w2, *, tm, rows_per_mod, mod_base,
                         final_g=None, casts=()):
    n = x2d.shape[0]
    steps = n // tm
    ff = w1.shape[1]
    const = lambda i: (0, 0)
    row = lambda i: (i, 0)
    resident = dict(pipeline_mode=pl.Buffered(1))
    in_specs = _token_mixed_specs(tm, rows_per_mod, mod_base) + [
        pl.BlockSpec((1, D_MODEL), const),
        pl.BlockSpec((D_MODEL, ff), const, **resident),
        pl.BlockSpec((D_MODEL, ff), const, **resident),
        pl.BlockSpec((ff, D_MODEL), const, **resident),
    ]
    args = [x2d, mod, y_ssd, y_local, w_a, w_b, g, w1, w3, w2]
    if final_g is not None:
        in_specs.append(pl.BlockSpec((1, D_MODEL), const))
        args.append(final_g)
    cast_in, cast_out, cast_shape = _cast_specs(casts, steps)
    in_specs += cast_in
    args += list(casts)
    out_specs = [pl.BlockSpec((tm, D_MODEL), row)] + cast_out
    out_shape = [jax.ShapeDtypeStruct((n, D_MODEL), F32)] + cast_shape
    outs = pl.pallas_call(
        functools.partial(_swiglu_kernel, final_norm=final_g is not None, n_casts=len(casts)),
        grid=(steps,),
        in_specs=in_specs,
        out_specs=out_specs,
        out_shape=out_shape,
        compiler_params=_params("parallel"),
        name="dense_channel_mixer",
    )(*args)
    return outs[0], list(outs[1:])


def _can_cast_in_steps(arr2d, steps):
    return arr2d.shape[0] % steps == 0 and (arr2d.shape[0] // steps) % (2 * SUBLANES) == 0


def _router_kernel(*refs, n_casts):
    x_ref, mod_ref, ya_ref, yb_ref, wa_ref, wb_ref, g_ref, r_ref = refs[0:8]
    cast_in = refs[8:8 + n_casts]
    x1_ref, h_ref, comb_ref, rank_ref, rankt_ref, cnt_ref = refs[8 + n_casts:14 + n_casts]
    cast_out = refs[14 + n_casts:14 + 2 * n_casts]
    carry_ref = refs[-1]
    _cast_slices(cast_in, cast_out)

    @pl.when(pl.program_id(0) == 0)
    def _():
        carry_ref[...] = jnp.zeros_like(carry_ref)

    x1 = _token_mixed(x_ref, mod_ref, ya_ref, yb_ref, wa_ref, wb_ref)
    x1_ref[...] = x1
    h = _norm_modulate(x1, g_ref[...], mod_ref[0, 3:4, :], mod_ref[0, 4:5, :])
    h_hi = h.astype(BF16)
    h_ref[...] = h_hi
    h_lo = (h - h_hi.astype(F32)).astype(BF16)
    parts = jnp.dot(jnp.concatenate([h_hi, h_lo], axis=0), r_ref[...], preferred_element_type=F32)
    parts = parts[0:h.shape[0], :] + parts[h.shape[0]:, :]
    logits = parts + pltpu.roll(parts, LANES - N_EXPERTS, axis=1)
    lane = lax.broadcasted_iota(jnp.int32, logits.shape, 1).astype(F32)
    logits = jnp.where(lane < N_EXPERTS, logits, -jnp.inf)
    m1 = jnp.max(logits, axis=-1, keepdims=True)
    i1 = jnp.min(jnp.where(logits == m1, lane, float(LANES)), axis=-1, keepdims=True)
    rest = jnp.where(lane == i1, -jnp.inf, logits)
    m2 = jnp.max(rest, axis=-1, keepdims=True)
    i2 = jnp.min(jnp.where(rest == m2, lane, float(LANES)), axis=-1, keepdims=True)
    e2 = jnp.exp(m2 - m1)
    denom = 1.0 + e2
    comb_ref[...] = jnp.where(lane == i1, 1.0 / denom, 0.0) + jnp.where(lane == i2, e2 / denom, 0.0)
    is_chosen = (lane == i1) | (lane == i2)
    chosen = jnp.where(is_chosen, 1.0, 0.0)
    tm = chosen.shape[0]
    earlier = lax.broadcasted_iota(jnp.int32, (tm, tm), 1) < lax.broadcasted_iota(jnp.int32, (tm, tm), 0)
    running = carry_ref[...]
    before = jnp.dot(jnp.where(earlier, 1.0, 0.0).astype(BF16), chosen.astype(BF16),
                     preferred_element_type=F32) + running.astype(F32)
    rank = jnp.where(is_chosen, before, -1.0)
    rank_ref[...] = rank
    rankt_ref[...] = rank.T[0:N_EXPERTS, :]
    chosen_i = jnp.where(is_chosen, 1, 0)
    for k in range(tm // TOK_TILE):
        running = running + jnp.sum(chosen_i[k * TOK_TILE:(k + 1) * TOK_TILE, :], axis=0, keepdims=True)
        cnt_ref[0, k:k + 1, :] = running
    carry_ref[...] = running


def _router(x2d, mod, y_ssd, y_local, w_a, w_b, g, router_pad, *, tm, rows_per_mod, mod_base, casts=()):
    n = x2d.shape[0]
    row = lambda i: (i, 0)
    const = lambda i: (0, 0)
    per = tm // TOK_TILE
    cast_in, cast_out, cast_shape = _cast_specs(casts, n // tm)
    return pl.pallas_call(
        functools.partial(_router_kernel, n_casts=len(casts)),
        grid=(n // tm,),
        in_specs=_token_mixed_specs(tm, rows_per_mod, mod_base) + [
            pl.BlockSpec((1, D_MODEL), const),
            pl.BlockSpec((D_MODEL, LANES), const),
        ] + cast_in,
        out_specs=[
            pl.BlockSpec((tm, D_MODEL), row),
            pl.BlockSpec((tm, D_MODEL), row),
            pl.BlockSpec((tm, LANES), row),
            pl.BlockSpec((tm, LANES), row),
            pl.BlockSpec((N_EXPERTS, tm), lambda i: (0, i)),
            pl.BlockSpec((1, per, LANES), lambda i: (i, 0, 0)),
        ] + cast_out,
        out_shape=[
            jax.ShapeDtypeStruct((n, D_MODEL), F32),
            jax.ShapeDtypeStruct((n, D_MODEL), BF16),
            jax.ShapeDtypeStruct((n, LANES), F32),
            jax.ShapeDtypeStruct((n, LANES), F32),
            jax.ShapeDtypeStruct((N_EXPERTS, n), F32),
            jax.ShapeDtypeStruct((n // tm, per, LANES), jnp.int32),
        ] + cast_shape,
        scratch_shapes=[pltpu.VMEM((1, LANES), jnp.int32)],
        compiler_params=_params("arbitrary"),
        name="moe_router",
    )(x2d, mod, y_ssd, y_local, w_a, w_b, g, router_pad, *casts)


def _routing_tables(cnt, n_tokens):
    i32 = jnp.int32
    n_t = n_tokens // TOK_TILE
    cnt_end = cnt.reshape(n_t, LANES)[:, :N_EXPERTS]
    cnt_start = jnp.concatenate([jnp.zeros((1, N_EXPERTS), i32), cnt_end[:-1]], axis=0)
    tot = cnt_end[-1]
    nblk = (tot + ROW_BLOCK - 1) // ROW_BLOCK
    blk_cum = jnp.cumsum(nblk)
    blk_first = blk_cum - nblk
    n_valid = blk_cum[-1]
    nb = n_tokens * TOP_K // ROW_BLOCK + N_EXPERTS
    bi = jnp.arange(nb, dtype=i32)
    valid = bi < n_valid
    e_of = jnp.sum((bi[:, None] >= blk_cum[None, :]).astype(i32), axis=1)
    e_last = jnp.sum(((n_valid - 1) >= blk_cum).astype(i32))
    blk_e = jnp.minimum(jnp.where(valid, e_of, e_last), N_EXPERTS - 1)
    base = jnp.where(valid, (bi - blk_first[blk_e]) * ROW_BLOCK, 0)
    lo = base[:, None] + jnp.arange(N_SUB, dtype=i32)[None, :] * SUB_BLOCK
    hi = jnp.where(valid[:, None], jnp.minimum(lo + SUB_BLOCK, tot[blk_e][:, None]), lo)
    cs = cnt_start[:, blk_e]
    ce = cnt_end[:, blk_e]
    rel = (cs[:, :, None] < hi[None]) & (ce[:, :, None] > lo[None])
    tile_id = jnp.arange(n_t, dtype=i32)[:, None, None]
    t_first = jnp.min(jnp.where(rel, tile_id, n_t), axis=0)
    sub_hi = jnp.max(jnp.where(rel, tile_id, -1), axis=0)
    sub_lo = jnp.where(sub_hi >= 0, t_first, 0)
    u_lo = jnp.min(t_first, axis=1)
    u_n = jnp.maximum(jnp.max(sub_hi, axis=1) - u_lo + 1, 0)
    u_lo = jnp.where(u_n > 0, u_lo, 0)
    per = COMBINE_TM // TOK_TILE
    seg_rows = blk_first * ROW_BLOCK
    lo_c = seg_rows[None, :] + cnt_start[0::per]
    cnt_c = cnt_end[per - 1::per] - cnt_start[0::per]
    first_c = (lo_c // 16) * 16
    nwin_c = jnp.where(cnt_c > 0, (lo_c + cnt_c - first_c + COMBINE_WIN - 1) // COMBINE_WIN, 0)
    return dict(blk_e=blk_e.astype(i32), valid=valid.astype(i32), base=base.astype(i32),
                u_lo=u_lo.astype(i32), u_n=u_n.astype(i32),
                sub_lo=sub_lo.reshape(-1).astype(i32), sub_hi=sub_hi.reshape(-1).astype(i32),
                seg_rows=seg_rows.astype(i32), first_c=first_c.reshape(-1).astype(i32),
                nwin_c=nwin_c.reshape(-1).astype(i32), n_blocks=nb)


def _experts_kernel(be_ref, bv_ref, base_ref, ulo_ref, un_ref, slo_ref, shi_ref,
                    h_hbm, rt_hbm, w1_ref, w3_ref, w2_ref, y_ref, hbuf, rbuf, hsem, rsem):
    i = pl.program_id(0)
    slot = i % 2

    def tile_copies(s, k, t):
        tok = pl.multiple_of(t * TOK_TILE, TOK_TILE)
        return (pltpu.make_async_copy(h_hbm.at[pl.ds(tok, TOK_TILE)], hbuf.at[s, k], hsem.at[s, k]),
                pltpu.make_async_copy(rt_hbm.at[:, pl.ds(tok, TOK_TILE)], rbuf.at[s, k], rsem.at[s, k]))

    def for_block_tiles(b, s, action):
        for k in range(GATHER_TILES):
            @pl.when(k < un_ref[b])
            def _():
                for cp in tile_copies(s, k, ulo_ref[b] + k):
                    action(cp)

    @pl.when(i == 0)
    def _():
        def clear(k, carry):
            hbuf[k // (GATHER_TILES + 1), k % (GATHER_TILES + 1)] = jnp.zeros((TOK_TILE, D_MODEL), BF16)
            rbuf[k // (GATHER_TILES + 1), k % (GATHER_TILES + 1)] = jnp.zeros((N_EXPERTS, TOK_TILE), F32)
            return carry

        lax.fori_loop(0, 2 * (GATHER_TILES + 1), clear, 0)
        for_block_tiles(0, 0, lambda cp: cp.start())

    for_block_tiles(i, slot, lambda cp: cp.wait())

    @pl.when(i + 1 < pl.num_programs(0))
    def _():
        for_block_tiles(i + 1, 1 - slot, lambda cp: cp.start())

    @pl.when(bv_ref[i] == 1)
    def _():
        e = be_ref[i]
        row_id = lax.broadcasted_iota(jnp.int32, (SUB_BLOCK, TOK_TILE), 0).astype(F32)
        expert_row = lax.broadcasted_iota(jnp.int32, (N_EXPERTS, TOK_TILE), 0) == e
        u_lo = ulo_ref[i]
        selected = []

        def tile_weights(ranks8, first_rank):
            ranks = jnp.sum(jnp.where(expert_row, ranks8, 0.0), axis=0, keepdims=True)
            return jnp.where(ranks == first_rank + row_id, 1.0, 0.0).astype(BF16)

        for sb in range(N_SUB):
            first_rank = (base_ref[i] + sb * SUB_BLOCK).astype(F32)
            t_lo, t_hi = slo_ref[N_SUB * i + sb], shi_ref[N_SUB * i + sb]
            k0 = jnp.clip(t_lo - u_lo, 0, GATHER_TILES + 1 - WINDOW_TILES)
            picks = []
            for j in range(WINDOW_TILES):
                t = u_lo + k0 + j
                live = (t >= t_lo) & (t <= t_hi) & (k0 + j < GATHER_TILES)
                picks.append(tile_weights(jnp.where(live, rbuf[slot, k0 + j], -1.0), first_rank))
            window = hbuf[slot, pl.ds(k0, WINDOW_TILES)].reshape(WINDOW_TILES * TOK_TILE, D_MODEL)
            rows = jnp.dot(jnp.concatenate(picks, axis=1), window, preferred_element_type=F32)

            def late_tile(t, carry):
                k = t - u_lo
                kk = jnp.minimum(k, GATHER_TILES)

                @pl.when(k >= GATHER_TILES)
                def _():
                    tok = pl.multiple_of(t * TOK_TILE, TOK_TILE)
                    pltpu.sync_copy(h_hbm.at[pl.ds(tok, TOK_TILE)], hbuf.at[slot, GATHER_TILES])
                    pltpu.sync_copy(rt_hbm.at[:, pl.ds(tok, TOK_TILE)], rbuf.at[slot, GATHER_TILES])

                return carry + jnp.dot(tile_weights(rbuf[slot, kk], first_rank), hbuf[slot, kk],
                                       preferred_element_type=F32)

            t_next = u_lo + jnp.minimum(k0 + WINDOW_TILES, GATHER_TILES)
            rows = lax.fori_loop(jnp.maximum(t_next, t_lo), t_hi + 1, late_tile, rows)
            selected.append(rows.astype(BF16))

        hg = jnp.concatenate(selected, axis=0)
        half = w1_ref.shape[2] // 2
        y = None
        for c in range(2):
            a = jnp.dot(hg, w1_ref[0, :, c * half:(c + 1) * half], preferred_element_type=F32)
            b = jnp.dot(hg, w3_ref[0, :, c * half:(c + 1) * half], preferred_element_type=F32)
            part = jnp.dot((_silu(a) * b).astype(BF16), w2_ref[0, c * half:(c + 1) * half, :],
                           preferred_element_type=F32)
            y = part if y is None else y + part
        y_ref[...] = y.astype(BF16)

    @pl.when(bv_ref[i] == 0)
    def _():
        y_ref[...] = jnp.zeros_like(y_ref)


def _moe_experts(h2, rank_t, w1, w3, w2, tb):
    nb = tb["n_blocks"]
    ff = w1.shape[2]
    w_map = lambda i, be, *_: (be[i], 0, 0)
    grid_spec = pltpu.PrefetchScalarGridSpec(
        num_scalar_prefetch=7,
        grid=(nb,),
        in_specs=[
            pl.BlockSpec(memory_space=pl.ANY),
            pl.BlockSpec(memory_space=pl.ANY),
            pl.BlockSpec((1, D_MODEL, ff), w_map, pipeline_mode=pl.Buffered(1)),
            pl.BlockSpec((1, D_MODEL, ff), w_map, pipeline_mode=pl.Buffered(1)),
            pl.BlockSpec((1, ff, D_MODEL), w_map, pipeline_mode=pl.Buffered(1)),
        ],
        out_specs=pl.BlockSpec((ROW_BLOCK, D_MODEL), lambda i, *_: (i, 0)),
        scratch_shapes=[
            pltpu.VMEM((2, GATHER_TILES + 1, TOK_TILE, D_MODEL), BF16),
            pltpu.VMEM((2, GATHER_TILES + 1, N_EXPERTS, TOK_TILE), F32),
            pltpu.SemaphoreType.DMA((2, GATHER_TILES)),
            pltpu.SemaphoreType.DMA((2, GATHER_TILES)),
        ],
    )
    return pl.pallas_call(
        _experts_kernel,
        grid_spec=grid_spec,
        out_shape=jax.ShapeDtypeStruct((nb * ROW_BLOCK, D_MODEL), BF16),
        compiler_params=_params("arbitrary"),
        name="moe_experts",
    )(tb["blk_e"], tb["valid"], tb["base"], tb["u_lo"], tb["u_n"], tb["sub_lo"], tb["sub_hi"],
      h2, rank_t, w1, w3, w2)


def _combine_kernel(*refs, final_norm, n_rows):
    first_ref, nwin_ref, seg_ref, x_ref, mod_ref, comb_ref, rank_ref, y_hbm = refs[0:8]
    fg_ref = refs[8] if final_norm else None
    o_ref, ybuf, ysem, acc_ref = refs[-4:]
    i = pl.program_id(0)
    slot = i % 2
    extra = N_EXPERTS

    def window(b, e):
        return first_ref[b * N_EXPERTS + e], nwin_ref[b * N_EXPERTS + e]

    def win_start(first, j):
        return pl.multiple_of(jnp.minimum(first + j * COMBINE_WIN, n_rows - COMBINE_WIN), 16)

    def for_tile_windows(b, s, action):
        for e in range(N_EXPERTS):
            first, n_win = window(b, e)

            @pl.when(n_win > 0)
            def _():
                action(pltpu.make_async_copy(y_hbm.at[pl.ds(win_start(first, 0), COMBINE_WIN)],
                                             ybuf.at[s, e], ysem.at[s, e]))

    @pl.when(i == 0)
    def _():
        def clear(k, carry):
            ybuf[k // (N_EXPERTS + 1), k % (N_EXPERTS + 1)] = jnp.zeros((COMBINE_WIN, D_MODEL), BF16)
            return carry

        lax.fori_loop(0, 2 * (N_EXPERTS + 1), clear, 0)
        for_tile_windows(0, 0, lambda cp: cp.start())

    for_tile_windows(i, slot, lambda cp: cp.wait())

    @pl.when(i + 1 < pl.num_programs(0))
    def _():
        for_tile_windows(i + 1, 1 - slot, lambda cp: cp.start())

    col_id = lax.broadcasted_iota(jnp.int32, (COMBINE_TM, COMBINE_WIN), 1).astype(F32)

    def weights(e, first, j):
        pos_col = rank_ref[:, e:e + 1] + seg_ref[e].astype(F32)
        start = win_start(first, j).astype(F32)
        return jnp.where(pos_col == start + col_id, comb_ref[:, e:e + 1], 0.0).astype(BF16)

    firsts = [window(i, e) for e in range(N_EXPERTS)]
    pick_all = jnp.concatenate([weights(e, firsts[e][0], 0) for e in range(N_EXPERTS)], axis=1)
    y_all = ybuf[slot, 0:N_EXPERTS].reshape(N_EXPERTS * COMBINE_WIN, D_MODEL)
    acc_ref[...] = jnp.dot(pick_all, y_all, preferred_element_type=F32)

    for e in range(N_EXPERTS):
        first, n_win = firsts[e]

        def more(j, carry):
            pltpu.sync_copy(y_hbm.at[pl.ds(win_start(first, j), COMBINE_WIN)], ybuf.at[slot, extra])
            acc_ref[...] += jnp.dot(weights(e, first, j), ybuf[slot, extra], preferred_element_type=F32)
            return carry

        lax.fori_loop(1, jnp.maximum(n_win, 1), more, 0)

    o_ref[...] = _residual_out(x_ref[...], mod_ref[0, 5:6, :], acc_ref[...], fg_ref)


def _moe_combine(x2d, mod, comb, rank, y_sorted, tb, *, rows_per_mod, mod_base, final_g=None):
    n = x2d.shape[0]
    tm = COMBINE_TM
    row = lambda i, *_: (i, 0)
    const = lambda i, *_: (0, 0)
    mod_idx = _mod_index(tm, rows_per_mod, mod_base)
    in_specs = [
        pl.BlockSpec((tm, D_MODEL), row),
        pl.BlockSpec((1, 6, D_MODEL), lambda i, *_: mod_idx(i)),
        pl.BlockSpec((tm, LANES), row),
        pl.BlockSpec((tm, LANES), row),
        pl.BlockSpec(memory_space=pl.ANY),
    ]
    args = [x2d, mod, comb, rank, y_sorted]
    if final_g is not None:
        in_specs.append(pl.BlockSpec((1, D_MODEL), const))
        args.append(final_g)
    grid_spec = pltpu.PrefetchScalarGridSpec(
        num_scalar_prefetch=3,
        grid=(n // tm,),
        in_specs=in_specs,
        out_specs=pl.BlockSpec((tm, D_MODEL), row),
        scratch_shapes=[
            pltpu.VMEM((2, N_EXPERTS + 1, COMBINE_WIN, D_MODEL), BF16),
            pltpu.SemaphoreType.DMA((2, N_EXPERTS)),
            pltpu.VMEM((tm, D_MODEL), F32),
        ],
    )
    return pl.pallas_call(
        functools.partial(_combine_kernel, final_norm=final_g is not None, n_rows=y_sorted.shape[0]),
        grid_spec=grid_spec,
        out_shape=jax.ShapeDtypeStruct((n, D_MODEL), F32),
        compiler_params=_params("arbitrary"),
        name="moe_combine",
    )(tb["first_c"], tb["nwin_c"], tb["seg_rows"], *args)


def _to_bf16_in(w, steps):
    if w.dtype == BF16:
        return None, w
    flat = w.reshape(-1, w.shape[-1])
    if _can_cast_in_steps(flat, steps):
        return flat, None
    return None, w.astype(BF16)


def _moe_channel_mixer(x2d, mod, y_ssd, y_local, w_a, w_b, g, router_pad, w1, w3, w2, *, rows_per_mod, mod_base,
                       final_g=None):
    w1 = w1.astype(BF16)
    w3 = w3.astype(BF16)
    w2_flat, w2_bf16 = _to_bf16_in(w2, x2d.shape[0] // ROUTER_TM)
    x1, h2, comb, rank, rank_t, cnt, *cast = _router(
        x2d, mod, y_ssd, y_local, w_a, w_b, g, router_pad, tm=ROUTER_TM, rows_per_mod=rows_per_mod,
        mod_base=mod_base, casts=() if w2_flat is None else (w2_flat,))
    if w2_bf16 is None:
        w2_bf16 = cast[0].reshape(w2.shape)
    w2 = w2_bf16
    tb = _routing_tables(cnt, x2d.shape[0])
    y_sorted = _moe_experts(h2, rank_t, w1, w3, w2, tb)
    return _moe_combine(x1, mod, comb, rank, y_sorted, tb, rows_per_mod=rows_per_mod, mod_base=mod_base,
                        final_g=final_g)


def _layer_params(l, w_in, w_out, ssd_conv_w, ssd_conv_b, ssd_dt_bias, ssd_a_log, ssd_d, ssd_norm_g,
                  gm_norm_g, gm_norm_b, gm_ws, gm_bs, sc_conv_w, cf_conv_w, cf_conv_b, cf_norm_g, cf_norm_b):
    wi = w_in[l]
    w_main = jnp.concatenate(
        [wi[:, _C_XBC:_C_DT], wi[:, _C_Z:_C_UV], wi[:, _C_UV:_C_SC], wi[:, _C_CF:_C_END], wi[:, _C_SC:_C_CF]],
        axis=1).astype(BF16)
    w_dt = jnp.pad(wi[:, _C_DT:_C_Z], ((0, 0), (0, LANES - 2 * SSD_HEADS))).astype(BF16)
    return {
        "w_main": w_main,
        "w_dt": w_dt,
        "w_out_a": w_out[l, 0:BRANCH].astype(BF16),
        "w_out_b": w_out[l, BRANCH:].astype(BF16),
        "ssd_conv_w": ssd_conv_w[l],
        "ssd_conv_b": ssd_conv_b[l].reshape(1, -1),
        "ssd_dtp": jnp.stack([jnp.tile(ssd_dt_bias[l].reshape(-1), LANES // (2 * SSD_HEADS)),
                              jnp.tile(ssd_a_log[l].reshape(-1), LANES // (2 * SSD_HEADS))]),
        "ssd_vec": jnp.stack([jnp.repeat(ssd_d[l], HEAD_DIM), ssd_norm_g[l]]),
        "gm_ws": gm_ws[l],
        "gm_bs": jnp.repeat(gm_bs[l].T, HEAD_DIM, axis=1),
        "sc_conv_w": sc_conv_w[l],
        "cf_conv_w": cf_conv_w[l],
        "local_vec": jnp.stack([gm_norm_g[l], gm_norm_b[l], cf_conv_b[l], cf_norm_g[l], cf_norm_b[l]]),
    }


def kernel(x, c, ctx, c_ctx, mod_w, mod_b, norm1_g, norm2_g, w_in, w_out, ssd_conv_w, ssd_conv_b, ssd_dt_bias, ssd_a_log, ssd_d, ssd_norm_g, gm_norm_g, gm_norm_b, gm_ws, gm_bs, sc_conv_w, cf_conv_w, cf_conv_b, cf_norm_g, cf_norm_b, ffn_w1, ffn_w3, ffn_w2, moe_router, moe_w1, moe_w3, moe_w2, final_norm_g):
    bsz, seq, _ = x.shape
    ctx_len = ctx.shape[1]
    depth = w_in.shape[0]
    n_x, n_c = bsz * seq, bsz * ctx_len
    tm = 512

    mod = _modulation(jnp.concatenate([c, c_ctx[None, :]], axis=0), mod_w, mod_b)
    x_map = dict(tm=tm, rows_per_mod=seq, mod_base=0)
    c_map = dict(tm=tm, rows_per_mod=n_c, mod_base=bsz)
    zero_state = jnp.zeros((bsz, 2, SSD_GROUPS, SSD_STATE, 2 * HEAD_DIM), F32)

    xs = x.reshape(n_x, D_MODEL)
    xc = ctx.reshape(n_c, D_MODEL)
    moe_bf16 = {}
    for l in range(depth):
        last = l == depth - 1
        lp = _layer_params(l, w_in, w_out, ssd_conv_w, ssd_conv_b, ssd_dt_bias, ssd_a_log, ssd_d, ssd_norm_g,
                           gm_norm_g, gm_norm_b, gm_ws, gm_bs, sc_conv_w, cf_conv_w, cf_conv_b, cf_norm_g,
                           cf_norm_b)
        g1 = norm1_g[l].reshape(1, D_MODEL)
        g2 = norm2_g[l].reshape(1, D_MODEL)
        zc_ssd, yc_loc, dtc = _token_mixer_front(xc, mod[l], g1, lp, tm=min(tm, ctx_len), seq=ctx_len,
                                                 rows_per_mod=n_c, mod_base=bsz, local=not last)
        yc_ssd, states = _ssd_mixer(zc_ssd.reshape(bsz, ctx_len, W_SSD), dtc.reshape(bsz, ctx_len, LANES),
                                    zero_state, lp)
        z_ssd, y_loc, dtx = _token_mixer_front(xs, mod[l], g1, lp, tm=min(FRONT_TM, seq), seq=seq,
                                               rows_per_mod=seq, mod_base=0)
        y_ssd, _ = _ssd_mixer(z_ssd.reshape(bsz, seq, W_SSD), dtx.reshape(bsz, seq, LANES), states, lp)
        streams = [(xs, y_ssd.reshape(n_x, BRANCH), y_loc, x_map)]
        if not last:
            streams.append((xc, yc_ssd.reshape(n_c, BRANCH), yc_loc, c_map))
        fg = final_norm_g.reshape(1, D_MODEL) if last else None
        i = l // 2
        if l % 2 == 0:
            w1, w3, w2 = ffn_w1[i].astype(BF16), ffn_w3[i].astype(BF16), ffn_w2[i].astype(BF16)
            casts = []
            if l + 1 < depth:
                casts = [w.reshape(-1, w.shape[-1]) for w in (moe_w1[i], moe_w3[i])]
                if not all(_can_cast_in_steps(w, n_x // tm) for w in casts):
                    casts = []
            outs = []
            for t, ya, yb, m in streams:
                out, cast = _dense_channel_mixer(t, mod[l], ya, yb, lp["w_out_a"], lp["w_out_b"], g2, w1, w3, w2,
                                                 final_g=fg if t is xs else None, casts=casts if t is xs else (),
                                                 **m)
                outs.append(out)
                if cast:
                    moe_bf16[i] = [c.reshape(moe_w1[i].shape) for c in cast]
        else:
            w1, w3 = moe_bf16.get(i) or (moe_w1[i], moe_w3[i])
            w2 = moe_w2[i]
            r_hi = moe_router[i].astype(BF16)
            r_lo = (moe_router[i] - r_hi.astype(F32)).astype(BF16)
            r_pad = jnp.pad(jnp.concatenate([r_hi, r_lo], axis=1), ((0, 0), (0, LANES - 2 * N_EXPERTS)))
            outs = [_moe_channel_mixer(t, mod[l], ya, yb, lp["w_out_a"], lp["w_out_b"], g2, r_pad, w1, w3, w2,
                                       rows_per_mod=m["rows_per_mod"], mod_base=m["mod_base"],
                                       final_g=fg if t is xs else None) for t, ya, yb, m in streams]
        xs = outs[0]
        if not last:
            xc = outs[1]
    return xs.reshape(bsz, seq, D_MODEL)
```

```python
import functools

import jax
import jax.numpy as jnp
from jax import lax
from jax.experimental import pallas as pl
from jax.experimental.pallas import tpu as pltpu

F32 = jnp.float32
BF16 = jnp.bfloat16

LANES = 128
SUBLANES = 8
VMEM_LIMIT_BYTES = 56 * 1024 * 1024

D_MODEL = 1024
BRANCH = 256
HEAD_DIM = 64
SSD_HEADS = 4
SSD_GROUPS = 2
SSD_STATE = 128
CHUNK = 128
SSD_ROWS_PER_STEP = 4096
SSD_MAX_SEQS_PER_STEP = 4
SSD_CONV_CH = BRANCH + 2 * SSD_GROUPS * SSD_STATE
N_EXPERTS = 8
TOP_K = 2
TOK_TILE = 256
ROW_BLOCK = 512
SUB_BLOCK = 256
N_SUB = ROW_BLOCK // SUB_BLOCK
GATHER_TILES = 11
WINDOW_TILES = 5
ROUTER_TM = 512
COMBINE_TM = 512
COMBINE_WIN = 256
FRONT_TM = 1024
HALO = 16
RMS_EPS = 1e-6
LN_EPS = 1e-5

_C_XBC = 0
_C_DT = _C_XBC + SSD_CONV_CH
_C_Z = _C_DT + 2 * SSD_HEADS
_C_UV = _C_Z + BRANCH
_C_SC = _C_UV + 2 * BRANCH
_C_CF = _C_SC + 3 * BRANCH
_C_END = _C_CF + 2 * BRANCH
W_SSD = SSD_CONV_CH + BRANCH
W_UVCF = 4 * BRANCH
W_SC = 3 * BRANCH
W_MAIN = W_SSD + W_UVCF + W_SC


def _params(*sem):
    return pltpu.CompilerParams(dimension_semantics=sem, vmem_limit_bytes=VMEM_LIMIT_BYTES)


def _sigmoid(v):
    return 0.5 + 0.5 * jnp.tanh(0.5 * v)


def _silu(v):
    half = 0.5 * v
    return half + half * jnp.tanh(half)


def _softplus(v):
    return jnp.maximum(v, 0.0) + jnp.log1p(jnp.exp(-jnp.abs(v)))


def _gelu_tanh(v):
    return 0.5 * v * (1.0 + jnp.tanh(0.7978845608028654 * (v + 0.044715 * (v * v * v))))


def _norm_modulate(x, g, shift, scale):
    ms = jnp.mean(x * x, axis=-1, keepdims=True)
    return (x * lax.rsqrt(ms + RMS_EPS) * g) * (1.0 + scale) + shift


def _mod_kernel(c_ref, w_ref, b_ref, o_ref):
    s = _silu(c_ref[...])
    s_hi = s.astype(BF16)
    s_lo = (s - s_hi.astype(F32)).astype(BF16)
    w = w_ref[0]
    w_hi = w.astype(BF16)
    w_lo = (w - w_hi.astype(F32)).astype(BF16)
    o_ref[0, 0] = (jnp.dot(s_hi, w_hi, preferred_element_type=F32)
                   + jnp.dot(s_lo, w_hi, preferred_element_type=F32)
                   + jnp.dot(s_hi, w_lo, preferred_element_type=F32) + b_ref[0, 0])


def _modulation(c_all, mod_w, mod_b):
    depth = mod_w.shape[0]
    rows = c_all.shape[0]
    out = pl.pallas_call(
        _mod_kernel,
        grid=(depth, 6),
        in_specs=[
            pl.BlockSpec((rows, D_MODEL), lambda l, j: (0, 0)),
            pl.BlockSpec((1, D_MODEL, D_MODEL), lambda l, j: (l, 0, j)),
            pl.BlockSpec((1, 1, 1, D_MODEL), lambda l, j: (l, j, 0, 0)),
        ],
        out_specs=pl.BlockSpec((1, 1, rows, D_MODEL), lambda l, j: (l, j, 0, 0)),
        out_shape=jax.ShapeDtypeStruct((depth, 6, rows, D_MODEL), F32),
        compiler_params=_params("arbitrary", "arbitrary"),
        name="modulation",
    )(c_all, mod_w, mod_b.reshape(depth, 6, 1, D_MODEL))
    return jnp.transpose(out, (0, 2, 1, 3))


def _mod_index(tm, rows_per_mod, mod_base):
    return lambda i: (mod_base + (i * tm) // rows_per_mod, 0, 0)


def _fill_stage(stage_ref, prev, main, nxt):
    rows = main.shape[0]
    stage_ref[0:HALO, :] = prev
    stage_ref[HALO:HALO + rows, :] = main
    stage_ref[HALO + rows:2 * HALO + rows, :] = nxt


def _dwconv_rows(stage_ref, w_ref, taps, r, rows, c0, width):
    acc = None
    for k in range(taps):
        lo = HALO + r + k - taps // 2
        term = stage_ref[lo:lo + rows, c0:c0 + width] * w_ref[k:k + 1, c0:c0 + width]
        acc = term if acc is None else acc + term
    return acc


def _front_kernel(*refs, tm, seq, local, n_casts):
    x_ref, xp_ref, xn_ref, mod_ref, g_ref, w_ref, wdt_ref, convw_ref, convb_ref = refs[0:9]
    pos = 9
    if local:
        ws_ref, bs_ref, scw_ref, cfw_ref, vec_ref = refs[pos:pos + 5]
        pos += 5
    cast_in = refs[pos:pos + n_casts]
    pos += n_casts
    zssd_ref = refs[pos]
    if local:
        yloc_ref, dt_ref = refs[pos + 1:pos + 3]
        pos += 3
    else:
        dt_ref = refs[pos + 1]
        pos += 2
    cast_out = refs[pos:pos + n_casts]
    pos += n_casts
    if local:
        ssd_stage, sc_stage, cf_stage, rot_ref = refs[pos:]
    else:
        (ssd_stage,) = refs[pos:]
    _cast_slices(cast_in, cast_out)
    i = pl.program_id(0)
    tiles_per_seq = seq // tm
    keep_prev = jnp.where(i % tiles_per_seq == 0, 0.0, 1.0)
    keep_next = jnp.where(i % tiles_per_seq == tiles_per_seq - 1, 0.0, 1.0)
    g, shift, scale = g_ref[...], mod_ref[0, 0:1, :], mod_ref[0, 1:2, :]
    h = _norm_modulate(x_ref[...], g, shift, scale).astype(BF16)
    h_halo = _norm_modulate(jnp.concatenate([xp_ref[...], xn_ref[...]], axis=0), g, shift, scale).astype(BF16)
    z = jnp.dot(h, w_ref[...], preferred_element_type=F32)
    z_halo = jnp.dot(h_halo, w_ref[...], preferred_element_type=F32)
    zp = z_halo[0:HALO] * keep_prev
    zn = z_halo[HALO:2 * HALO] * keep_next
    dt_ref[...] = jnp.dot(h, wdt_ref[...], preferred_element_type=F32)

    _fill_stage(ssd_stage, zp[:, 0:SSD_CONV_CH], z[:, 0:SSD_CONV_CH], zn[:, 0:SSD_CONV_CH])
    for r in range(0, tm, CHUNK):
        for c0 in range(0, SSD_CONV_CH, BRANCH):
            v = _dwconv_rows(ssd_stage, convw_ref, 5, r, CHUNK, c0, BRANCH) + convb_ref[0:1, c0:c0 + BRANCH]
            zssd_ref[r:r + CHUNK, c0:c0 + BRANCH] = _silu(v).astype(BF16)
    zssd_ref[:, SSD_CONV_CH:W_SSD] = z[:, SSD_CONV_CH:W_SSD].astype(BF16)
    if local:
        _local_mixers_tile(z[:, W_SSD:W_MAIN], zp[:, W_SSD:W_MAIN], zn[:, W_SSD:W_MAIN], ws_ref, bs_ref, scw_ref,
                           cfw_ref, vec_ref, sc_stage, cf_stage, rot_ref, yloc_ref, tm)


def _local_mixers_tile(z, zp, zn, ws_ref, bs_ref, scw_ref, cfw_ref, vec_ref, sc_stage, cf_stage, rot_ref,
                       yloc_ref, tm):
    lane = lax.broadcasted_iota(jnp.int32, (CHUNK, BRANCH), 1)
    gm_g, gm_b = vec_ref[0:1, :], vec_ref[1:2, :]
    cf_b, cf_g, cf_nb = vec_ref[2:3, :], vec_ref[3:4, :], vec_ref[4:5, :]
    c_uv, c_cf, c_sc = 0, 2 * BRANCH, W_UVCF

    def layer_norm(v, gain, bias):
        mu = jnp.mean(v, axis=-1, keepdims=True)
        var = jnp.mean(jnp.square(v - mu), axis=-1, keepdims=True)
        return (v - mu) * lax.rsqrt(var + LN_EPS) * gain + bias

    def sc_in(t):
        return t[:, c_sc + BRANCH:c_sc + 2 * BRANCH] * t[:, c_sc + 2 * BRANCH:c_sc + 3 * BRANCH]

    def cf_in(t):
        return t[:, c_cf:c_cf + BRANCH] * _sigmoid(t[:, c_cf + BRANCH:c_cf + 2 * BRANCH])

    _fill_stage(sc_stage, sc_in(zp), sc_in(z), sc_in(zn))
    _fill_stage(cf_stage, cf_in(zp), cf_in(z), cf_in(zn))
    span = tm + 2 * HALO - SUBLANES
    for phase in range(SUBLANES):
        rot_ref[phase] = cf_stage[phase:phase + span, :]

    for r in range(0, tm, CHUNK):
        ge = _gelu_tanh(z[r:r + CHUNK, c_uv:c_uv + 2 * BRANCH])
        u = ge[:, 0:BRANCH]
        v = layer_norm(ge[:, BRANCH:2 * BRANCH], gm_g, gm_b)
        s = bs_ref[...]
        for hd in range(BRANCH // HEAD_DIM):
            in_head = (lane >= hd * HEAD_DIM) & (lane < (hd + 1) * HEAD_DIM)
            s = s + jnp.dot(ws_ref[hd].astype(BF16), jnp.where(in_head, v, 0.0).astype(BF16),
                            preferred_element_type=F32)
        yloc_ref[r:r + CHUNK, 0:BRANCH] = (u * s).astype(BF16)
        gate = z[r:r + CHUNK, c_sc:c_sc + BRANCH]
        yloc_ref[r:r + CHUNK, BRANCH:2 * BRANCH] = (gate * _dwconv_rows(sc_stage, scw_ref, 3, r, CHUNK, 0, BRANCH)
                                                    ).astype(BF16)
        cv = cf_b
        for k in range(31):
            lo = HALO + r + k - 31 // 2
            q8 = (lo // SUBLANES) * SUBLANES
            cv = cv + rot_ref[lo % SUBLANES, q8:q8 + CHUNK, :] * cfw_ref[k:k + 1, :]
        yloc_ref[r:r + CHUNK, 2 * BRANCH:3 * BRANCH] = _silu(layer_norm(cv, cf_g, cf_nb)).astype(BF16)


def _local_mixer_param_specs():
    const = lambda i, *_: (0, 0)
    return [
        pl.BlockSpec((BRANCH // HEAD_DIM, CHUNK, CHUNK), lambda i, *_: (0, 0, 0)),
        pl.BlockSpec((CHUNK, BRANCH), const),
        pl.BlockSpec((3, BRANCH), const),
        pl.BlockSpec((31, BRANCH), const),
        pl.BlockSpec((5, BRANCH), const),
    ]


def _local_mixer_params(lp):
    return [lp["gm_ws"], lp["gm_bs"], lp["sc_conv_w"], lp["cf_conv_w"], lp["local_vec"]]


def _local_mixer_scratch(tm):
    span = tm + 2 * HALO
    return [pltpu.VMEM((span, BRANCH), F32), pltpu.VMEM((span, BRANCH), F32),
            pltpu.VMEM((SUBLANES, span - SUBLANES, BRANCH), F32)]


def _token_mixer_front(x2d, mod, g, lp, *, tm, seq, rows_per_mod, mod_base, local=True, casts=()):
    n = x2d.shape[0]
    steps = n // tm
    per = tm // HALO
    row = lambda i: (i, 0)
    const = lambda i: (0, 0)
    in_specs = [
        pl.BlockSpec((tm, D_MODEL), row),
        pl.BlockSpec((HALO, D_MODEL), lambda i: (jnp.maximum(i * per - 1, 0), 0)),
        pl.BlockSpec((HALO, D_MODEL), lambda i: (jnp.minimum((i + 1) * per, n // HALO - 1), 0)),
        pl.BlockSpec((1, 6, D_MODEL), _mod_index(tm, rows_per_mod, mod_base)),
        pl.BlockSpec((1, D_MODEL), const),
        pl.BlockSpec((D_MODEL, W_MAIN if local else W_SSD), const),
        pl.BlockSpec((D_MODEL, LANES), const),
        pl.BlockSpec((5, SSD_CONV_CH), const),
        pl.BlockSpec((1, SSD_CONV_CH), const),
    ]
    args = [x2d, x2d, x2d, mod, g, lp["w_main"], lp["w_dt"], lp["ssd_conv_w"], lp["ssd_conv_b"]]
    out_specs = [pl.BlockSpec((tm, W_SSD), row)]
    out_shape = [jax.ShapeDtypeStruct((n, W_SSD), BF16)]
    scratch = [pltpu.VMEM((tm + 2 * HALO, SSD_CONV_CH), F32)]
    if local:
        in_specs += _local_mixer_param_specs()
        args += _local_mixer_params(lp)
        out_specs.append(pl.BlockSpec((tm, 3 * BRANCH), row))
        out_shape.append(jax.ShapeDtypeStruct((n, 3 * BRANCH), BF16))
        scratch += _local_mixer_scratch(tm)
    out_specs.append(pl.BlockSpec((tm, LANES), row))
    out_shape.append(jax.ShapeDtypeStruct((n, LANES), F32))
    cast_in, cast_out, cast_shape = _cast_specs(casts, steps)
    outs = pl.pallas_call(
        functools.partial(_front_kernel, tm=tm, seq=seq, local=local, n_casts=len(casts)),
        grid=(steps,),
        in_specs=in_specs + cast_in,
        out_specs=out_specs + cast_out,
        out_shape=out_shape + cast_shape,
        scratch_shapes=scratch,
        compiler_params=_params("parallel"),
        name="token_mixer_front",
    )(*args, *casts)
    n_main = 3 if local else 2
    main = (outs[0], outs[1], outs[2]) if local else (outs[0], None, outs[1])
    return main + (list(outs[n_main:]),)


def _ssd_kernel(z_ref, dt_ref, init_ref, dtp_ref, vec_ref,
                y_ref, fin_ref, p_ref, dts_ref, pt_ref, dtt_ref, yacc_ref, st_ref, *, seq, n_seq):
    nc = seq // CHUNK
    lane = lax.broadcasted_iota(jnp.int32, (CHUNK, LANES), 1)
    row_i = lax.broadcasted_iota(jnp.int32, (CHUNK, CHUNK), 0)
    col_i = lax.broadcasted_iota(jnp.int32, (CHUNK, CHUNK), 1)
    lower = col_i <= row_i
    upper = col_i >= row_i
    head0 = lane < HEAD_DIM

    fwd_lane = (lane % (2 * SSD_HEADS)) < SSD_HEADS
    for q in range(n_seq):
        dt_all = _softplus(dt_ref[q] + dtp_ref[0:1, :])
        adt = dt_all * (-jnp.exp(dtp_ref[1:2, :]))
        p_all = (jnp.dot(lower.astype(F32), jnp.where(fwd_lane, adt, 0.0), preferred_element_type=F32,
                         precision=lax.Precision.HIGHEST)
                 + jnp.dot(upper.astype(F32), jnp.where(fwd_lane, 0.0, adt), preferred_element_type=F32,
                           precision=lax.Precision.HIGHEST))
        pt_ref[q] = p_all.T
        dtt_ref[q] = dt_all.T
        for c in range(nc):
            shift = (LANES - 2 * SSD_HEADS * c) % LANES
            p_ref[q, c * CHUNK:(c + 1) * CHUNK, :] = pltpu.roll(p_all, shift, axis=1) if shift else p_all
            dts_ref[q, c * CHUNK:(c + 1) * CHUNK, :] = pltpu.roll(dt_all, shift, axis=1) if shift else dt_all

    def group_inputs(q, rows, g):
        xg = z_ref[q, rows, g * LANES:(g + 1) * LANES].astype(F32)
        bg = z_ref[q, rows, BRANCH + g * SSD_STATE:BRANCH + (g + 1) * SSD_STATE]
        cg = z_ref[q, rows, BRANCH + (SSD_GROUPS + g) * SSD_STATE:BRANCH + (SSD_GROUPS + g + 1) * SSD_STATE]
        return xg, bg, cg

    def spread(m, j):
        return jnp.broadcast_to(m[:, j:j + 1], (CHUNK, LANES))

    def pair(a0, a1):
        return jnp.where(head0, a0, a1)

    st_ref[...] = init_ref[:, 0]

    def fwd_chunk(q, c):
        r0 = pl.multiple_of(c * CHUNK, CHUNK)
        rows = pl.ds(r0, CHUNK)
        pm = p_ref[q, rows, :]
        dt = dts_ref[q, rows, :]
        head_rows = pl.ds(pl.multiple_of(c * 2 * SSD_HEADS, 2 * SSD_HEADS), 2 * SSD_HEADS)
        pm_t = pt_ref[q, head_rows, :]
        dt_t = dtt_ref[q, head_rows, :]
        last = pm[CHUNK - 1:CHUNK, :]
        to_end = dt * jnp.exp(last - pm)
        for g in range(SSD_GROUPS):
            xg, bg, cg = group_inputs(q, rows, g)
            gram = lax.dot_general(cg, bg, (((1,), (1,)), ((), ())), preferred_element_type=F32)
            s_prev = st_ref[q, g]
            y = jnp.dot(cg, s_prev.astype(BF16), preferred_element_type=F32)
            hf0, hf1 = 2 * g, 2 * g + 1
            hb0, hb1 = SSD_HEADS + hf0, SSD_HEADS + hf1
            p_f0, p_f1 = spread(pm, hf0), spread(pm, hf1)
            y = y * jnp.exp(pair(p_f0, p_f1))
            for p_f, hf, hb, mask in ((p_f0, hf0, hb0, head0), (p_f1, hf1, hb1, jnp.logical_not(head0))):
                dec_f = jnp.exp(jnp.where(lower, p_f - pm_t[hf:hf + 1, :], -jnp.inf))
                dec_b = jnp.exp(jnp.where(upper, spread(pm, hb) - pm_t[hb:hb + 1, :], -jnp.inf))
                m = gram * (dec_f * dt_t[hf:hf + 1, :] + dec_b * dt_t[hb:hb + 1, :])
                xh = jnp.where(mask, xg, 0.0).astype(BF16)
                y = y + jnp.dot(m.astype(BF16), xh, preferred_element_type=F32)
            yacc_ref[q, rows, g * LANES:(g + 1) * LANES] = y
            w_end = pair(spread(to_end, hf0), spread(to_end, hf1))
            upd = lax.dot_general(bg, (xg * w_end).astype(BF16), (((0,), (0,)), ((), ())),
                                  preferred_element_type=F32)
            decay = jnp.exp(jnp.where(head0[0:1, :], last[:, hf0:hf0 + 1], last[:, hf1:hf1 + 1]))
            st_ref[q, g] = s_prev * decay + upd

    def fwd(c, carry):
        for q in range(n_seq):
            fwd_chunk(q, c)
        return carry

    lax.fori_loop(0, nc, fwd, 0, unroll=2)
    fin_ref[:, 0] = st_ref[...]

    st_ref[...] = init_ref[:, 1]
    d_skip = vec_ref[0:1, :]
    norm_g = vec_ref[1:2, :]

    def bwd_chunk(q, c):
        r0 = pl.multiple_of(c * CHUNK, CHUNK)
        rows = pl.ds(r0, CHUNK)
        pm = p_ref[q, rows, :]
        dt = dts_ref[q, rows, :]
        first = pm[0:1, :]
        to_end = dt * jnp.exp(first - pm)
        for g in range(SSD_GROUPS):
            xg, bg, cg = group_inputs(q, rows, g)
            hb0, hb1 = SSD_HEADS + 2 * g, SSD_HEADS + 2 * g + 1
            s_prev = st_ref[q, g]
            y = (jnp.dot(cg, s_prev.astype(BF16), preferred_element_type=F32)
                 * jnp.exp(pair(spread(pm, hb0), spread(pm, hb1))))
            yacc_ref[q, rows, g * LANES:(g + 1) * LANES] = (
                yacc_ref[q, rows, g * LANES:(g + 1) * LANES] + y + d_skip[:, g * LANES:(g + 1) * LANES] * xg)
            w_end = pair(spread(to_end, hb0), spread(to_end, hb1))
            upd = lax.dot_general(bg, (xg * w_end).astype(BF16), (((0,), (0,)), ((), ())),
                                  preferred_element_type=F32)
            decay = jnp.exp(jnp.where(head0[0:1, :], first[:, hb0:hb0 + 1], first[:, hb1:hb1 + 1]))
            st_ref[q, g] = s_prev * decay + upd
        gated = yacc_ref[q, rows, :] * _silu(z_ref[q, rows, SSD_CONV_CH:W_SSD].astype(F32))
        ms = jnp.mean(gated * gated, axis=-1, keepdims=True)
        y_ref[q, rows, :] = (gated * lax.rsqrt(ms + RMS_EPS) * norm_g).astype(BF16)

    def bwd(k, carry):
        for q in range(n_seq):
            bwd_chunk(q, nc - 1 - k)
        return carry

    lax.fori_loop(0, nc, bwd, 0)
    fin_ref[:, 1] = st_ref[...]


def _ssd_mixer(z_ssd, dt, init, lp):
    bsz, seq, _ = z_ssd.shape
    nc = seq // CHUNK
    n_hd = 2 * SSD_HEADS
    n_seq = max(1, min(SSD_MAX_SEQS_PER_STEP, SSD_ROWS_PER_STEP // seq))
    while bsz % n_seq:
        n_seq -= 1
    assert nc * n_hd <= LANES
    dt = dt[:, :, :n_hd].reshape(bsz, nc, CHUNK, n_hd).transpose(0, 2, 1, 3).reshape(bsz, CHUNK, nc * n_hd)
    dt = jnp.pad(dt, ((0, 0), (0, 0), (0, LANES - nc * n_hd)))
    seq_map = lambda b: (b, 0, 0)
    st_map = lambda b: (b, 0, 0, 0, 0)
    const = lambda b: (0, 0)
    st_shape = (bsz, 2, SSD_GROUPS, SSD_STATE, 2 * HEAD_DIM)
    return pl.pallas_call(
        functools.partial(_ssd_kernel, seq=seq, n_seq=n_seq),
        grid=(bsz // n_seq,),
        in_specs=[
            pl.BlockSpec((n_seq, seq, W_SSD), seq_map),
            pl.BlockSpec((n_seq, CHUNK, LANES), seq_map),
            pl.BlockSpec((n_seq,) + st_shape[1:], st_map),
            pl.BlockSpec((2, LANES), const),
            pl.BlockSpec((2, BRANCH), const),
        ],
        out_specs=[
            pl.BlockSpec((n_seq, seq, BRANCH), seq_map),
            pl.BlockSpec((n_seq,) + st_shape[1:], st_map),
        ],
        out_shape=[
            jax.ShapeDtypeStruct((bsz, seq, BRANCH), BF16),
            jax.ShapeDtypeStruct(st_shape, F32),
        ],
        scratch_shapes=[
            pltpu.VMEM((n_seq, seq, LANES), F32),
            pltpu.VMEM((n_seq, seq, LANES), F32),
            pltpu.VMEM((n_seq, LANES, CHUNK), F32),
            pltpu.VMEM((n_seq, LANES, CHUNK), F32),
            pltpu.VMEM((n_seq, seq, BRANCH), F32),
            pltpu.VMEM((n_seq, SSD_GROUPS, SSD_STATE, 2 * HEAD_DIM), F32),
        ],
        compiler_params=_params("parallel"),
        name="ssd_mixer",
    )(z_ssd, dt, init, lp["ssd_dtp"], lp["ssd_vec"])


def _token_mixed(x_ref, mod_ref, ya_ref, yb_ref, wa_ref, wb_ref):
    proj = (jnp.dot(ya_ref[...], wa_ref[...], preferred_element_type=F32)
            + jnp.dot(yb_ref[...], wb_ref[...], preferred_element_type=F32))
    return x_ref[...] + mod_ref[0, 2:3, :] * proj


def _token_mixed_specs(tm, rows_per_mod, mod_base):
    row = lambda i, *_: (i, 0)
    const = lambda i, *_: (0, 0)
    mod_idx = _mod_index(tm, rows_per_mod, mod_base)
    return [
        pl.BlockSpec((tm, D_MODEL), row),
        pl.BlockSpec((1, 6, D_MODEL), lambda i, *_: mod_idx(i)),
        pl.BlockSpec((tm, BRANCH), row),
        pl.BlockSpec((tm, 3 * BRANCH), row),
        pl.BlockSpec((BRANCH, D_MODEL), const),
        pl.BlockSpec((3 * BRANCH, D_MODEL), const),
    ]


def _residual_out(x, gate, delta, fg_ref):
    out = x + gate * delta
    if fg_ref is not None:
        ms = jnp.mean(out * out, axis=-1, keepdims=True)
        out = out * lax.rsqrt(ms + RMS_EPS) * fg_ref[...]
    return out


def _cast_slices(srcs, dsts):
    for src, dst in zip(srcs, dsts):
        dst[...] = src[...].astype(BF16)


def _cast_specs(casts, steps):
    row = lambda i, *_: (i, 0)
    blocks = [(arr.shape[0] // steps, arr.shape[1]) for arr in casts]
    return ([pl.BlockSpec(b, row) for b in blocks], [pl.BlockSpec(b, row) for b in blocks],
            [jax.ShapeDtypeStruct(arr.shape, BF16) for arr in casts])


def _swiglu_kernel(*refs, final_norm, n_casts):
    x_ref, mod_ref, ya_ref, yb_ref, wa_ref, wb_ref, g_ref, w1_ref, w3_ref, w2_ref = refs[0:10]
    pos = 10
    fg_ref = None
    if final_norm:
        fg_ref = refs[pos]
        pos += 1
    cast_in = refs[pos:pos + n_casts]
    o_ref = refs[pos + n_casts]
    cast_out = refs[pos + n_casts + 1:pos + 2 * n_casts + 1]
    x1 = _token_mixed(x_ref, mod_ref, ya_ref, yb_ref, wa_ref, wb_ref)
    h = _norm_modulate(x1, g_ref[...], mod_ref[0, 3:4, :], mod_ref[0, 4:5, :]).astype(BF16)
    a = jnp.dot(h, w1_ref[...], preferred_element_type=F32)
    b = jnp.dot(h, w3_ref[...], preferred_element_type=F32)
    y = jnp.dot((_silu(a) * b).astype(BF16), w2_ref[...], preferred_element_type=F32)
    o_ref[...] = _residual_out(x1, mod_ref[0, 5:6, :], y, fg_ref)
    _cast_slices(cast_in, cast_out)


def _dense_channel_mixer(x2d, mod, y_ssd, y_local, w_a, w_b, g, w1, w3, w2, *, tm, rows_per_mod, mod_base,
                         final_g=None, casts=()):
    n = x2d.shape[0]
    steps = n // tm
    ff = w1.shape[1]
    const = lambda i: (0, 0)
    row = lambda i: (i, 0)
    resident = dict(pipeline_mode=pl.Buffered(1))
    in_specs = _token_mixed_specs(tm, rows_per_mod, mod_base) + [
        pl.BlockSpec((1, D_MODEL), const),
        pl.BlockSpec((D_MODEL, ff), const, **resident),
        pl.BlockSpec((D_MODEL, ff), const, **resident),
        pl.BlockSpec((ff, D_MODEL), const, **resident),
    ]
    args = [x2d, mod, y_ssd, y_local, w_a, w_b, g, w1, w3, w2]
    if final_g is not None:
        in_specs.append(pl.BlockSpec((1, D_MODEL), const))
        args.append(final_g)
    cast_in, cast_out, cast_shape = _cast_specs(casts, steps)
    in_specs += cast_in
    args += list(casts)
    out_specs = [pl.BlockSpec((tm, D_MODEL), row)] + cast_out
    out_shape = [jax.ShapeDtypeStruct((n, D_MODEL), F32)] + cast_shape
    outs = pl.pallas_call(
        functools.partial(_swiglu_kernel, final_norm=final_g is not None, n_casts=len(casts)),
        grid=(steps,),
        in_specs=in_specs,
        out_specs=out_specs,
        out_shape=out_shape,
        compiler_params=_params("parallel"),
        name="dense_channel_mixer",
    )(*args)
    return outs[0], list(outs[1:])


def _can_cast_in_steps(arr2d, steps):
    return arr2d.shape[0] % steps == 0 and (arr2d.shape[0] // steps) % (2 * SUBLANES) == 0


def _router_kernel(*refs, n_casts):
    x_ref, mod_ref, ya_ref, yb_ref, wa_ref, wb_ref, g_ref, r_ref = refs[0:8]
    cast_in = refs[8:8 + n_casts]
    x1_ref, h_ref, comb_ref, rank_ref, rankt_ref, cnt_ref = refs[8 + n_casts:14 + n_casts]
    cast_out = refs[14 + n_casts:14 + 2 * n_casts]
    carry_ref = refs[-1]
    _cast_slices(cast_in, cast_out)

    @pl.when(pl.program_id(0) == 0)
    def _():
        carry_ref[...] = jnp.zeros_like(carry_ref)

    x1 = _token_mixed(x_ref, mod_ref, ya_ref, yb_ref, wa_ref, wb_ref)
    x1_ref[...] = x1
    h = _norm_modulate(x1, g_ref[...], mod_ref[0, 3:4, :], mod_ref[0, 4:5, :])
    h_hi = h.astype(BF16)
    h_ref[...] = h_hi
    h_lo = (h - h_hi.astype(F32)).astype(BF16)
    parts = jnp.dot(jnp.concatenate([h_hi, h_lo], axis=0), r_ref[...], preferred_element_type=F32)
    parts = parts[0:h.shape[0], :] + parts[h.shape[0]:, :]
    logits = parts + pltpu.roll(parts, LANES - N_EXPERTS, axis=1)
    lane = lax.broadcasted_iota(jnp.int32, logits.shape, 1).astype(F32)
    logits = jnp.where(lane < N_EXPERTS, logits, -jnp.inf)
    m1 = jnp.max(logits, axis=-1, keepdims=True)
    i1 = jnp.min(jnp.where(logits == m1, lane, float(LANES)), axis=-1, keepdims=True)
    rest = jnp.where(lane == i1, -jnp.inf, logits)
    m2 = jnp.max(rest, axis=-1, keepdims=True)
    i2 = jnp.min(jnp.where(rest == m2, lane, float(LANES)), axis=-1, keepdims=True)
    e2 = jnp.exp(m2 - m1)
    denom = 1.0 + e2
    comb_ref[...] = jnp.where(lane == i1, 1.0 / denom, 0.0) + jnp.where(lane == i2, e2 / denom, 0.0)
    is_chosen = (lane == i1) | (lane == i2)
    chosen = jnp.where(is_chosen, 1.0, 0.0)
    tm = chosen.shape[0]
    earlier = lax.broadcasted_iota(jnp.int32, (tm, tm), 1) < lax.broadcasted_iota(jnp.int32, (tm, tm), 0)
    running = carry_ref[...]
    before = jnp.dot(jnp.where(earlier, 1.0, 0.0).astype(BF16), chosen.astype(BF16),
                     preferred_element_type=F32) + running.astype(F32)
    rank = jnp.where(is_chosen, before, -1.0)
    rank_ref[...] = rank
    rankt_ref[...] = rank.T[0:N_EXPERTS, :]
    chosen_i = jnp.where(is_chosen, 1, 0)
    for k in range(tm // TOK_TILE):
        running = running + jnp.sum(chosen_i[k * TOK_TILE:(k + 1) * TOK_TILE, :], axis=0, keepdims=True)
        cnt_ref[0, k:k + 1, :] = running
    carry_ref[...] = running


def _router(x2d, mod, y_ssd, y_local, w_a, w_b, g, router_pad, *, tm, rows_per_mod, mod_base, casts=()):
    n = x2d.shape[0]
    row = lambda i: (i, 0)
    const = lambda i: (0, 0)
    per = tm // TOK_TILE
    cast_in, cast_out, cast_shape = _cast_specs(casts, n // tm)
    return pl.pallas_call(
        functools.partial(_router_kernel, n_casts=len(casts)),
        grid=(n // tm,),
        in_specs=_token_mixed_specs(tm, rows_per_mod, mod_base) + [
            pl.BlockSpec((1, D_MODEL), const),
            pl.BlockSpec((D_MODEL, LANES), const),
        ] + cast_in,
        out_specs=[
            pl.BlockSpec((tm, D_MODEL), row),
            pl.BlockSpec((tm, D_MODEL), row),
            pl.BlockSpec((tm, LANES), row),
            pl.BlockSpec((tm, LANES), row),
            pl.BlockSpec((N_EXPERTS, tm), lambda i: (0, i)),
            pl.BlockSpec((1, per, LANES), lambda i: (i, 0, 0)),
        ] + cast_out,
        out_shape=[
            jax.ShapeDtypeStruct((n, D_MODEL), F32),
            jax.ShapeDtypeStruct((n, D_MODEL), BF16),
            jax.ShapeDtypeStruct((n, LANES), F32),
            jax.ShapeDtypeStruct((n, LANES), F32),
            jax.ShapeDtypeStruct((N_EXPERTS, n), F32),
            jax.ShapeDtypeStruct((n // tm, per, LANES), jnp.int32),
        ] + cast_shape,
        scratch_shapes=[pltpu.VMEM((1, LANES), jnp.int32)],
        compiler_params=_params("arbitrary"),
        name="moe_router",
    )(x2d, mod, y_ssd, y_local, w_a, w_b, g, router_pad, *casts)


def _routing_tables(cnt, n_tokens):
    i32 = jnp.int32
    n_t = n_tokens // TOK_TILE
    cnt_end = cnt.reshape(n_t, LANES)[:, :N_EXPERTS]
    cnt_start = jnp.concatenate([jnp.zeros((1, N_EXPERTS), i32), cnt_end[:-1]], axis=0)
    tot = cnt_end[-1]
    nblk = (tot + ROW_BLOCK - 1) // ROW_BLOCK
    blk_cum = jnp.cumsum(nblk)
    blk_first = blk_cum - nblk
    n_valid = blk_cum[-1]
    nb = n_tokens * TOP_K // ROW_BLOCK + N_EXPERTS
    bi = jnp.arange(nb, dtype=i32)
    valid = bi < n_valid
    e_of = jnp.sum((bi[:, None] >= blk_cum[None, :]).astype(i32), axis=1)
    e_last = jnp.sum(((n_valid - 1) >= blk_cum).astype(i32))
    blk_e = jnp.minimum(jnp.where(valid, e_of, e_last), N_EXPERTS - 1)
    base = jnp.where(valid, (bi - blk_first[blk_e]) * ROW_BLOCK, 0)
    lo = base[:, None] + jnp.arange(N_SUB, dtype=i32)[None, :] * SUB_BLOCK
    hi = jnp.where(valid[:, None], jnp.minimum(lo + SUB_BLOCK, tot[blk_e][:, None]), lo)
    cs = cnt_start[:, blk_e]
    ce = cnt_end[:, blk_e]
    rel = (cs[:, :, None] < hi[None]) & (ce[:, :, None] > lo[None])
    tile_id = jnp.arange(n_t, dtype=i32)[:, None, None]
    t_first = jnp.min(jnp.where(rel, tile_id, n_t), axis=0)
    sub_hi = jnp.max(jnp.where(rel, tile_id, -1), axis=0)
    sub_lo = jnp.where(sub_hi >= 0, t_first, 0)
    u_lo = jnp.min(t_first, axis=1)
    u_n = jnp.maximum(jnp.max(sub_hi, axis=1) - u_lo + 1, 0)
    u_lo = jnp.where(u_n > 0, u_lo, 0)
    per = COMBINE_TM // TOK_TILE
    seg_rows = blk_first * ROW_BLOCK
    lo_c = seg_rows[None, :] + cnt_start[0::per]
    cnt_c = cnt_end[per - 1::per] - cnt_start[0::per]
    first_c = (lo_c // 16) * 16
    nwin_c = jnp.where(cnt_c > 0, (lo_c + cnt_c - first_c + COMBINE_WIN - 1) // COMBINE_WIN, 0)
    return dict(blk_e=blk_e.astype(i32), valid=valid.astype(i32), base=base.astype(i32),
                u_lo=u_lo.astype(i32), u_n=u_n.astype(i32),
                sub_lo=sub_lo.reshape(-1).astype(i32), sub_hi=sub_hi.reshape(-1).astype(i32),
                seg_rows=seg_rows.astype(i32), first_c=first_c.reshape(-1).astype(i32),
                nwin_c=nwin_c.reshape(-1).astype(i32), n_blocks=nb)


def _experts_kernel(be_ref, bv_ref, base_ref, ulo_ref, un_ref, slo_ref, shi_ref,
                    h_hbm, rt_hbm, w1_ref, w3_ref, w2_ref, y_ref, hbuf, rbuf, hsem, rsem):
    i = pl.program_id(0)
    slot = i % 2

    def tile_copies(s, k, t):
        tok = pl.multiple_of(t * TOK_TILE, TOK_TILE)
        return (pltpu.make_async_copy(h_hbm.at[pl.ds(tok, TOK_TILE)], hbuf.at[s, k], hsem.at[s, k]),
                pltpu.make_async_copy(rt_hbm.at[:, pl.ds(tok, TOK_TILE)], rbuf.at[s, k], rsem.at[s, k]))

    def for_block_tiles(b, s, action):
        for k in range(GATHER_TILES):
            @pl.when(k < un_ref[b])
            def _():
                for cp in tile_copies(s, k, ulo_ref[b] + k):
                    action(cp)

    @pl.when(i == 0)
    def _():
        def clear(k, carry):
            hbuf[k // (GATHER_TILES + 1), k % (GATHER_TILES + 1)] = jnp.zeros((TOK_TILE, D_MODEL), BF16)
            rbuf[k // (GATHER_TILES + 1), k % (GATHER_TILES + 1)] = jnp.zeros((N_EXPERTS, TOK_TILE), F32)
            return carry

        lax.fori_loop(0, 2 * (GATHER_TILES + 1), clear, 0)
        for_block_tiles(0, 0, lambda cp: cp.start())

    for_block_tiles(i, slot, lambda cp: cp.wait())

    @pl.when(i + 1 < pl.num_programs(0))
    def _():
        for_block_tiles(i + 1, 1 - slot, lambda cp: cp.start())

    @pl.when(bv_ref[i] == 1)
    def _():
        e = be_ref[i]
        row_id = lax.broadcasted_iota(jnp.int32, (SUB_BLOCK, TOK_TILE), 0).astype(F32)
        expert_row = lax.broadcasted_iota(jnp.int32, (N_EXPERTS, TOK_TILE), 0) == e
        u_lo = ulo_ref[i]
        selected = []

        def tile_weights(ranks8, first_rank):
            ranks = jnp.sum(jnp.where(expert_row, ranks8, 0.0), axis=0, keepdims=True)
            return jnp.where(ranks == first_rank + row_id, 1.0, 0.0).astype(BF16)

        for sb in range(N_SUB):
            first_rank = (base_ref[i] + sb * SUB_BLOCK).astype(F32)
            t_lo, t_hi = slo_ref[N_SUB * i + sb], shi_ref[N_SUB * i + sb]
            k0 = jnp.clip(t_lo - u_lo, 0, GATHER_TILES + 1 - WINDOW_TILES)
            picks = []
            for j in range(WINDOW_TILES):
                t = u_lo + k0 + j
                live = (t >= t_lo) & (t <= t_hi) & (k0 + j < GATHER_TILES)
                picks.append(tile_weights(jnp.where(live, rbuf[slot, k0 + j], -1.0), first_rank))
            window = hbuf[slot, pl.ds(k0, WINDOW_TILES)].reshape(WINDOW_TILES * TOK_TILE, D_MODEL)
            rows = jnp.dot(jnp.concatenate(picks, axis=1), window, preferred_element_type=F32)

            def late_tile(t, carry):
                k = t - u_lo
                kk = jnp.minimum(k, GATHER_TILES)

                @pl.when(k >= GATHER_TILES)
                def _():
                    tok = pl.multiple_of(t * TOK_TILE, TOK_TILE)
                    pltpu.sync_copy(h_hbm.at[pl.ds(tok, TOK_TILE)], hbuf.at[slot, GATHER_TILES])
                    pltpu.sync_copy(rt_hbm.at[:, pl.ds(tok, TOK_TILE)], rbuf.at[slot, GATHER_TILES])

                return carry + jnp.dot(tile_weights(rbuf[slot, kk], first_rank), hbuf[slot, kk],
                                       preferred_element_type=F32)

            t_next = u_lo + jnp.minimum(k0 + WINDOW_TILES, GATHER_TILES)
            rows = lax.fori_loop(jnp.maximum(t_next, t_lo), t_hi + 1, late_tile, rows)
            selected.append(rows.astype(BF16))

        hg = jnp.concatenate(selected, axis=0)
        half = w1_ref.shape[2] // 2
        y = None
        for c in range(2):
            a = jnp.dot(hg, w1_ref[0, :, c * half:(c + 1) * half], preferred_element_type=F32)
            b = jnp.dot(hg, w3_ref[0, :, c * half:(c + 1) * half], preferred_element_type=F32)
            part = jnp.dot((_silu(a) * b).astype(BF16), w2_ref[0, c * half:(c + 1) * half, :],
                           preferred_element_type=F32)
            y = part if y is None else y + part
        y_ref[...] = y.astype(BF16)

    @pl.when(bv_ref[i] == 0)
    def _():
        y_ref[...] = jnp.zeros_like(y_ref)


def _moe_experts(h2, rank_t, w1, w3, w2, tb):
    nb = tb["n_blocks"]
    ff = w1.shape[2]
    w_map = lambda i, be, *_: (be[i], 0, 0)
    grid_spec = pltpu.PrefetchScalarGridSpec(
        num_scalar_prefetch=7,
        grid=(nb,),
        in_specs=[
            pl.BlockSpec(memory_space=pl.ANY),
            pl.BlockSpec(memory_space=pl.ANY),
            pl.BlockSpec((1, D_MODEL, ff), w_map, pipeline_mode=pl.Buffered(1)),
            pl.BlockSpec((1, D_MODEL, ff), w_map, pipeline_mode=pl.Buffered(1)),
            pl.BlockSpec((1, ff, D_MODEL), w_map, pipeline_mode=pl.Buffered(1)),
        ],
        out_specs=pl.BlockSpec((ROW_BLOCK, D_MODEL), lambda i, *_: (i, 0)),
        scratch_shapes=[
            pltpu.VMEM((2, GATHER_TILES + 1, TOK_TILE, D_MODEL), BF16),
            pltpu.VMEM((2, GATHER_TILES + 1, N_EXPERTS, TOK_TILE), F32),
            pltpu.SemaphoreType.DMA((2, GATHER_TILES)),
            pltpu.SemaphoreType.DMA((2, GATHER_TILES)),
        ],
    )
    return pl.pallas_call(
        _experts_kernel,
        grid_spec=grid_spec,
        out_shape=jax.ShapeDtypeStruct((nb * ROW_BLOCK, D_MODEL), BF16),
        compiler_params=_params("arbitrary"),
        name="moe_experts",
    )(tb["blk_e"], tb["valid"], tb["base"], tb["u_lo"], tb["u_n"], tb["sub_lo"], tb["sub_hi"],
      h2, rank_t, w1, w3, w2)


def _combine_kernel(*refs, final_norm, n_rows):
    first_ref, nwin_ref, seg_ref, x_ref, mod_ref, comb_ref, rank_ref, y_hbm = refs[0:8]
    fg_ref = refs[8] if final_norm else None
    o_ref, ybuf, ysem, acc_ref = refs[-4:]
    i = pl.program_id(0)
    slot = i % 2
    extra = N_EXPERTS

    def window(b, e):
        return first_ref[b * N_EXPERTS + e], nwin_ref[b * N_EXPERTS + e]

    def win_start(first, j):
        return pl.multiple_of(jnp.minimum(first + j * COMBINE_WIN, n_rows - COMBINE_WIN), 16)

    def for_tile_windows(b, s, action):
        for e in range(N_EXPERTS):
            first, n_win = window(b, e)

            @pl.when(n_win > 0)
            def _():
                action(pltpu.make_async_copy(y_hbm.at[pl.ds(win_start(first, 0), COMBINE_WIN)],
                                             ybuf.at[s, e], ysem.at[s, e]))

    @pl.when(i == 0)
    def _():
        def clear(k, carry):
            ybuf[k // (N_EXPERTS + 1), k % (N_EXPERTS + 1)] = jnp.zeros((COMBINE_WIN, D_MODEL), BF16)
            return carry

        lax.fori_loop(0, 2 * (N_EXPERTS + 1), clear, 0)
        for_tile_windows(0, 0, lambda cp: cp.start())

    for_tile_windows(i, slot, lambda cp: cp.wait())

    @pl.when(i + 1 < pl.num_programs(0))
    def _():
        for_tile_windows(i + 1, 1 - slot, lambda cp: cp.start())

    col_id = lax.broadcasted_iota(jnp.int32, (COMBINE_TM, COMBINE_WIN), 1).astype(F32)

    def weights(e, first, j):
        pos_col = rank_ref[:, e:e + 1] + seg_ref[e].astype(F32)
        start = win_start(first, j).astype(F32)
        return jnp.where(pos_col == start + col_id, comb_ref[:, e:e + 1], 0.0).astype(BF16)

    firsts = [window(i, e) for e in range(N_EXPERTS)]
    pick_all = jnp.concatenate([weights(e, firsts[e][0], 0) for e in range(N_EXPERTS)], axis=1)
    y_all = ybuf[slot, 0:N_EXPERTS].reshape(N_EXPERTS * COMBINE_WIN, D_MODEL)
    acc_ref[...] = jnp.dot(pick_all, y_all, preferred_element_type=F32)

    for e in range(N_EXPERTS):
        first, n_win = firsts[e]

        def more(j, carry):
            pltpu.sync_copy(y_hbm.at[pl.ds(win_start(first, j), COMBINE_WIN)], ybuf.at[slot, extra])
            acc_ref[...] += jnp.dot(weights(e, first, j), ybuf[slot, extra], preferred_element_type=F32)
            return carry

        lax.fori_loop(1, jnp.maximum(n_win, 1), more, 0)

    o_ref[...] = _residual_out(x_ref[...], mod_ref[0, 5:6, :], acc_ref[...], fg_ref)


def _moe_combine(x2d, mod, comb, rank, y_sorted, tb, *, rows_per_mod, mod_base, final_g=None):
    n = x2d.shape[0]
    tm = COMBINE_TM
    row = lambda i, *_: (i, 0)
    const = lambda i, *_: (0, 0)
    mod_idx = _mod_index(tm, rows_per_mod, mod_base)
    in_specs = [
        pl.BlockSpec((tm, D_MODEL), row),
        pl.BlockSpec((1, 6, D_MODEL), lambda i, *_: mod_idx(i)),
        pl.BlockSpec((tm, LANES), row),
        pl.BlockSpec((tm, LANES), row),
        pl.BlockSpec(memory_space=pl.ANY),
    ]
    args = [x2d, mod, comb, rank, y_sorted]
    if final_g is not None:
        in_specs.append(pl.BlockSpec((1, D_MODEL), const))
        args.append(final_g)
    grid_spec = pltpu.PrefetchScalarGridSpec(
        num_scalar_prefetch=3,
        grid=(n // tm,),
        in_specs=in_specs,
        out_specs=pl.BlockSpec((tm, D_MODEL), row),
        scratch_shapes=[
            pltpu.VMEM((2, N_EXPERTS + 1, COMBINE_WIN, D_MODEL), BF16),
            pltpu.SemaphoreType.DMA((2, N_EXPERTS)),
            pltpu.VMEM((tm, D_MODEL), F32),
        ],
    )
    return pl.pallas_call(
        functools.partial(_combine_kernel, final_norm=final_g is not None, n_rows=y_sorted.shape[0]),
        grid_spec=grid_spec,
        out_shape=jax.ShapeDtypeStruct((n, D_MODEL), F32),
        compiler_params=_params("arbitrary"),
        name="moe_combine",
    )(tb["first_c"], tb["nwin_c"], tb["seg_rows"], *args)


def _to_bf16_in(w, steps):
    if w.dtype == BF16:
        return None, w
    flat = w.reshape(-1, w.shape[-1])
    if _can_cast_in_steps(flat, steps):
        return flat, None
    return None, w.astype(BF16)


def _moe_channel_mixer(x2d, mod, y_ssd, y_local, w_a, w_b, g, router_pad, w1, w3, w2, *, rows_per_mod, mod_base,
                       final_g=None):
    w1 = w1.astype(BF16)
    w3 = w3.astype(BF16)
    w2_flat, w2_bf16 = _to_bf16_in(w2, x2d.shape[0] // ROUTER_TM)
    x1, h2, comb, rank, rank_t, cnt, *cast = _router(
        x2d, mod, y_ssd, y_local, w_a, w_b, g, router_pad, tm=ROUTER_TM, rows_per_mod=rows_per_mod,
        mod_base=mod_base, casts=() if w2_flat is None else (w2_flat,))
    if w2_bf16 is None:
        w2_bf16 = cast[0].reshape(w2.shape)
    w2 = w2_bf16
    tb = _routing_tables(cnt, x2d.shape[0])
    y_sorted = _moe_experts(h2, rank_t, w1, w3, w2, tb)
    return _moe_combine(x1, mod, comb, rank, y_sorted, tb, rows_per_mod=rows_per_mod, mod_base=mod_base,
                        final_g=final_g)


def _layer_params(l, w_in, w_out, ssd_conv_w, ssd_conv_b, ssd_dt_bias, ssd_a_log, ssd_d, ssd_norm_g,
                  gm_norm_g, gm_norm_b, gm_ws, gm_bs, sc_conv_w, cf_conv_w, cf_conv_b, cf_norm_g, cf_norm_b):
    wi = w_in[l]
    w_main = jnp.concatenate(
        [wi[:, _C_XBC:_C_DT], wi[:, _C_Z:_C_UV], wi[:, _C_UV:_C_SC], wi[:, _C_CF:_C_END], wi[:, _C_SC:_C_CF]],
        axis=1).astype(BF16)
    w_dt = jnp.pad(wi[:, _C_DT:_C_Z], ((0, 0), (0, LANES - 2 * SSD_HEADS))).astype(BF16)
    return {
        "w_main": w_main,
        "w_dt": w_dt,
        "w_out_a": w_out[l, 0:BRANCH].astype(BF16),
        "w_out_b": w_out[l, BRANCH:].astype(BF16),
        "ssd_conv_w": ssd_conv_w[l],
        "ssd_conv_b": ssd_conv_b[l].reshape(1, -1),
        "ssd_dtp": jnp.stack([jnp.tile(ssd_dt_bias[l].reshape(-1), LANES // (2 * SSD_HEADS)),
                              jnp.tile(ssd_a_log[l].reshape(-1), LANES // (2 * SSD_HEADS))]),
        "ssd_vec": jnp.stack([jnp.repeat(ssd_d[l], HEAD_DIM), ssd_norm_g[l]]),
        "gm_ws": gm_ws[l],
        "gm_bs": jnp.repeat(gm_bs[l].T, HEAD_DIM, axis=1),
        "sc_conv_w": sc_conv_w[l],
        "cf_conv_w": cf_conv_w[l],
        "local_vec": jnp.stack([gm_norm_g[l], gm_norm_b[l], cf_conv_b[l], cf_norm_g[l], cf_norm_b[l]]),
    }


def kernel(x, c, ctx, c_ctx, mod_w, mod_b, norm1_g, norm2_g, w_in, w_out, ssd_conv_w, ssd_conv_b, ssd_dt_bias, ssd_a_log, ssd_d, ssd_norm_g, gm_norm_g, gm_norm_b, gm_ws, gm_bs, sc_conv_w, cf_conv_w, cf_conv_b, cf_norm_g, cf_norm_b, ffn_w1, ffn_w3, ffn_w2, moe_router, moe_w1, moe_w3, moe_w2, final_norm_g):
    bsz, seq, _ = x.shape
    ctx_len = ctx.shape[1]
    depth = w_in.shape[0]
    n_x, n_c = bsz * seq, bsz * ctx_len
    tm = 512

    mod = _modulation(jnp.concatenate([c, c_ctx[None, :]], axis=0), mod_w, mod_b)
    x_map = dict(tm=tm, rows_per_mod=seq, mod_base=0)
    c_map = dict(tm=tm, rows_per_mod=n_c, mod_base=bsz)
    zero_state = jnp.zeros((bsz, 2, SSD_GROUPS, SSD_STATE, 2 * HEAD_DIM), F32)

    xs = x.reshape(n_x, D_MODEL)
    xc = ctx.reshape(n_c, D_MODEL)
    moe_bf16 = {}
    for l in range(depth):
        last = l == depth - 1
        lp = _layer_params(l, w_in, w_out, ssd_conv_w, ssd_conv_b, ssd_dt_bias, ssd_a_log, ssd_d, ssd_norm_g,
                           gm_norm_g, gm_norm_b, gm_ws, gm_bs, sc_conv_w, cf_conv_w, cf_conv_b, cf_norm_g,
                           cf_norm_b)
        g1 = norm1_g[l].reshape(1, D_MODEL)
        g2 = norm2_g[l].reshape(1, D_MODEL)
        ffn_f32 = [ffn_w1[l // 2], ffn_w3[l // 2], ffn_w2[l // 2]] if l % 2 == 0 else []
        ctx_tm = min(tm, ctx_len)
        if not all(_can_cast_in_steps(w, n_c // ctx_tm) for w in ffn_f32):
            ffn_f32 = []
        zc_ssd, yc_loc, dtc, ffn_bf16 = _token_mixer_front(xc, mod[l], g1, lp, tm=ctx_tm, seq=ctx_len,
                                                           rows_per_mod=n_c, mod_base=bsz, local=not last,
                                                           casts=ffn_f32)
        yc_ssd, states = _ssd_mixer(zc_ssd.reshape(bsz, ctx_len, W_SSD), dtc.reshape(bsz, ctx_len, LANES),
                                    zero_state, lp)
        z_ssd, y_loc, dtx, _ = _token_mixer_front(xs, mod[l], g1, lp, tm=min(FRONT_TM, seq), seq=seq,
                                                  rows_per_mod=seq, mod_base=0)
        y_ssd, _ = _ssd_mixer(z_ssd.reshape(bsz, seq, W_SSD), dtx.reshape(bsz, seq, LANES), states, lp)
        streams = [(xs, y_ssd.reshape(n_x, BRANCH), y_loc, x_map)]
        if not last:
            streams.append((xc, yc_ssd.reshape(n_c, BRANCH), yc_loc, c_map))
        fg = final_norm_g.reshape(1, D_MODEL) if last else None
        i = l // 2
        if l % 2 == 0:
            w1, w3, w2 = ffn_bf16 or (ffn_w1[i].astype(BF16), ffn_w3[i].astype(BF16), ffn_w2[i].astype(BF16))
            casts = []
            if l + 1 < depth:
                casts = [w.reshape(-1, w.shape[-1]) for w in (moe_w1[i], moe_w3[i])]
                if not all(_can_cast_in_steps(w, n_x // tm) for w in casts):
                    casts = []
            outs = []
            for t, ya, yb, m in streams:
                out, cast = _dense_channel_mixer(t, mod[l], ya, yb, lp["w_out_a"], lp["w_out_b"], g2, w1, w3, w2,
                                                 final_g=fg if t is xs else None, casts=casts if t is xs else (),
                                                 **m)
                outs.append(out)
                if cast:
                    moe_bf16[i] = [c.reshape(moe_w1[i].shape) for c in cast]
        else:
            w1, w3 = moe_bf16.get(i) or (moe_w1[i], moe_w3[i])
            w2 = moe_w2[i]
            r_hi = moe_router[i].astype(BF16)
            r_lo = (moe_router[i] - r_hi.astype(F32)).astype(BF16)
            r_pad = jnp.pad(jnp.concatenate([r_hi, r_lo], axis=1), ((0, 0), (0, LANES - 2 * N_EXPERTS)))
            outs = [_moe_channel_mixer(t, mod[l], ya, yb, lp["w_out_a"], lp["w_out_b"], g2, r_pad, w1, w3, w2,
                                       rows_per_mod=m["rows_per_mod"], mod_base=m["mod_base"],
                                       final_g=fg if t is xs else None) for t, ya, yb, m in streams]
        xs = outs[0]
        if not last:
            xc = outs[1]
    return xs.reshape(bsz, seq, D_MODEL)
```

```python
import functools
import math

import jax
import jax.numpy as jnp
from jax import lax
from jax.experimental import pallas as pl
from jax.experimental.pallas import tpu as pltpu

F32 = jnp.float32
BF16 = jnp.bfloat16

LANES = 128
SUBLANES = 8
VMEM_LIMIT_BYTES = 56 * 1024 * 1024

D_MODEL = 1024
BRANCH = 256
HEAD_DIM = 64
SSD_HEADS = 4
SSD_GROUPS = 2
SSD_STATE = 128
CHUNK = 128
SSD_ROWS_PER_STEP = 4096
SSD_MAX_SEQS_PER_STEP = 4
SSD_CONV_CH = BRANCH + 2 * SSD_GROUPS * SSD_STATE
N_EXPERTS = 8
TOP_K = 2
TOK_TILE = 256
ROW_BLOCK = 512
SUB_BLOCK = 256
N_SUB = ROW_BLOCK // SUB_BLOCK
GATHER_TILES = 11
WINDOW_TILES = 5
ROUTER_TM = 1024
COMBINE_TM = 512
COMBINE_WIN = 256
FRONT_TM = 1024
HALO = 16
RMS_EPS = 1e-6
LN_EPS = 1e-5

_C_XBC = 0
_C_DT = _C_XBC + SSD_CONV_CH
_C_Z = _C_DT + 2 * SSD_HEADS
_C_UV = _C_Z + BRANCH
_C_SC = _C_UV + 2 * BRANCH
_C_CF = _C_SC + 3 * BRANCH
_C_END = _C_CF + 2 * BRANCH
W_SSD = SSD_CONV_CH + BRANCH
W_UVCF = 4 * BRANCH
W_SC = 3 * BRANCH
W_MAIN = W_SSD + W_UVCF + W_SC


def _params(*sem):
    return pltpu.CompilerParams(dimension_semantics=sem, vmem_limit_bytes=VMEM_LIMIT_BYTES)


def _sigmoid(v):
    return 0.5 + 0.5 * jnp.tanh(0.5 * v)


def _silu(v):
    half = 0.5 * v
    return half + half * jnp.tanh(half)


def _softplus(v):
    return jnp.maximum(v, 0.0) + jnp.log1p(jnp.exp(-jnp.abs(v)))


def _gelu_tanh(v):
    return 0.5 * v * (1.0 + jnp.tanh(0.7978845608028654 * (v + 0.044715 * (v * v * v))))


def _norm_modulate(x, g, shift, scale):
    ms = jnp.mean(x * x, axis=-1, keepdims=True)
    return (x * lax.rsqrt(ms + RMS_EPS) * g) * (1.0 + scale) + shift


def _mod_kernel(c_ref, w_ref, b_ref, o_ref):
    s = _silu(c_ref[...])
    s_hi = s.astype(BF16)
    s_lo = (s - s_hi.astype(F32)).astype(BF16)
    w = w_ref[0]
    w_hi = w.astype(BF16)
    w_lo = (w - w_hi.astype(F32)).astype(BF16)
    o_ref[0, 0] = (jnp.dot(s_hi, w_hi, preferred_element_type=F32)
                   + jnp.dot(s_lo, w_hi, preferred_element_type=F32)
                   + jnp.dot(s_hi, w_lo, preferred_element_type=F32) + b_ref[0, 0])


def _modulation(c_all, mod_w, mod_b):
    depth = mod_w.shape[0]
    rows = c_all.shape[0]
    out = pl.pallas_call(
        _mod_kernel,
        grid=(depth, 6),
        in_specs=[
            pl.BlockSpec((rows, D_MODEL), lambda l, j: (0, 0)),
            pl.BlockSpec((1, D_MODEL, D_MODEL), lambda l, j: (l, 0, j)),
            pl.BlockSpec((1, 1, 1, D_MODEL), lambda l, j: (l, j, 0, 0)),
        ],
        out_specs=pl.BlockSpec((1, 1, rows, D_MODEL), lambda l, j: (l, j, 0, 0)),
        out_shape=jax.ShapeDtypeStruct((depth, 6, rows, D_MODEL), F32),
        compiler_params=_params("arbitrary", "arbitrary"),
        name="modulation",
    )(c_all, mod_w, mod_b.reshape(depth, 6, 1, D_MODEL))
    return jnp.transpose(out, (0, 2, 1, 3))


def _mod_index(tm, rows_per_mod, mod_base):
    return lambda i: (mod_base + (i * tm) // rows_per_mod, 0, 0)


def _fill_stage(stage_ref, prev, main, nxt):
    rows = main.shape[0]
    stage_ref[0:HALO, :] = prev
    stage_ref[HALO:HALO + rows, :] = main
    stage_ref[HALO + rows:2 * HALO + rows, :] = nxt


def _dwconv_rows(stage_ref, w_ref, taps, r, rows, c0, width):
    acc = None
    for k in range(taps):
        lo = HALO + r + k - taps // 2
        term = stage_ref[lo:lo + rows, c0:c0 + width] * w_ref[k:k + 1, c0:c0 + width]
        acc = term if acc is None else acc + term
    return acc


def _front_kernel(*refs, tm, seq, local, n_casts):
    x_ref, xp_ref, xn_ref, mod_ref, g_ref, w_ref, wdt_ref, convw_ref, convb_ref = refs[0:9]
    pos = 9
    if local:
        ws_ref, bs_ref, scw_ref, cfw_ref, vec_ref = refs[pos:pos + 5]
        pos += 5
    cast_in = refs[pos:pos + n_casts]
    pos += n_casts
    zssd_ref = refs[pos]
    if local:
        yloc_ref, dt_ref = refs[pos + 1:pos + 3]
        pos += 3
    else:
        dt_ref = refs[pos + 1]
        pos += 2
    cast_out = refs[pos:pos + n_casts]
    pos += n_casts
    if local:
        ssd_stage, sc_stage, cf_stage, rot_ref = refs[pos:]
    else:
        (ssd_stage,) = refs[pos:]
    _cast_slices(cast_in, cast_out)
    i = pl.program_id(0)
    tiles_per_seq = seq // tm
    keep_prev = jnp.where(i % tiles_per_seq == 0, 0.0, 1.0)
    keep_next = jnp.where(i % tiles_per_seq == tiles_per_seq - 1, 0.0, 1.0)
    g, shift, scale = g_ref[...], mod_ref[0, 0:1, :], mod_ref[0, 1:2, :]
    h = _norm_modulate(x_ref[...], g, shift, scale).astype(BF16)
    h_halo = _norm_modulate(jnp.concatenate([xp_ref[...], xn_ref[...]], axis=0), g, shift, scale).astype(BF16)
    z = jnp.dot(h, w_ref[...], preferred_element_type=F32)
    z_halo = jnp.dot(h_halo, w_ref[...], preferred_element_type=F32)
    zp = z_halo[0:HALO] * keep_prev
    zn = z_halo[HALO:2 * HALO] * keep_next
    dt_ref[...] = jnp.dot(h, wdt_ref[...], preferred_element_type=F32)

    _fill_stage(ssd_stage, zp[:, 0:SSD_CONV_CH], z[:, 0:SSD_CONV_CH], zn[:, 0:SSD_CONV_CH])
    for r in range(0, tm, CHUNK):
        for c0 in range(0, SSD_CONV_CH, BRANCH):
            v = _dwconv_rows(ssd_stage, convw_ref, 5, r, CHUNK, c0, BRANCH) + convb_ref[0:1, c0:c0 + BRANCH]
            zssd_ref[r:r + CHUNK, c0:c0 + BRANCH] = _silu(v).astype(BF16)
    zssd_ref[:, SSD_CONV_CH:W_SSD] = z[:, SSD_CONV_CH:W_SSD].astype(BF16)
    if local:
        _local_mixers_tile(z[:, W_SSD:W_MAIN], zp[:, W_SSD:W_MAIN], zn[:, W_SSD:W_MAIN], ws_ref, bs_ref, scw_ref,
                           cfw_ref, vec_ref, sc_stage, cf_stage, rot_ref, yloc_ref, tm)


def _local_mixers_tile(z, zp, zn, ws_ref, bs_ref, scw_ref, cfw_ref, vec_ref, sc_stage, cf_stage, rot_ref,
                       yloc_ref, tm):
    lane = lax.broadcasted_iota(jnp.int32, (CHUNK, BRANCH), 1)
    gm_g, gm_b = vec_ref[0:1, :], vec_ref[1:2, :]
    cf_b, cf_g, cf_nb = vec_ref[2:3, :], vec_ref[3:4, :], vec_ref[4:5, :]
    c_uv, c_cf, c_sc = 0, 2 * BRANCH, W_UVCF

    def layer_norm(v, gain, bias):
        mu = jnp.mean(v, axis=-1, keepdims=True)
        var = jnp.mean(jnp.square(v - mu), axis=-1, keepdims=True)
        return (v - mu) * lax.rsqrt(var + LN_EPS) * gain + bias

    def sc_in(t):
        return t[:, c_sc + BRANCH:c_sc + 2 * BRANCH] * t[:, c_sc + 2 * BRANCH:c_sc + 3 * BRANCH]

    def cf_in(t):
        return t[:, c_cf:c_cf + BRANCH] * _sigmoid(t[:, c_cf + BRANCH:c_cf + 2 * BRANCH])

    _fill_stage(sc_stage, sc_in(zp), sc_in(z), sc_in(zn))
    _fill_stage(cf_stage, cf_in(zp), cf_in(z), cf_in(zn))
    span = tm + 2 * HALO - SUBLANES
    for phase in range(SUBLANES):
        rot_ref[phase] = cf_stage[phase:phase + span, :]

    for r in range(0, tm, CHUNK):
        ge = _gelu_tanh(z[r:r + CHUNK, c_uv:c_uv + 2 * BRANCH])
        u = ge[:, 0:BRANCH]
        v = layer_norm(ge[:, BRANCH:2 * BRANCH], gm_g, gm_b)
        s = bs_ref[...]
        for hd in range(BRANCH // HEAD_DIM):
            in_head = (lane >= hd * HEAD_DIM) & (lane < (hd + 1) * HEAD_DIM)
            s = s + jnp.dot(ws_ref[hd].astype(BF16), jnp.where(in_head, v, 0.0).astype(BF16),
                            preferred_element_type=F32)
        yloc_ref[r:r + CHUNK, 0:BRANCH] = (u * s).astype(BF16)
        gate = z[r:r + CHUNK, c_sc:c_sc + BRANCH]
        yloc_ref[r:r + CHUNK, BRANCH:2 * BRANCH] = (gate * _dwconv_rows(sc_stage, scw_ref, 3, r, CHUNK, 0, BRANCH)
                                                    ).astype(BF16)
        cv = cf_b
        for k in range(31):
            lo = HALO + r + k - 31 // 2
            q8 = (lo // SUBLANES) * SUBLANES
            cv = cv + rot_ref[lo % SUBLANES, q8:q8 + CHUNK, :] * cfw_ref[k:k + 1, :]
        yloc_ref[r:r + CHUNK, 2 * BRANCH:3 * BRANCH] = _silu(layer_norm(cv, cf_g, cf_nb)).astype(BF16)


def _local_mixer_param_specs():
    const = lambda i, *_: (0, 0)
    return [
        pl.BlockSpec((BRANCH // HEAD_DIM, CHUNK, CHUNK), lambda i, *_: (0, 0, 0)),
        pl.BlockSpec((CHUNK, BRANCH), const),
        pl.BlockSpec((3, BRANCH), const),
        pl.BlockSpec((31, BRANCH), const),
        pl.BlockSpec((5, BRANCH), const),
    ]


def _local_mixer_params(lp):
    return [lp["gm_ws"], lp["gm_bs"], lp["sc_conv_w"], lp["cf_conv_w"], lp["local_vec"]]


def _local_mixer_scratch(tm):
    span = tm + 2 * HALO
    return [pltpu.VMEM((span, BRANCH), F32), pltpu.VMEM((span, BRANCH), F32),
            pltpu.VMEM((SUBLANES, span - SUBLANES, BRANCH), F32)]


def _token_mixer_front(x2d, mod, g, lp, *, tm, seq, rows_per_mod, mod_base, local=True, casts=()):
    n = x2d.shape[0]
    steps = n // tm
    per = tm // HALO
    row = lambda i: (i, 0)
    const = lambda i: (0, 0)
    in_specs = [
        pl.BlockSpec((tm, D_MODEL), row),
        pl.BlockSpec((HALO, D_MODEL), lambda i: (jnp.maximum(i * per - 1, 0), 0)),
        pl.BlockSpec((HALO, D_MODEL), lambda i: (jnp.minimum((i + 1) * per, n // HALO - 1), 0)),
        pl.BlockSpec((1, 6, D_MODEL), _mod_index(tm, rows_per_mod, mod_base)),
        pl.BlockSpec((1, D_MODEL), const),
        pl.BlockSpec((D_MODEL, W_MAIN if local else W_SSD), const),
        pl.BlockSpec((D_MODEL, LANES), const),
        pl.BlockSpec((5, SSD_CONV_CH), const),
        pl.BlockSpec((1, SSD_CONV_CH), const),
    ]
    args = [x2d, x2d, x2d, mod, g, lp["w_main"], lp["w_dt"], lp["ssd_conv_w"], lp["ssd_conv_b"]]
    out_specs = [pl.BlockSpec((tm, W_SSD), row)]
    out_shape = [jax.ShapeDtypeStruct((n, W_SSD), BF16)]
    scratch = [pltpu.VMEM((tm + 2 * HALO, SSD_CONV_CH), F32)]
    if local:
        in_specs += _local_mixer_param_specs()
        args += _local_mixer_params(lp)
        out_specs.append(pl.BlockSpec((tm, 3 * BRANCH), row))
        out_shape.append(jax.ShapeDtypeStruct((n, 3 * BRANCH), BF16))
        scratch += _local_mixer_scratch(tm)
    out_specs.append(pl.BlockSpec((tm, LANES), row))
    out_shape.append(jax.ShapeDtypeStruct((n, LANES), F32))
    cast_in, cast_out, cast_shape = _cast_specs(casts, steps)
    outs = pl.pallas_call(
        functools.partial(_front_kernel, tm=tm, seq=seq, local=local, n_casts=len(casts)),
        grid=(steps,),
        in_specs=in_specs + cast_in,
        out_specs=out_specs + cast_out,
        out_shape=out_shape + cast_shape,
        scratch_shapes=scratch,
        compiler_params=_params("parallel"),
        name="token_mixer_front",
    )(*args, *casts)
    n_main = 3 if local else 2
    main = (outs[0], outs[1], outs[2]) if local else (outs[0], None, outs[1])
    return main + (list(outs[n_main:]),)


def _ssd_kernel(z_ref, dt_ref, init_ref, dtp_ref, vec_ref,
                y_ref, fin_ref, p_ref, dts_ref, pt_ref, dtt_ref, yacc_ref, st_ref, *, seq, n_seq):
    nc = seq // CHUNK
    lane = lax.broadcasted_iota(jnp.int32, (CHUNK, LANES), 1)
    row_i = lax.broadcasted_iota(jnp.int32, (CHUNK, CHUNK), 0)
    col_i = lax.broadcasted_iota(jnp.int32, (CHUNK, CHUNK), 1)
    lower = col_i <= row_i
    upper = col_i >= row_i
    head0 = lane < HEAD_DIM

    fwd_lane = (lane % (2 * SSD_HEADS)) < SSD_HEADS
    for q in range(n_seq):
        dt_all = _softplus(dt_ref[q] + dtp_ref[0:1, :])
        adt = dt_all * (-jnp.exp(dtp_ref[1:2, :]))
        p_all = (jnp.dot(lower.astype(F32), jnp.where(fwd_lane, adt, 0.0), preferred_element_type=F32,
                         precision=lax.Precision.HIGHEST)
                 + jnp.dot(upper.astype(F32), jnp.where(fwd_lane, 0.0, adt), preferred_element_type=F32,
                           precision=lax.Precision.HIGHEST))
        pt_ref[q] = p_all.T
        dtt_ref[q] = dt_all.T
        for c in range(nc):
            shift = (LANES - 2 * SSD_HEADS * c) % LANES
            p_ref[q, c * CHUNK:(c + 1) * CHUNK, :] = pltpu.roll(p_all, shift, axis=1) if shift else p_all
            dts_ref[q, c * CHUNK:(c + 1) * CHUNK, :] = pltpu.roll(dt_all, shift, axis=1) if shift else dt_all

    def group_inputs(q, rows, g):
        xg = z_ref[q, rows, g * LANES:(g + 1) * LANES].astype(F32)
        bg = z_ref[q, rows, BRANCH + g * SSD_STATE:BRANCH + (g + 1) * SSD_STATE]
        cg = z_ref[q, rows, BRANCH + (SSD_GROUPS + g) * SSD_STATE:BRANCH + (SSD_GROUPS + g + 1) * SSD_STATE]
        return xg, bg, cg

    def spread(m, j):
        return jnp.broadcast_to(m[:, j:j + 1], (CHUNK, LANES))

    def pair(a0, a1):
        return jnp.where(head0, a0, a1)

    st_ref[...] = init_ref[:, 0]

    def fwd_chunk(q, c):
        r0 = pl.multiple_of(c * CHUNK, CHUNK)
        rows = pl.ds(r0, CHUNK)
        pm = p_ref[q, rows, :]
        dt = dts_ref[q, rows, :]
        head_rows = pl.ds(pl.multiple_of(c * 2 * SSD_HEADS, 2 * SSD_HEADS), 2 * SSD_HEADS)
        pm_t = pt_ref[q, head_rows, :]
        dt_t = dtt_ref[q, head_rows, :]
        last = pm[CHUNK - 1:CHUNK, :]
        to_end = dt * jnp.exp(last - pm)
        for g in range(SSD_GROUPS):
            xg, bg, cg = group_inputs(q, rows, g)
            gram = lax.dot_general(cg, bg, (((1,), (1,)), ((), ())), preferred_element_type=F32)
            s_prev = st_ref[q, g]
            y = jnp.dot(cg, s_prev.astype(BF16), preferred_element_type=F32)
            hf0, hf1 = 2 * g, 2 * g + 1
            hb0, hb1 = SSD_HEADS + hf0, SSD_HEADS + hf1
            p_f0, p_f1 = spread(pm, hf0), spread(pm, hf1)
            y = y * jnp.exp(pair(p_f0, p_f1))
            for p_f, hf, hb, mask in ((p_f0, hf0, hb0, head0), (p_f1, hf1, hb1, jnp.logical_not(head0))):
                dec_f = jnp.exp(jnp.where(lower, p_f - pm_t[hf:hf + 1, :], -jnp.inf))
                dec_b = jnp.exp(jnp.where(upper, spread(pm, hb) - pm_t[hb:hb + 1, :], -jnp.inf))
                m = gram * (dec_f * dt_t[hf:hf + 1, :] + dec_b * dt_t[hb:hb + 1, :])
                xh = jnp.where(mask, xg, 0.0).astype(BF16)
                y = y + jnp.dot(m.astype(BF16), xh, preferred_element_type=F32)
            yacc_ref[q, rows, g * LANES:(g + 1) * LANES] = y
            w_end = pair(spread(to_end, hf0), spread(to_end, hf1))
            upd = lax.dot_general(bg, (xg * w_end).astype(BF16), (((0,), (0,)), ((), ())),
                                  preferred_element_type=F32)
            decay = jnp.exp(jnp.where(head0[0:1, :], last[:, hf0:hf0 + 1], last[:, hf1:hf1 + 1]))
            st_ref[q, g] = s_prev * decay + upd

    def fwd(c, carry):
        for q in range(n_seq):
            fwd_chunk(q, c)
        return carry

    lax.fori_loop(0, nc, fwd, 0, unroll=2)
    fin_ref[:, 0] = st_ref[...]

    st_ref[...] = init_ref[:, 1]
    d_skip = vec_ref[0:1, :]
    norm_g = vec_ref[1:2, :]

    def bwd_chunk(q, c):
        r0 = pl.multiple_of(c * CHUNK, CHUNK)
        rows = pl.ds(r0, CHUNK)
        pm = p_ref[q, rows, :]
        dt = dts_ref[q, rows, :]
        first = pm[0:1, :]
        to_end = dt * jnp.exp(first - pm)
        for g in range(SSD_GROUPS):
            xg, bg, cg = group_inputs(q, rows, g)
            hb0, hb1 = SSD_HEADS + 2 * g, SSD_HEADS + 2 * g + 1
            s_prev = st_ref[q, g]
            y = (jnp.dot(cg, s_prev.astype(BF16), preferred_element_type=F32)
                 * jnp.exp(pair(spread(pm, hb0), spread(pm, hb1))))
            yacc_ref[q, rows, g * LANES:(g + 1) * LANES] = (
                yacc_ref[q, rows, g * LANES:(g + 1) * LANES] + y + d_skip[:, g * LANES:(g + 1) * LANES] * xg)
            w_end = pair(spread(to_end, hb0), spread(to_end, hb1))
            upd = lax.dot_general(bg, (xg * w_end).astype(BF16), (((0,), (0,)), ((), ())),
                                  preferred_element_type=F32)
            decay = jnp.exp(jnp.where(head0[0:1, :], first[:, hb0:hb0 + 1], first[:, hb1:hb1 + 1]))
            st_ref[q, g] = s_prev * decay + upd
        gated = yacc_ref[q, rows, :] * _silu(z_ref[q, rows, SSD_CONV_CH:W_SSD].astype(F32))
        ms = jnp.mean(gated * gated, axis=-1, keepdims=True)
        y_ref[q, rows, :] = (gated * lax.rsqrt(ms + RMS_EPS) * norm_g).astype(BF16)

    def bwd(k, carry):
        for q in range(n_seq):
            bwd_chunk(q, nc - 1 - k)
        return carry

    lax.fori_loop(0, nc, bwd, 0)
    fin_ref[:, 1] = st_ref[...]


def _ssd_mixer(z_ssd, dt, init, lp):
    bsz, seq, _ = z_ssd.shape
    nc = seq // CHUNK
    n_hd = 2 * SSD_HEADS
    n_seq = max(1, min(SSD_MAX_SEQS_PER_STEP, SSD_ROWS_PER_STEP // seq))
    while bsz % n_seq:
        n_seq -= 1
    assert nc * n_hd <= LANES
    dt = dt[:, :, :n_hd].reshape(bsz, nc, CHUNK, n_hd).transpose(0, 2, 1, 3).reshape(bsz, CHUNK, nc * n_hd)
    dt = jnp.pad(dt, ((0, 0), (0, 0), (0, LANES - nc * n_hd)))
    seq_map = lambda b: (b, 0, 0)
    st_map = lambda b: (b, 0, 0, 0, 0)
    const = lambda b: (0, 0)
    st_shape = (bsz, 2, SSD_GROUPS, SSD_STATE, 2 * HEAD_DIM)
    return pl.pallas_call(
        functools.partial(_ssd_kernel, seq=seq, n_seq=n_seq),
        grid=(bsz // n_seq,),
        in_specs=[
            pl.BlockSpec((n_seq, seq, W_SSD), seq_map),
            pl.BlockSpec((n_seq, CHUNK, LANES), seq_map),
            pl.BlockSpec((n_seq,) + st_shape[1:], st_map),
            pl.BlockSpec((2, LANES), const),
            pl.BlockSpec((2, BRANCH), const),
        ],
        out_specs=[
            pl.BlockSpec((n_seq, seq, BRANCH), seq_map),
            pl.BlockSpec((n_seq,) + st_shape[1:], st_map),
        ],
        out_shape=[
            jax.ShapeDtypeStruct((bsz, seq, BRANCH), BF16),
            jax.ShapeDtypeStruct(st_shape, F32),
        ],
        scratch_shapes=[
            pltpu.VMEM((n_seq, seq, LANES), F32),
            pltpu.VMEM((n_seq, seq, LANES), F32),
            pltpu.VMEM((n_seq, LANES, CHUNK), F32),
            pltpu.VMEM((n_seq, LANES, CHUNK), F32),
            pltpu.VMEM((n_seq, seq, BRANCH), F32),
            pltpu.VMEM((n_seq, SSD_GROUPS, SSD_STATE, 2 * HEAD_DIM), F32),
        ],
        compiler_params=_params("parallel"),
        name="ssd_mixer",
    )(z_ssd, dt, init, lp["ssd_dtp"], lp["ssd_vec"])


def _token_mixed(x_ref, mod_ref, ya_ref, yb_ref, wa_ref, wb_ref):
    proj = (jnp.dot(ya_ref[...], wa_ref[...], preferred_element_type=F32)
            + jnp.dot(yb_ref[...], wb_ref[...], preferred_element_type=F32))
    return x_ref[...] + mod_ref[0, 2:3, :] * proj


def _token_mixed_specs(tm, rows_per_mod, mod_base):
    row = lambda i, *_: (i, 0)
    const = lambda i, *_: (0, 0)
    mod_idx = _mod_index(tm, rows_per_mod, mod_base)
    return [
        pl.BlockSpec((tm, D_MODEL), row),
        pl.BlockSpec((1, 6, D_MODEL), lambda i, *_: mod_idx(i)),
        pl.BlockSpec((tm, BRANCH), row),
        pl.BlockSpec((tm, 3 * BRANCH), row),
        pl.BlockSpec((BRANCH, D_MODEL), const),
        pl.BlockSpec((3 * BRANCH, D_MODEL), const),
    ]


def _residual_out(x, gate, delta, fg_ref):
    out = x + gate * delta
    if fg_ref is not None:
        ms = jnp.mean(out * out, axis=-1, keepdims=True)
        out = out * lax.rsqrt(ms + RMS_EPS) * fg_ref[...]
    return out


def _cast_slices(srcs, dsts):
    for src, dst in zip(srcs, dsts):
        dst[...] = src[...].astype(BF16)


def _cast_specs(casts, steps):
    row = lambda i, *_: (i, 0)
    blocks = [(arr.shape[0] // steps, arr.shape[1]) for arr in casts]
    return ([pl.BlockSpec(b, row) for b in blocks], [pl.BlockSpec(b, row) for b in blocks],
            [jax.ShapeDtypeStruct(arr.shape, BF16) for arr in casts])


def _swiglu_kernel(*refs, final_norm, n_casts):
    x_ref, mod_ref, ya_ref, yb_ref, wa_ref, wb_ref, g_ref, w1_ref, w3_ref, w2_ref = refs[0:10]
    pos = 10
    fg_ref = None
    if final_norm:
        fg_ref = refs[pos]
        pos += 1
    cast_in = refs[pos:pos + n_casts]
    o_ref = refs[pos + n_casts]
    cast_out = refs[pos + n_casts + 1:pos + 2 * n_casts + 1]
    x1 = _token_mixed(x_ref, mod_ref, ya_ref, yb_ref, wa_ref, wb_ref)
    h = _norm_modulate(x1, g_ref[...], mod_ref[0, 3:4, :], mod_ref[0, 4:5, :]).astype(BF16)
    a = jnp.dot(h, w1_ref[...], preferred_element_type=F32)
    b = jnp.dot(h, w3_ref[...], preferred_element_type=F32)
    y = jnp.dot((_silu(a) * b).astype(BF16), w2_ref[...], preferred_element_type=F32)
    o_ref[...] = _residual_out(x1, mod_ref[0, 5:6, :], y, fg_ref)
    _cast_slices(cast_in, cast_out)


def _dense_channel_mixer(x2d, mod, y_ssd, y_local, w_a, w_b, g, w1, w3, w2, *, tm, rows_per_mod, mod_base,
                         final_g=None, casts=()):
    n = x2d.shape[0]
    steps = n // tm
    ff = w1.shape[1]
    const = lambda i: (0, 0)
    row = lambda i: (i, 0)
    resident = dict(pipeline_mode=pl.Buffered(1))
    in_specs = _token_mixed_specs(tm, rows_per_mod, mod_base) + [
        pl.BlockSpec((1, D_MODEL), const),
        pl.BlockSpec((D_MODEL, ff), const, **resident),
        pl.BlockSpec((D_MODEL, ff), const, **resident),
        pl.BlockSpec((ff, D_MODEL), const, **resident),
    ]
    args = [x2d, mod, y_ssd, y_local, w_a, w_b, g, w1, w3, w2]
    if final_g is not None:
        in_specs.append(pl.BlockSpec((1, D_MODEL), const))
        args.append(final_g)
    cast_in, cast_out, cast_shape = _cast_specs(casts, steps)
    in_specs += cast_in
    args += list(casts)
    out_specs = [pl.BlockSpec((tm, D_MODEL), row)] + cast_out
    out_shape = [jax.ShapeDtypeStruct((n, D_MODEL), F32)] + cast_shape
    outs = pl.pallas_call(
        functools.partial(_swiglu_kernel, final_norm=final_g is not None, n_casts=len(casts)),
        grid=(steps,),
        in_specs=in_specs,
        out_specs=out_specs,
        out_shape=out_shape,
        compiler_params=_params("parallel"),
        name="dense_channel_mixer",
    )(*args)
    return outs[0], list(outs[1:])


def _can_cast_in_steps(arr2d, steps):
    return arr2d.shape[0] % steps == 0 and (arr2d.shape[0] // steps) % (2 * SUBLANES) == 0


def _router_kernel(*refs, n_casts):
    x_ref, mod_ref, ya_ref, yb_ref, wa_ref, wb_ref, g_ref, r_ref = refs[0:8]
    cast_in = refs[8:8 + n_casts]
    x1_ref, h_ref, comb_ref, rank_ref, rankt_ref, cnt_ref = refs[8 + n_casts:14 + n_casts]
    cast_out = refs[14 + n_casts:14 + 2 * n_casts]
    carry_ref = refs[-1]
    _cast_slices(cast_in, cast_out)

    @pl.when(pl.program_id(0) == 0)
    def _():
        carry_ref[...] = jnp.zeros_like(carry_ref)

    x1 = _token_mixed(x_ref, mod_ref, ya_ref, yb_ref, wa_ref, wb_ref)
    x1_ref[...] = x1
    h = _norm_modulate(x1, g_ref[...], mod_ref[0, 3:4, :], mod_ref[0, 4:5, :])
    h_hi = h.astype(BF16)
    h_ref[...] = h_hi
    h_lo = (h - h_hi.astype(F32)).astype(BF16)
    parts = jnp.dot(jnp.concatenate([h_hi, h_lo], axis=0), r_ref[...], preferred_element_type=F32)
    parts = parts[0:h.shape[0], :] + parts[h.shape[0]:, :]
    logits = parts + pltpu.roll(parts, LANES - N_EXPERTS, axis=1)
    lane = lax.broadcasted_iota(jnp.int32, logits.shape, 1).astype(F32)
    logits = jnp.where(lane < N_EXPERTS, logits, -jnp.inf)
    m1 = jnp.max(logits, axis=-1, keepdims=True)
    i1 = jnp.min(jnp.where(logits == m1, lane, float(LANES)), axis=-1, keepdims=True)
    rest = jnp.where(lane == i1, -jnp.inf, logits)
    m2 = jnp.max(rest, axis=-1, keepdims=True)
    i2 = jnp.min(jnp.where(rest == m2, lane, float(LANES)), axis=-1, keepdims=True)
    e2 = jnp.exp(m2 - m1)
    denom = 1.0 + e2
    comb_ref[...] = jnp.where(lane == i1, 1.0 / denom, 0.0) + jnp.where(lane == i2, e2 / denom, 0.0)
    is_chosen = (lane == i1) | (lane == i2)
    chosen = jnp.where(is_chosen, 1.0, 0.0)
    tm = chosen.shape[0]
    earlier = lax.broadcasted_iota(jnp.int32, (tm, tm), 1) < lax.broadcasted_iota(jnp.int32, (tm, tm), 0)
    running = carry_ref[...]
    before = jnp.dot(jnp.where(earlier, 1.0, 0.0).astype(BF16), chosen.astype(BF16),
                     preferred_element_type=F32) + running.astype(F32)
    rank = jnp.where(is_chosen, before, -1.0)
    rank_ref[...] = rank
    rankt_ref[...] = rank.T[0:N_EXPERTS, :]
    chosen_i = jnp.where(is_chosen, 1, 0)
    for k in range(tm // TOK_TILE):
        running = running + jnp.sum(chosen_i[k * TOK_TILE:(k + 1) * TOK_TILE, :], axis=0, keepdims=True)
        cnt_ref[0, k:k + 1, :] = running
    carry_ref[...] = running


def _router(x2d, mod, y_ssd, y_local, w_a, w_b, g, router_pad, *, tm, rows_per_mod, mod_base, casts=()):
    n = x2d.shape[0]
    row = lambda i: (i, 0)
    const = lambda i: (0, 0)
    per = tm // TOK_TILE
    cast_in, cast_out, cast_shape = _cast_specs(casts, n // tm)
    return pl.pallas_call(
        functools.partial(_router_kernel, n_casts=len(casts)),
        grid=(n // tm,),
        in_specs=_token_mixed_specs(tm, rows_per_mod, mod_base) + [
            pl.BlockSpec((1, D_MODEL), const),
            pl.BlockSpec((D_MODEL, LANES), const),
        ] + cast_in,
        out_specs=[
            pl.BlockSpec((tm, D_MODEL), row),
            pl.BlockSpec((tm, D_MODEL), row),
            pl.BlockSpec((tm, LANES), row),
            pl.BlockSpec((tm, LANES), row),
            pl.BlockSpec((N_EXPERTS, tm), lambda i: (0, i)),
            pl.BlockSpec((1, per, LANES), lambda i: (i, 0, 0)),
        ] + cast_out,
        out_shape=[
            jax.ShapeDtypeStruct((n, D_MODEL), F32),
            jax.ShapeDtypeStruct((n, D_MODEL), BF16),
            jax.ShapeDtypeStruct((n, LANES), F32),
            jax.ShapeDtypeStruct((n, LANES), F32),
            jax.ShapeDtypeStruct((N_EXPERTS, n), F32),
            jax.ShapeDtypeStruct((n // tm, per, LANES), jnp.int32),
        ] + cast_shape,
        scratch_shapes=[pltpu.VMEM((1, LANES), jnp.int32)],
        compiler_params=_params("arbitrary"),
        name="moe_router",
    )(x2d, mod, y_ssd, y_local, w_a, w_b, g, router_pad, *casts)


def _routing_tables(cnt, n_tokens):
    i32 = jnp.int32
    n_t = n_tokens // TOK_TILE
    cnt_end = cnt.reshape(n_t, LANES)[:, :N_EXPERTS]
    cnt_start = jnp.concatenate([jnp.zeros((1, N_EXPERTS), i32), cnt_end[:-1]], axis=0)
    tot = cnt_end[-1]
    nblk = (tot + ROW_BLOCK - 1) // ROW_BLOCK
    blk_cum = jnp.cumsum(nblk)
    blk_first = blk_cum - nblk
    n_valid = blk_cum[-1]
    nb = n_tokens * TOP_K // ROW_BLOCK + N_EXPERTS
    bi = jnp.arange(nb, dtype=i32)
    valid = bi < n_valid
    e_of = jnp.sum((bi[:, None] >= blk_cum[None, :]).astype(i32), axis=1)
    e_last = jnp.sum(((n_valid - 1) >= blk_cum).astype(i32))
    blk_e = jnp.minimum(jnp.where(valid, e_of, e_last), N_EXPERTS - 1)
    base = jnp.where(valid, (bi - blk_first[blk_e]) * ROW_BLOCK, 0)
    lo = base[:, None] + jnp.arange(N_SUB, dtype=i32)[None, :] * SUB_BLOCK
    hi = jnp.where(valid[:, None], jnp.minimum(lo + SUB_BLOCK, tot[blk_e][:, None]), lo)
    cs = cnt_start[:, blk_e]
    ce = cnt_end[:, blk_e]
    rel = (cs[:, :, None] < hi[None]) & (ce[:, :, None] > lo[None])
    tile_id = jnp.arange(n_t, dtype=i32)[:, None, None]
    t_first = jnp.min(jnp.where(rel, tile_id, n_t), axis=0)
    sub_hi = jnp.max(jnp.where(rel, tile_id, -1), axis=0)
    sub_lo = jnp.where(sub_hi >= 0, t_first, 0)
    u_lo = jnp.min(t_first, axis=1)
    u_n = jnp.maximum(jnp.max(sub_hi, axis=1) - u_lo + 1, 0)
    u_lo = jnp.where(u_n > 0, u_lo, 0)
    per = COMBINE_TM // TOK_TILE
    seg_rows = blk_first * ROW_BLOCK
    lo_c = seg_rows[None, :] + cnt_start[0::per]
    cnt_c = cnt_end[per - 1::per] - cnt_start[0::per]
    first_c = (lo_c // 16) * 16
    nwin_c = jnp.where(cnt_c > 0, (lo_c + cnt_c - first_c + COMBINE_WIN - 1) // COMBINE_WIN, 0)
    return dict(blk_e=blk_e.astype(i32), valid=valid.astype(i32), base=base.astype(i32),
                u_lo=u_lo.astype(i32), u_n=u_n.astype(i32),
                sub_lo=sub_lo.reshape(-1).astype(i32), sub_hi=sub_hi.reshape(-1).astype(i32),
                seg_rows=seg_rows.astype(i32), first_c=first_c.reshape(-1).astype(i32),
                nwin_c=nwin_c.reshape(-1).astype(i32), n_blocks=nb)


def _experts_kernel(be_ref, bv_ref, base_ref, ulo_ref, un_ref, slo_ref, shi_ref,
                    h_hbm, rt_hbm, w1_ref, w3_ref, w2_ref, y_ref, hbuf, rbuf, hsem, rsem):
    i = pl.program_id(0)
    slot = i % 2

    def tile_copies(s, k, t):
        tok = pl.multiple_of(t * TOK_TILE, TOK_TILE)
        return (pltpu.make_async_copy(h_hbm.at[pl.ds(tok, TOK_TILE)], hbuf.at[s, k], hsem.at[s, k]),
                pltpu.make_async_copy(rt_hbm.at[:, pl.ds(tok, TOK_TILE)], rbuf.at[s, k], rsem.at[s, k]))

    def for_block_tiles(b, s, action):
        for k in range(GATHER_TILES):
            @pl.when(k < un_ref[b])
            def _():
                for cp in tile_copies(s, k, ulo_ref[b] + k):
                    action(cp)

    @pl.when(i == 0)
    def _():
        def clear(k, carry):
            hbuf[k // (GATHER_TILES + 1), k % (GATHER_TILES + 1)] = jnp.zeros((TOK_TILE, D_MODEL), BF16)
            rbuf[k // (GATHER_TILES + 1), k % (GATHER_TILES + 1)] = jnp.zeros((N_EXPERTS, TOK_TILE), F32)
            return carry

        lax.fori_loop(0, 2 * (GATHER_TILES + 1), clear, 0)
        for_block_tiles(0, 0, lambda cp: cp.start())

    for_block_tiles(i, slot, lambda cp: cp.wait())

    @pl.when(i + 1 < pl.num_programs(0))
    def _():
        for_block_tiles(i + 1, 1 - slot, lambda cp: cp.start())

    @pl.when(bv_ref[i] == 1)
    def _():
        e = be_ref[i]
        row_id = lax.broadcasted_iota(jnp.int32, (SUB_BLOCK, TOK_TILE), 0).astype(F32)
        expert_row = lax.broadcasted_iota(jnp.int32, (N_EXPERTS, TOK_TILE), 0) == e
        u_lo = ulo_ref[i]
        selected = []

        def tile_weights(ranks8, first_rank):
            ranks = jnp.sum(jnp.where(expert_row, ranks8, 0.0), axis=0, keepdims=True)
            return jnp.where(ranks == first_rank + row_id, 1.0, 0.0).astype(BF16)

        for sb in range(N_SUB):
            first_rank = (base_ref[i] + sb * SUB_BLOCK).astype(F32)
            t_lo, t_hi = slo_ref[N_SUB * i + sb], shi_ref[N_SUB * i + sb]
            k0 = jnp.clip(t_lo - u_lo, 0, GATHER_TILES + 1 - WINDOW_TILES)
            picks = []
            for j in range(WINDOW_TILES):
                t = u_lo + k0 + j
                live = (t >= t_lo) & (t <= t_hi) & (k0 + j < GATHER_TILES)
                picks.append(tile_weights(jnp.where(live, rbuf[slot, k0 + j], -1.0), first_rank))
            window = hbuf[slot, pl.ds(k0, WINDOW_TILES)].reshape(WINDOW_TILES * TOK_TILE, D_MODEL)
            rows = jnp.dot(jnp.concatenate(picks, axis=1), window, preferred_element_type=F32)

            def late_tile(t, carry):
                k = t - u_lo
                kk = jnp.minimum(k, GATHER_TILES)

                @pl.when(k >= GATHER_TILES)
                def _():
                    tok = pl.multiple_of(t * TOK_TILE, TOK_TILE)
                    pltpu.sync_copy(h_hbm.at[pl.ds(tok, TOK_TILE)], hbuf.at[slot, GATHER_TILES])
                    pltpu.sync_copy(rt_hbm.at[:, pl.ds(tok, TOK_TILE)], rbuf.at[slot, GATHER_TILES])

                return carry + jnp.dot(tile_weights(rbuf[slot, kk], first_rank), hbuf[slot, kk],
                                       preferred_element_type=F32)

            t_next = u_lo + jnp.minimum(k0 + WINDOW_TILES, GATHER_TILES)
            rows = lax.fori_loop(jnp.maximum(t_next, t_lo), t_hi + 1, late_tile, rows)
            selected.append(rows.astype(BF16))

        hg = jnp.concatenate(selected, axis=0)
        half = w1_ref.shape[2] // 2
        y = None
        for c in range(2):
            a = jnp.dot(hg, w1_ref[0, :, c * half:(c + 1) * half], preferred_element_type=F32)
            b = jnp.dot(hg, w3_ref[0, :, c * half:(c + 1) * half], preferred_element_type=F32)
            part = jnp.dot((_silu(a) * b).astype(BF16), w2_ref[0, c * half:(c + 1) * half, :],
                           preferred_element_type=F32)
            y = part if y is None else y + part
        y_ref[...] = y.astype(BF16)

    @pl.when(bv_ref[i] == 0)
    def _():
        y_ref[...] = jnp.zeros_like(y_ref)


def _moe_experts(h2, rank_t, w1, w3, w2, tb):
    nb = tb["n_blocks"]
    ff = w1.shape[2]
    w_map = lambda i, be, *_: (be[i], 0, 0)
    grid_spec = pltpu.PrefetchScalarGridSpec(
        num_scalar_prefetch=7,
        grid=(nb,),
        in_specs=[
            pl.BlockSpec(memory_space=pl.ANY),
            pl.BlockSpec(memory_space=pl.ANY),
            pl.BlockSpec((1, D_MODEL, ff), w_map, pipeline_mode=pl.Buffered(1)),
            pl.BlockSpec((1, D_MODEL, ff), w_map, pipeline_mode=pl.Buffered(1)),
            pl.BlockSpec((1, ff, D_MODEL), w_map, pipeline_mode=pl.Buffered(1)),
        ],
        out_specs=pl.BlockSpec((ROW_BLOCK, D_MODEL), lambda i, *_: (i, 0)),
        scratch_shapes=[
            pltpu.VMEM((2, GATHER_TILES + 1, TOK_TILE, D_MODEL), BF16),
            pltpu.VMEM((2, GATHER_TILES + 1, N_EXPERTS, TOK_TILE), F32),
            pltpu.SemaphoreType.DMA((2, GATHER_TILES)),
            pltpu.SemaphoreType.DMA((2, GATHER_TILES)),
        ],
    )
    return pl.pallas_call(
        _experts_kernel,
        grid_spec=grid_spec,
        out_shape=jax.ShapeDtypeStruct((nb * ROW_BLOCK, D_MODEL), BF16),
        compiler_params=_params("arbitrary"),
        name="moe_experts",
    )(tb["blk_e"], tb["valid"], tb["base"], tb["u_lo"], tb["u_n"], tb["sub_lo"], tb["sub_hi"],
      h2, rank_t, w1, w3, w2)


def _combine_kernel(*refs, final_norm, n_rows):
    first_ref, nwin_ref, seg_ref, x_ref, mod_ref, comb_ref, rank_ref, y_hbm = refs[0:8]
    fg_ref = refs[8] if final_norm else None
    o_ref, ybuf, ysem, acc_ref = refs[-4:]
    i = pl.program_id(0)
    slot = i % 2
    extra = N_EXPERTS

    def window(b, e):
        return first_ref[b * N_EXPERTS + e], nwin_ref[b * N_EXPERTS + e]

    def win_start(first, j):
        return pl.multiple_of(jnp.minimum(first + j * COMBINE_WIN, n_rows - COMBINE_WIN), 16)

    def for_tile_windows(b, s, action):
        for e in range(N_EXPERTS):
            first, n_win = window(b, e)

            @pl.when(n_win > 0)
            def _():
                action(pltpu.make_async_copy(y_hbm.at[pl.ds(win_start(first, 0), COMBINE_WIN)],
                                             ybuf.at[s, e], ysem.at[s, e]))

    @pl.when(i == 0)
    def _():
        def clear(k, carry):
            ybuf[k // (N_EXPERTS + 1), k % (N_EXPERTS + 1)] = jnp.zeros((COMBINE_WIN, D_MODEL), BF16)
            return carry

        lax.fori_loop(0, 2 * (N_EXPERTS + 1), clear, 0)
        for_tile_windows(0, 0, lambda cp: cp.start())

    for_tile_windows(i, slot, lambda cp: cp.wait())

    @pl.when(i + 1 < pl.num_programs(0))
    def _():
        for_tile_windows(i + 1, 1 - slot, lambda cp: cp.start())

    col_id = lax.broadcasted_iota(jnp.int32, (COMBINE_TM, COMBINE_WIN), 1).astype(F32)

    def weights(e, first, j):
        pos_col = rank_ref[:, e:e + 1] + seg_ref[e].astype(F32)
        start = win_start(first, j).astype(F32)
        return jnp.where(pos_col == start + col_id, comb_ref[:, e:e + 1], 0.0).astype(BF16)

    firsts = [window(i, e) for e in range(N_EXPERTS)]
    pick_all = jnp.concatenate([weights(e, firsts[e][0], 0) for e in range(N_EXPERTS)], axis=1)
    y_all = ybuf[slot, 0:N_EXPERTS].reshape(N_EXPERTS * COMBINE_WIN, D_MODEL)
    acc_ref[...] = jnp.dot(pick_all, y_all, preferred_element_type=F32)

    for e in range(N_EXPERTS):
        first, n_win = firsts[e]

        def more(j, carry):
            pltpu.sync_copy(y_hbm.at[pl.ds(win_start(first, j), COMBINE_WIN)], ybuf.at[slot, extra])
            acc_ref[...] += jnp.dot(weights(e, first, j), ybuf[slot, extra], preferred_element_type=F32)
            return carry

        lax.fori_loop(1, jnp.maximum(n_win, 1), more, 0)

    o_ref[...] = _residual_out(x_ref[...], mod_ref[0, 5:6, :], acc_ref[...], fg_ref)


def _moe_combine(x2d, mod, comb, rank, y_sorted, tb, *, rows_per_mod, mod_base, final_g=None):
    n = x2d.shape[0]
    tm = COMBINE_TM
    row = lambda i, *_: (i, 0)
    const = lambda i, *_: (0, 0)
    mod_idx = _mod_index(tm, rows_per_mod, mod_base)
    in_specs = [
        pl.BlockSpec((tm, D_MODEL), row),
        pl.BlockSpec((1, 6, D_MODEL), lambda i, *_: mod_idx(i)),
        pl.BlockSpec((tm, LANES), row),
        pl.BlockSpec((tm, LANES), row),
        pl.BlockSpec(memory_space=pl.ANY),
    ]
    args = [x2d, mod, comb, rank, y_sorted]
    if final_g is not None:
        in_specs.append(pl.BlockSpec((1, D_MODEL), const))
        args.append(final_g)
    grid_spec = pltpu.PrefetchScalarGridSpec(
        num_scalar_prefetch=3,
        grid=(n // tm,),
        in_specs=in_specs,
        out_specs=pl.BlockSpec((tm, D_MODEL), row),
        scratch_shapes=[
            pltpu.VMEM((2, N_EXPERTS + 1, COMBINE_WIN, D_MODEL), BF16),
            pltpu.SemaphoreType.DMA((2, N_EXPERTS)),
            pltpu.VMEM((tm, D_MODEL), F32),
        ],
    )
    return pl.pallas_call(
        functools.partial(_combine_kernel, final_norm=final_g is not None, n_rows=y_sorted.shape[0]),
        grid_spec=grid_spec,
        out_shape=jax.ShapeDtypeStruct((n, D_MODEL), F32),
        compiler_params=_params("arbitrary"),
        name="moe_combine",
    )(tb["first_c"], tb["nwin_c"], tb["seg_rows"], *args)


def _to_bf16_in(w, steps):
    if w.dtype == BF16:
        return None, w
    flat = w.reshape(-1, w.shape[-1])
    if _can_cast_in_steps(flat, steps):
        return flat, None
    return None, w.astype(BF16)


def _moe_channel_mixer(x2d, mod, y_ssd, y_local, w_a, w_b, g, router_pad, w1, w3, w2, *, rows_per_mod, mod_base,
                       final_g=None):
    w1 = w1.astype(BF16)
    w3 = w3.astype(BF16)
    tm = math.gcd(ROUTER_TM, rows_per_mod)
    w2_flat, w2_bf16 = _to_bf16_in(w2, x2d.shape[0] // tm)
    x1, h2, comb, rank, rank_t, cnt, *cast = _router(
        x2d, mod, y_ssd, y_local, w_a, w_b, g, router_pad, tm=tm, rows_per_mod=rows_per_mod,
        mod_base=mod_base, casts=() if w2_flat is None else (w2_flat,))
    if w2_bf16 is None:
        w2_bf16 = cast[0].reshape(w2.shape)
    w2 = w2_bf16
    tb = _routing_tables(cnt, x2d.shape[0])
    y_sorted = _moe_experts(h2, rank_t, w1, w3, w2, tb)
    return _moe_combine(x1, mod, comb, rank, y_sorted, tb, rows_per_mod=rows_per_mod, mod_base=mod_base,
                        final_g=final_g)


def _layer_params(l, w_in, w_out, ssd_conv_w, ssd_conv_b, ssd_dt_bias, ssd_a_log, ssd_d, ssd_norm_g,
                  gm_norm_g, gm_norm_b, gm_ws, gm_bs, sc_conv_w, cf_conv_w, cf_conv_b, cf_norm_g, cf_norm_b):
    wi = w_in[l]
    w_main = jnp.concatenate(
        [wi[:, _C_XBC:_C_DT], wi[:, _C_Z:_C_UV], wi[:, _C_UV:_C_SC], wi[:, _C_CF:_C_END], wi[:, _C_SC:_C_CF]],
        axis=1).astype(BF16)
    w_dt = jnp.pad(wi[:, _C_DT:_C_Z], ((0, 0), (0, LANES - 2 * SSD_HEADS))).astype(BF16)
    return {
        "w_main": w_main,
        "w_dt": w_dt,
        "w_out_a": w_out[l, 0:BRANCH].astype(BF16),
        "w_out_b": w_out[l, BRANCH:].astype(BF16),
        "ssd_conv_w": ssd_conv_w[l],
        "ssd_conv_b": ssd_conv_b[l].reshape(1, -1),
        "ssd_dtp": jnp.stack([jnp.tile(ssd_dt_bias[l].reshape(-1), LANES // (2 * SSD_HEADS)),
                              jnp.tile(ssd_a_log[l].reshape(-1), LANES // (2 * SSD_HEADS))]),
        "ssd_vec": jnp.stack([jnp.repeat(ssd_d[l], HEAD_DIM), ssd_norm_g[l]]),
        "gm_ws": gm_ws[l],
        "gm_bs": jnp.repeat(gm_bs[l].T, HEAD_DIM, axis=1),
        "sc_conv_w": sc_conv_w[l],
        "cf_conv_w": cf_conv_w[l],
        "local_vec": jnp.stack([gm_norm_g[l], gm_norm_b[l], cf_conv_b[l], cf_norm_g[l], cf_norm_b[l]]),
    }


def kernel(x, c, ctx, c_ctx, mod_w, mod_b, norm1_g, norm2_g, w_in, w_out, ssd_conv_w, ssd_conv_b, ssd_dt_bias, ssd_a_log, ssd_d, ssd_norm_g, gm_norm_g, gm_norm_b, gm_ws, gm_bs, sc_conv_w, cf_conv_w, cf_conv_b, cf_norm_g, cf_norm_b, ffn_w1, ffn_w3, ffn_w2, moe_router, moe_w1, moe_w3, moe_w2, final_norm_g):
    bsz, seq, _ = x.shape
    ctx_len = ctx.shape[1]
    depth = w_in.shape[0]
    n_x, n_c = bsz * seq, bsz * ctx_len
    tm = 512

    mod = _modulation(jnp.concatenate([c, c_ctx[None, :]], axis=0), mod_w, mod_b)
    x_map = dict(tm=tm, rows_per_mod=seq, mod_base=0)
    c_map = dict(tm=tm, rows_per_mod=n_c, mod_base=bsz)
    zero_state = jnp.zeros((bsz, 2, SSD_GROUPS, SSD_STATE, 2 * HEAD_DIM), F32)

    xs = x.reshape(n_x, D_MODEL)
    xc = ctx.reshape(n_c, D_MODEL)
    moe_bf16 = {}
    for l in range(depth):
        last = l == depth - 1
        lp = _layer_params(l, w_in, w_out, ssd_conv_w, ssd_conv_b, ssd_dt_bias, ssd_a_log, ssd_d, ssd_norm_g,
                           gm_norm_g, gm_norm_b, gm_ws, gm_bs, sc_conv_w, cf_conv_w, cf_conv_b, cf_norm_g,
                           cf_norm_b)
        g1 = norm1_g[l].reshape(1, D_MODEL)
        g2 = norm2_g[l].reshape(1, D_MODEL)
        ffn_f32 = [ffn_w1[l // 2], ffn_w3[l // 2], ffn_w2[l // 2]] if l % 2 == 0 else []
        ctx_tm = min(tm, ctx_len)
        if not all(_can_cast_in_steps(w, n_c // ctx_tm) for w in ffn_f32):
            ffn_f32 = []
        zc_ssd, yc_loc, dtc, ffn_bf16 = _token_mixer_front(xc, mod[l], g1, lp, tm=ctx_tm, seq=ctx_len,
                                                           rows_per_mod=n_c, mod_base=bsz, local=not last,
                                                           casts=ffn_f32)
        yc_ssd, states = _ssd_mixer(zc_ssd.reshape(bsz, ctx_len, W_SSD), dtc.reshape(bsz, ctx_len, LANES),
                                    zero_state, lp)
        z_ssd, y_loc, dtx, _ = _token_mixer_front(xs, mod[l], g1, lp, tm=min(FRONT_TM, seq), seq=seq,
                                                  rows_per_mod=seq, mod_base=0)
        y_ssd, _ = _ssd_mixer(z_ssd.reshape(bsz, seq, W_SSD), dtx.reshape(bsz, seq, LANES), states, lp)
        streams = [(xs, y_ssd.reshape(n_x, BRANCH), y_loc, x_map)]
        if not last:
            streams.append((xc, yc_ssd.reshape(n_c, BRANCH), yc_loc, c_map))
        fg = final_norm_g.reshape(1, D_MODEL) if last else None
        i = l // 2
        if l % 2 == 0:
            w1, w3, w2 = ffn_bf16 or (ffn_w1[i].astype(BF16), ffn_w3[i].astype(BF16), ffn_w2[i].astype(BF16))
            casts = []
            if l + 1 < depth:
                casts = [w.reshape(-1, w.shape[-1]) for w in (moe_w1[i], moe_w3[i])]
                if not all(_can_cast_in_steps(w, n_x // tm) for w in casts):
                    casts = []
            outs = []
            for t, ya, yb, m in streams:
                out, cast = _dense_channel_mixer(t, mod[l], ya, yb, lp["w_out_a"], lp["w_out_b"], g2, w1, w3, w2,
                                                 final_g=fg if t is xs else None, casts=casts if t is xs else (),
                                                 **m)
                outs.append(out)
                if cast:
                    moe_bf16[i] = [c.reshape(moe_w1[i].shape) for c in cast]
        else:
            w1, w3 = moe_bf16.get(i) or (moe_w1[i], moe_w3[i])
            w2 = moe_w2[i]
            r_hi = moe_router[i].astype(BF16)
            r_lo = (moe_router[i] - r_hi.astype(F32)).astype(BF16)
            r_pad = jnp.pad(jnp.concatenate([r_hi, r_lo], axis=1), ((0, 0), (0, LANES - 2 * N_EXPERTS)))
            outs = [_moe_channel_mixer(t, mod[l], ya, yb, lp["w_out_a"], lp["w_out_b"], g2, r_pad, w1, w3, w2,
                                       rows_per_mod=m["rows_per_mod"], mod_base=m["mod_base"],
                                       final_g=fg if t is xs else None) for t, ya, yb, m in streams]
        xs = outs[0]
        if not last:
            xc = outs[1]
    return xs.reshape(bsz, seq, D_MODEL)
```

```python
import functools
import math

import jax
import jax.numpy as jnp
from jax import lax
from jax.experimental import pallas as pl
from jax.experimental.pallas import tpu as pltpu

F32 = jnp.float32
BF16 = jnp.bfloat16

LANES = 128
SUBLANES = 8
VMEM_LIMIT_BYTES = 56 * 1024 * 1024

D_MODEL = 1024
BRANCH = 256
HEAD_DIM = 64
SSD_HEADS = 4
SSD_GROUPS = 2
SSD_STATE = 128
CHUNK = 128
SSD_ROWS_PER_STEP = 4096
SSD_MAX_SEQS_PER_STEP = 4
SSD_CONV_CH = BRANCH + 2 * SSD_GROUPS * SSD_STATE
N_EXPERTS = 8
TOP_K = 2
TOK_TILE = 256
ROW_BLOCK = 512
SUB_BLOCK = 256
N_SUB = ROW_BLOCK // SUB_BLOCK
GATHER_TILES = 11
WINDOW_TILES = 5
ROUTER_TM = 1024
COMBINE_TM = 512
COMBINE_WIN = 256
FRONT_TM = 1024
HALO = 16
RMS_EPS = 1e-6
LN_EPS = 1e-5

_C_XBC = 0
_C_DT = _C_XBC + SSD_CONV_CH
_C_Z = _C_DT + 2 * SSD_HEADS
_C_UV = _C_Z + BRANCH
_C_SC = _C_UV + 2 * BRANCH
_C_CF = _C_SC + 3 * BRANCH
_C_END = _C_CF + 2 * BRANCH
W_SSD = SSD_CONV_CH + BRANCH
W_UVCF = 4 * BRANCH
W_SC = 3 * BRANCH
W_MAIN = W_SSD + W_UVCF + W_SC


def _params(*sem):
    return pltpu.CompilerParams(dimension_semantics=sem, vmem_limit_bytes=VMEM_LIMIT_BYTES)


def _sigmoid(v):
    return 0.5 + 0.5 * jnp.tanh(0.5 * v)


def _silu(v):
    half = 0.5 * v
    return half + half * jnp.tanh(half)


def _softplus(v):
    return jnp.maximum(v, 0.0) + jnp.log1p(jnp.exp(-jnp.abs(v)))


def _gelu_tanh(v):
    return 0.5 * v * (1.0 + jnp.tanh(0.7978845608028654 * (v + 0.044715 * (v * v * v))))


def _norm_modulate(x, g, shift, scale):
    ms = jnp.mean(x * x, axis=-1, keepdims=True)
    return (x * lax.rsqrt(ms + RMS_EPS) * g) * (1.0 + scale) + shift


def _mod_kernel(c_ref, w_ref, b_ref, o_ref):
    s = _silu(c_ref[...])
    s_hi = s.astype(BF16)
    s_lo = (s - s_hi.astype(F32)).astype(BF16)
    w = w_ref[0]
    w_hi = w.astype(BF16)
    w_lo = (w - w_hi.astype(F32)).astype(BF16)
    o_ref[0, 0] = (jnp.dot(s_hi, w_hi, preferred_element_type=F32)
                   + jnp.dot(s_lo, w_hi, preferred_element_type=F32)
                   + jnp.dot(s_hi, w_lo, preferred_element_type=F32) + b_ref[0, 0])


def _modulation(c_all, mod_w, mod_b):
    depth = mod_w.shape[0]
    rows = c_all.shape[0]
    out = pl.pallas_call(
        _mod_kernel,
        grid=(depth, 6),
        in_specs=[
            pl.BlockSpec((rows, D_MODEL), lambda l, j: (0, 0)),
            pl.BlockSpec((1, D_MODEL, D_MODEL), lambda l, j: (l, 0, j)),
            pl.BlockSpec((1, 1, 1, D_MODEL), lambda l, j: (l, j, 0, 0)),
        ],
        out_specs=pl.BlockSpec((1, 1, rows, D_MODEL), lambda l, j: (l, j, 0, 0)),
        out_shape=jax.ShapeDtypeStruct((depth, 6, rows, D_MODEL), F32),
        compiler_params=_params("arbitrary", "arbitrary"),
        name="modulation",
    )(c_all, mod_w, mod_b.reshape(depth, 6, 1, D_MODEL))
    return jnp.transpose(out, (0, 2, 1, 3))


def _mod_index(tm, rows_per_mod, mod_base):
    return lambda i: (mod_base + (i * tm) // rows_per_mod, 0, 0)


def _fill_stage(stage_ref, prev, main, nxt):
    rows = main.shape[0]
    stage_ref[0:HALO, :] = prev
    stage_ref[HALO:HALO + rows, :] = main
    stage_ref[HALO + rows:2 * HALO + rows, :] = nxt


def _dwconv_rows(stage_ref, w_ref, taps, r, rows, c0, width):
    acc = None
    for k in range(taps):
        lo = HALO + r + k - taps // 2
        term = stage_ref[lo:lo + rows, c0:c0 + width] * w_ref[k:k + 1, c0:c0 + width]
        acc = term if acc is None else acc + term
    return acc


def _front_kernel(*refs, tm, seq, local, n_casts):
    x_ref, xp_ref, xn_ref, mod_ref, g_ref, w_ref, wdt_ref, convw_ref, convb_ref = refs[0:9]
    pos = 9
    if local:
        ws_ref, bs_ref, scw_ref, cfw_ref, vec_ref = refs[pos:pos + 5]
        pos += 5
    cast_in = refs[pos:pos + n_casts]
    pos += n_casts
    zssd_ref = refs[pos]
    if local:
        yloc_ref, dt_ref = refs[pos + 1:pos + 3]
        pos += 3
    else:
        dt_ref = refs[pos + 1]
        pos += 2
    cast_out = refs[pos:pos + n_casts]
    pos += n_casts
    if local:
        ssd_stage, sc_stage, cf_stage, rot_ref = refs[pos:]
    else:
        (ssd_stage,) = refs[pos:]
    _cast_slices(cast_in, cast_out)
    i = pl.program_id(0)
    tiles_per_seq = seq // tm
    keep_prev = jnp.where(i % tiles_per_seq == 0, 0.0, 1.0)
    keep_next = jnp.where(i % tiles_per_seq == tiles_per_seq - 1, 0.0, 1.0)
    g, shift, scale = g_ref[...], mod_ref[0, 0:1, :], mod_ref[0, 1:2, :]
    h = _norm_modulate(x_ref[...], g, shift, scale).astype(BF16)
    h_halo = _norm_modulate(jnp.concatenate([xp_ref[...], xn_ref[...]], axis=0), g, shift, scale).astype(BF16)
    z = jnp.dot(h, w_ref[...], preferred_element_type=F32)
    z_halo = jnp.dot(h_halo, w_ref[...], preferred_element_type=F32)
    zp = z_halo[0:HALO] * keep_prev
    zn = z_halo[HALO:2 * HALO] * keep_next
    dt_ref[...] = jnp.dot(h, wdt_ref[...], preferred_element_type=F32)[:, 0:2 * SSD_HEADS]

    _fill_stage(ssd_stage, zp[:, 0:SSD_CONV_CH], z[:, 0:SSD_CONV_CH], zn[:, 0:SSD_CONV_CH])
    for r in range(0, tm, CHUNK):
        for c0 in range(0, SSD_CONV_CH, BRANCH):
            v = _dwconv_rows(ssd_stage, convw_ref, 5, r, CHUNK, c0, BRANCH) + convb_ref[0:1, c0:c0 + BRANCH]
            zssd_ref[r:r + CHUNK, c0:c0 + BRANCH] = _silu(v).astype(BF16)
    zssd_ref[:, SSD_CONV_CH:W_SSD] = z[:, SSD_CONV_CH:W_SSD].astype(BF16)
    if local:
        _local_mixers_tile(z[:, W_SSD:W_MAIN], zp[:, W_SSD:W_MAIN], zn[:, W_SSD:W_MAIN], ws_ref, bs_ref, scw_ref,
                           cfw_ref, vec_ref, sc_stage, cf_stage, rot_ref, yloc_ref, tm)


def _local_mixers_tile(z, zp, zn, ws_ref, bs_ref, scw_ref, cfw_ref, vec_ref, sc_stage, cf_stage, rot_ref,
                       yloc_ref, tm):
    lane = lax.broadcasted_iota(jnp.int32, (CHUNK, BRANCH), 1)
    gm_g, gm_b = vec_ref[0:1, :], vec_ref[1:2, :]
    cf_b, cf_g, cf_nb = vec_ref[2:3, :], vec_ref[3:4, :], vec_ref[4:5, :]
    c_uv, c_cf, c_sc = 0, 2 * BRANCH, W_UVCF

    def layer_norm(v, gain, bias):
        mu = jnp.mean(v, axis=-1, keepdims=True)
        var = jnp.mean(jnp.square(v - mu), axis=-1, keepdims=True)
        return (v - mu) * lax.rsqrt(var + LN_EPS) * gain + bias

    def sc_in(t):
        return t[:, c_sc + BRANCH:c_sc + 2 * BRANCH] * t[:, c_sc + 2 * BRANCH:c_sc + 3 * BRANCH]

    def cf_in(t):
        return t[:, c_cf:c_cf + BRANCH] * _sigmoid(t[:, c_cf + BRANCH:c_cf + 2 * BRANCH])

    _fill_stage(sc_stage, sc_in(zp), sc_in(z), sc_in(zn))
    _fill_stage(cf_stage, cf_in(zp), cf_in(z), cf_in(zn))
    span = tm + 2 * HALO - SUBLANES
    for phase in range(SUBLANES):
        rot_ref[phase] = cf_stage[phase:phase + span, :]

    for r in range(0, tm, CHUNK):
        ge = _gelu_tanh(z[r:r + CHUNK, c_uv:c_uv + 2 * BRANCH])
        u = ge[:, 0:BRANCH]
        v = layer_norm(ge[:, BRANCH:2 * BRANCH], gm_g, gm_b)
        s = bs_ref[...]
        for hd in range(BRANCH // HEAD_DIM):
            in_head = (lane >= hd * HEAD_DIM) & (lane < (hd + 1) * HEAD_DIM)
            s = s + jnp.dot(ws_ref[hd].astype(BF16), jnp.where(in_head, v, 0.0).astype(BF16),
                            preferred_element_type=F32)
        yloc_ref[r:r + CHUNK, 0:BRANCH] = (u * s).astype(BF16)
        gate = z[r:r + CHUNK, c_sc:c_sc + BRANCH]
        yloc_ref[r:r + CHUNK, BRANCH:2 * BRANCH] = (gate * _dwconv_rows(sc_stage, scw_ref, 3, r, CHUNK, 0, BRANCH)
                                                    ).astype(BF16)
        cv = cf_b
        for k in range(31):
            lo = HALO + r + k - 31 // 2
            q8 = (lo // SUBLANES) * SUBLANES
            cv = cv + rot_ref[lo % SUBLANES, q8:q8 + CHUNK, :] * cfw_ref[k:k + 1, :]
        yloc_ref[r:r + CHUNK, 2 * BRANCH:3 * BRANCH] = _silu(layer_norm(cv, cf_g, cf_nb)).astype(BF16)


def _local_mixer_param_specs():
    const = lambda i, *_: (0, 0)
    return [
        pl.BlockSpec((BRANCH // HEAD_DIM, CHUNK, CHUNK), lambda i, *_: (0, 0, 0)),
        pl.BlockSpec((CHUNK, BRANCH), const),
        pl.BlockSpec((3, BRANCH), const),
        pl.BlockSpec((31, BRANCH), const),
        pl.BlockSpec((5, BRANCH), const),
    ]


def _local_mixer_params(lp):
    return [lp["gm_ws"], lp["gm_bs"], lp["sc_conv_w"], lp["cf_conv_w"], lp["local_vec"]]


def _local_mixer_scratch(tm):
    span = tm + 2 * HALO
    return [pltpu.VMEM((span, BRANCH), F32), pltpu.VMEM((span, BRANCH), F32),
            pltpu.VMEM((SUBLANES, span - SUBLANES, BRANCH), F32)]


def _token_mixer_front(x2d, mod, g, lp, *, tm, seq, rows_per_mod, mod_base, local=True, casts=()):
    n = x2d.shape[0]
    steps = n // tm
    per = tm // HALO
    row = lambda i: (i, 0)
    const = lambda i: (0, 0)
    in_specs = [
        pl.BlockSpec((tm, D_MODEL), row),
        pl.BlockSpec((HALO, D_MODEL), lambda i: (jnp.maximum(i * per - 1, 0), 0)),
        pl.BlockSpec((HALO, D_MODEL), lambda i: (jnp.minimum((i + 1) * per, n // HALO - 1), 0)),
        pl.BlockSpec((1, 6, D_MODEL), _mod_index(tm, rows_per_mod, mod_base)),
        pl.BlockSpec((1, D_MODEL), const),
        pl.BlockSpec((D_MODEL, W_MAIN if local else W_SSD), const),
        pl.BlockSpec((D_MODEL, LANES), const),
        pl.BlockSpec((5, SSD_CONV_CH), const),
        pl.BlockSpec((1, SSD_CONV_CH), const),
    ]
    args = [x2d, x2d, x2d, mod, g, lp["w_main"], lp["w_dt"], lp["ssd_conv_w"], lp["ssd_conv_b"]]
    out_specs = [pl.BlockSpec((tm, W_SSD), row)]
    out_shape = [jax.ShapeDtypeStruct((n, W_SSD), BF16)]
    scratch = [pltpu.VMEM((tm + 2 * HALO, SSD_CONV_CH), F32)]
    if local:
        in_specs += _local_mixer_param_specs()
        args += _local_mixer_params(lp)
        out_specs.append(pl.BlockSpec((tm, 3 * BRANCH), row))
        out_shape.append(jax.ShapeDtypeStruct((n, 3 * BRANCH), BF16))
        scratch += _local_mixer_scratch(tm)
    out_specs.append(pl.BlockSpec((tm, 2 * SSD_HEADS), row))
    out_shape.append(jax.ShapeDtypeStruct((n, 2 * SSD_HEADS), F32))
    cast_in, cast_out, cast_shape = _cast_specs(casts, steps)
    outs = pl.pallas_call(
        functools.partial(_front_kernel, tm=tm, seq=seq, local=local, n_casts=len(casts)),
        grid=(steps,),
        in_specs=in_specs + cast_in,
        out_specs=out_specs + cast_out,
        out_shape=out_shape + cast_shape,
        scratch_shapes=scratch,
        compiler_params=_params("parallel"),
        name="token_mixer_front",
    )(*args, *casts)
    n_main = 3 if local else 2
    main = (outs[0], outs[1], outs[2]) if local else (outs[0], None, outs[1])
    return main + (list(outs[n_main:]),)


def _ssd_kernel(z_ref, dt_ref, init_ref, dtp_ref, vec_ref,
                y_ref, fin_ref, p_ref, dts_ref, pt_ref, dtt_ref, yacc_ref, st_ref, *, seq, n_seq):
    nc = seq // CHUNK
    lane = lax.broadcasted_iota(jnp.int32, (CHUNK, LANES), 1)
    row_i = lax.broadcasted_iota(jnp.int32, (CHUNK, CHUNK), 0)
    col_i = lax.broadcasted_iota(jnp.int32, (CHUNK, CHUNK), 1)
    lower = col_i <= row_i
    upper = col_i >= row_i
    head0 = lane < HEAD_DIM

    fwd_lane = (lane % (2 * SSD_HEADS)) < SSD_HEADS
    for q in range(n_seq):
        dt_all = _softplus(dt_ref[q] + dtp_ref[0:1, :])
        adt = dt_all * (-jnp.exp(dtp_ref[1:2, :]))
        p_all = (jnp.dot(lower.astype(F32), jnp.where(fwd_lane, adt, 0.0), preferred_element_type=F32,
                         precision=lax.Precision.HIGHEST)
                 + jnp.dot(upper.astype(F32), jnp.where(fwd_lane, 0.0, adt), preferred_element_type=F32,
                           precision=lax.Precision.HIGHEST))
        pt_ref[q] = p_all.T
        dtt_ref[q] = dt_all.T
        for c in range(nc):
            shift = (LANES - 2 * SSD_HEADS * c) % LANES
            p_ref[q, c * CHUNK:(c + 1) * CHUNK, :] = pltpu.roll(p_all, shift, axis=1) if shift else p_all
            dts_ref[q, c * CHUNK:(c + 1) * CHUNK, :] = pltpu.roll(dt_all, shift, axis=1) if shift else dt_all

    def group_inputs(q, rows, g):
        xg = z_ref[q, rows, g * LANES:(g + 1) * LANES].astype(F32)
        bg = z_ref[q, rows, BRANCH + g * SSD_STATE:BRANCH + (g + 1) * SSD_STATE]
        cg = z_ref[q, rows, BRANCH + (SSD_GROUPS + g) * SSD_STATE:BRANCH + (SSD_GROUPS + g + 1) * SSD_STATE]
        return xg, bg, cg

    def spread(m, j):
        return jnp.broadcast_to(m[:, j:j + 1], (CHUNK, LANES))

    def pair(a0, a1):
        return jnp.where(head0, a0, a1)

    st_ref[...] = init_ref[:, 0]

    def fwd_chunk(q, c):
        r0 = pl.multiple_of(c * CHUNK, CHUNK)
        rows = pl.ds(r0, CHUNK)
        pm = p_ref[q, rows, :]
        dt = dts_ref[q, rows, :]
        head_rows = pl.ds(pl.multiple_of(c * 2 * SSD_HEADS, 2 * SSD_HEADS), 2 * SSD_HEADS)
        pm_t = pt_ref[q, head_rows, :]
        dt_t = dtt_ref[q, head_rows, :]
        last = pm[CHUNK - 1:CHUNK, :]
        to_end = dt * jnp.exp(last - pm)
        for g in range(SSD_GROUPS):
            xg, bg, cg = group_inputs(q, rows, g)
            gram = lax.dot_general(cg, bg, (((1,), (1,)), ((), ())), preferred_element_type=F32)
            s_prev = st_ref[q, g]
            y = jnp.dot(cg, s_prev.astype(BF16), preferred_element_type=F32)
            hf0, hf1 = 2 * g, 2 * g + 1
            hb0, hb1 = SSD_HEADS + hf0, SSD_HEADS + hf1
            p_f0, p_f1 = spread(pm, hf0), spread(pm, hf1)
            y = y * jnp.exp(pair(p_f0, p_f1))
            for p_f, hf, hb, mask in ((p_f0, hf0, hb0, head0), (p_f1, hf1, hb1, jnp.logical_not(head0))):
                dec_f = jnp.exp(jnp.where(lower, p_f - pm_t[hf:hf + 1, :], -jnp.inf))
                dec_b = jnp.exp(jnp.where(upper, spread(pm, hb) - pm_t[hb:hb + 1, :], -jnp.inf))
                m = gram * (dec_f * dt_t[hf:hf + 1, :] + dec_b * dt_t[hb:hb + 1, :])
                xh = jnp.where(mask, xg, 0.0).astype(BF16)
                y = y + jnp.dot(m.astype(BF16), xh, preferred_element_type=F32)
            yacc_ref[q, rows, g * LANES:(g + 1) * LANES] = y
            w_end = pair(spread(to_end, hf0), spread(to_end, hf1))
            upd = lax.dot_general(bg, (xg * w_end).astype(BF16), (((0,), (0,)), ((), ())),
                                  preferred_element_type=F32)
            decay = jnp.exp(jnp.where(head0[0:1, :], last[:, hf0:hf0 + 1], last[:, hf1:hf1 + 1]))
            st_ref[q, g] = s_prev * decay + upd

    def fwd(c, carry):
        for q in range(n_seq):
            fwd_chunk(q, c)
        return carry

    lax.fori_loop(0, nc, fwd, 0, unroll=2)
    fin_ref[:, 0] = st_ref[...]

    st_ref[...] = init_ref[:, 1]
    d_skip = vec_ref[0:1, :]
    norm_g = vec_ref[1:2, :]

    def bwd_chunk(q, c):
        r0 = pl.multiple_of(c * CHUNK, CHUNK)
        rows = pl.ds(r0, CHUNK)
        pm = p_ref[q, rows, :]
        dt = dts_ref[q, rows, :]
        first = pm[0:1, :]
        to_end = dt * jnp.exp(first - pm)
        for g in range(SSD_GROUPS):
            xg, bg, cg = group_inputs(q, rows, g)
            hb0, hb1 = SSD_HEADS + 2 * g, SSD_HEADS + 2 * g + 1
            s_prev = st_ref[q, g]
            y = (jnp.dot(cg, s_prev.astype(BF16), preferred_element_type=F32)
                 * jnp.exp(pair(spread(pm, hb0), spread(pm, hb1))))
            yacc_ref[q, rows, g * LANES:(g + 1) * LANES] = (
                yacc_ref[q, rows, g * LANES:(g + 1) * LANES] + y + d_skip[:, g * LANES:(g + 1) * LANES] * xg)
            w_end = pair(spread(to_end, hb0), spread(to_end, hb1))
            upd = lax.dot_general(bg, (xg * w_end).astype(BF16), (((0,), (0,)), ((), ())),
                                  preferred_element_type=F32)
            decay = jnp.exp(jnp.where(head0[0:1, :], first[:, hb0:hb0 + 1], first[:, hb1:hb1 + 1]))
            st_ref[q, g] = s_prev * decay + upd
        gated = yacc_ref[q, rows, :] * _silu(z_ref[q, rows, SSD_CONV_CH:W_SSD].astype(F32))
        ms = jnp.mean(gated * gated, axis=-1, keepdims=True)
        y_ref[q, rows, :] = (gated * lax.rsqrt(ms + RMS_EPS) * norm_g).astype(BF16)

    def bwd(k, carry):
        for q in range(n_seq):
            bwd_chunk(q, nc - 1 - k)
        return carry

    lax.fori_loop(0, nc, bwd, 0)
    fin_ref[:, 1] = st_ref[...]


def _ssd_mixer(z_ssd, dt, init, lp):
    bsz, seq, _ = z_ssd.shape
    nc = seq // CHUNK
    n_hd = 2 * SSD_HEADS
    n_seq = max(1, min(SSD_MAX_SEQS_PER_STEP, SSD_ROWS_PER_STEP // seq))
    while bsz % n_seq:
        n_seq -= 1
    assert nc * n_hd <= LANES
    dt = dt.reshape(bsz, nc, CHUNK, n_hd).transpose(0, 2, 1, 3).reshape(bsz, CHUNK, nc * n_hd)
    dt = jnp.pad(dt, ((0, 0), (0, 0), (0, LANES - nc * n_hd)))
    seq_map = lambda b: (b, 0, 0)
    st_map = lambda b: (b, 0, 0, 0, 0)
    const = lambda b: (0, 0)
    st_shape = (bsz, 2, SSD_GROUPS, SSD_STATE, 2 * HEAD_DIM)
    return pl.pallas_call(
        functools.partial(_ssd_kernel, seq=seq, n_seq=n_seq),
        grid=(bsz // n_seq,),
        in_specs=[
            pl.BlockSpec((n_seq, seq, W_SSD), seq_map),
            pl.BlockSpec((n_seq, CHUNK, LANES), seq_map),
            pl.BlockSpec((n_seq,) + st_shape[1:], st_map),
            pl.BlockSpec((2, LANES), const),
            pl.BlockSpec((2, BRANCH), const),
        ],
        out_specs=[
            pl.BlockSpec((n_seq, seq, BRANCH), seq_map),
            pl.BlockSpec((n_seq,) + st_shape[1:], st_map),
        ],
        out_shape=[
            jax.ShapeDtypeStruct((bsz, seq, BRANCH), BF16),
            jax.ShapeDtypeStruct(st_shape, F32),
        ],
        scratch_shapes=[
            pltpu.VMEM((n_seq, seq, LANES), F32),
            pltpu.VMEM((n_seq, seq, LANES), F32),
            pltpu.VMEM((n_seq, LANES, CHUNK), F32),
            pltpu.VMEM((n_seq, LANES, CHUNK), F32),
            pltpu.VMEM((n_seq, seq, BRANCH), F32),
            pltpu.VMEM((n_seq, SSD_GROUPS, SSD_STATE, 2 * HEAD_DIM), F32),
        ],
        compiler_params=_params("parallel"),
        name="ssd_mixer",
    )(z_ssd, dt, init, lp["ssd_dtp"], lp["ssd_vec"])


def _token_mixed(x_ref, mod_ref, ya_ref, yb_ref, wa_ref, wb_ref):
    proj = (jnp.dot(ya_ref[...], wa_ref[...], preferred_element_type=F32)
            + jnp.dot(yb_ref[...], wb_ref[...], preferred_element_type=F32))
    return x_ref[...] + mod_ref[0, 2:3, :] * proj


def _token_mixed_specs(tm, rows_per_mod, mod_base):
    row = lambda i, *_: (i, 0)
    const = lambda i, *_: (0, 0)
    mod_idx = _mod_index(tm, rows_per_mod, mod_base)
    return [
        pl.BlockSpec((tm, D_MODEL), row),
        pl.BlockSpec((1, 6, D_MODEL), lambda i, *_: mod_idx(i)),
        pl.BlockSpec((tm, BRANCH), row),
        pl.BlockSpec((tm, 3 * BRANCH), row),
        pl.BlockSpec((BRANCH, D_MODEL), const),
        pl.BlockSpec((3 * BRANCH, D_MODEL), const),
    ]


def _residual_out(x, gate, delta, fg_ref):
    out = x + gate * delta
    if fg_ref is not None:
        ms = jnp.mean(out * out, axis=-1, keepdims=True)
        out = out * lax.rsqrt(ms + RMS_EPS) * fg_ref[...]
    return out


def _cast_slices(srcs, dsts):
    for src, dst in zip(srcs, dsts):
        dst[...] = src[...].astype(BF16)


def _cast_specs(casts, steps):
    row = lambda i, *_: (i, 0)
    blocks = [(arr.shape[0] // steps, arr.shape[1]) for arr in casts]
    return ([pl.BlockSpec(b, row) for b in blocks], [pl.BlockSpec(b, row) for b in blocks],
            [jax.ShapeDtypeStruct(arr.shape, BF16) for arr in casts])


def _swiglu_kernel(*refs, final_norm, n_casts):
    x_ref, mod_ref, ya_ref, yb_ref, wa_ref, wb_ref, g_ref, w1_ref, w3_ref, w2_ref = refs[0:10]
    pos = 10
    fg_ref = None
    if final_norm:
        fg_ref = refs[pos]
        pos += 1
    cast_in = refs[pos:pos + n_casts]
    o_ref = refs[pos + n_casts]
    cast_out = refs[pos + n_casts + 1:pos + 2 * n_casts + 1]
    x1 = _token_mixed(x_ref, mod_ref, ya_ref, yb_ref, wa_ref, wb_ref)
    h = _norm_modulate(x1, g_ref[...], mod_ref[0, 3:4, :], mod_ref[0, 4:5, :]).astype(BF16)
    a = jnp.dot(h, w1_ref[...], preferred_element_type=F32)
    b = jnp.dot(h, w3_ref[...], preferred_element_type=F32)
    y = jnp.dot((_silu(a) * b).astype(BF16), w2_ref[...], preferred_element_type=F32)
    o_ref[...] = _residual_out(x1, mod_ref[0, 5:6, :], y, fg_ref)
    _cast_slices(cast_in, cast_out)


def _dense_channel_mixer(x2d, mod, y_ssd, y_local, w_a, w_b, g, w1, w3, w2, *, tm, rows_per_mod, mod_base,
                         final_g=None, casts=()):
    n = x2d.shape[0]
    steps = n // tm
    ff = w1.shape[1]
    const = lambda i: (0, 0)
    row = lambda i: (i, 0)
    resident = dict(pipeline_mode=pl.Buffered(1))
    in_specs = _token_mixed_specs(tm, rows_per_mod, mod_base) + [
        pl.BlockSpec((1, D_MODEL), const),
        pl.BlockSpec((D_MODEL, ff), const, **resident),
        pl.BlockSpec((D_MODEL, ff), const, **resident),
        pl.BlockSpec((ff, D_MODEL), const, **resident),
    ]
    args = [x2d, mod, y_ssd, y_local, w_a, w_b, g, w1, w3, w2]
    if final_g is not None:
        in_specs.append(pl.BlockSpec((1, D_MODEL), const))
        args.append(final_g)
    cast_in, cast_out, cast_shape = _cast_specs(casts, steps)
    in_specs += cast_in
    args += list(casts)
    out_specs = [pl.BlockSpec((tm, D_MODEL), row)] + cast_out
    out_shape = [jax.ShapeDtypeStruct((n, D_MODEL), F32)] + cast_shape
    outs = pl.pallas_call(
        functools.partial(_swiglu_kernel, final_norm=final_g is not None, n_casts=len(casts)),
        grid=(steps,),
        in_specs=in_specs,
        out_specs=out_specs,
        out_shape=out_shape,
        compiler_params=_params("parallel"),
        name="dense_channel_mixer",
    )(*args)
    return outs[0], list(outs[1:])


def _can_cast_in_steps(arr2d, steps):
    return arr2d.shape[0] % steps == 0 and (arr2d.shape[0] // steps) % (2 * SUBLANES) == 0


def _router_kernel(*refs, n_casts):
    x_ref, mod_ref, ya_ref, yb_ref, wa_ref, wb_ref, g_ref, r_ref = refs[0:8]
    cast_in = refs[8:8 + n_casts]
    x1_ref, h_ref, comb_ref, rank_ref, rankt_ref, cnt_ref = refs[8 + n_casts:14 + n_casts]
    cast_out = refs[14 + n_casts:14 + 2 * n_casts]
    carry_ref = refs[-1]
    _cast_slices(cast_in, cast_out)

    @pl.when(pl.program_id(0) == 0)
    def _():
        carry_ref[...] = jnp.zeros_like(carry_ref)

    x1 = _token_mixed(x_ref, mod_ref, ya_ref, yb_ref, wa_ref, wb_ref)
    x1_ref[...] = x1
    h = _norm_modulate(x1, g_ref[...], mod_ref[0, 3:4, :], mod_ref[0, 4:5, :])
    h_hi = h.astype(BF16)
    h_ref[...] = h_hi
    h_lo = (h - h_hi.astype(F32)).astype(BF16)
    parts = jnp.dot(jnp.concatenate([h_hi, h_lo], axis=0), r_ref[...], preferred_element_type=F32)
    parts = parts[0:h.shape[0], :] + parts[h.shape[0]:, :]
    logits = parts + pltpu.roll(parts, LANES - N_EXPERTS, axis=1)
    lane = lax.broadcasted_iota(jnp.int32, logits.shape, 1).astype(F32)
    logits = jnp.where(lane < N_EXPERTS, logits, -jnp.inf)
    m1 = jnp.max(logits, axis=-1, keepdims=True)
    i1 = jnp.min(jnp.where(logits == m1, lane, float(LANES)), axis=-1, keepdims=True)
    rest = jnp.where(lane == i1, -jnp.inf, logits)
    m2 = jnp.max(rest, axis=-1, keepdims=True)
    i2 = jnp.min(jnp.where(rest == m2, lane, float(LANES)), axis=-1, keepdims=True)
    e2 = jnp.exp(m2 - m1)
    denom = 1.0 + e2
    comb_ref[...] = jnp.where(lane == i1, 1.0 / denom, 0.0) + jnp.where(lane == i2, e2 / denom, 0.0)
    is_chosen = (lane == i1) | (lane == i2)
    chosen = jnp.where(is_chosen, 1.0, 0.0)
    tm = chosen.shape[0]
    earlier = lax.broadcasted_iota(jnp.int32, (tm, tm), 1) < lax.broadcasted_iota(jnp.int32, (tm, tm), 0)
    running = carry_ref[...]
    before = jnp.dot(jnp.where(earlier, 1.0, 0.0).astype(BF16), chosen.astype(BF16),
                     preferred_element_type=F32) + running.astype(F32)
    rank = jnp.where(is_chosen, before, -1.0)
    rank_ref[...] = rank
    rankt_ref[...] = rank.T[0:N_EXPERTS, :]
    chosen_i = jnp.where(is_chosen, 1, 0)
    for k in range(tm // TOK_TILE):
        running = running + jnp.sum(chosen_i[k * TOK_TILE:(k + 1) * TOK_TILE, :], axis=0, keepdims=True)
        cnt_ref[0, k:k + 1, :] = running
    carry_ref[...] = running


def _router(x2d, mod, y_ssd, y_local, w_a, w_b, g, router_pad, *, tm, rows_per_mod, mod_base, casts=()):
    n = x2d.shape[0]
    row = lambda i: (i, 0)
    const = lambda i: (0, 0)
    per = tm // TOK_TILE
    cast_in, cast_out, cast_shape = _cast_specs(casts, n // tm)
    return pl.pallas_call(
        functools.partial(_router_kernel, n_casts=len(casts)),
        grid=(n // tm,),
        in_specs=_token_mixed_specs(tm, rows_per_mod, mod_base) + [
            pl.BlockSpec((1, D_MODEL), const),
            pl.BlockSpec((D_MODEL, LANES), const),
        ] + cast_in,
        out_specs=[
            pl.BlockSpec((tm, D_MODEL), row),
            pl.BlockSpec((tm, D_MODEL), row),
            pl.BlockSpec((tm, LANES), row),
            pl.BlockSpec((tm, LANES), row),
            pl.BlockSpec((N_EXPERTS, tm), lambda i: (0, i)),
            pl.BlockSpec((1, per, LANES), lambda i: (i, 0, 0)),
        ] + cast_out,
        out_shape=[
            jax.ShapeDtypeStruct((n, D_MODEL), F32),
            jax.ShapeDtypeStruct((n, D_MODEL), BF16),
            jax.ShapeDtypeStruct((n, LANES), F32),
            jax.ShapeDtypeStruct((n, LANES), F32),
            jax.ShapeDtypeStruct((N_EXPERTS, n), F32),
            jax.ShapeDtypeStruct((n // tm, per, LANES), jnp.int32),
        ] + cast_shape,
        scratch_shapes=[pltpu.VMEM((1, LANES), jnp.int32)],
        compiler_params=_params("arbitrary"),
        name="moe_router",
    )(x2d, mod, y_ssd, y_local, w_a, w_b, g, router_pad, *casts)


def _routing_tables(cnt, n_tokens):
    i32 = jnp.int32
    n_t = n_tokens // TOK_TILE
    cnt_end = cnt.reshape(n_t, LANES)[:, :N_EXPERTS]
    cnt_start = jnp.concatenate([jnp.zeros((1, N_EXPERTS), i32), cnt_end[:-1]], axis=0)
    tot = cnt_end[-1]
    nblk = (tot + ROW_BLOCK - 1) // ROW_BLOCK
    blk_cum = jnp.cumsum(nblk)
    blk_first = blk_cum - nblk
    n_valid = blk_cum[-1]
    nb = n_tokens * TOP_K // ROW_BLOCK + N_EXPERTS
    bi = jnp.arange(nb, dtype=i32)
    valid = bi < n_valid
    e_of = jnp.sum((bi[:, None] >= blk_cum[None, :]).astype(i32), axis=1)
    e_last = jnp.sum(((n_valid - 1) >= blk_cum).astype(i32))
    blk_e = jnp.minimum(jnp.where(valid, e_of, e_last), N_EXPERTS - 1)
    base = jnp.where(valid, (bi - blk_first[blk_e]) * ROW_BLOCK, 0)
    lo = base[:, None] + jnp.arange(N_SUB, dtype=i32)[None, :] * SUB_BLOCK
    hi = jnp.where(valid[:, None], jnp.minimum(lo + SUB_BLOCK, tot[blk_e][:, None]), lo)
    cs = cnt_start[:, blk_e]
    ce = cnt_end[:, blk_e]
    rel = (cs[:, :, None] < hi[None]) & (ce[:, :, None] > lo[None])
    tile_id = jnp.arange(n_t, dtype=i32)[:, None, None]
    t_first = jnp.min(jnp.where(rel, tile_id, n_t), axis=0)
    sub_hi = jnp.max(jnp.where(rel, tile_id, -1), axis=0)
    sub_lo = jnp.where(sub_hi >= 0, t_first, 0)
    u_lo = jnp.min(t_first, axis=1)
    u_n = jnp.maximum(jnp.max(sub_hi, axis=1) - u_lo + 1, 0)
    u_lo = jnp.where(u_n > 0, u_lo, 0)
    per = COMBINE_TM // TOK_TILE
    seg_rows = blk_first * ROW_BLOCK
    lo_c = seg_rows[None, :] + cnt_start[0::per]
    cnt_c = cnt_end[per - 1::per] - cnt_start[0::per]
    first_c = (lo_c // 16) * 16
    nwin_c = jnp.where(cnt_c > 0, (lo_c + cnt_c - first_c + COMBINE_WIN - 1) // COMBINE_WIN, 0)
    return dict(blk_e=blk_e.astype(i32), valid=valid.astype(i32), base=base.astype(i32),
                u_lo=u_lo.astype(i32), u_n=u_n.astype(i32),
                sub_lo=sub_lo.reshape(-1).astype(i32), sub_hi=sub_hi.reshape(-1).astype(i32),
                seg_rows=seg_rows.astype(i32), first_c=first_c.reshape(-1).astype(i32),
                nwin_c=nwin_c.reshape(-1).astype(i32), n_blocks=nb)


def _experts_kernel(be_ref, bv_ref, base_ref, ulo_ref, un_ref, slo_ref, shi_ref,
                    h_hbm, rt_hbm, w1_ref, w3_ref, w2_ref, y_ref, hbuf, rbuf, hsem, rsem):
    i = pl.program_id(0)
    slot = i % 2

    def tile_copies(s, k, t):
        tok = pl.multiple_of(t * TOK_TILE, TOK_TILE)
        return (pltpu.make_async_copy(h_hbm.at[pl.ds(tok, TOK_TILE)], hbuf.at[s, k], hsem.at[s, k]),
                pltpu.make_async_copy(rt_hbm.at[:, pl.ds(tok, TOK_TILE)], rbuf.at[s, k], rsem.at[s, k]))

    def for_block_tiles(b, s, action):
        for k in range(GATHER_TILES):
            @pl.when(k < un_ref[b])
            def _():
                for cp in tile_copies(s, k, ulo_ref[b] + k):
                    action(cp)

    @pl.when(i == 0)
    def _():
        def clear(k, carry):
            hbuf[k // (GATHER_TILES + 1), k % (GATHER_TILES + 1)] = jnp.zeros((TOK_TILE, D_MODEL), BF16)
            rbuf[k // (GATHER_TILES + 1), k % (GATHER_TILES + 1)] = jnp.zeros((N_EXPERTS, TOK_TILE), F32)
            return carry

        lax.fori_loop(0, 2 * (GATHER_TILES + 1), clear, 0)
        for_block_tiles(0, 0, lambda cp: cp.start())

    for_block_tiles(i, slot, lambda cp: cp.wait())

    @pl.when(i + 1 < pl.num_programs(0))
    def _():
        for_block_tiles(i + 1, 1 - slot, lambda cp: cp.start())

    @pl.when(bv_ref[i] == 1)
    def _():
        e = be_ref[i]
        row_id = lax.broadcasted_iota(jnp.int32, (SUB_BLOCK, TOK_TILE), 0).astype(F32)
        expert_row = lax.broadcasted_iota(jnp.int32, (N_EXPERTS, TOK_TILE), 0) == e
        u_lo = ulo_ref[i]
        selected = []

        def tile_weights(ranks8, first_rank):
            ranks = jnp.sum(jnp.where(expert_row, ranks8, 0.0), axis=0, keepdims=True)
            return jnp.where(ranks == first_rank + row_id, 1.0, 0.0).astype(BF16)

        for sb in range(N_SUB):
            first_rank = (base_ref[i] + sb * SUB_BLOCK).astype(F32)
            t_lo, t_hi = slo_ref[N_SUB * i + sb], shi_ref[N_SUB * i + sb]
            k0 = jnp.clip(t_lo - u_lo, 0, GATHER_TILES + 1 - WINDOW_TILES)
            picks = []
            for j in range(WINDOW_TILES):
                t = u_lo + k0 + j
                live = (t >= t_lo) & (t <= t_hi) & (k0 + j < GATHER_TILES)
                picks.append(tile_weights(jnp.where(live, rbuf[slot, k0 + j], -1.0), first_rank))
            window = hbuf[slot, pl.ds(k0, WINDOW_TILES)].reshape(WINDOW_TILES * TOK_TILE, D_MODEL)
            rows = jnp.dot(jnp.concatenate(picks, axis=1), window, preferred_element_type=F32)

            def late_tile(t, carry):
                k = t - u_lo
                kk = jnp.minimum(k, GATHER_TILES)

                @pl.when(k >= GATHER_TILES)
                def _():
                    tok = pl.multiple_of(t * TOK_TILE, TOK_TILE)
                    pltpu.sync_copy(h_hbm.at[pl.ds(tok, TOK_TILE)], hbuf.at[slot, GATHER_TILES])
                    pltpu.sync_copy(rt_hbm.at[:, pl.ds(tok, TOK_TILE)], rbuf.at[slot, GATHER_TILES])

                return carry + jnp.dot(tile_weights(rbuf[slot, kk], first_rank), hbuf[slot, kk],
                                       preferred_element_type=F32)

            t_next = u_lo + jnp.minimum(k0 + WINDOW_TILES, GATHER_TILES)
            rows = lax.fori_loop(jnp.maximum(t_next, t_lo), t_hi + 1, late_tile, rows)
            selected.append(rows.astype(BF16))

        hg = jnp.concatenate(selected, axis=0)
        half = w1_ref.shape[2] // 2
        y = None
        for c in range(2):
            a = jnp.dot(hg, w1_ref[0, :, c * half:(c + 1) * half], preferred_element_type=F32)
            b = jnp.dot(hg, w3_ref[0, :, c * half:(c + 1) * half], preferred_element_type=F32)
            part = jnp.dot((_silu(a) * b).astype(BF16), w2_ref[0, c * half:(c + 1) * half, :],
                           preferred_element_type=F32)
            y = part if y is None else y + part
        y_ref[...] = y.astype(BF16)

    @pl.when(bv_ref[i] == 0)
    def _():
        y_ref[...] = jnp.zeros_like(y_ref)


def _moe_experts(h2, rank_t, w1, w3, w2, tb):
    nb = tb["n_blocks"]
    ff = w1.shape[2]
    w_map = lambda i, be, *_: (be[i], 0, 0)
    grid_spec = pltpu.PrefetchScalarGridSpec(
        num_scalar_prefetch=7,
        grid=(nb,),
        in_specs=[
            pl.BlockSpec(memory_space=pl.ANY),
            pl.BlockSpec(memory_space=pl.ANY),
            pl.BlockSpec((1, D_MODEL, ff), w_map, pipeline_mode=pl.Buffered(1)),
            pl.BlockSpec((1, D_MODEL, ff), w_map, pipeline_mode=pl.Buffered(1)),
            pl.BlockSpec((1, ff, D_MODEL), w_map, pipeline_mode=pl.Buffered(1)),
        ],
        out_specs=pl.BlockSpec((ROW_BLOCK, D_MODEL), lambda i, *_: (i, 0)),
        scratch_shapes=[
            pltpu.VMEM((2, GATHER_TILES + 1, TOK_TILE, D_MODEL), BF16),
            pltpu.VMEM((2, GATHER_TILES + 1, N_EXPERTS, TOK_TILE), F32),
            pltpu.SemaphoreType.DMA((2, GATHER_TILES)),
            pltpu.SemaphoreType.DMA((2, GATHER_TILES)),
        ],
    )
    return pl.pallas_call(
        _experts_kernel,
        grid_spec=grid_spec,
        out_shape=jax.ShapeDtypeStruct((nb * ROW_BLOCK, D_MODEL), BF16),
        compiler_params=_params("arbitrary"),
        name="moe_experts",
    )(tb["blk_e"], tb["valid"], tb["base"], tb["u_lo"], tb["u_n"], tb["sub_lo"], tb["sub_hi"],
      h2, rank_t, w1, w3, w2)


def _combine_kernel(*refs, final_norm, n_rows):
    first_ref, nwin_ref, seg_ref, x_ref, mod_ref, comb_ref, rank_ref, y_hbm = refs[0:8]
    fg_ref = refs[8] if final_norm else None
    o_ref, ybuf, ysem, acc_ref = refs[-4:]
    i = pl.program_id(0)
    slot = i % 2
    extra = N_EXPERTS

    def window(b, e):
        return first_ref[b * N_EXPERTS + e], nwin_ref[b * N_EXPERTS + e]

    def win_start(first, j):
        return pl.multiple_of(jnp.minimum(first + j * COMBINE_WIN, n_rows - COMBINE_WIN), 16)

    def for_tile_windows(b, s, action):
        for e in range(N_EXPERTS):
            first, n_win = window(b, e)

            @pl.when(n_win > 0)
            def _():
                action(pltpu.make_async_copy(y_hbm.at[pl.ds(win_start(first, 0), COMBINE_WIN)],
                                             ybuf.at[s, e], ysem.at[s, e]))

    @pl.when(i == 0)
    def _():
        def clear(k, carry):
            ybuf[k // (N_EXPERTS + 1), k % (N_EXPERTS + 1)] = jnp.zeros((COMBINE_WIN, D_MODEL), BF16)
            return carry

        lax.fori_loop(0, 2 * (N_EXPERTS + 1), clear, 0)
        for_tile_windows(0, 0, lambda cp: cp.start())

    for_tile_windows(i, slot, lambda cp: cp.wait())

    @pl.when(i + 1 < pl.num_programs(0))
    def _():
        for_tile_windows(i + 1, 1 - slot, lambda cp: cp.start())

    col_id = lax.broadcasted_iota(jnp.int32, (COMBINE_TM, COMBINE_WIN), 1).astype(F32)

    def weights(e, first, j):
        pos_col = rank_ref[:, e:e + 1] + seg_ref[e].astype(F32)
        start = win_start(first, j).astype(F32)
        return jnp.where(pos_col == start + col_id, comb_ref[:, e:e + 1], 0.0).astype(BF16)

    firsts = [window(i, e) for e in range(N_EXPERTS)]
    pick_all = jnp.concatenate([weights(e, firsts[e][0], 0) for e in range(N_EXPERTS)], axis=1)
    y_all = ybuf[slot, 0:N_EXPERTS].reshape(N_EXPERTS * COMBINE_WIN, D_MODEL)
    acc_ref[...] = jnp.dot(pick_all, y_all, preferred_element_type=F32)

    for e in range(N_EXPERTS):
        first, n_win = firsts[e]

        def more(j, carry):
            pltpu.sync_copy(y_hbm.at[pl.ds(win_start(first, j), COMBINE_WIN)], ybuf.at[slot, extra])
            acc_ref[...] += jnp.dot(weights(e, first, j), ybuf[slot, extra], preferred_element_type=F32)
            return carry

        lax.fori_loop(1, jnp.maximum(n_win, 1), more, 0)

    o_ref[...] = _residual_out(x_ref[...], mod_ref[0, 5:6, :], acc_ref[...], fg_ref)


def _moe_combine(x2d, mod, comb, rank, y_sorted, tb, *, rows_per_mod, mod_base, final_g=None):
    n = x2d.shape[0]
    tm = COMBINE_TM
    row = lambda i, *_: (i, 0)
    const = lambda i, *_: (0, 0)
    mod_idx = _mod_index(tm, rows_per_mod, mod_base)
    in_specs = [
        pl.BlockSpec((tm, D_MODEL), row),
        pl.BlockSpec((1, 6, D_MODEL), lambda i, *_: mod_idx(i)),
        pl.BlockSpec((tm, LANES), row),
        pl.BlockSpec((tm, LANES), row),
        pl.BlockSpec(memory_space=pl.ANY),
    ]
    args = [x2d, mod, comb, rank, y_sorted]
    if final_g is not None:
        in_specs.append(pl.BlockSpec((1, D_MODEL), const))
        args.append(final_g)
    grid_spec = pltpu.PrefetchScalarGridSpec(
        num_scalar_prefetch=3,
        grid=(n // tm,),
        in_specs=in_specs,
        out_specs=pl.BlockSpec((tm, D_MODEL), row),
        scratch_shapes=[
            pltpu.VMEM((2, N_EXPERTS + 1, COMBINE_WIN, D_MODEL), BF16),
            pltpu.SemaphoreType.DMA((2, N_EXPERTS)),
            pltpu.VMEM((tm, D_MODEL), F32),
        ],
    )
    return pl.pallas_call(
        functools.partial(_combine_kernel, final_norm=final_g is not None, n_rows=y_sorted.shape[0]),
        grid_spec=grid_spec,
        out_shape=jax.ShapeDtypeStruct((n, D_MODEL), F32),
        compiler_params=_params("arbitrary"),
        name="moe_combine",
    )(tb["first_c"], tb["nwin_c"], tb["seg_rows"], *args)


def _to_bf16_in(w, steps):
    if w.dtype == BF16:
        return None, w
    flat = w.reshape(-1, w.shape[-1])
    if _can_cast_in_steps(flat, steps):
        return flat, None
    return None, w.astype(BF16)


def _moe_channel_mixer(x2d, mod, y_ssd, y_local, w_a, w_b, g, router_pad, w1, w3, w2, *, rows_per_mod, mod_base,
                       final_g=None):
    w1 = w1.astype(BF16)
    w3 = w3.astype(BF16)
    tm = math.gcd(ROUTER_TM, rows_per_mod)
    w2_flat, w2_bf16 = _to_bf16_in(w2, x2d.shape[0] // tm)
    x1, h2, comb, rank, rank_t, cnt, *cast = _router(
        x2d, mod, y_ssd, y_local, w_a, w_b, g, router_pad, tm=tm, rows_per_mod=rows_per_mod,
        mod_base=mod_base, casts=() if w2_flat is None else (w2_flat,))
    if w2_bf16 is None:
        w2_bf16 = cast[0].reshape(w2.shape)
    w2 = w2_bf16
    tb = _routing_tables(cnt, x2d.shape[0])
    y_sorted = _moe_experts(h2, rank_t, w1, w3, w2, tb)
    return _moe_combine(x1, mod, comb, rank, y_sorted, tb, rows_per_mod=rows_per_mod, mod_base=mod_base,
                        final_g=final_g)


def _layer_params(l, w_in, w_out, ssd_conv_w, ssd_conv_b, ssd_dt_bias, ssd_a_log, ssd_d, ssd_norm_g,
                  gm_norm_g, gm_norm_b, gm_ws, gm_bs, sc_conv_w, cf_conv_w, cf_conv_b, cf_norm_g, cf_norm_b):
    wi = w_in[l]
    w_main = jnp.concatenate(
        [wi[:, _C_XBC:_C_DT], wi[:, _C_Z:_C_UV], wi[:, _C_UV:_C_SC], wi[:, _C_CF:_C_END], wi[:, _C_SC:_C_CF]],
        axis=1).astype(BF16)
    w_dt = jnp.pad(wi[:, _C_DT:_C_Z], ((0, 0), (0, LANES - 2 * SSD_HEADS))).astype(BF16)
    return {
        "w_main": w_main,
        "w_dt": w_dt,
        "w_out_a": w_out[l, 0:BRANCH].astype(BF16),
        "w_out_b": w_out[l, BRANCH:].astype(BF16),
        "ssd_conv_w": ssd_conv_w[l],
        "ssd_conv_b": ssd_conv_b[l].reshape(1, -1),
        "ssd_dtp": jnp.stack([jnp.tile(ssd_dt_bias[l].reshape(-1), LANES // (2 * SSD_HEADS)),
                              jnp.tile(ssd_a_log[l].reshape(-1), LANES // (2 * SSD_HEADS))]),
        "ssd_vec": jnp.stack([jnp.repeat(ssd_d[l], HEAD_DIM), ssd_norm_g[l]]),
        "gm_ws": gm_ws[l],
        "gm_bs": jnp.repeat(gm_bs[l].T, HEAD_DIM, axis=1),
        "sc_conv_w": sc_conv_w[l],
        "cf_conv_w": cf_conv_w[l],
        "local_vec": jnp.stack([gm_norm_g[l], gm_norm_b[l], cf_conv_b[l], cf_norm_g[l], cf_norm_b[l]]),
    }


def kernel(x, c, ctx, c_ctx, mod_w, mod_b, norm1_g, norm2_g, w_in, w_out, ssd_conv_w, ssd_conv_b, ssd_dt_bias, ssd_a_log, ssd_d, ssd_norm_g, gm_norm_g, gm_norm_b, gm_ws, gm_bs, sc_conv_w, cf_conv_w, cf_conv_b, cf_norm_g, cf_norm_b, ffn_w1, ffn_w3, ffn_w2, moe_router, moe_w1, moe_w3, moe_w2, final_norm_g):
    bsz, seq, _ = x.shape
    ctx_len = ctx.shape[1]
    depth = w_in.shape[0]
    n_x, n_c = bsz * seq, bsz * ctx_len
    tm = 512

    mod = _modulation(jnp.concatenate([c, c_ctx[None, :]], axis=0), mod_w, mod_b)
    x_map = dict(tm=tm, rows_per_mod=seq, mod_base=0)
    c_map = dict(tm=tm, rows_per_mod=n_c, mod_base=bsz)
    zero_state = jnp.zeros((bsz, 2, SSD_GROUPS, SSD_STATE, 2 * HEAD_DIM), F32)

    xs = x.reshape(n_x, D_MODEL)
    xc = ctx.reshape(n_c, D_MODEL)
    moe_bf16 = {}
    for l in range(depth):
        last = l == depth - 1
        lp = _layer_params(l, w_in, w_out, ssd_conv_w, ssd_conv_b, ssd_dt_bias, ssd_a_log, ssd_d, ssd_norm_g,
                           gm_norm_g, gm_norm_b, gm_ws, gm_bs, sc_conv_w, cf_conv_w, cf_conv_b, cf_norm_g,
                           cf_norm_b)
        g1 = norm1_g[l].reshape(1, D_MODEL)
        g2 = norm2_g[l].reshape(1, D_MODEL)
        ffn_f32 = [ffn_w1[l // 2], ffn_w3[l // 2], ffn_w2[l // 2]] if l % 2 == 0 else []
        ctx_tm = min(tm, ctx_len)
        if not all(_can_cast_in_steps(w, n_c // ctx_tm) for w in ffn_f32):
            ffn_f32 = []
        zc_ssd, yc_loc, dtc, ffn_bf16 = _token_mixer_front(xc, mod[l], g1, lp, tm=ctx_tm, seq=ctx_len,
                                                           rows_per_mod=n_c, mod_base=bsz, local=not last,
                                                           casts=ffn_f32)
        yc_ssd, states = _ssd_mixer(zc_ssd.reshape(bsz, ctx_len, W_SSD), dtc.reshape(bsz, ctx_len, -1),
                                    zero_state, lp)
        z_ssd, y_loc, dtx, _ = _token_mixer_front(xs, mod[l], g1, lp, tm=min(FRONT_TM, seq), seq=seq,
                                                  rows_per_mod=seq, mod_base=0)
        y_ssd, _ = _ssd_mixer(z_ssd.reshape(bsz, seq, W_SSD), dtx.reshape(bsz, seq, -1), states, lp)
        streams = [(xs, y_ssd.reshape(n_x, BRANCH), y_loc, x_map)]
        if not last:
            streams.append((xc, yc_ssd.reshape(n_c, BRANCH), yc_loc, c_map))
        fg = final_norm_g.reshape(1, D_MODEL) if last else None
        i = l // 2
        if l % 2 == 0:
            w1, w3, w2 = ffn_bf16 or (ffn_w1[i].astype(BF16), ffn_w3[i].astype(BF16), ffn_w2[i].astype(BF16))
            casts = []
            if l + 1 < depth:
                casts = [w.reshape(-1, w.shape[-1]) for w in (moe_w1[i], moe_w3[i])]
                if not all(_can_cast_in_steps(w, n_x // tm) for w in casts):
                    casts = []
            outs = []
            for t, ya, yb, m in streams:
                out, cast = _dense_channel_mixer(t, mod[l], ya, yb, lp["w_out_a"], lp["w_out_b"], g2, w1, w3, w2,
                                                 final_g=fg if t is xs else None, casts=casts if t is xs else (),
                                                 **m)
                outs.append(out)
                if cast:
                    moe_bf16[i] = [c.reshape(moe_w1[i].shape) for c in cast]
        else:
            w1, w3 = moe_bf16.get(i) or (moe_w1[i], moe_w3[i])
            w2 = moe_w2[i]
            r_hi = moe_router[i].astype(BF16)
            r_lo = (moe_router[i] - r_hi.astype(F32)).astype(BF16)
            r_pad = jnp.pad(jnp.concatenate([r_hi, r_lo], axis=1), ((0, 0), (0, LANES - 2 * N_EXPERTS)))
            outs = [_moe_channel_mixer(t, mod[l], ya, yb, lp["w_out_a"], lp["w_out_b"], g2, r_pad, w1, w3, w2,
                                       rows_per_mod=m["rows_per_mod"], mod_base=m["mod_base"],
                                       final_g=fg if t is xs else None) for t, ya, yb, m in streams]
        xs = outs[0]
        if not last:
            xc = outs[1]
    return xs.reshape(bsz, seq, D_MODEL)
```

```python
import functools
import math

import jax
import jax.numpy as jnp
from jax import lax
from jax.experimental import pallas as pl
from jax.experimental.pallas import tpu as pltpu

F32 = jnp.float32
BF16 = jnp.bfloat16

LANES = 128
SUBLANES = 8
VMEM_LIMIT_BYTES = 56 * 1024 * 1024

D_MODEL = 1024
BRANCH = 256
HEAD_DIM = 64
SSD_HEADS = 4
SSD_GROUPS = 2
SSD_STATE = 128
CHUNK = 128
SSD_ROWS_PER_STEP = 4096
SSD_MAX_SEQS_PER_STEP = 4
SSD_CONV_CH = BRANCH + 2 * SSD_GROUPS * SSD_STATE
N_EXPERTS = 8
TOP_K = 2
TOK_TILE = 256
ROW_BLOCK = 512
SUB_BLOCK = 256
N_SUB = ROW_BLOCK // SUB_BLOCK
GATHER_TILES = 11
WINDOW_TILES = 5
ROUTER_TM = 1024
COMBINE_TM = 512
COMBINE_WIN = 256
FRONT_TM = 1024
HALO = 16
RMS_EPS = 1e-6
LN_EPS = 1e-5

_C_XBC = 0
_C_DT = _C_XBC + SSD_CONV_CH
_C_Z = _C_DT + 2 * SSD_HEADS
_C_UV = _C_Z + BRANCH
_C_SC = _C_UV + 2 * BRANCH
_C_CF = _C_SC + 3 * BRANCH
_C_END = _C_CF + 2 * BRANCH
W_SSD = SSD_CONV_CH + BRANCH
W_UVCF = 4 * BRANCH
W_SC = 3 * BRANCH
W_MAIN = W_SSD + W_UVCF + W_SC


def _params(*sem, vmem_mib=None):
    limit = VMEM_LIMIT_BYTES if vmem_mib is None else vmem_mib * 1024 * 1024
    return pltpu.CompilerParams(dimension_semantics=sem, vmem_limit_bytes=limit)


def _sigmoid(v):
    return 0.5 + 0.5 * jnp.tanh(0.5 * v)


def _silu(v):
    half = 0.5 * v
    return half + half * jnp.tanh(half)


def _softplus(v):
    return jnp.maximum(v, 0.0) + jnp.log1p(jnp.exp(-jnp.abs(v)))


def _gelu_tanh(v):
    return 0.5 * v * (1.0 + jnp.tanh(0.7978845608028654 * (v + 0.044715 * (v * v * v))))


def _norm_modulate(x, g, shift, scale):
    ms = jnp.mean(x * x, axis=-1, keepdims=True)
    return (x * lax.rsqrt(ms + RMS_EPS) * g) * (1.0 + scale) + shift


def _mod_kernel(c_ref, w_ref, b_ref, o_ref):
    s = _silu(c_ref[...])
    s_hi = s.astype(BF16)
    s_lo = (s - s_hi.astype(F32)).astype(BF16)
    w = w_ref[0]
    w_hi = w.astype(BF16)
    w_lo = (w - w_hi.astype(F32)).astype(BF16)
    o_ref[0, 0] = (jnp.dot(s_hi, w_hi, preferred_element_type=F32)
                   + jnp.dot(s_lo, w_hi, preferred_element_type=F32)
                   + jnp.dot(s_hi, w_lo, preferred_element_type=F32) + b_ref[0, 0])


def _modulation(c_all, mod_w, mod_b):
    depth = mod_w.shape[0]
    rows = c_all.shape[0]
    out = pl.pallas_call(
        _mod_kernel,
        grid=(depth, 6),
        in_specs=[
            pl.BlockSpec((rows, D_MODEL), lambda l, j: (0, 0)),
            pl.BlockSpec((1, D_MODEL, D_MODEL), lambda l, j: (l, 0, j)),
            pl.BlockSpec((1, 1, 1, D_MODEL), lambda l, j: (l, j, 0, 0)),
        ],
        out_specs=pl.BlockSpec((1, 1, rows, D_MODEL), lambda l, j: (l, j, 0, 0)),
        out_shape=jax.ShapeDtypeStruct((depth, 6, rows, D_MODEL), F32),
        compiler_params=_params("arbitrary", "arbitrary", vmem_mib=24),
        name="modulation",
    )(c_all, mod_w, mod_b.reshape(depth, 6, 1, D_MODEL))
    return jnp.transpose(out, (0, 2, 1, 3))


def _mod_index(tm, rows_per_mod, mod_base):
    return lambda i: (mod_base + (i * tm) // rows_per_mod, 0, 0)


def _fill_stage(stage_ref, prev, main, nxt):
    rows = main.shape[0]
    stage_ref[0:HALO, :] = prev
    stage_ref[HALO:HALO + rows, :] = main
    stage_ref[HALO + rows:2 * HALO + rows, :] = nxt


def _dwconv_rows(stage_ref, w_ref, taps, r, rows, c0, width):
    acc = None
    for k in range(taps):
        lo = HALO + r + k - taps // 2
        term = stage_ref[lo:lo + rows, c0:c0 + width] * w_ref[k:k + 1, c0:c0 + width]
        acc = term if acc is None else acc + term
    return acc


def _front_kernel(*refs, tm, seq, local, n_casts):
    x_ref, xp_ref, xn_ref, mod_ref, g_ref, w_ref, wdt_ref, convw_ref, convb_ref = refs[0:9]
    pos = 9
    if local:
        ws_ref, bs_ref, scw_ref, cfw_ref, vec_ref = refs[pos:pos + 5]
        pos += 5
    cast_in = refs[pos:pos + n_casts]
    pos += n_casts
    zssd_ref = refs[pos]
    if local:
        yloc_ref, dt_ref = refs[pos + 1:pos + 3]
        pos += 3
    else:
        dt_ref = refs[pos + 1]
        pos += 2
    cast_out = refs[pos:pos + n_casts]
    pos += n_casts
    if local:
        ssd_stage, sc_stage, cf_stage, rot_ref = refs[pos:]
    else:
        (ssd_stage,) = refs[pos:]
    _cast_slices(cast_in, cast_out)
    i = pl.program_id(0)
    tiles_per_seq = seq // tm
    keep_prev = jnp.where(i % tiles_per_seq == 0, 0.0, 1.0)
    keep_next = jnp.where(i % tiles_per_seq == tiles_per_seq - 1, 0.0, 1.0)
    g, shift, scale = g_ref[...], mod_ref[0, 0:1, :], mod_ref[0, 1:2, :]
    h = _norm_modulate(x_ref[...], g, shift, scale).astype(BF16)
    h_halo = _norm_modulate(jnp.concatenate([xp_ref[...], xn_ref[...]], axis=0), g, shift, scale).astype(BF16)
    z = jnp.dot(h, w_ref[...], preferred_element_type=F32)
    z_halo = jnp.dot(h_halo, w_ref[...], preferred_element_type=F32)
    zp = z_halo[0:HALO] * keep_prev
    zn = z_halo[HALO:2 * HALO] * keep_next
    dt_ref[...] = jnp.dot(h, wdt_ref[...], preferred_element_type=F32)[:, 0:2 * SSD_HEADS]

    _fill_stage(ssd_stage, zp[:, 0:SSD_CONV_CH], z[:, 0:SSD_CONV_CH], zn[:, 0:SSD_CONV_CH])
    for r in range(0, tm, CHUNK):
        for c0 in range(0, SSD_CONV_CH, BRANCH):
            v = _dwconv_rows(ssd_stage, convw_ref, 5, r, CHUNK, c0, BRANCH) + convb_ref[0:1, c0:c0 + BRANCH]
            zssd_ref[r:r + CHUNK, c0:c0 + BRANCH] = _silu(v).astype(BF16)
    zssd_ref[:, SSD_CONV_CH:W_SSD] = z[:, SSD_CONV_CH:W_SSD].astype(BF16)
    if local:
        _local_mixers_tile(z[:, W_SSD:W_MAIN], zp[:, W_SSD:W_MAIN], zn[:, W_SSD:W_MAIN], ws_ref, bs_ref, scw_ref,
                           cfw_ref, vec_ref, sc_stage, cf_stage, rot_ref, yloc_ref, tm)


def _local_mixers_tile(z, zp, zn, ws_ref, bs_ref, scw_ref, cfw_ref, vec_ref, sc_stage, cf_stage, rot_ref,
                       yloc_ref, tm):
    lane = lax.broadcasted_iota(jnp.int32, (CHUNK, BRANCH), 1)
    gm_g, gm_b = vec_ref[0:1, :], vec_ref[1:2, :]
    cf_b, cf_g, cf_nb = vec_ref[2:3, :], vec_ref[3:4, :], vec_ref[4:5, :]
    c_uv, c_cf, c_sc = 0, 2 * BRANCH, W_UVCF

    def layer_norm(v, gain, bias):
        mu = jnp.mean(v, axis=-1, keepdims=True)
        var = jnp.mean(jnp.square(v - mu), axis=-1, keepdims=True)
        return (v - mu) * lax.rsqrt(var + LN_EPS) * gain + bias

    def sc_in(t):
        return t[:, c_sc + BRANCH:c_sc + 2 * BRANCH] * t[:, c_sc + 2 * BRANCH:c_sc + 3 * BRANCH]

    def cf_in(t):
        return t[:, c_cf:c_cf + BRANCH] * _sigmoid(t[:, c_cf + BRANCH:c_cf + 2 * BRANCH])

    _fill_stage(sc_stage, sc_in(zp), sc_in(z), sc_in(zn))
    _fill_stage(cf_stage, cf_in(zp), cf_in(z), cf_in(zn))
    span = tm + 2 * HALO - SUBLANES
    for phase in range(SUBLANES):
        rot_ref[phase] = cf_stage[phase:phase + span, :]

    for r in range(0, tm, CHUNK):
        ge = _gelu_tanh(z[r:r + CHUNK, c_uv:c_uv + 2 * BRANCH])
        u = ge[:, 0:BRANCH]
        v = layer_norm(ge[:, BRANCH:2 * BRANCH], gm_g, gm_b)
        s = bs_ref[...]
        for hd in range(BRANCH // HEAD_DIM):
            in_head = (lane >= hd * HEAD_DIM) & (lane < (hd + 1) * HEAD_DIM)
            s = s + jnp.dot(ws_ref[hd].astype(BF16), jnp.where(in_head, v, 0.0).astype(BF16),
                            preferred_element_type=F32)
        yloc_ref[r:r + CHUNK, 0:BRANCH] = (u * s).astype(BF16)
        gate = z[r:r + CHUNK, c_sc:c_sc + BRANCH]
        yloc_ref[r:r + CHUNK, BRANCH:2 * BRANCH] = (gate * _dwconv_rows(sc_stage, scw_ref, 3, r, CHUNK, 0, BRANCH)
                                                    ).astype(BF16)
        cv = cf_b
        for k in range(31):
            lo = HALO + r + k - 31 // 2
            q8 = (lo // SUBLANES) * SUBLANES
            cv = cv + rot_ref[lo % SUBLANES, q8:q8 + CHUNK, :] * cfw_ref[k:k + 1, :]
        yloc_ref[r:r + CHUNK, 2 * BRANCH:3 * BRANCH] = _silu(layer_norm(cv, cf_g, cf_nb)).astype(BF16)


def _local_mixer_param_specs():
    const = lambda i, *_: (0, 0)
    return [
        pl.BlockSpec((BRANCH // HEAD_DIM, CHUNK, CHUNK), lambda i, *_: (0, 0, 0)),
        pl.BlockSpec((CHUNK, BRANCH), const),
        pl.BlockSpec((3, BRANCH), const),
        pl.BlockSpec((31, BRANCH), const),
        pl.BlockSpec((5, BRANCH), const),
    ]


def _local_mixer_params(lp):
    return [lp["gm_ws"], lp["gm_bs"], lp["sc_conv_w"], lp["cf_conv_w"], lp["local_vec"]]


def _local_mixer_scratch(tm):
    span = tm + 2 * HALO
    return [pltpu.VMEM((span, BRANCH), F32), pltpu.VMEM((span, BRANCH), F32),
            pltpu.VMEM((SUBLANES, span - SUBLANES, BRANCH), F32)]


def _token_mixer_front(x2d, mod, g, lp, *, tm, seq, rows_per_mod, mod_base, local=True, casts=()):
    n = x2d.shape[0]
    steps = n // tm
    per = tm // HALO
    row = lambda i: (i, 0)
    const = lambda i: (0, 0)
    in_specs = [
        pl.BlockSpec((tm, D_MODEL), row),
        pl.BlockSpec((HALO, D_MODEL), lambda i: (jnp.maximum(i * per - 1, 0), 0)),
        pl.BlockSpec((HALO, D_MODEL), lambda i: (jnp.minimum((i + 1) * per, n // HALO - 1), 0)),
        pl.BlockSpec((1, 6, D_MODEL), _mod_index(tm, rows_per_mod, mod_base)),
        pl.BlockSpec((1, D_MODEL), const),
        pl.BlockSpec((D_MODEL, W_MAIN if local else W_SSD), const),
        pl.BlockSpec((D_MODEL, LANES), const),
        pl.BlockSpec((5, SSD_CONV_CH), const),
        pl.BlockSpec((1, SSD_CONV_CH), const),
    ]
    args = [x2d, x2d, x2d, mod, g, lp["w_main"], lp["w_dt"], lp["ssd_conv_w"], lp["ssd_conv_b"]]
    out_specs = [pl.BlockSpec((tm, W_SSD), row)]
    out_shape = [jax.ShapeDtypeStruct((n, W_SSD), BF16)]
    scratch = [pltpu.VMEM((tm + 2 * HALO, SSD_CONV_CH), F32)]
    if local:
        in_specs += _local_mixer_param_specs()
        args += _local_mixer_params(lp)
        out_specs.append(pl.BlockSpec((tm, 3 * BRANCH), row))
        out_shape.append(jax.ShapeDtypeStruct((n, 3 * BRANCH), BF16))
        scratch += _local_mixer_scratch(tm)
    out_specs.append(pl.BlockSpec((tm, 2 * SSD_HEADS), row))
    out_shape.append(jax.ShapeDtypeStruct((n, 2 * SSD_HEADS), F32))
    cast_in, cast_out, cast_shape = _cast_specs(casts, steps)
    outs = pl.pallas_call(
        functools.partial(_front_kernel, tm=tm, seq=seq, local=local, n_casts=len(casts)),
        grid=(steps,),
        in_specs=in_specs + cast_in,
        out_specs=out_specs + cast_out,
        out_shape=out_shape + cast_shape,
        scratch_shapes=scratch,
        compiler_params=_params("parallel"),
        name="token_mixer_front",
    )(*args, *casts)
    n_main = 3 if local else 2
    main = (outs[0], outs[1], outs[2]) if local else (outs[0], None, outs[1])
    return main + (list(outs[n_main:]),)


def _ssd_kernel(z_ref, dt_ref, init_ref, dtp_ref, vec_ref,
                y_ref, fin_ref, p_ref, dts_ref, pt_ref, dtt_ref, yacc_ref, st_ref, *, seq, n_seq):
    nc = seq // CHUNK
    lane = lax.broadcasted_iota(jnp.int32, (CHUNK, LANES), 1)
    row_i = lax.broadcasted_iota(jnp.int32, (CHUNK, CHUNK), 0)
    col_i = lax.broadcasted_iota(jnp.int32, (CHUNK, CHUNK), 1)
    lower = col_i <= row_i
    upper = col_i >= row_i
    head0 = lane < HEAD_DIM

    fwd_lane = (lane % (2 * SSD_HEADS)) < SSD_HEADS
    for q in range(n_seq):
        dt_all = _softplus(dt_ref[q] + dtp_ref[0:1, :])
        adt = dt_all * (-jnp.exp(dtp_ref[1:2, :]))
        p_all = (jnp.dot(lower.astype(F32), jnp.where(fwd_lane, adt, 0.0), preferred_element_type=F32,
                         precision=lax.Precision.HIGHEST)
                 + jnp.dot(upper.astype(F32), jnp.where(fwd_lane, 0.0, adt), preferred_element_type=F32,
                           precision=lax.Precision.HIGHEST))
        pt_ref[q] = p_all.T
        dtt_ref[q] = dt_all.T
        for c in range(nc):
            shift = (LANES - 2 * SSD_HEADS * c) % LANES
            p_ref[q, c * CHUNK:(c + 1) * CHUNK, :] = pltpu.roll(p_all, shift, axis=1) if shift else p_all
            dts_ref[q, c * CHUNK:(c + 1) * CHUNK, :] = pltpu.roll(dt_all, shift, axis=1) if shift else dt_all

    def group_inputs(q, rows, g):
        xg = z_ref[q, rows, g * LANES:(g + 1) * LANES].astype(F32)
        bg = z_ref[q, rows, BRANCH + g * SSD_STATE:BRANCH + (g + 1) * SSD_STATE]
        cg = z_ref[q, rows, BRANCH + (SSD_GROUPS + g) * SSD_STATE:BRANCH + (SSD_GROUPS + g + 1) * SSD_STATE]
        return xg, bg, cg

    def spread(m, j):
        return jnp.broadcast_to(m[:, j:j + 1], (CHUNK, LANES))

    def pair(a0, a1):
        return jnp.where(head0, a0, a1)

    st_ref[...] = init_ref[:, 0]

    def fwd_chunk(q, c):
        r0 = pl.multiple_of(c * CHUNK, CHUNK)
        rows = pl.ds(r0, CHUNK)
        pm = p_ref[q, rows, :]
        dt = dts_ref[q, rows, :]
        head_rows = pl.ds(pl.multiple_of(c * 2 * SSD_HEADS, 2 * SSD_HEADS), 2 * SSD_HEADS)
        pm_t = pt_ref[q, head_rows, :]
        dt_t = dtt_ref[q, head_rows, :]
        last = pm[CHUNK - 1:CHUNK, :]
        to_end = dt * jnp.exp(last - pm)
        for g in range(SSD_GROUPS):
            xg, bg, cg = group_inputs(q, rows, g)
            gram = lax.dot_general(cg, bg, (((1,), (1,)), ((), ())), preferred_element_type=F32)
            s_prev = st_ref[q, g]
            y = jnp.dot(cg, s_prev.astype(BF16), preferred_element_type=F32)
            hf0, hf1 = 2 * g, 2 * g + 1
            hb0, hb1 = SSD_HEADS + hf0, SSD_HEADS + hf1
            p_f0, p_f1 = spread(pm, hf0), spread(pm, hf1)
            y = y * jnp.exp(pair(p_f0, p_f1))
            for p_f, hf, hb, mask in ((p_f0, hf0, hb0, head0), (p_f1, hf1, hb1, jnp.logical_not(head0))):
                dec_f = jnp.exp(jnp.where(lower, p_f - pm_t[hf:hf + 1, :], -jnp.inf))
                dec_b = jnp.exp(jnp.where(upper, spread(pm, hb) - pm_t[hb:hb + 1, :], -jnp.inf))
                m = gram * (dec_f * dt_t[hf:hf + 1, :] + dec_b * dt_t[hb:hb + 1, :])
                xh = jnp.where(mask, xg, 0.0).astype(BF16)
                y = y + jnp.dot(m.astype(BF16), xh, preferred_element_type=F32)
            yacc_ref[q, rows, g * LANES:(g + 1) * LANES] = y
            w_end = pair(spread(to_end, hf0), spread(to_end, hf1))
            upd = lax.dot_general(bg, (xg * w_end).astype(BF16), (((0,), (0,)), ((), ())),
                                  preferred_element_type=F32)
            decay = jnp.exp(jnp.where(head0[0:1, :], last[:, hf0:hf0 + 1], last[:, hf1:hf1 + 1]))
            st_ref[q, g] = s_prev * decay + upd

    def fwd(c, carry):
        for q in range(n_seq):
            fwd_chunk(q, c)
        return carry

    lax.fori_loop(0, nc, fwd, 0, unroll=2)
    fin_ref[:, 0] = st_ref[...]

    st_ref[...] = init_ref[:, 1]
    d_skip = vec_ref[0:1, :]
    norm_g = vec_ref[1:2, :]

    def bwd_chunk(q, c):
        r0 = pl.multiple_of(c * CHUNK, CHUNK)
        rows = pl.ds(r0, CHUNK)
        pm = p_ref[q, rows, :]
        dt = dts_ref[q, rows, :]
        first = pm[0:1, :]
        to_end = dt * jnp.exp(first - pm)
        for g in range(SSD_GROUPS):
            xg, bg, cg = group_inputs(q, rows, g)
            hb0, hb1 = SSD_HEADS + 2 * g, SSD_HEADS + 2 * g + 1
            s_prev = st_ref[q, g]
            y = (jnp.dot(cg, s_prev.astype(BF16), preferred_element_type=F32)
                 * jnp.exp(pair(spread(pm, hb0), spread(pm, hb1))))
            yacc_ref[q, rows, g * LANES:(g + 1) * LANES] = (
                yacc_ref[q, rows, g * LANES:(g + 1) * LANES] + y + d_skip[:, g * LANES:(g + 1) * LANES] * xg)
            w_end = pair(spread(to_end, hb0), spread(to_end, hb1))
            upd = lax.dot_general(bg, (xg * w_end).astype(BF16), (((0,), (0,)), ((), ())),
                                  preferred_element_type=F32)
            decay = jnp.exp(jnp.where(head0[0:1, :], first[:, hb0:hb0 + 1], first[:, hb1:hb1 + 1]))
            st_ref[q, g] = s_prev * decay + upd
        gated = yacc_ref[q, rows, :] * _silu(z_ref[q, rows, SSD_CONV_CH:W_SSD].astype(F32))
        ms = jnp.mean(gated * gated, axis=-1, keepdims=True)
        y_ref[q, rows, :] = (gated * lax.rsqrt(ms + RMS_EPS) * norm_g).astype(BF16)

    def bwd(k, carry):
        for q in range(n_seq):
            bwd_chunk(q, nc - 1 - k)
        return carry

    lax.fori_loop(0, nc, bwd, 0)
    fin_ref[:, 1] = st_ref[...]


def _ssd_mixer(z_ssd, dt, init, lp):
    bsz, seq, _ = z_ssd.shape
    nc = seq // CHUNK
    n_hd = 2 * SSD_HEADS
    n_seq = max(1, min(SSD_MAX_SEQS_PER_STEP, SSD_ROWS_PER_STEP // seq))
    while bsz % n_seq:
        n_seq -= 1
    assert nc * n_hd <= LANES
    dt = dt.reshape(bsz, nc, CHUNK, n_hd).transpose(0, 2, 1, 3).reshape(bsz, CHUNK, nc * n_hd)
    dt = jnp.pad(dt, ((0, 0), (0, 0), (0, LANES - nc * n_hd)))
    seq_map = lambda b: (b, 0, 0)
    st_map = lambda b: (b, 0, 0, 0, 0)
    const = lambda b: (0, 0)
    st_shape = (bsz, 2, SSD_GROUPS, SSD_STATE, 2 * HEAD_DIM)
    return pl.pallas_call(
        functools.partial(_ssd_kernel, seq=seq, n_seq=n_seq),
        grid=(bsz // n_seq,),
        in_specs=[
            pl.BlockSpec((n_seq, seq, W_SSD), seq_map),
            pl.BlockSpec((n_seq, CHUNK, LANES), seq_map),
            pl.BlockSpec((n_seq,) + st_shape[1:], st_map),
            pl.BlockSpec((2, LANES), const),
            pl.BlockSpec((2, BRANCH), const),
        ],
        out_specs=[
            pl.BlockSpec((n_seq, seq, BRANCH), seq_map),
            pl.BlockSpec((n_seq,) + st_shape[1:], st_map),
        ],
        out_shape=[
            jax.ShapeDtypeStruct((bsz, seq, BRANCH), BF16),
            jax.ShapeDtypeStruct(st_shape, F32),
        ],
        scratch_shapes=[
            pltpu.VMEM((n_seq, seq, LANES), F32),
            pltpu.VMEM((n_seq, seq, LANES), F32),
            pltpu.VMEM((n_seq, LANES, CHUNK), F32),
            pltpu.VMEM((n_seq, LANES, CHUNK), F32),
            pltpu.VMEM((n_seq, seq, BRANCH), F32),
            pltpu.VMEM((n_seq, SSD_GROUPS, SSD_STATE, 2 * HEAD_DIM), F32),
        ],
        compiler_params=_params("parallel", vmem_mib=40),
        name="ssd_mixer",
    )(z_ssd, dt, init, lp["ssd_dtp"], lp["ssd_vec"])


def _token_mixed(x_ref, mod_ref, ya_ref, yb_ref, wa_ref, wb_ref):
    proj = (jnp.dot(ya_ref[...], wa_ref[...], preferred_element_type=F32)
            + jnp.dot(yb_ref[...], wb_ref[...], preferred_element_type=F32))
    return x_ref[...] + mod_ref[0, 2:3, :] * proj


def _token_mixed_specs(tm, rows_per_mod, mod_base):
    row = lambda i, *_: (i, 0)
    const = lambda i, *_: (0, 0)
    mod_idx = _mod_index(tm, rows_per_mod, mod_base)
    return [
        pl.BlockSpec((tm, D_MODEL), row),
        pl.BlockSpec((1, 6, D_MODEL), lambda i, *_: mod_idx(i)),
        pl.BlockSpec((tm, BRANCH), row),
        pl.BlockSpec((tm, 3 * BRANCH), row),
        pl.BlockSpec((BRANCH, D_MODEL), const),
        pl.BlockSpec((3 * BRANCH, D_MODEL), const),
    ]


def _residual_out(x, gate, delta, fg_ref):
    out = x + gate * delta
    if fg_ref is not None:
        ms = jnp.mean(out * out, axis=-1, keepdims=True)
        out = out * lax.rsqrt(ms + RMS_EPS) * fg_ref[...]
    return out


def _cast_slices(srcs, dsts):
    for src, dst in zip(srcs, dsts):
        dst[...] = src[...].astype(BF16)


def _cast_specs(casts, steps):
    row = lambda i, *_: (i, 0)
    blocks = [(arr.shape[0] // steps, arr.shape[1]) for arr in casts]
    return ([pl.BlockSpec(b, row) for b in blocks], [pl.BlockSpec(b, row) for b in blocks],
            [jax.ShapeDtypeStruct(arr.shape, BF16) for arr in casts])


def _swiglu_kernel(*refs, final_norm, n_casts):
    x_ref, mod_ref, ya_ref, yb_ref, wa_ref, wb_ref, g_ref, w1_ref, w3_ref, w2_ref = refs[0:10]
    pos = 10
    fg_ref = None
    if final_norm:
        fg_ref = refs[pos]
        pos += 1
    cast_in = refs[pos:pos + n_casts]
    o_ref = refs[pos + n_casts]
    cast_out = refs[pos + n_casts + 1:pos + 2 * n_casts + 1]
    x1 = _token_mixed(x_ref, mod_ref, ya_ref, yb_ref, wa_ref, wb_ref)
    h = _norm_modulate(x1, g_ref[...], mod_ref[0, 3:4, :], mod_ref[0, 4:5, :]).astype(BF16)
    a = jnp.dot(h, w1_ref[...], preferred_element_type=F32)
    b = jnp.dot(h, w3_ref[...], preferred_element_type=F32)
    y = jnp.dot((_silu(a) * b).astype(BF16), w2_ref[...], preferred_element_type=F32)
    o_ref[...] = _residual_out(x1, mod_ref[0, 5:6, :], y, fg_ref)
    _cast_slices(cast_in, cast_out)


def _dense_channel_mixer(x2d, mod, y_ssd, y_local, w_a, w_b, g, w1, w3, w2, *, tm, rows_per_mod, mod_base,
                         final_g=None, casts=()):
    n = x2d.shape[0]
    steps = n // tm
    ff = w1.shape[1]
    const = lambda i: (0, 0)
    row = lambda i: (i, 0)
    resident = dict(pipeline_mode=pl.Buffered(1))
    in_specs = _token_mixed_specs(tm, rows_per_mod, mod_base) + [
        pl.BlockSpec((1, D_MODEL), const),
        pl.BlockSpec((D_MODEL, ff), const, **resident),
        pl.BlockSpec((D_MODEL, ff), const, **resident),
        pl.BlockSpec((ff, D_MODEL), const, **resident),
    ]
    args = [x2d, mod, y_ssd, y_local, w_a, w_b, g, w1, w3, w2]
    if final_g is not None:
        in_specs.append(pl.BlockSpec((1, D_MODEL), const))
        args.append(final_g)
    cast_in, cast_out, cast_shape = _cast_specs(casts, steps)
    in_specs += cast_in
    args += list(casts)
    out_specs = [pl.BlockSpec((tm, D_MODEL), row)] + cast_out
    out_shape = [jax.ShapeDtypeStruct((n, D_MODEL), F32)] + cast_shape
    outs = pl.pallas_call(
        functools.partial(_swiglu_kernel, final_norm=final_g is not None, n_casts=len(casts)),
        grid=(steps,),
        in_specs=in_specs,
        out_specs=out_specs,
        out_shape=out_shape,
        compiler_params=_params("parallel"),
        name="dense_channel_mixer",
    )(*args)
    return outs[0], list(outs[1:])


def _can_cast_in_steps(arr2d, steps):
    return arr2d.shape[0] % steps == 0 and (arr2d.shape[0] // steps) % (2 * SUBLANES) == 0


def _router_kernel(*refs, n_casts):
    x_ref, mod_ref, ya_ref, yb_ref, wa_ref, wb_ref, g_ref, r_ref = refs[0:8]
    cast_in = refs[8:8 + n_casts]
    x1_ref, h_ref, comb_ref, rank_ref, rankt_ref, cnt_ref = refs[8 + n_casts:14 + n_casts]
    cast_out = refs[14 + n_casts:14 + 2 * n_casts]
    carry_ref = refs[-1]
    _cast_slices(cast_in, cast_out)

    @pl.when(pl.program_id(0) == 0)
    def _():
        carry_ref[...] = jnp.zeros_like(carry_ref)

    x1 = _token_mixed(x_ref, mod_ref, ya_ref, yb_ref, wa_ref, wb_ref)
    x1_ref[...] = x1
    h = _norm_modulate(x1, g_ref[...], mod_ref[0, 3:4, :], mod_ref[0, 4:5, :])
    h_hi = h.astype(BF16)
    h_ref[...] = h_hi
    h_lo = (h - h_hi.astype(F32)).astype(BF16)
    parts = jnp.dot(jnp.concatenate([h_hi, h_lo], axis=0), r_ref[...], preferred_element_type=F32)
    parts = parts[0:h.shape[0], :] + parts[h.shape[0]:, :]
    logits = parts + pltpu.roll(parts, LANES - N_EXPERTS, axis=1)
    lane = lax.broadcasted_iota(jnp.int32, logits.shape, 1).astype(F32)
    logits = jnp.where(lane < N_EXPERTS, logits, -jnp.inf)
    m1 = jnp.max(logits, axis=-1, keepdims=True)
    i1 = jnp.min(jnp.where(logits == m1, lane, float(LANES)), axis=-1, keepdims=True)
    rest = jnp.where(lane == i1, -jnp.inf, logits)
    m2 = jnp.max(rest, axis=-1, keepdims=True)
    i2 = jnp.min(jnp.where(rest == m2, lane, float(LANES)), axis=-1, keepdims=True)
    e2 = jnp.exp(m2 - m1)
    denom = 1.0 + e2
    comb_ref[...] = jnp.where(lane == i1, 1.0 / denom, 0.0) + jnp.where(lane == i2, e2 / denom, 0.0)
    is_chosen = (lane == i1) | (lane == i2)
    chosen = jnp.where(is_chosen, 1.0, 0.0)
    tm = chosen.shape[0]
    earlier = lax.broadcasted_iota(jnp.int32, (tm, tm), 1) < lax.broadcasted_iota(jnp.int32, (tm, tm), 0)
    running = carry_ref[...]
    before = jnp.dot(jnp.where(earlier, 1.0, 0.0).astype(BF16), chosen.astype(BF16),
                     preferred_element_type=F32) + running.astype(F32)
    rank = jnp.where(is_chosen, before, -1.0)
    rank_ref[...] = rank
    rankt_ref[...] = rank.T[0:N_EXPERTS, :]
    chosen_i = jnp.where(is_chosen, 1, 0)
    for k in range(tm // TOK_TILE):
        running = running + jnp.sum(chosen_i[k * TOK_TILE:(k + 1) * TOK_TILE, :], axis=0, keepdims=True)
        cnt_ref[0, k:k + 1, :] = running
    carry_ref[...] = running


def _router(x2d, mod, y_ssd, y_local, w_a, w_b, g, router_pad, *, tm, rows_per_mod, mod_base, casts=()):
    n = x2d.shape[0]
    row = lambda i: (i, 0)
    const = lambda i: (0, 0)
    per = tm // TOK_TILE
    cast_in, cast_out, cast_shape = _cast_specs(casts, n // tm)
    return pl.pallas_call(
        functools.partial(_router_kernel, n_casts=len(casts)),
        grid=(n // tm,),
        in_specs=_token_mixed_specs(tm, rows_per_mod, mod_base) + [
            pl.BlockSpec((1, D_MODEL), const),
            pl.BlockSpec((D_MODEL, LANES), const),
        ] + cast_in,
        out_specs=[
            pl.BlockSpec((tm, D_MODEL), row),
            pl.BlockSpec((tm, D_MODEL), row),
            pl.BlockSpec((tm, LANES), row),
            pl.BlockSpec((tm, LANES), row),
            pl.BlockSpec((N_EXPERTS, tm), lambda i: (0, i)),
            pl.BlockSpec((1, per, LANES), lambda i: (i, 0, 0)),
        ] + cast_out,
        out_shape=[
            jax.ShapeDtypeStruct((n, D_MODEL), F32),
            jax.ShapeDtypeStruct((n, D_MODEL), BF16),
            jax.ShapeDtypeStruct((n, LANES), F32),
            jax.ShapeDtypeStruct((n, LANES), F32),
            jax.ShapeDtypeStruct((N_EXPERTS, n), F32),
            jax.ShapeDtypeStruct((n // tm, per, LANES), jnp.int32),
        ] + cast_shape,
        scratch_shapes=[pltpu.VMEM((1, LANES), jnp.int32)],
        compiler_params=_params("arbitrary", vmem_mib=48),
        name="moe_router",
    )(x2d, mod, y_ssd, y_local, w_a, w_b, g, router_pad, *casts)


def _routing_tables(cnt, n_tokens):
    i32 = jnp.int32
    n_t = n_tokens // TOK_TILE
    cnt_end = cnt.reshape(n_t, LANES)[:, :N_EXPERTS]
    cnt_start = jnp.concatenate([jnp.zeros((1, N_EXPERTS), i32), cnt_end[:-1]], axis=0)
    tot = cnt_end[-1]
    nblk = (tot + ROW_BLOCK - 1) // ROW_BLOCK
    blk_cum = jnp.cumsum(nblk)
    blk_first = blk_cum - nblk
    n_valid = blk_cum[-1]
    nb = n_tokens * TOP_K // ROW_BLOCK + N_EXPERTS
    bi = jnp.arange(nb, dtype=i32)
    valid = bi < n_valid
    e_of = jnp.sum((bi[:, None] >= blk_cum[None, :]).astype(i32), axis=1)
    e_last = jnp.sum(((n_valid - 1) >= blk_cum).astype(i32))
    blk_e = jnp.minimum(jnp.where(valid, e_of, e_last), N_EXPERTS - 1)
    base = jnp.where(valid, (bi - blk_first[blk_e]) * ROW_BLOCK, 0)
    lo = base[:, None] + jnp.arange(N_SUB, dtype=i32)[None, :] * SUB_BLOCK
    hi = jnp.where(valid[:, None], jnp.minimum(lo + SUB_BLOCK, tot[blk_e][:, None]), lo)
    cs = cnt_start[:, blk_e]
    ce = cnt_end[:, blk_e]
    rel = (cs[:, :, None] < hi[None]) & (ce[:, :, None] > lo[None])
    tile_id = jnp.arange(n_t, dtype=i32)[:, None, None]
    t_first = jnp.min(jnp.where(rel, tile_id, n_t), axis=0)
    sub_hi = jnp.max(jnp.where(rel, tile_id, -1), axis=0)
    sub_lo = jnp.where(sub_hi >= 0, t_first, 0)
    u_lo = jnp.min(t_first, axis=1)
    u_n = jnp.maximum(jnp.max(sub_hi, axis=1) - u_lo + 1, 0)
    u_lo = jnp.where(u_n > 0, u_lo, 0)
    per = COMBINE_TM // TOK_TILE
    seg_rows = blk_first * ROW_BLOCK
    lo_c = seg_rows[None, :] + cnt_start[0::per]
    cnt_c = cnt_end[per - 1::per] - cnt_start[0::per]
    first_c = (lo_c // 16) * 16
    nwin_c = jnp.where(cnt_c > 0, (lo_c + cnt_c - first_c + COMBINE_WIN - 1) // COMBINE_WIN, 0)
    return dict(blk_e=blk_e.astype(i32), valid=valid.astype(i32), base=base.astype(i32),
                u_lo=u_lo.astype(i32), u_n=u_n.astype(i32),
                sub_lo=sub_lo.reshape(-1).astype(i32), sub_hi=sub_hi.reshape(-1).astype(i32),
                seg_rows=seg_rows.astype(i32), first_c=first_c.reshape(-1).astype(i32),
                nwin_c=nwin_c.reshape(-1).astype(i32), n_blocks=nb)


def _experts_kernel(be_ref, bv_ref, base_ref, ulo_ref, un_ref, slo_ref, shi_ref,
                    h_hbm, rt_hbm, w1_ref, w3_ref, w2_ref, y_ref, hbuf, rbuf, hsem, rsem):
    i = pl.program_id(0)
    slot = i % 2

    def tile_copies(s, k, t):
        tok = pl.multiple_of(t * TOK_TILE, TOK_TILE)
        return (pltpu.make_async_copy(h_hbm.at[pl.ds(tok, TOK_TILE)], hbuf.at[s, k], hsem.at[s, k]),
                pltpu.make_async_copy(rt_hbm.at[:, pl.ds(tok, TOK_TILE)], rbuf.at[s, k], rsem.at[s, k]))

    def for_block_tiles(b, s, action):
        for k in range(GATHER_TILES):
            @pl.when(k < un_ref[b])
            def _():
                for cp in tile_copies(s, k, ulo_ref[b] + k):
                    action(cp)

    @pl.when(i == 0)
    def _():
        def clear(k, carry):
            hbuf[k // (GATHER_TILES + 1), k % (GATHER_TILES + 1)] = jnp.zeros((TOK_TILE, D_MODEL), BF16)
            rbuf[k // (GATHER_TILES + 1), k % (GATHER_TILES + 1)] = jnp.zeros((N_EXPERTS, TOK_TILE), F32)
            return carry

        lax.fori_loop(0, 2 * (GATHER_TILES + 1), clear, 0)
        for_block_tiles(0, 0, lambda cp: cp.start())

    for_block_tiles(i, slot, lambda cp: cp.wait())

    @pl.when(i + 1 < pl.num_programs(0))
    def _():
        for_block_tiles(i + 1, 1 - slot, lambda cp: cp.start())

    @pl.when(bv_ref[i] == 1)
    def _():
        e = be_ref[i]
        row_id = lax.broadcasted_iota(jnp.int32, (SUB_BLOCK, TOK_TILE), 0).astype(F32)
        expert_row = lax.broadcasted_iota(jnp.int32, (N_EXPERTS, TOK_TILE), 0) == e
        u_lo = ulo_ref[i]
        selected = []

        def tile_weights(ranks8, first_rank):
            ranks = jnp.sum(jnp.where(expert_row, ranks8, 0.0), axis=0, keepdims=True)
            return jnp.where(ranks == first_rank + row_id, 1.0, 0.0).astype(BF16)

        for sb in range(N_SUB):
            first_rank = (base_ref[i] + sb * SUB_BLOCK).astype(F32)
            t_lo, t_hi = slo_ref[N_SUB * i + sb], shi_ref[N_SUB * i + sb]
            k0 = jnp.clip(t_lo - u_lo, 0, GATHER_TILES + 1 - WINDOW_TILES)
            picks = []
            for j in range(WINDOW_TILES):
                t = u_lo + k0 + j
                live = (t >= t_lo) & (t <= t_hi) & (k0 + j < GATHER_TILES)
                picks.append(tile_weights(jnp.where(live, rbuf[slot, k0 + j], -1.0), first_rank))
            window = hbuf[slot, pl.ds(k0, WINDOW_TILES)].reshape(WINDOW_TILES * TOK_TILE, D_MODEL)
            rows = jnp.dot(jnp.concatenate(picks, axis=1), window, preferred_element_type=F32)

            def late_tile(t, carry):
                k = t - u_lo
                kk = jnp.minimum(k, GATHER_TILES)

                @pl.when(k >= GATHER_TILES)
                def _():
                    tok = pl.multiple_of(t * TOK_TILE, TOK_TILE)
                    pltpu.sync_copy(h_hbm.at[pl.ds(tok, TOK_TILE)], hbuf.at[slot, GATHER_TILES])
                    pltpu.sync_copy(rt_hbm.at[:, pl.ds(tok, TOK_TILE)], rbuf.at[slot, GATHER_TILES])

                return carry + jnp.dot(tile_weights(rbuf[slot, kk], first_rank), hbuf[slot, kk],
                                       preferred_element_type=F32)

            t_next = u_lo + jnp.minimum(k0 + WINDOW_TILES, GATHER_TILES)
            rows = lax.fori_loop(jnp.maximum(t_next, t_lo), t_hi + 1, late_tile, rows)
            selected.append(rows.astype(BF16))

        hg = jnp.concatenate(selected, axis=0)
        half = w1_ref.shape[2] // 2
        y = None
        for c in range(2):
            a = jnp.dot(hg, w1_ref[0, :, c * half:(c + 1) * half], preferred_element_type=F32)
            b = jnp.dot(hg, w3_ref[0, :, c * half:(c + 1) * half], preferred_element_type=F32)
            part = jnp.dot((_silu(a) * b).astype(BF16), w2_ref[0, c * half:(c + 1) * half, :],
                           preferred_element_type=F32)
            y = part if y is None else y + part
        y_ref[...] = y.astype(BF16)

    @pl.when(bv_ref[i] == 0)
    def _():
        y_ref[...] = jnp.zeros_like(y_ref)


def _moe_experts(h2, rank_t, w1, w3, w2, tb):
    nb = tb["n_blocks"]
    ff = w1.shape[2]
    w_map = lambda i, be, *_: (be[i], 0, 0)
    grid_spec = pltpu.PrefetchScalarGridSpec(
        num_scalar_prefetch=7,
        grid=(nb,),
        in_specs=[
            pl.BlockSpec(memory_space=pl.ANY),
            pl.BlockSpec(memory_space=pl.ANY),
            pl.BlockSpec((1, D_MODEL, ff), w_map, pipeline_mode=pl.Buffered(1)),
            pl.BlockSpec((1, D_MODEL, ff), w_map, pipeline_mode=pl.Buffered(1)),
            pl.BlockSpec((1, ff, D_MODEL), w_map, pipeline_mode=pl.Buffered(1)),
        ],
        out_specs=pl.BlockSpec((ROW_BLOCK, D_MODEL), lambda i, *_: (i, 0)),
        scratch_shapes=[
            pltpu.VMEM((2, GATHER_TILES + 1, TOK_TILE, D_MODEL), BF16),
            pltpu.VMEM((2, GATHER_TILES + 1, N_EXPERTS, TOK_TILE), F32),
            pltpu.SemaphoreType.DMA((2, GATHER_TILES)),
            pltpu.SemaphoreType.DMA((2, GATHER_TILES)),
        ],
    )
    return pl.pallas_call(
        _experts_kernel,
        grid_spec=grid_spec,
        out_shape=jax.ShapeDtypeStruct((nb * ROW_BLOCK, D_MODEL), BF16),
        compiler_params=_params("arbitrary"),
        name="moe_experts",
    )(tb["blk_e"], tb["valid"], tb["base"], tb["u_lo"], tb["u_n"], tb["sub_lo"], tb["sub_hi"],
      h2, rank_t, w1, w3, w2)


def _combine_kernel(*refs, final_norm, n_rows):
    first_ref, nwin_ref, seg_ref, x_ref, mod_ref, comb_ref, rank_ref, y_hbm = refs[0:8]
    fg_ref = refs[8] if final_norm else None
    o_ref, ybuf, ysem, acc_ref = refs[-4:]
    i = pl.program_id(0)
    slot = i % 2
    extra = N_EXPERTS

    def window(b, e):
        return first_ref[b * N_EXPERTS + e], nwin_ref[b * N_EXPERTS + e]

    def win_start(first, j):
        return pl.multiple_of(jnp.minimum(first + j * COMBINE_WIN, n_rows - COMBINE_WIN), 16)

    def for_tile_windows(b, s, action):
        for e in range(N_EXPERTS):
            first, n_win = window(b, e)

            @pl.when(n_win > 0)
            def _():
                action(pltpu.make_async_copy(y_hbm.at[pl.ds(win_start(first, 0), COMBINE_WIN)],
                                             ybuf.at[s, e], ysem.at[s, e]))

    @pl.when(i == 0)
    def _():
        def clear(k, carry):
            ybuf[k // (N_EXPERTS + 1), k % (N_EXPERTS + 1)] = jnp.zeros((COMBINE_WIN, D_MODEL), BF16)
            return carry

        lax.fori_loop(0, 2 * (N_EXPERTS + 1), clear, 0)
        for_tile_windows(0, 0, lambda cp: cp.start())

    for_tile_windows(i, slot, lambda cp: cp.wait())

    @pl.when(i + 1 < pl.num_programs(0))
    def _():
        for_tile_windows(i + 1, 1 - slot, lambda cp: cp.start())

    col_id = lax.broadcasted_iota(jnp.int32, (COMBINE_TM, COMBINE_WIN), 1).astype(F32)

    def weights(e, first, j):
        pos_col = rank_ref[:, e:e + 1] + seg_ref[e].astype(F32)
        start = win_start(first, j).astype(F32)
        return jnp.where(pos_col == start + col_id, comb_ref[:, e:e + 1], 0.0).astype(BF16)

    firsts = [window(i, e) for e in range(N_EXPERTS)]
    pick_all = jnp.concatenate([weights(e, firsts[e][0], 0) for e in range(N_EXPERTS)], axis=1)
    y_all = ybuf[slot, 0:N_EXPERTS].reshape(N_EXPERTS * COMBINE_WIN, D_MODEL)
    acc_ref[...] = jnp.dot(pick_all, y_all, preferred_element_type=F32)

    for e in range(N_EXPERTS):
        first, n_win = firsts[e]

        def more(j, carry):
            pltpu.sync_copy(y_hbm.at[pl.ds(win_start(first, j), COMBINE_WIN)], ybuf.at[slot, extra])
            acc_ref[...] += jnp.dot(weights(e, first, j), ybuf[slot, extra], preferred_element_type=F32)
            return carry

        lax.fori_loop(1, jnp.maximum(n_win, 1), more, 0)

    o_ref[...] = _residual_out(x_ref[...], mod_ref[0, 5:6, :], acc_ref[...], fg_ref)


def _moe_combine(x2d, mod, comb, rank, y_sorted, tb, *, rows_per_mod, mod_base, final_g=None):
    n = x2d.shape[0]
    tm = COMBINE_TM
    row = lambda i, *_: (i, 0)
    const = lambda i, *_: (0, 0)
    mod_idx = _mod_index(tm, rows_per_mod, mod_base)
    in_specs = [
        pl.BlockSpec((tm, D_MODEL), row),
        pl.BlockSpec((1, 6, D_MODEL), lambda i, *_: mod_idx(i)),
        pl.BlockSpec((tm, LANES), row),
        pl.BlockSpec((tm, LANES), row),
        pl.BlockSpec(memory_space=pl.ANY),
    ]
    args = [x2d, mod, comb, rank, y_sorted]
    if final_g is not None:
        in_specs.append(pl.BlockSpec((1, D_MODEL), const))
        args.append(final_g)
    grid_spec = pltpu.PrefetchScalarGridSpec(
        num_scalar_prefetch=3,
        grid=(n // tm,),
        in_specs=in_specs,
        out_specs=pl.BlockSpec((tm, D_MODEL), row),
        scratch_shapes=[
            pltpu.VMEM((2, N_EXPERTS + 1, COMBINE_WIN, D_MODEL), BF16),
            pltpu.SemaphoreType.DMA((2, N_EXPERTS)),
            pltpu.VMEM((tm, D_MODEL), F32),
        ],
    )
    return pl.pallas_call(
        functools.partial(_combine_kernel, final_norm=final_g is not None, n_rows=y_sorted.shape[0]),
        grid_spec=grid_spec,
        out_shape=jax.ShapeDtypeStruct((n, D_MODEL), F32),
        compiler_params=_params("arbitrary", vmem_mib=44),
        name="moe_combine",
    )(tb["first_c"], tb["nwin_c"], tb["seg_rows"], *args)


def _to_bf16_in(w, steps):
    if w.dtype == BF16:
        return None, w
    flat = w.reshape(-1, w.shape[-1])
    if _can_cast_in_steps(flat, steps):
        return flat, None
    return None, w.astype(BF16)


def _moe_channel_mixer(x2d, mod, y_ssd, y_local, w_a, w_b, g, router_pad, w1, w3, w2, *, rows_per_mod, mod_base,
                       final_g=None):
    w1 = w1.astype(BF16)
    w3 = w3.astype(BF16)
    tm = math.gcd(ROUTER_TM, rows_per_mod)
    w2_flat, w2_bf16 = _to_bf16_in(w2, x2d.shape[0] // tm)
    x1, h2, comb, rank, rank_t, cnt, *cast = _router(
        x2d, mod, y_ssd, y_local, w_a, w_b, g, router_pad, tm=tm, rows_per_mod=rows_per_mod,
        mod_base=mod_base, casts=() if w2_flat is None else (w2_flat,))
    if w2_bf16 is None:
        w2_bf16 = cast[0].reshape(w2.shape)
    w2 = w2_bf16
    tb = _routing_tables(cnt, x2d.shape[0])
    y_sorted = _moe_experts(h2, rank_t, w1, w3, w2, tb)
    return _moe_combine(x1, mod, comb, rank, y_sorted, tb, rows_per_mod=rows_per_mod, mod_base=mod_base,
                        final_g=final_g)


def _layer_params(l, w_in, w_out, ssd_conv_w, ssd_conv_b, ssd_dt_bias, ssd_a_log, ssd_d, ssd_norm_g,
                  gm_norm_g, gm_norm_b, gm_ws, gm_bs, sc_conv_w, cf_conv_w, cf_conv_b, cf_norm_g, cf_norm_b):
    wi = w_in[l]
    w_main = jnp.concatenate(
        [wi[:, _C_XBC:_C_DT], wi[:, _C_Z:_C_UV], wi[:, _C_UV:_C_SC], wi[:, _C_CF:_C_END], wi[:, _C_SC:_C_CF]],
        axis=1).astype(BF16)
    w_dt = jnp.pad(wi[:, _C_DT:_C_Z], ((0, 0), (0, LANES - 2 * SSD_HEADS))).astype(BF16)
    return {
        "w_main": w_main,
        "w_dt": w_dt,
        "w_out_a": w_out[l, 0:BRANCH].astype(BF16),
        "w_out_b": w_out[l, BRANCH:].astype(BF16),
        "ssd_conv_w": ssd_conv_w[l],
        "ssd_conv_b": ssd_conv_b[l].reshape(1, -1),
        "ssd_dtp": jnp.stack([jnp.tile(ssd_dt_bias[l].reshape(-1), LANES // (2 * SSD_HEADS)),
                              jnp.tile(ssd_a_log[l].reshape(-1), LANES // (2 * SSD_HEADS))]),
        "ssd_vec": jnp.stack([jnp.repeat(ssd_d[l], HEAD_DIM), ssd_norm_g[l]]),
        "gm_ws": gm_ws[l],
        "gm_bs": jnp.repeat(gm_bs[l].T, HEAD_DIM, axis=1),
        "sc_conv_w": sc_conv_w[l],
        "cf_conv_w": cf_conv_w[l],
        "local_vec": jnp.stack([gm_norm_g[l], gm_norm_b[l], cf_conv_b[l], cf_norm_g[l], cf_norm_b[l]]),
    }


def kernel(x, c, ctx, c_ctx, mod_w, mod_b, norm1_g, norm2_g, w_in, w_out, ssd_conv_w, ssd_conv_b, ssd_dt_bias, ssd_a_log, ssd_d, ssd_norm_g, gm_norm_g, gm_norm_b, gm_ws, gm_bs, sc_conv_w, cf_conv_w, cf_conv_b, cf_norm_g, cf_norm_b, ffn_w1, ffn_w3, ffn_w2, moe_router, moe_w1, moe_w3, moe_w2, final_norm_g):
    bsz, seq, _ = x.shape
    ctx_len = ctx.shape[1]
    depth = w_in.shape[0]
    n_x, n_c = bsz * seq, bsz * ctx_len
    tm = 512

    mod = _modulation(jnp.concatenate([c, c_ctx[None, :]], axis=0), mod_w, mod_b)
    x_map = dict(tm=tm, rows_per_mod=seq, mod_base=0)
    c_map = dict(tm=tm, rows_per_mod=n_c, mod_base=bsz)
    zero_state = jnp.zeros((bsz, 2, SSD_GROUPS, SSD_STATE, 2 * HEAD_DIM), F32)

    xs = x.reshape(n_x, D_MODEL)
    xc = ctx.reshape(n_c, D_MODEL)
    moe_bf16 = {}
    for l in range(depth):
        last = l == depth - 1
        lp = _layer_params(l, w_in, w_out, ssd_conv_w, ssd_conv_b, ssd_dt_bias, ssd_a_log, ssd_d, ssd_norm_g,
                           gm_norm_g, gm_norm_b, gm_ws, gm_bs, sc_conv_w, cf_conv_w, cf_conv_b, cf_norm_g,
                           cf_norm_b)
        g1 = norm1_g[l].reshape(1, D_MODEL)
        g2 = norm2_g[l].reshape(1, D_MODEL)
        ffn_f32 = [ffn_w1[l // 2], ffn_w3[l // 2], ffn_w2[l // 2]] if l % 2 == 0 else []
        ctx_tm = min(tm, ctx_len)
        if not all(_can_cast_in_steps(w, n_c // ctx_tm) for w in ffn_f32):
            ffn_f32 = []
        zc_ssd, yc_loc, dtc, ffn_bf16 = _token_mixer_front(xc, mod[l], g1, lp, tm=ctx_tm, seq=ctx_len,
                                                           rows_per_mod=n_c, mod_base=bsz, local=not last,
                                                           casts=ffn_f32)
        yc_ssd, states = _ssd_mixer(zc_ssd.reshape(bsz, ctx_len, W_SSD), dtc.reshape(bsz, ctx_len, -1),
                                    zero_state, lp)
        z_ssd, y_loc, dtx, _ = _token_mixer_front(xs, mod[l], g1, lp, tm=min(FRONT_TM, seq), seq=seq,
                                                  rows_per_mod=seq, mod_base=0)
        y_ssd, _ = _ssd_mixer(z_ssd.reshape(bsz, seq, W_SSD), dtx.reshape(bsz, seq, -1), states, lp)
        streams = [(xs, y_ssd.reshape(n_x, BRANCH), y_loc, x_map)]
        if not last:
            streams.append((xc, yc_ssd.reshape(n_c, BRANCH), yc_loc, c_map))
        fg = final_norm_g.reshape(1, D_MODEL) if last else None
        i = l // 2
        if l % 2 == 0:
            w1, w3, w2 = ffn_bf16 or (ffn_w1[i].astype(BF16), ffn_w3[i].astype(BF16), ffn_w2[i].astype(BF16))
            casts = []
            if l + 1 < depth:
                casts = [w.reshape(-1, w.shape[-1]) for w in (moe_w1[i], moe_w3[i])]
                if not all(_can_cast_in_steps(w, n_x // tm) for w in casts):
                    casts = []
            outs = []
            for t, ya, yb, m in streams:
                out, cast = _dense_channel_mixer(t, mod[l], ya, yb, lp["w_out_a"], lp["w_out_b"], g2, w1, w3, w2,
                                                 final_g=fg if t is xs else None, casts=casts if t is xs else (),
                                                 **m)
                outs.append(out)
                if cast:
                    moe_bf16[i] = [c.reshape(moe_w1[i].shape) for c in cast]
        else:
            w1, w3 = moe_bf16.get(i) or (moe_w1[i], moe_w3[i])
            w2 = moe_w2[i]
            r_hi = moe_router[i].astype(BF16)
            r_lo = (moe_router[i] - r_hi.astype(F32)).astype(BF16)
            r_pad = jnp.pad(jnp.concatenate([r_hi, r_lo], axis=1), ((0, 0), (0, LANES - 2 * N_EXPERTS)))
            outs = [_moe_channel_mixer(t, mod[l], ya, yb, lp["w_out_a"], lp["w_out_b"], g2, r_pad, w1, w3, w2,
                                       rows_per_mod=m["rows_per_mod"], mod_base=m["mod_base"],
                                       final_g=fg if t is xs else None) for t, ya, yb, m in streams]
        xs = outs[0]
        if not last:
            xc = outs[1]
    return xs.reshape(bsz, seq, D_MODEL)
```
